```python
import jax, jax.numpy as jnp
from jax import lax
import numpy as np

D_MODEL = 1024
BATCH = 4
SEQ = 4096
DEPTH = 2
DEC_BATCH = 32
DEC_SEQ = 8
PAST_LEN = 16384
PAGE_SIZE = 128

N_EVEN = (DEPTH + 1) // 2
N_ODD = DEPTH // 2
D_A = D_MODEL // 2
CONV_W = 3
H_B = 8
HEAD_DIM = 64
D_B = H_B * HEAD_DIM
BRANCHES = ((128, 1), (512, 4), (2048, 16))
WIN_MAX = 2048
ROPE_THETA = 10000.0
D_IN_EVEN = 3 * D_A + 3 * D_B
D_FF = 2816
N_EXPERTS = 8
TOP_K = 2
D_FF_E = 3584
D_C = D_MODEL
C_GROUPS = 4
CHUNK = 128
EPS = 1e-6

kernel_name = "hybrid_conv_dilattn_gmlp_moe_step"


def rmsnorm(x, g):
    xf = x.astype(jnp.float32)
    y = xf * lax.rsqrt(jnp.mean(xf * xf, axis=-1, keepdims=True) + EPS)
    return (y * g.astype(jnp.float32)).astype(x.dtype)


def rope(x, pos):
    half = HEAD_DIM // 2
    inv = ROPE_THETA ** (-jnp.arange(half, dtype=jnp.float32) / half)
    ang = pos.astype(jnp.float32)[:, None] * inv[None, :]
    cos = jnp.cos(ang)[None, :, None, :]
    sin = jnp.sin(ang)[None, :, None, :]
    xf = x.astype(jnp.float32)
    x1, x2 = xf[..., :half], xf[..., half:]
    return jnp.concatenate([x1 * cos - x2 * sin, x2 * cos + x1 * sin], axis=-1).astype(x.dtype)


def masked_softmax_stats(s, valid):
    s = jnp.where(valid, s, -jnp.inf)
    mx = jnp.max(s, axis=-1, keepdims=True)
    p = jnp.exp(s - mx)
    l = jnp.sum(p, axis=-1, keepdims=True)
    return p / l, (mx + jnp.log(l))[..., 0]


def dilated_branch_prompt(q, k, v, window, dil):
    b, t, h, d = q.shape
    n = window // dil
    m = t // dil
    blk = n
    nb = -(-m // blk)
    pad = nb * blk - m

    def strided(a):
        return a.reshape(b, m, dil, h, d).transpose(0, 2, 1, 3, 4)

    qs = jnp.pad(strided(q), ((0, 0), (0, 0), (0, pad), (0, 0), (0, 0)))
    kv_pad = ((0, 0), (0, 0), (blk, pad), (0, 0), (0, 0))
    ks = jnp.pad(strided(k), kv_pad)
    vs = jnp.pad(strided(v), kv_pad)
    qb = qs.reshape(b, dil, nb, blk, h, d)
    kb = ks.reshape(b, dil, nb + 1, blk, h, d)
    vb = vs.reshape(b, dil, nb + 1, blk, h, d)
    kcat = jnp.concatenate([kb[:, :, :-1], kb[:, :, 1:]], axis=3)
    vcat = jnp.concatenate([vb[:, :, :-1], vb[:, :, 1:]], axis=3)
    s = jnp.einsum('brnqhd,brnkhd->brnhqk', qb, kcat).astype(jnp.float32) * (HEAD_DIM ** -0.5)
    qi = jnp.arange(blk)[:, None] + blk
    kj = jnp.arange(2 * blk)[None, :]
    dist = qi - kj
    band = (dist >= 0) & (dist <= n)
    first = (jnp.arange(nb)[:, None, None] == 0) & (kj[None] < blk)
    valid = (band[None] & ~first)[None, None, :, None]
    p, lse = masked_softmax_stats(s, valid)
    o = jnp.einsum('brnhqk,brnkhd->brnqhd', p.astype(v.dtype), vcat)
    o = o.reshape(b, dil, nb * blk, h, d)[:, :, :m].transpose(0, 2, 1, 3, 4).reshape(b, t, h, d)
    lse = lse.transpose(0, 1, 2, 4, 3).reshape(b, dil, nb * blk, h)[:, :, :m]
    lse = lse.transpose(0, 2, 1, 3).reshape(b, t, h)
    return o, lse


def dilated_branch_sample(q, kc, vc, window, dil, lb):
    s_len = q.shape[1]
    n = window // dil
    idx = lb + jnp.arange(s_len)[:, None] - dil * jnp.arange(n + 1)[None, :]
    valid = idx >= 0
    idx = jnp.maximum(idx, 0)
    kg = kc[:, idx]
    vg = vc[:, idx]
    s = jnp.einsum('bshd,bskhd->bhsk', q, kg).astype(jnp.float32) * (HEAD_DIM ** -0.5)
    p, lse = masked_softmax_stats(s, valid[None, None])
    o = jnp.einsum('bhsk,bskhd->bshd', p.astype(vc.dtype), vg)
    return o, lse.transpose(0, 2, 1)


def dilated_mixture(outs):
    o = jnp.stack([a for a, _ in outs], axis=0).astype(jnp.float32)
    lse = jnp.stack([l for _, l in outs], axis=0)
    w = jax.nn.softmax(lse, axis=0)
    return jnp.sum(w[..., None] * o, axis=0)


def even_mixer(xn, pos, conv_prev, kv_buf, w_in, conv_w, q_gain, k_gain, w_out):
    b, t, _ = xn.shape
    hcat = xn @ w_in
    a_in, a_gc, a_gb, q, k, v = jnp.split(
        hcat, [D_A, 2 * D_A, 3 * D_A, 3 * D_A + D_B, 3 * D_A + 2 * D_B], axis=-1)
    z = a_gc * a_in
    if conv_prev is None:
        conv_prev = jnp.zeros((b, CONV_W - 1, D_A), z.dtype)
    zc = jnp.concatenate([conv_prev, z], axis=1)
    conv = sum(conv_w[j] * zc[:, j:j + t] for j in range(CONV_W))
    a_out = a_gb * conv
    conv_new = zc[:, zc.shape[1] - (CONV_W - 1):]
    q = rope(rmsnorm(q.reshape(b, t, H_B, HEAD_DIM), q_gain), pos)
    k = rope(rmsnorm(k.reshape(b, t, H_B, HEAD_DIM), k_gain), pos)
    v = v.reshape(b, t, H_B, HEAD_DIM)
    if kv_buf is None:
        outs = [dilated_branch_prompt(q, k, v, wnd, dl) for wnd, dl in BRANCHES]
        lw = min(WIN_MAX, t)
        kv_new = jnp.stack([k[:, t - lw:], v[:, t - lw:]], axis=2)
    else:
        lb = kv_buf.shape[1]
        kc = jnp.concatenate([kv_buf[:, :, 0], k], axis=1)
        vc = jnp.concatenate([kv_buf[:, :, 1], v], axis=1)
        outs = [dilated_branch_sample(q, kc, vc, wnd, dl, lb) for wnd, dl in BRANCHES]
        kv_new = jnp.stack([k, v], axis=2)
    o = dilated_mixture(outs).astype(xn.dtype).reshape(b, t, D_B)
    y = jnp.concatenate([a_out, o], axis=-1) @ w_out
    return y, conv_new, kv_new


def chunk_gmlp(xn, w_uv, v_gain, w_s, b_s, w_out):
    bsz, length, _ = xn.shape
    h = jax.nn.gelu(xn @ w_uv)
    u, v = jnp.split(h, 2, axis=-1)
    v = rmsnorm(v, v_gain)
    lc = min(length, CHUNK)
    nc = -(-length // lc)
    pad = nc * lc - length
    vp = jnp.pad(v, ((0, 0), (0, pad), (0, 0))).reshape(bsz, nc, lc, C_GROUPS, D_C // C_GROUPS)
    tri = jnp.tril(jnp.ones((lc, lc), dtype=bool))
    ws = jnp.where(tri[None], w_s[:, :lc, :lc], 0)
    mixed = jnp.einsum('gij,bcjgd->bcigd', ws, vp) + b_s[:, :lc].T[None, None, :, :, None]
    mixed = mixed.reshape(bsz, nc * lc, D_C)[:, :length]
    y = (u * mixed) @ w_out
    tail = ((length - 1) // CHUNK) * CHUNK
    return y, v[:, tail:]


def swiglu(xn, wg, wu, wd):
    return (jax.nn.silu(xn @ wg) * (xn @ wu)) @ wd


def moe_swiglu(xn, w_router, w_gate, w_up, w_down):
    logits = (xn @ w_router).astype(jnp.float32)
    top_val, top_idx = lax.top_k(logits, TOP_K)
    top_w = jax.nn.softmax(top_val, axis=-1)
    comb = jnp.sum(jax.nn.one_hot(top_idx, N_EXPERTS, dtype=jnp.float32) * top_w[..., None], axis=-2)
    comb = comb.astype(xn.dtype)
    y = jnp.zeros_like(xn)
    for e in range(N_EXPERTS):
        y = y + comb[..., e:e + 1] * swiglu(xn, w_gate[e], w_up[e], w_down[e])
    return y


def setup_inputs(seed: int = 0) -> dict:
    key = jax.random.key(seed)
    ks = jax.random.split(key, 25)

    def nrm(k, shape, scale=1.0):
        return jax.random.normal(k, shape, jnp.float32) * scale

    lw = min(WIN_MAX, PAST_LEN)
    return {
        'x_prompt': nrm(ks[0], (BATCH, SEQ, D_MODEL)),
        'x_sample': nrm(ks[1], (DEC_BATCH, DEC_SEQ, D_MODEL)),
        'state_conv': nrm(ks[2], (N_EVEN, DEC_BATCH, CONV_W - 1, D_A)),
        'cache_kv_win': nrm(ks[3], (N_EVEN, DEC_BATCH, lw, 2, H_B, HEAD_DIM)),
        'norm_mix_e': 1.0 + nrm(ks[4], (N_EVEN, D_MODEL), 0.1),
        'w_in_e': nrm(ks[5], (N_EVEN, D_MODEL, D_IN_EVEN), D_MODEL ** -0.5),
        'conv_w': nrm(ks[6], (N_EVEN, CONV_W, D_A), CONV_W ** -0.5),
        'q_gain': 1.0 + nrm(ks[7], (N_EVEN, HEAD_DIM), 0.1),
        'k_gain': 1.0 + nrm(ks[8], (N_EVEN, HEAD_DIM), 0.1),
        'w_out_e': nrm(ks[9], (N_EVEN, D_A + D_B, D_MODEL), (D_A + D_B) ** -0.5),
        'norm_ffn_e': 1.0 + nrm(ks[10], (N_EVEN, D_MODEL), 0.1),
        'ffn_w_gate': nrm(ks[11], (N_EVEN, D_MODEL, D_FF), D_MODEL ** -0.5),
        'ffn_w_up': nrm(ks[12], (N_EVEN, D_MODEL, D_FF), D_MODEL ** -0.5),
        'ffn_w_down': nrm(ks[13], (N_EVEN, D_FF, D_MODEL), D_FF ** -0.5),
        'norm_mix_o': 1.0 + nrm(ks[14], (N_ODD, D_MODEL), 0.1),
        'w_uv': nrm(ks[15], (N_ODD, D_MODEL, 2 * D_C), D_MODEL ** -0.5),
        'v_gain': 1.0 + nrm(ks[16], (N_ODD, D_C), 0.1),
        'w_s': nrm(ks[17], (N_ODD, C_GROUPS, CHUNK, CHUNK), CHUNK ** -0.5),
        'b_s': 1.0 + nrm(ks[18], (N_ODD, C_GROUPS, CHUNK), 0.1),
        'w_out_o': nrm(ks[19], (N_ODD, D_C, D_MODEL), D_C ** -0.5),
        'norm_moe': 1.0 + nrm(ks[20], (N_ODD, D_MODEL), 0.1),
        'w_router': nrm(ks[21], (N_ODD, D_MODEL, N_EXPERTS), D_MODEL ** -0.5),
        'moe_w_gate': nrm(ks[22], (N_ODD, N_EXPERTS, D_MODEL, D_FF_E), D_MODEL ** -0.5),
        'moe_w_up': nrm(ks[23], (N_ODD, N_EXPERTS, D_MODEL, D_FF_E), D_MODEL ** -0.5),
        'moe_w_down': nrm(ks[24], (N_ODD, N_EXPERTS, D_FF_E, D_MODEL), D_FF_E ** -0.5),
    }


def reference(x_prompt, x_sample, state_conv, cache_kv_win, norm_mix_e, w_in_e, conv_w, q_gain,
              k_gain, w_out_e, norm_ffn_e, ffn_w_gate, ffn_w_up, ffn_w_down, norm_mix_o, w_uv,
              v_gain, w_s, b_s, w_out_o, norm_moe, w_router, moe_w_gate, moe_w_up, moe_w_down):
    pos_p = jnp.arange(x_prompt.shape[1], dtype=jnp.int32)
    pos_s = PAST_LEN + jnp.arange(x_sample.shape[1], dtype=jnp.int32)
    hp, hs = x_prompt, x_sample
    conv_p, conv_s, kv_p, kv_s, v_p, v_s = [], [], [], [], [], []
    for layer in range(DEPTH):
        i = layer // 2
        if layer % 2 == 0:
            mix_w = (w_in_e[i], conv_w[i], q_gain[i], k_gain[i], w_out_e[i])
            mp, cp, kp = even_mixer(rmsnorm(hp, norm_mix_e[i]), pos_p, None, None, *mix_w)
            ms, cs, kvs = even_mixer(rmsnorm(hs, norm_mix_e[i]), pos_s, state_conv[i],
                                     cache_kv_win[i], *mix_w)
            hp = hp + mp
            hs = hs + ms
            conv_p.append(cp); conv_s.append(cs); kv_p.append(kp); kv_s.append(kvs)
            hp = hp + swiglu(rmsnorm(hp, norm_ffn_e[i]), ffn_w_gate[i], ffn_w_up[i], ffn_w_down[i])
            hs = hs + swiglu(rmsnorm(hs, norm_ffn_e[i]), ffn_w_gate[i], ffn_w_up[i], ffn_w_down[i])
        else:
            gw = (w_uv[i], v_gain[i], w_s[i], b_s[i], w_out_o[i])
            mp, vp = chunk_gmlp(rmsnorm(hp, norm_mix_o[i]), *gw)
            ms, vs = chunk_gmlp(rmsnorm(hs, norm_mix_o[i]), *gw)
            hp = hp + mp
            hs = hs + ms
            v_p.append(vp); v_s.append(vs)
            ew = (w_router[i], moe_w_gate[i], moe_w_up[i], moe_w_down[i])
            hp = hp + moe_swiglu(rmsnorm(hp, norm_moe[i]), *ew)
            hs = hs + moe_swiglu(rmsnorm(hs, norm_moe[i]), *ew)
    return (hp, hs, jnp.stack(conv_p), jnp.stack(conv_s), jnp.stack(kv_p), jnp.stack(kv_s),
            jnp.stack(v_p), jnp.stack(v_s))
```

```python
import functools

import jax
import jax.numpy as jnp
from jax import lax
from jax.experimental import pallas as pl
from jax.experimental.pallas import tpu as pltpu

F32 = jnp.float32
BF16 = jnp.bfloat16

D_MODEL = 1024
D_A = 512
H_B = 8
HEAD_DIM = 64
D_B = H_B * HEAD_DIM
BRANCHES = ((128, 1), (512, 4), (2048, 16))
BAND = 128
ROPE_THETA = 10000.0
PAST_LEN = 16384
N_EXPERTS = 8
C_GROUPS = 4
CHUNK = 128
EPS = 1e-6
LANES = 128
VMEM_LIMIT = 56 * 1024 * 1024


def _params(n_axes):
    return pltpu.CompilerParams(dimension_semantics=("arbitrary",) * n_axes,
                                vmem_limit_bytes=VMEM_LIMIT)


def _rms_bf16(x, g):
    ms = jnp.mean(x * x, axis=-1, keepdims=True)
    return (x * lax.rsqrt(ms + EPS) * g).astype(BF16)


def _inproj_kernel(x_ref, g_ref, w_ref, cos_ref, sina_ref, sinb_ref, qg_ref, kg_ref, bd_ref,
                   z_ref, gb_ref, q_ref, k16_ref, v16_ref, kv_ref):
    xn = _rms_bf16(x_ref[...], g_ref[...])

    def col(j):
        return jnp.dot(xn, w_ref[:, j * 512:(j + 1) * 512], preferred_element_type=F32)

    z_ref[...] = col(1) * col(0)
    gb_ref[...] = col(2)
    cos, sina, sinb = cos_ref[...], sina_ref[...], sinb_ref[...]

    def head_norm_rope(t, gain):
        outs = []
        for c2 in range(2):
            tc = t[:, c2 * 256:(c2 + 1) * 256]
            ss = jnp.dot((tc * tc).astype(BF16), bd_ref[...], preferred_element_type=F32)
            tn = tc * lax.rsqrt(ss * (1.0 / HEAD_DIM) + EPS) * gain
            for c in range(2):
                u = tn[:, c * LANES:(c + 1) * LANES]
                outs.append(u * cos + pltpu.roll(u, LANES - 32, 1) * sina
                            + pltpu.roll(u, 32, 1) * sinb)
        return outs

    q = head_norm_rope(col(3), qg_ref[...])
    for c in range(4):
        q_ref[:, c * LANES:(c + 1) * LANES] = (q[c] * (HEAD_DIM ** -0.5)).astype(BF16)
    k = head_norm_rope(col(4), kg_ref[...])
    for c in range(4):
        kv_ref[:, c * LANES:(c + 1) * LANES] = k[c]
        k16_ref[:, c * LANES:(c + 1) * LANES] = k[c].astype(BF16)
    v = col(5)
    kv_ref[:, D_B:] = v
    v16_ref[...] = v.astype(BF16)


def _inproj(x, g, w16, tabs, qg, kg, bd, tm):
    n = x.shape[0]
    cos, sina, sinb = tabs
    tt = cos.shape[0] // tm
    row = lambda i: (i, 0)
    fix = lambda i: (0, 0)
    tab = lambda i: (i % tt, 0)
    return pl.pallas_call(
        _inproj_kernel,
        grid=(n // tm,),
        in_specs=[pl.BlockSpec((tm, D_MODEL), row), pl.BlockSpec((1, D_MODEL), fix),
                  pl.BlockSpec(w16.shape, fix),
                  pl.BlockSpec((tm, LANES), tab), pl.BlockSpec((tm, LANES), tab),
                  pl.BlockSpec((tm, LANES), tab),
                  pl.BlockSpec((1, 256), fix), pl.BlockSpec((1, 256), fix),
                  pl.BlockSpec((256, 256), fix)],
        out_specs=[pl.BlockSpec((tm, D_A), row), pl.BlockSpec((tm, D_A), row),
                   pl.BlockSpec((tm, D_B), row), pl.BlockSpec((tm, D_B), row),
                   pl.BlockSpec((tm, D_B), row), pl.BlockSpec((tm, 2 * D_B), row)],
        out_shape=[jax.ShapeDtypeStruct((n, D_A), F32), jax.ShapeDtypeStruct((n, D_A), F32),
                   jax.ShapeDtypeStruct((n, D_B), BF16), jax.ShapeDtypeStruct((n, D_B), BF16),
                   jax.ShapeDtypeStruct((n, D_B), BF16), jax.ShapeDtypeStruct((n, 2 * D_B), F32)],
        compiler_params=_params(1), name="inproj",
    )(x, g, w16, cos, sina, sinb, qg, kg, bd)


def _attn_prompt_kernel(q_ref, kc_ref, kp_ref, vc_ref, vp_ref, o_ref, lse_ref, *, qb):
    j = pl.program_id(2)
    nsub = qb // BAND
    q = q_ref[0]
    kcat = jnp.concatenate([kp_ref[0], kc_ref[0]], axis=0)
    vcat = jnp.concatenate([vp_ref[0], vc_ref[0]], axis=0)
    lane = lax.broadcasted_iota(jnp.int32, (1, LANES), 1)
    head0 = lane < HEAD_DIM
    qi = lax.broadcasted_iota(jnp.int32, (BAND, 2 * BAND), 0) + BAND
    kj = lax.broadcasted_iota(jnp.int32, (BAND, 2 * BAND), 1)
    dist = qi - kj
    band = (dist >= 0) & (dist <= BAND)
    for sub in range(nsub):
        qs = q[sub * BAND:(sub + 1) * BAND]
        ks = kcat[sub * BAND:(sub + 2) * BAND]
        vs = vcat[sub * BAND:(sub + 2) * BAND]
        if sub == 0:
            valid = band & ((kj >= BAND) | (j > 0))
        else:
            valid = band
        o_h, lse_h = [], []
        for hh in range(2):
            qm = jnp.where(head0 if hh == 0 else ~head0, qs, jnp.zeros_like(qs))
            s = lax.dot_general(qm, ks, (((1,), (1,)), ((), ())), preferred_element_type=F32)
            s = jnp.where(valid, s, -jnp.inf)
            mx = jnp.max(s, axis=-1, keepdims=True)
            p = jnp.exp(s - mx)
            l = jnp.sum(p, axis=-1, keepdims=True)
            o_h.append(jnp.dot(p.astype(BF16), vs, preferred_element_type=F32) / l)
            lse_h.append(mx + jnp.log(l))
        o_ref[0, sub * BAND:(sub + 1) * BAND, :] = jnp.where(head0, o_h[0], o_h[1]).astype(BF16)
        lse_ref[0, sub * BAND:(sub + 1) * BAND, :] = jnp.where(head0, lse_h[0], lse_h[1])


def _attn_prompt(q, k, v, qb):
    b, m, w = q.shape
    nsub = qb // BAND
    cur = lambda bi, hp, j: (bi, j, hp)
    prev = lambda bi, hp, j: (bi, jnp.maximum(j * nsub - 1, 0), hp)
    return pl.pallas_call(
        functools.partial(_attn_prompt_kernel, qb=qb),
        grid=(b, w // LANES, m // qb),
        in_specs=[pl.BlockSpec((1, qb, LANES), cur), pl.BlockSpec((1, qb, LANES), cur),
                  pl.BlockSpec((1, BAND, LANES), prev), pl.BlockSpec((1, qb, LANES), cur),
                  pl.BlockSpec((1, BAND, LANES), prev)],
        out_specs=[pl.BlockSpec((1, qb, LANES), cur), pl.BlockSpec((1, qb, LANES), cur)],
        out_shape=[jax.ShapeDtypeStruct((b, m, w), BF16), jax.ShapeDtypeStruct((b, m, w), F32)],
        compiler_params=_params(3), name="attn_prompt",
    )(q, k, k, v, v)


def _outproj_prompt_kernel(z_ref, zh_ref, gb_ref, cw_ref, o1_ref, o2_ref, o3_ref,
                           l1_ref, l2_ref, l3_ref, w_ref, h_ref, out_ref, *, tiles_per_batch):
    i = pl.program_id(0)
    z = z_ref[...]
    tm = z.shape[0]
    zh = jnp.where(i % tiles_per_batch == 0, 0.0, zh_ref[...])
    row = lax.broadcasted_iota(jnp.int32, (tm, 1), 0)
    zm1 = jnp.where(row >= 1, pltpu.roll(z, 1, 0), zh[7:8])
    zm2 = jnp.where(row >= 2, pltpu.roll(z, 2, 0), jnp.where(row == 1, zh[7:8], zh[6:7]))
    cw = cw_ref[...]
    a_out = gb_ref[...] * (cw[0:1] * zm2 + cw[1:2] * zm1 + cw[2:3] * z)
    l1, l2, l3 = l1_ref[...], l2_ref[...], l3_ref[...]
    lm = jnp.maximum(jnp.maximum(l1, l2), l3)
    e1, e2, e3 = jnp.exp(l1 - lm), jnp.exp(l2 - lm), jnp.exp(l3 - lm)
    o = (e1 * o1_ref[...].astype(F32) + e2 * o2_ref[...].astype(F32)
         + e3 * o3_ref[...].astype(F32)) / (e1 + e2 + e3)
    y = jnp.dot(a_out.astype(BF16), w_ref[:D_A, :], preferred_element_type=F32)
    y = y + jnp.dot(o.astype(BF16), w_ref[D_A:, :], preferred_element_type=F32)
    out_ref[...] = h_ref[...] + y


def _outproj_prompt(z, gb, cw, os_, ls_, w16, h, tm, seq):
    n = z.shape[0]
    row = lambda i: (i, 0)
    fix = lambda i: (0, 0)
    halo = lambda i: (jnp.maximum(i * (tm // 8) - 1, 0), 0)
    half = pl.BlockSpec((tm, D_A), row)
    return pl.pallas_call(
        functools.partial(_outproj_prompt_kernel, tiles_per_batch=seq // tm),
        grid=(n // tm,),
        in_specs=[half, pl.BlockSpec((8, D_A), halo), half, pl.BlockSpec((3, D_A), fix),
                  half, half, half, half, half, half,
                  pl.BlockSpec((D_MODEL, D_MODEL), fix), pl.BlockSpec((tm, D_MODEL), row)],
        out_specs=pl.BlockSpec((tm, D_MODEL), row),
        out_shape=jax.ShapeDtypeStruct((n, D_MODEL), F32),
        compiler_params=_params(1), name="outproj_prompt",
    )(z, z, gb, cw, *os_, *ls_, w16, h)


def _mix_sample_kernel(q_ref, kvn_ref, cache_ref, z_ref, gb_ref, st_ref, cw_ref, cat_ref,
                       *, lb):
    s_len = q_ref.shape[1]
    rows = H_B * s_len
    npad = LANES
    q = q_ref[0]
    r_i = lax.broadcasted_iota(jnp.int32, (rows, D_B), 0)
    c_i = lax.broadcasted_iota(jnp.int32, (rows, D_B), 1)
    qt = jnp.concatenate([q] * H_B, axis=0)
    qbd = jnp.where(r_i // s_len == c_i // HEAD_DIM, qt, 0.0).astype(BF16)
    kc = cache_ref[0, :, :D_B].astype(BF16)
    vc = cache_ref[0, :, D_B:].astype(BF16)
    kvn = jnp.concatenate([kvn_ref[0], jnp.zeros((npad - s_len, 2 * D_B), F32)], axis=0)
    kn = kvn[:, :D_B].astype(BF16)
    vn = kvn[:, D_B:].astype(BF16)
    nt = (((1,), (1,)), ((), ()))
    sc = lax.dot_general(qbd, kc, nt, preferred_element_type=F32)
    sn = lax.dot_general(qbd, kn, nt, preferred_element_type=F32)

    def multiplicity(delta):
        cnt = jnp.zeros(delta.shape, F32)
        for _, dil in BRANCHES:
            ok = (delta >= 0) & (delta <= BAND * dil) & ((delta & (dil - 1)) == 0)
            cnt = cnt + jnp.where(ok, 1.0, 0.0)
        return cnt

    tok_c = lax.broadcasted_iota(jnp.int32, (rows, lb), 0) % s_len
    cnt_c = multiplicity(lb + tok_c - lax.broadcasted_iota(jnp.int32, (rows, lb), 1))
    tok_n = lax.broadcasted_iota(jnp.int32, (rows, npad), 0) % s_len
    cnt_n = multiplicity(tok_n - lax.broadcasted_iota(jnp.int32, (rows, npad), 1))
    mx = jnp.maximum(
        jnp.max(jnp.where(cnt_c > 0, sc, -jnp.inf), axis=-1, keepdims=True),
        jnp.max(jnp.where(cnt_n > 0, sn, -jnp.inf), axis=-1, keepdims=True))
    pc = cnt_c * jnp.exp(jnp.where(cnt_c > 0, sc, -jnp.inf) - mx)
    pn = cnt_n * jnp.exp(jnp.where(cnt_n > 0, sn, -jnp.inf) - mx)
    den = jnp.sum(pc, axis=-1, keepdims=True) + jnp.sum(pn, axis=-1, keepdims=True)
    of = jnp.dot(pc.astype(BF16), vc, preferred_element_type=F32)
    of = (of + jnp.dot(pn.astype(BF16), vn, preferred_element_type=F32)) / den
    lane_head = lax.broadcasted_iota(jnp.int32, (s_len, D_B), 1) // HEAD_DIM
    o = jnp.zeros((s_len, D_B), F32)
    for h in range(H_B):
        o = o + jnp.where(lane_head == h, of[h * s_len:(h + 1) * s_len], 0.0)
    z = z_ref[0]
    st = st_ref[0]
    t = lax.broadcasted_iota(jnp.int32, (s_len, 1), 0)
    zm1 = jnp.where(t >= 1, pltpu.roll(z, 1, 0), st[1:2])
    zm2 = jnp.where(t >= 2, pltpu.roll(z, 2, 0), jnp.where(t == 1, st[1:2], st[0:1]))
    cw = cw_ref[...]
    cat_ref[0, :, :D_A] = gb_ref[0] * (cw[0:1] * zm2 + cw[1:2] * zm1 + cw[2:3] * z)
    cat_ref[0, :, D_A:] = o


def _mix_sample(q, kvn, cache, z, gb, st, cw):
    b, s_len, _ = q.shape
    lb = cache.shape[1]
    per = lambda i: (i, 0, 0)
    fix = lambda i: (0, 0)
    blk = lambda a: pl.BlockSpec((1,) + a.shape[1:], per)
    return pl.pallas_call(
        functools.partial(_mix_sample_kernel, lb=lb),
        grid=(b,),
        in_specs=[blk(q), blk(kvn), blk(cache), blk(z), blk(gb), blk(st),
                  pl.BlockSpec((3, D_A), fix)],
        out_specs=pl.BlockSpec((1, s_len, D_MODEL), per),
        out_shape=jax.ShapeDtypeStruct((b, s_len, D_MODEL), F32),
        compiler_params=_params(1), name="mix_sample",
    )(q, kvn, cache, z, gb, st, cw)


def _linear_res_kernel(x_ref, w_ref, h_ref, out_ref):
    out_ref[...] = h_ref[...] + jnp.dot(x_ref[...].astype(BF16), w_ref[...],
                                        preferred_element_type=F32)


def _linear_res(x, w16, h):
    n, k = x.shape
    full = lambda a: pl.BlockSpec(a.shape, lambda i: (0, 0))
    return pl.pallas_call(
        _linear_res_kernel, grid=(1,),
        in_specs=[full(x), full(w16), full(h)], out_specs=full(h),
        out_shape=jax.ShapeDtypeStruct(h.shape, F32),
        compiler_params=_params(1), name="linear_res",
    )(x, w16, h)


def _ffn_kernel(te_ref, x_ref, g_ref, sc_ref, res_ref, wg_ref, wu_ref, wd_ref, out_ref,
                xn_sc, acc_sc, *, norm):
    del te_ref
    f = pl.program_id(1)

    @pl.when(f == 0)
    def _():
        if norm:
            xn_sc[...] = _rms_bf16(x_ref[...], g_ref[...])
        else:
            xn_sc[...] = x_ref[...]
        acc_sc[...] = jnp.zeros_like(acc_sc)

    xn = xn_sc[...]
    a = jnp.dot(xn, wg_ref[0], preferred_element_type=F32)
    b = jnp.dot(xn, wu_ref[0], preferred_element_type=F32)
    t = (jax.nn.silu(a) * b).astype(BF16)
    acc_sc[...] += jnp.dot(t, wd_ref[0], preferred_element_type=F32)

    @pl.when(f == pl.num_programs(1) - 1)
    def _():
        out_ref[...] = res_ref[...] + sc_ref[...] * acc_sc[...]


def _ffn(tile_expert, x, g, scale, res, wg, wu, wd, *, tm, fc, norm):
    n = x.shape[0]
    ff = wg.shape[2]
    row = lambda i, f, te: (i, 0)
    fix = lambda i, f, te: (0, 0)
    grid_spec = pltpu.PrefetchScalarGridSpec(
        num_scalar_prefetch=1, grid=(n // tm, ff // fc),
        in_specs=[pl.BlockSpec((tm, D_MODEL), row), pl.BlockSpec((1, D_MODEL), fix),
                  pl.BlockSpec((tm, 1), row), pl.BlockSpec((tm, D_MODEL), row),
                  pl.BlockSpec((1, D_MODEL, fc), lambda i, f, te: (te[i], 0, f)),
                  pl.BlockSpec((1, D_MODEL, fc), lambda i, f, te: (te[i], 0, f)),
                  pl.BlockSpec((1, fc, D_MODEL), lambda i, f, te: (te[i], f, 0))],
        out_specs=pl.BlockSpec((tm, D_MODEL), row),
        scratch_shapes=[pltpu.VMEM((tm, D_MODEL), BF16), pltpu.VMEM((tm, D_MODEL), F32)])
    return pl.pallas_call(
        functools.partial(_ffn_kernel, norm=norm), grid_spec=grid_spec,
        out_shape=jax.ShapeDtypeStruct((n, D_MODEL), F32),
        compiler_params=_params(2), name="ffn",
    )(tile_expert, x, g, scale, res, wg, wu, wd)


def _gmlp_kernel(h_ref, g_ref, wuv_ref, vg_ref, mix_ref, bs_ref, wo_ref, out_ref, v_ref, um_sc,
                 *, period):
    h = h_ref[...]
    tm = h.shape[0]
    ln = mix_ref.shape[1]
    gw = D_MODEL // C_GROUPS
    xn = _rms_bf16(h, g_ref[...])
    vv = jax.nn.gelu(jnp.dot(xn, wuv_ref[:, D_MODEL:], preferred_element_type=F32))
    v = vv * lax.rsqrt(jnp.mean(vv * vv, axis=-1, keepdims=True) + EPS) * vg_ref[...]
    v_ref[...] = v
    vb = v.astype(BF16)
    u = jax.nn.gelu(jnp.dot(xn, wuv_ref[:, :D_MODEL], preferred_element_type=F32))
    r = lax.broadcasted_iota(jnp.int32, (ln, ln), 0)
    c = lax.broadcasted_iota(jnp.int32, (ln, ln), 1)
    causal = (r // period == c // period) & (c % period <= r % period)
    for gi in range(C_GROUPS):
        wm = jnp.where(causal, mix_ref[gi], 0.0).astype(BF16)
        bias = bs_ref[:, gi:gi + 1]
        for ch in range(tm // ln):
            rs = slice(ch * ln, (ch + 1) * ln)
            cs = slice(gi * gw, (gi + 1) * gw)
            mixed = jnp.dot(wm, vb[rs, cs], preferred_element_type=F32) + bias
            um_sc[rs, cs] = (u[rs, cs] * mixed).astype(BF16)
    out_ref[...] = h + jnp.dot(um_sc[...], wo_ref[...], preferred_element_type=F32)


def _gmlp(h, g, wuv16, vg, mix, bs, wo16, *, tm, period):
    n = h.shape[0]
    row = lambda i: (i, 0)
    full = lambda a: pl.BlockSpec(a.shape, lambda i: (0,) * a.ndim)
    return pl.pallas_call(
        functools.partial(_gmlp_kernel, period=period),
        grid=(n // tm,),
        in_specs=[pl.BlockSpec((tm, D_MODEL), row), full(g), full(wuv16), full(vg), full(mix),
                  full(bs), full(wo16)],
        out_specs=[pl.BlockSpec((tm, D_MODEL), row), pl.BlockSpec((tm, D_MODEL), row)],
        out_shape=[jax.ShapeDtypeStruct((n, D_MODEL), F32),
                   jax.ShapeDtypeStruct((n, D_MODEL), F32)],
        scratch_shapes=[pltpu.VMEM((tm, D_MODEL), BF16)],
        compiler_params=_params(1), name="gmlp",
    )(h, g, wuv16, vg, mix, bs, wo16)


def _router_kernel(h_ref, g_ref, wr_ref, xn_ref, comb_ref):
    x = h_ref[...]
    xn = x * lax.rsqrt(jnp.mean(x * x, axis=-1, keepdims=True) + EPS) * g_ref[...]
    xn_ref[...] = xn.astype(BF16)
    logits = jnp.dot(xn, wr_ref[...], preferred_element_type=F32,
                     precision=lax.Precision.HIGHEST)
    lane = lax.broadcasted_iota(jnp.int32, logits.shape, 1)
    lanef = lane.astype(F32)
    logits = jnp.where(lane < N_EXPERTS, logits, -jnp.inf)
    m1 = jnp.max(logits, axis=-1, keepdims=True)
    i1 = jnp.min(jnp.where(logits == m1, lanef, float(LANES)), axis=-1, keepdims=True)
    rest = jnp.where(lanef == i1, -jnp.inf, logits)
    m2 = jnp.max(rest, axis=-1, keepdims=True)
    i2 = jnp.min(jnp.where(rest == m2, lanef, float(LANES)), axis=-1, keepdims=True)
    e2 = jnp.exp(m2 - m1)
    w1 = 1.0 / (1.0 + e2)
    w2 = e2 / (1.0 + e2)
    comb_ref[...] = jnp.where(lanef == i1, w1, 0.0) + jnp.where(lanef == i2, w2, 0.0)


def _router(h, g, wr_pad, tm):
    n = h.shape[0]
    row = lambda i: (i, 0)
    fix = lambda i: (0, 0)
    return pl.pallas_call(
        _router_kernel, grid=(n // tm,),
        in_specs=[pl.BlockSpec((tm, D_MODEL), row), pl.BlockSpec((1, D_MODEL), fix),
                  pl.BlockSpec((D_MODEL, LANES), fix)],
        out_specs=[pl.BlockSpec((tm, D_MODEL), row), pl.BlockSpec((tm, LANES), row)],
        out_shape=[jax.ShapeDtypeStruct((n, D_MODEL), BF16),
                   jax.ShapeDtypeStruct((n, LANES), F32)],
        compiler_params=_params(1), name="router",
    )(h, g, wr_pad)


def _rope_tables(pos):
    half = HEAD_DIM // 2
    inv = ROPE_THETA ** (-jnp.arange(half, dtype=F32) / half)
    ang = pos.astype(F32)[:, None] * inv[None, :]
    cos, sin = jnp.cos(ang), jnp.sin(ang)
    zero = jnp.zeros_like(sin)
    rep = lambda a, b: jnp.tile(jnp.concatenate([a, b], axis=-1), (1, LANES // HEAD_DIM))
    return rep(cos, cos), rep(-sin, zero), rep(zero, sin)


def _tile_for(n, pref):
    return pref if n % pref == 0 else n


def kernel(x_prompt, x_sample, state_conv, cache_kv_win, norm_mix_e, w_in_e, conv_w, q_gain,
           k_gain, w_out_e, norm_ffn_e, ffn_w_gate, ffn_w_up, ffn_w_down, norm_mix_o, w_uv,
           v_gain, w_s, b_s, w_out_o, norm_moe, w_router, moe_w_gate, moe_w_up, moe_w_down):
    bp, seq, _ = x_prompt.shape
    bs_, s_len, _ = x_sample.shape
    n_p, n_s = bp * seq, bs_ * s_len
    hp = x_prompt.reshape(n_p, D_MODEL)
    hs = x_sample.reshape(n_s, D_MODEL)
    tm_p = 512
    row = lambda a: a.reshape(1, -1)

    tabs_p = _rope_tables(jnp.arange(seq, dtype=jnp.int32))
    tabs_s = tuple(jnp.tile(t, (bs_, 1)) for t in
                   _rope_tables(PAST_LEN + jnp.arange(s_len, dtype=jnp.int32)))
    hd = jnp.arange(256) // HEAD_DIM
    bd = (hd[:, None] == hd[None, :]).astype(BF16)
    ones_p = jnp.ones((n_p, 1), F32)
    ones_s = jnp.ones((n_s, 1), F32)

    w_in16 = w_in_e[0].astype(BF16)
    w_out16 = w_out_e[0].astype(BF16)
    qg = jnp.tile(q_gain[0], 4).reshape(1, 256)
    kg = jnp.tile(k_gain[0], 4).reshape(1, 256)
    g0 = row(norm_mix_e[0])

    z_p, gb_p, q_p, k_p, v_p, kv_p = _inproj(hp, g0, w_in16, tabs_p, qg, kg, bd, tm_p)
    z_s, gb_s, q_s, k_s, v_s, kv_s = _inproj(hs, g0, w_in16, tabs_s, qg, kg, bd, n_s)

    os_, ls_ = [], []
    for _, dil in BRANCHES:
        m = seq // dil
        view = lambda a: a.reshape(bp, m, dil * D_B)
        o, lse = _attn_prompt(view(q_p), view(k_p), view(v_p), qb=min(256, m))
        os_.append(o.reshape(n_p, D_B))
        ls_.append(lse.reshape(n_p, D_B))
    hp = _outproj_prompt(z_p, gb_p, conv_w[0], os_, ls_, w_out16, hp, tm_p, seq)

    cat_s = _mix_sample(q_s.astype(F32).reshape(bs_, s_len, D_B),
                        kv_s.reshape(bs_, s_len, 2 * D_B),
                        cache_kv_win[0].reshape(bs_, -1, 2 * D_B),
                        z_s.reshape(bs_, s_len, D_A), gb_s.reshape(bs_, s_len, D_A),
                        state_conv[0], conv_w[0])
    hs = _linear_res(cat_s.reshape(n_s, D_MODEL), w_out16, hs)

    conv_prompt = z_p.reshape(bp, seq, D_A)[:, seq - 2:][None]
    conv_sample = z_s.reshape(bs_, s_len, D_A)[:, s_len - 2:][None]
    lw = min(2048, seq)
    kv_prompt = kv_p.reshape(bp, seq, 2, H_B, HEAD_DIM)[:, seq - lw:][None]
    kv_sample = kv_s.reshape(bs_, s_len, 2, H_B, HEAD_DIM)[None]

    zero_p = jnp.zeros((n_p // tm_p,), jnp.int32)
    zero_s = jnp.zeros((1,), jnp.int32)
    fg, fu, fd = (ffn_w_gate.astype(BF16), ffn_w_up.astype(BF16), ffn_w_down.astype(BF16))
    gf = row(norm_ffn_e[0])
    hp = _ffn(zero_p, hp, gf, ones_p, hp, fg, fu, fd, tm=tm_p, fc=1408, norm=True)
    hs = _ffn(zero_s, hs, gf, ones_s, hs, fg, fu, fd, tm=n_s, fc=1408, norm=True)

    wuv16 = w_uv[0].astype(BF16)
    wo16 = w_out_o[0].astype(BF16)
    g1, vg = row(norm_mix_o[0]), row(v_gain[0])
    hp, vfull_p = _gmlp(hp, g1, wuv16, vg, w_s[0], b_s[0].T, wo16, tm=tm_p, period=CHUNK)
    mix_s = jnp.tile(w_s[0][:, :s_len, :s_len], (1, bs_, bs_))
    bias_s = jnp.tile(b_s[0][:, :s_len].T, (bs_, 1))
    hs, vfull_s = _gmlp(hs, g1, wuv16, vg, mix_s, bias_s, wo16, tm=n_s, period=s_len)
    tail = ((seq - 1) // CHUNK) * CHUNK
    v_prompt = vfull_p.reshape(bp, seq, D_MODEL)[:, tail:][None]
    v_sample = vfull_s.reshape(bs_, s_len, D_MODEL)[None]

    wr_pad = jnp.pad(w_router[0], ((0, 0), (0, LANES - N_EXPERTS)))
    gm = row(norm_moe[0])
    xn_p, comb_p = _router(hp, gm, wr_pad, tm_p)
    xn_s, comb_s = _router(hs, gm, wr_pad, n_s)
    mg, mu, md = (moe_w_gate[0].astype(BF16), moe_w_up[0].astype(BF16),
                  moe_w_down[0].astype(BF16))
    for e in range(N_EXPERTS):
        hp = _ffn(zero_p + e, xn_p, gm, comb_p[:, e:e + 1], hp, mg, mu, md,
                  tm=tm_p, fc=896, norm=False)
        hs = _ffn(zero_s + e, xn_s, gm, comb_s[:, e:e + 1], hs, mg, mu, md,
                  tm=n_s, fc=896, norm=False)

    return (hp.reshape(bp, seq, D_MODEL), hs.reshape(bs_, s_len, D_MODEL),
            conv_prompt, conv_sample, kv_prompt, kv_sample, v_prompt, v_sample)
```

```python
import functools

import jax
import jax.numpy as jnp
from jax import lax
from jax.experimental import pallas as pl
from jax.experimental.pallas import tpu as pltpu
from jax.experimental.pallas import tpu_sc as plsc

F32 = jnp.float32
BF16 = jnp.bfloat16

D_MODEL = 1024
D_A = 512
H_B = 8
HEAD_DIM = 64
D_B = H_B * HEAD_DIM
BRANCHES = ((128, 1), (512, 4), (2048, 16))
BAND = 128
ROPE_THETA = 10000.0
PAST_LEN = 16384
N_EXPERTS = 8
C_GROUPS = 4
CHUNK = 128
EPS = 1e-6
LANES = 128
VMEM_LIMIT = 56 * 1024 * 1024
SC_CORES = 2
SC_WORKERS = SC_CORES * 16
SC_CHUNK = 64
MOE_ROWS = 512


def _params(n_axes):
    return pltpu.CompilerParams(dimension_semantics=("arbitrary",) * n_axes,
                                vmem_limit_bytes=VMEM_LIMIT)


def _rms_bf16(x, g):
    ms = jnp.mean(x * x, axis=-1, keepdims=True)
    return (x * lax.rsqrt(ms + EPS) * g).astype(BF16)


def _inproj_kernel(x_ref, g_ref, w_ref, cos_ref, sina_ref, sinb_ref, qg_ref, kg_ref, bd_ref,
                   z_ref, gb_ref, q_ref, k16_ref, v16_ref, kv_ref):
    xn = _rms_bf16(x_ref[...], g_ref[...])

    def col(j):
        return jnp.dot(xn, w_ref[:, j * 512:(j + 1) * 512], preferred_element_type=F32)

    z_ref[...] = col(1) * col(0)
    gb_ref[...] = col(2)
    cos, sina, sinb = cos_ref[...], sina_ref[...], sinb_ref[...]

    def head_norm_rope(t, gain):
        outs = []
        for c2 in range(2):
            tc = t[:, c2 * 256:(c2 + 1) * 256]
            ss = jnp.dot((tc * tc).astype(BF16), bd_ref[...], preferred_element_type=F32)
            tn = tc * lax.rsqrt(ss * (1.0 / HEAD_DIM) + EPS) * gain
            for c in range(2):
                u = tn[:, c * LANES:(c + 1) * LANES]
                outs.append(u * cos + pltpu.roll(u, LANES - 32, 1) * sina
                            + pltpu.roll(u, 32, 1) * sinb)
        return outs

    q = head_norm_rope(col(3), qg_ref[...])
    for c in range(4):
        q_ref[:, c * LANES:(c + 1) * LANES] = (q[c] * (HEAD_DIM ** -0.5)).astype(BF16)
    k = head_norm_rope(col(4), kg_ref[...])
    for c in range(4):
        kv_ref[:, c * LANES:(c + 1) * LANES] = k[c]
        k16_ref[:, c * LANES:(c + 1) * LANES] = k[c].astype(BF16)
    v = col(5)
    kv_ref[:, D_B:] = v
    v16_ref[...] = v.astype(BF16)


def _inproj(x, g, w16, tabs, qg, kg, bd, tm):
    n = x.shape[0]
    cos, sina, sinb = tabs
    tt = cos.shape[0] // tm
    row = lambda i: (i, 0)
    fix = lambda i: (0, 0)
    tab = lambda i: (i % tt, 0)
    return pl.pallas_call(
        _inproj_kernel,
        grid=(n // tm,),
        in_specs=[pl.BlockSpec((tm, D_MODEL), row), pl.BlockSpec((1, D_MODEL), fix),
                  pl.BlockSpec(w16.shape, fix),
                  pl.BlockSpec((tm, LANES), tab), pl.BlockSpec((tm, LANES), tab),
                  pl.BlockSpec((tm, LANES), tab),
                  pl.BlockSpec((1, 256), fix), pl.BlockSpec((1, 256), fix),
                  pl.BlockSpec((256, 256), fix)],
        out_specs=[pl.BlockSpec((tm, D_A), row), pl.BlockSpec((tm, D_A), row),
                   pl.BlockSpec((tm, D_B), row), pl.BlockSpec((tm, D_B), row),
                   pl.BlockSpec((tm, D_B), row), pl.BlockSpec((tm, 2 * D_B), row)],
        out_shape=[jax.ShapeDtypeStruct((n, D_A), F32), jax.ShapeDtypeStruct((n, D_A), F32),
                   jax.ShapeDtypeStruct((n, D_B), BF16), jax.ShapeDtypeStruct((n, D_B), BF16),
                   jax.ShapeDtypeStruct((n, D_B), BF16), jax.ShapeDtypeStruct((n, 2 * D_B), F32)],
        compiler_params=_params(1), name="inproj",
    )(x, g, w16, cos, sina, sinb, qg, kg, bd)


def _attn_prompt_kernel(q_ref, kc_ref, kp_ref, vc_ref, vp_ref, o_ref, lse_ref, *, qb):
    j = pl.program_id(2)
    nsub = qb // BAND
    q = q_ref[0]
    kcat = jnp.concatenate([kp_ref[0], kc_ref[0]], axis=0)
    vcat = jnp.concatenate([vp_ref[0], vc_ref[0]], axis=0)
    lane = lax.broadcasted_iota(jnp.int32, (1, LANES), 1)
    head0 = lane < HEAD_DIM
    qi = lax.broadcasted_iota(jnp.int32, (BAND, 2 * BAND), 0) + BAND
    kj = lax.broadcasted_iota(jnp.int32, (BAND, 2 * BAND), 1)
    dist = qi - kj
    band = (dist >= 0) & (dist <= BAND)
    for sub in range(nsub):
        qs = q[sub * BAND:(sub + 1) * BAND]
        ks = kcat[sub * BAND:(sub + 2) * BAND]
        vs = vcat[sub * BAND:(sub + 2) * BAND]
        if sub == 0:
            valid = band & ((kj >= BAND) | (j > 0))
        else:
            valid = band
        o_h, lse_h = [], []
        for hh in range(2):
            qm = jnp.where(head0 if hh == 0 else ~head0, qs, jnp.zeros_like(qs))
            s = lax.dot_general(qm, ks, (((1,), (1,)), ((), ())), preferred_element_type=F32)
            s = jnp.where(valid, s, -jnp.inf)
            mx = jnp.max(s, axis=-1, keepdims=True)
            p = jnp.exp(s - mx)
            l = jnp.sum(p, axis=-1, keepdims=True)
            o_h.append(jnp.dot(p.astype(BF16), vs, preferred_element_type=F32) / l)
            lse_h.append(mx + jnp.log(l))
        o_ref[0, sub * BAND:(sub + 1) * BAND, :] = jnp.where(head0, o_h[0], o_h[1]).astype(BF16)
        lse_ref[0, sub * BAND:(sub + 1) * BAND, :] = jnp.where(head0, lse_h[0], lse_h[1])


def _attn_prompt(q, k, v, qb):
    b, m, w = q.shape
    nsub = qb // BAND
    cur = lambda bi, hp, j: (bi, j, hp)
    prev = lambda bi, hp, j: (bi, jnp.maximum(j * nsub - 1, 0), hp)
    return pl.pallas_call(
        functools.partial(_attn_prompt_kernel, qb=qb),
        grid=(b, w // LANES, m // qb),
        in_specs=[pl.BlockSpec((1, qb, LANES), cur), pl.BlockSpec((1, qb, LANES), cur),
                  pl.BlockSpec((1, BAND, LANES), prev), pl.BlockSpec((1, qb, LANES), cur),
                  pl.BlockSpec((1, BAND, LANES), prev)],
        out_specs=[pl.BlockSpec((1, qb, LANES), cur), pl.BlockSpec((1, qb, LANES), cur)],
        out_shape=[jax.ShapeDtypeStruct((b, m, w), BF16), jax.ShapeDtypeStruct((b, m, w), F32)],
        compiler_params=_params(3), name="attn_prompt",
    )(q, k, k, v, v)


def _outproj_prompt_kernel(z_ref, zh_ref, gb_ref, cw_ref, o1_ref, o2_ref, o3_ref,
                           l1_ref, l2_ref, l3_ref, w_ref, h_ref, out_ref, *, tiles_per_batch):
    i = pl.program_id(0)
    z = z_ref[...]
    tm = z.shape[0]
    zh = jnp.where(i % tiles_per_batch == 0, 0.0, zh_ref[...])
    row = lax.broadcasted_iota(jnp.int32, (tm, 1), 0)
    zm1 = jnp.where(row >= 1, pltpu.roll(z, 1, 0), zh[7:8])
    zm2 = jnp.where(row >= 2, pltpu.roll(z, 2, 0), jnp.where(row == 1, zh[7:8], zh[6:7]))
    cw = cw_ref[...]
    a_out = gb_ref[...] * (cw[0:1] * zm2 + cw[1:2] * zm1 + cw[2:3] * z)
    l1, l2, l3 = l1_ref[...], l2_ref[...], l3_ref[...]
    lm = jnp.maximum(jnp.maximum(l1, l2), l3)
    e1, e2, e3 = jnp.exp(l1 - lm), jnp.exp(l2 - lm), jnp.exp(l3 - lm)
    o = (e1 * o1_ref[...].astype(F32) + e2 * o2_ref[...].astype(F32)
         + e3 * o3_ref[...].astype(F32)) / (e1 + e2 + e3)
    y = jnp.dot(a_out.astype(BF16), w_ref[:D_A, :], preferred_element_type=F32)
    y = y + jnp.dot(o.astype(BF16), w_ref[D_A:, :], preferred_element_type=F32)
    out_ref[...] = h_ref[...] + y


def _outproj_prompt(z, gb, cw, os_, ls_, w16, h, tm, seq):
    n = z.shape[0]
    row = lambda i: (i, 0)
    fix = lambda i: (0, 0)
    halo = lambda i: (jnp.maximum(i * (tm // 8) - 1, 0), 0)
    half = pl.BlockSpec((tm, D_A), row)
    return pl.pallas_call(
        functools.partial(_outproj_prompt_kernel, tiles_per_batch=seq // tm),
        grid=(n // tm,),
        in_specs=[half, pl.BlockSpec((8, D_A), halo), half, pl.BlockSpec((3, D_A), fix),
                  half, half, half, half, half, half,
                  pl.BlockSpec((D_MODEL, D_MODEL), fix), pl.BlockSpec((tm, D_MODEL), row)],
        out_specs=pl.BlockSpec((tm, D_MODEL), row),
        out_shape=jax.ShapeDtypeStruct((n, D_MODEL), F32),
        compiler_params=_params(1), name="outproj_prompt",
    )(z, z, gb, cw, *os_, *ls_, w16, h)


def _mix_sample_kernel(q_ref, kvn_ref, cache_ref, z_ref, gb_ref, st_ref, cw_ref, cat_ref,
                       *, lb):
    s_len = q_ref.shape[1]
    rows = H_B * s_len
    npad = LANES
    q = q_ref[0]
    r_i = lax.broadcasted_iota(jnp.int32, (rows, D_B), 0)
    c_i = lax.broadcasted_iota(jnp.int32, (rows, D_B), 1)
    qt = jnp.concatenate([q] * H_B, axis=0)
    qbd = jnp.where(r_i // s_len == c_i // HEAD_DIM, qt, 0.0).astype(BF16)
    kc = cache_ref[0, :, :D_B].astype(BF16)
    vc = cache_ref[0, :, D_B:].astype(BF16)
    kvn = jnp.concatenate([kvn_ref[0], jnp.zeros((npad - s_len, 2 * D_B), F32)], axis=0)
    kn = kvn[:, :D_B].astype(BF16)
    vn = kvn[:, D_B:].astype(BF16)
    nt = (((1,), (1,)), ((), ()))
    sc = lax.dot_general(qbd, kc, nt, preferred_element_type=F32)
    sn = lax.dot_general(qbd, kn, nt, preferred_element_type=F32)

    def multiplicity(delta):
        cnt = jnp.zeros(delta.shape, F32)
        for _, dil in BRANCHES:
            ok = (delta >= 0) & (delta <= BAND * dil) & ((delta & (dil - 1)) == 0)
            cnt = cnt + jnp.where(ok, 1.0, 0.0)
        return cnt

    tok_c = lax.broadcasted_iota(jnp.int32, (rows, lb), 0) % s_len
    cnt_c = multiplicity(lb + tok_c - lax.broadcasted_iota(jnp.int32, (rows, lb), 1))
    tok_n = lax.broadcasted_iota(jnp.int32, (rows, npad), 0) % s_len
    cnt_n = multiplicity(tok_n - lax.broadcasted_iota(jnp.int32, (rows, npad), 1))
    mx = jnp.maximum(
        jnp.max(jnp.where(cnt_c > 0, sc, -jnp.inf), axis=-1, keepdims=True),
        jnp.max(jnp.where(cnt_n > 0, sn, -jnp.inf), axis=-1, keepdims=True))
    pc = cnt_c * jnp.exp(jnp.where(cnt_c > 0, sc, -jnp.inf) - mx)
    pn = cnt_n * jnp.exp(jnp.where(cnt_n > 0, sn, -jnp.inf) - mx)
    den = jnp.sum(pc, axis=-1, keepdims=True) + jnp.sum(pn, axis=-1, keepdims=True)
    of = jnp.dot(pc.astype(BF16), vc, preferred_element_type=F32)
    of = (of + jnp.dot(pn.astype(BF16), vn, preferred_element_type=F32)) / den
    lane_head = lax.broadcasted_iota(jnp.int32, (s_len, D_B), 1) // HEAD_DIM
    o = jnp.zeros((s_len, D_B), F32)
    for h in range(H_B):
        o = o + jnp.where(lane_head == h, of[h * s_len:(h + 1) * s_len], 0.0)
    z = z_ref[0]
    st = st_ref[0]
    t = lax.broadcasted_iota(jnp.int32, (s_len, 1), 0)
    zm1 = jnp.where(t >= 1, pltpu.roll(z, 1, 0), st[1:2])
    zm2 = jnp.where(t >= 2, pltpu.roll(z, 2, 0), jnp.where(t == 1, st[1:2], st[0:1]))
    cw = cw_ref[...]
    cat_ref[0, :, :D_A] = gb_ref[0] * (cw[0:1] * zm2 + cw[1:2] * zm1 + cw[2:3] * z)
    cat_ref[0, :, D_A:] = o


def _mix_sample(q, kvn, cache, z, gb, st, cw):
    b, s_len, _ = q.shape
    lb = cache.shape[1]
    per = lambda i: (i, 0, 0)
    fix = lambda i: (0, 0)
    blk = lambda a: pl.BlockSpec((1,) + a.shape[1:], per)
    return pl.pallas_call(
        functools.partial(_mix_sample_kernel, lb=lb),
        grid=(b,),
        in_specs=[blk(q), blk(kvn), blk(cache), blk(z), blk(gb), blk(st),
                  pl.BlockSpec((3, D_A), fix)],
        out_specs=pl.BlockSpec((1, s_len, D_MODEL), per),
        out_shape=jax.ShapeDtypeStruct((b, s_len, D_MODEL), F32),
        compiler_params=_params(1), name="mix_sample",
    )(q, kvn, cache, z, gb, st, cw)


def _linear_res_kernel(x_ref, w_ref, h_ref, out_ref):
    out_ref[...] = h_ref[...] + jnp.dot(x_ref[...].astype(BF16), w_ref[...],
                                        preferred_element_type=F32)


def _linear_res(x, w16, h):
    n, k = x.shape
    full = lambda a: pl.BlockSpec(a.shape, lambda i: (0, 0))
    return pl.pallas_call(
        _linear_res_kernel, grid=(1,),
        in_specs=[full(x), full(w16), full(h)], out_specs=full(h),
        out_shape=jax.ShapeDtypeStruct(h.shape, F32),
        compiler_params=_params(1), name="linear_res",
    )(x, w16, h)


def _swiglu_step(xn, wg_ref, wu_ref, wd_ref, acc_sc):
    a = jnp.dot(xn, wg_ref[0], preferred_element_type=F32)
    b = jnp.dot(xn, wu_ref[0], preferred_element_type=F32)
    t = (jax.nn.silu(a) * b).astype(BF16)
    acc_sc[...] += jnp.dot(t, wd_ref[0], preferred_element_type=F32)


def _ffn_dense_kernel(x_ref, g_ref, wg_ref, wu_ref, wd_ref, out_ref, xn_sc, acc_sc):
    f = pl.program_id(1)

    @pl.when(f == 0)
    def _():
        xn_sc[...] = _rms_bf16(x_ref[...], g_ref[...])
        acc_sc[...] = jnp.zeros_like(acc_sc)

    _swiglu_step(xn_sc[...], wg_ref, wu_ref, wd_ref, acc_sc)

    @pl.when(f == pl.num_programs(1) - 1)
    def _():
        out_ref[...] = x_ref[...] + acc_sc[...]


def _ffn_dense(x, g, wg, wu, wd, *, tm, fc):
    n = x.shape[0]
    ff = wg.shape[2]
    row = lambda i, f: (i, 0)
    return pl.pallas_call(
        _ffn_dense_kernel, grid=(n // tm, ff // fc),
        in_specs=[pl.BlockSpec((tm, D_MODEL), row), pl.BlockSpec((1, D_MODEL), lambda i, f: (0, 0)),
                  pl.BlockSpec((1, D_MODEL, fc), lambda i, f: (0, 0, f)),
                  pl.BlockSpec((1, D_MODEL, fc), lambda i, f: (0, 0, f)),
                  pl.BlockSpec((1, fc, D_MODEL), lambda i, f: (0, f, 0))],
        out_specs=pl.BlockSpec((tm, D_MODEL), row),
        out_shape=jax.ShapeDtypeStruct((n, D_MODEL), F32),
        scratch_shapes=[pltpu.VMEM((tm, D_MODEL), BF16), pltpu.VMEM((tm, D_MODEL), F32)],
        compiler_params=_params(2), name="ffn_dense",
    )(x, g, wg, wu, wd)


def _ffn_experts_kernel(te_ref, nu_ref, x_ref, sc_ref, wg_ref, wu_ref, wd_ref, out_ref,
                        xn_sc, acc_sc):
    del te_ref
    i, f = pl.program_id(0), pl.program_id(1)

    @pl.when(i < nu_ref[0])
    def _():
        @pl.when(f == 0)
        def _():
            xn_sc[...] = x_ref[...].astype(BF16)
            acc_sc[...] = jnp.zeros_like(acc_sc)

        _swiglu_step(xn_sc[...], wg_ref, wu_ref, wd_ref, acc_sc)

        @pl.when(f == pl.num_programs(1) - 1)
        def _():
            out_ref[...] = sc_ref[...] * acc_sc[...]


def _ffn_experts(tile_expert, n_used, x, scale, wg, wu, wd, *, tm, fc):
    n = x.shape[0]
    ff = wg.shape[2]
    row = lambda i, f, te, nu: (i, 0)
    grid_spec = pltpu.PrefetchScalarGridSpec(
        num_scalar_prefetch=2, grid=(n // tm, ff // fc),
        in_specs=[pl.BlockSpec((tm, D_MODEL), row), pl.BlockSpec((tm, 1), row),
                  pl.BlockSpec((1, D_MODEL, fc), lambda i, f, te, nu: (te[i], 0, f)),
                  pl.BlockSpec((1, D_MODEL, fc), lambda i, f, te, nu: (te[i], 0, f)),
                  pl.BlockSpec((1, fc, D_MODEL), lambda i, f, te, nu: (te[i], f, 0))],
        out_specs=pl.BlockSpec((tm, D_MODEL), row),
        scratch_shapes=[pltpu.VMEM((tm, D_MODEL), BF16), pltpu.VMEM((tm, D_MODEL), F32)])
    return pl.pallas_call(
        _ffn_experts_kernel, grid_spec=grid_spec,
        out_shape=jax.ShapeDtypeStruct((n, D_MODEL), F32),
        compiler_params=_params(2), name="ffn_experts",
    )(tile_expert, n_used, x, scale, wg, wu, wd)


def _sc_gather(table, idx):
    _, w = table.shape
    b = idx.shape[0]
    per_w = b // SC_WORKERS
    n_chunks = per_w // SC_CHUNK
    mesh = plsc.VectorSubcoreMesh(core_axis_name="c", subcore_axis_name="s")

    @functools.partial(
        pl.kernel, mesh=mesh, out_type=jax.ShapeDtypeStruct((b, w), table.dtype),
        scratch_types=[pltpu.VMEM((SC_CHUNK,), jnp.int32), pltpu.VMEM((SC_CHUNK, w), table.dtype),
                       pltpu.SemaphoreType.DMA])
    def gather(table_hbm, idx_hbm, out_hbm, idx_v, rows_v, sem):
        wid = lax.axis_index("s") * SC_CORES + lax.axis_index("c")

        @pl.loop(0, n_chunks)
        def _(c):
            base = pl.multiple_of(wid * per_w + c * SC_CHUNK, SC_CHUNK)
            pltpu.sync_copy(idx_hbm.at[pl.ds(base, SC_CHUNK)], idx_v)
            pltpu.async_copy(table_hbm.at[idx_v], rows_v, sem).wait()
            pltpu.sync_copy(rows_v, out_hbm.at[pl.ds(base, SC_CHUNK)])

    return gather(table, idx)


def _combine_kernel(h_ref, y_ref, out_ref):
    out_ref[...] = h_ref[...] + y_ref[:, :D_MODEL] + y_ref[:, D_MODEL:]


def _combine(h, y2, blk_off, tm):
    n = h.shape[0]
    return pl.pallas_call(
        _combine_kernel, grid=(n // tm,),
        in_specs=[pl.BlockSpec((tm, D_MODEL), lambda i: (i, 0)),
                  pl.BlockSpec((tm, 2 * D_MODEL), lambda i: (i + blk_off, 0))],
        out_specs=pl.BlockSpec((tm, D_MODEL), lambda i: (i, 0)),
        out_shape=jax.ShapeDtypeStruct((n, D_MODEL), F32),
        compiler_params=_params(1), name="combine",
    )(h, y2)


def _route_plan(sel, te_rows, n_slots):
    e = sel[:, :2].astype(jnp.int32).reshape(-1)
    wgt = sel[:, 2:4].reshape(-1)
    n_asg = e.shape[0]
    onehot = (e[:, None] == jnp.arange(N_EXPERTS, dtype=jnp.int32)[None, :]).astype(jnp.int32)
    csum = jnp.cumsum(onehot, axis=0)
    rank = jnp.sum((csum - onehot) * onehot, axis=1)
    padded = (csum[-1] + te_rows - 1) // te_rows * te_rows
    ends = jnp.cumsum(padded)
    pos = (ends - padded)[e] + rank
    src = jnp.zeros((n_slots,), jnp.int32).at[pos].set(jnp.arange(n_asg, dtype=jnp.int32) // 2)
    scale = jnp.zeros((n_slots,), F32).at[pos].set(wgt).reshape(n_slots, 1)
    starts = jnp.arange(n_slots // te_rows, dtype=jnp.int32) * te_rows
    tile_expert = jnp.minimum(jnp.searchsorted(ends, starts, side="right"),
                              N_EXPERTS - 1).astype(jnp.int32)
    n_used = (ends[-1:] // te_rows).astype(jnp.int32)
    return src, scale, tile_expert, n_used, pos


def _gmlp_kernel(h_ref, g_ref, wuv_ref, vg_ref, mix_ref, bs_ref, wo_ref, out_ref, v_ref, um_sc,
                 *, period):
    h = h_ref[...]
    tm = h.shape[0]
    ln = mix_ref.shape[1]
    gw = D_MODEL // C_GROUPS
    xn = _rms_bf16(h, g_ref[...])
    vv = jax.nn.gelu(jnp.dot(xn, wuv_ref[:, D_MODEL:], preferred_element_type=F32))
    v = vv * lax.rsqrt(jnp.mean(vv * vv, axis=-1, keepdims=True) + EPS) * vg_ref[...]
    v_ref[...] = v
    vb = v.astype(BF16)
    u = jax.nn.gelu(jnp.dot(xn, wuv_ref[:, :D_MODEL], preferred_element_type=F32))
    r = lax.broadcasted_iota(jnp.int32, (ln, ln), 0)
    c = lax.broadcasted_iota(jnp.int32, (ln, ln), 1)
    causal = (r // period == c // period) & (c % period <= r % period)
    for gi in range(C_GROUPS):
        wm = jnp.where(causal, mix_ref[gi], 0.0).astype(BF16)
        bias = bs_ref[:, gi:gi + 1]
        for ch in range(tm // ln):
            rs = slice(ch * ln, (ch + 1) * ln)
            cs = slice(gi * gw, (gi + 1) * gw)
            mixed = jnp.dot(wm, vb[rs, cs], preferred_element_type=F32) + bias
            um_sc[rs, cs] = (u[rs, cs] * mixed).astype(BF16)
    out_ref[...] = h + jnp.dot(um_sc[...], wo_ref[...], preferred_element_type=F32)


def _gmlp(h, g, wuv16, vg, mix, bs, wo16, *, tm, period):
    n = h.shape[0]
    row = lambda i: (i, 0)
    full = lambda a: pl.BlockSpec(a.shape, lambda i: (0,) * a.ndim)
    return pl.pallas_call(
        functools.partial(_gmlp_kernel, period=period),
        grid=(n // tm,),
        in_specs=[pl.BlockSpec((tm, D_MODEL), row), full(g), full(wuv16), full(vg), full(mix),
                  full(bs), full(wo16)],
        out_specs=[pl.BlockSpec((tm, D_MODEL), row), pl.BlockSpec((tm, D_MODEL), row)],
        out_shape=[jax.ShapeDtypeStruct((n, D_MODEL), F32),
                   jax.ShapeDtypeStruct((n, D_MODEL), F32)],
        scratch_shapes=[pltpu.VMEM((tm, D_MODEL), BF16)],
        compiler_params=_params(1), name="gmlp",
    )(h, g, wuv16, vg, mix, bs, wo16)


def _router_kernel(h_ref, g_ref, wr_ref, xn_ref, sel_ref):
    x = h_ref[...]
    xn = x * lax.rsqrt(jnp.mean(x * x, axis=-1, keepdims=True) + EPS) * g_ref[...]
    xn_ref[...] = xn
    logits = jnp.dot(xn, wr_ref[...], preferred_element_type=F32,
                     precision=lax.Precision.HIGHEST)
    lane = lax.broadcasted_iota(jnp.int32, logits.shape, 1)
    lanef = lane.astype(F32)
    logits = jnp.where(lane < N_EXPERTS, logits, -jnp.inf)
    m1 = jnp.max(logits, axis=-1, keepdims=True)
    i1 = jnp.min(jnp.where(logits == m1, lanef, float(LANES)), axis=-1, keepdims=True)
    rest = jnp.where(lanef == i1, -jnp.inf, logits)
    m2 = jnp.max(rest, axis=-1, keepdims=True)
    i2 = jnp.min(jnp.where(rest == m2, lanef, float(LANES)), axis=-1, keepdims=True)
    e2 = jnp.exp(m2 - m1)
    w1 = 1.0 / (1.0 + e2)
    w2 = e2 / (1.0 + e2)
    sel_ref[...] = jnp.where(lane == 0, i1, jnp.where(lane == 1, i2,
                             jnp.where(lane == 2, w1, jnp.where(lane == 3, w2, 0.0))))


def _router(h, g, wr_pad, tm):
    n = h.shape[0]
    row = lambda i: (i, 0)
    fix = lambda i: (0, 0)
    return pl.pallas_call(
        _router_kernel, grid=(n // tm,),
        in_specs=[pl.BlockSpec((tm, D_MODEL), row), pl.BlockSpec((1, D_MODEL), fix),
                  pl.BlockSpec((D_MODEL, LANES), fix)],
        out_specs=[pl.BlockSpec((tm, D_MODEL), row), pl.BlockSpec((tm, LANES), row)],
        out_shape=[jax.ShapeDtypeStruct((n, D_MODEL), F32),
                   jax.ShapeDtypeStruct((n, LANES), F32)],
        compiler_params=_params(1), name="router",
    )(h, g, wr_pad)


def _rope_tables(pos):
    half = HEAD_DIM // 2
    inv = ROPE_THETA ** (-jnp.arange(half, dtype=F32) / half)
    ang = pos.astype(F32)[:, None] * inv[None, :]
    cos, sin = jnp.cos(ang), jnp.sin(ang)
    zero = jnp.zeros_like(sin)
    rep = lambda a, b: jnp.tile(jnp.concatenate([a, b], axis=-1), (1, LANES // HEAD_DIM))
    return rep(cos, cos), rep(-sin, zero), rep(zero, sin)


def _tile_for(n, pref):
    return pref if n % pref == 0 else n


def kernel(x_prompt, x_sample, state_conv, cache_kv_win, norm_mix_e, w_in_e, conv_w, q_gain,
           k_gain, w_out_e, norm_ffn_e, ffn_w_gate, ffn_w_up, ffn_w_down, norm_mix_o, w_uv,
           v_gain, w_s, b_s, w_out_o, norm_moe, w_router, moe_w_gate, moe_w_up, moe_w_down):
    bp, seq, _ = x_prompt.shape
    bs_, s_len, _ = x_sample.shape
    n_p, n_s = bp * seq, bs_ * s_len
    hp = x_prompt.reshape(n_p, D_MODEL)
    hs = x_sample.reshape(n_s, D_MODEL)
    tm_p = 512
    row = lambda a: a.reshape(1, -1)

    tabs_p = _rope_tables(jnp.arange(seq, dtype=jnp.int32))
    tabs_s = tuple(jnp.tile(t, (bs_, 1)) for t in
                   _rope_tables(PAST_LEN + jnp.arange(s_len, dtype=jnp.int32)))
    hd = jnp.arange(256) // HEAD_DIM
    bd = (hd[:, None] == hd[None, :]).astype(BF16)

    w_in16 = w_in_e[0].astype(BF16)
    w_out16 = w_out_e[0].astype(BF16)
    qg = jnp.tile(q_gain[0], 4).reshape(1, 256)
    kg = jnp.tile(k_gain[0], 4).reshape(1, 256)
    g0 = row(norm_mix_e[0])

    z_p, gb_p, q_p, k_p, v_p, kv_p = _inproj(hp, g0, w_in16, tabs_p, qg, kg, bd, tm_p)
    z_s, gb_s, q_s, k_s, v_s, kv_s = _inproj(hs, g0, w_in16, tabs_s, qg, kg, bd, n_s)

    os_, ls_ = [], []
    for _, dil in BRANCHES:
        m = seq // dil
        view = lambda a: a.reshape(bp, m, dil * D_B)
        o, lse = _attn_prompt(view(q_p), view(k_p), view(v_p), qb=min(256, m))
        os_.append(o.reshape(n_p, D_B))
        ls_.append(lse.reshape(n_p, D_B))
    hp = _outproj_prompt(z_p, gb_p, conv_w[0], os_, ls_, w_out16, hp, tm_p, seq)

    cat_s = _mix_sample(q_s.astype(F32).reshape(bs_, s_len, D_B),
                        kv_s.reshape(bs_, s_len, 2 * D_B),
                        cache_kv_win[0].reshape(bs_, -1, 2 * D_B),
                        z_s.reshape(bs_, s_len, D_A), gb_s.reshape(bs_, s_len, D_A),
                        state_conv[0], conv_w[0])
    hs = _linear_res(cat_s.reshape(n_s, D_MODEL), w_out16, hs)

    conv_prompt = z_p.reshape(bp, seq, D_A)[:, seq - 2:][None]
    conv_sample = z_s.reshape(bs_, s_len, D_A)[:, s_len - 2:][None]
    lw = min(2048, seq)
    kv_prompt = kv_p.reshape(bp, seq, 2, H_B, HEAD_DIM)[:, seq - lw:][None]
    kv_sample = kv_s.reshape(bs_, s_len, 2, H_B, HEAD_DIM)[None]

    fg, fu, fd = (ffn_w_gate.astype(BF16), ffn_w_up.astype(BF16), ffn_w_down.astype(BF16))
    gf = row(norm_ffn_e[0])
    hp = _ffn_dense(hp, gf, fg, fu, fd, tm=tm_p, fc=1408)
    hs = _ffn_dense(hs, gf, fg, fu, fd, tm=n_s, fc=1408)

    wuv16 = w_uv[0].astype(BF16)
    wo16 = w_out_o[0].astype(BF16)
    g1, vg = row(norm_mix_o[0]), row(v_gain[0])
    hp, vfull_p = _gmlp(hp, g1, wuv16, vg, w_s[0], b_s[0].T, wo16, tm=tm_p, period=CHUNK)
    mix_s = jnp.tile(w_s[0][:, :s_len, :s_len], (1, bs_, bs_))
    bias_s = jnp.tile(b_s[0][:, :s_len].T, (bs_, 1))
    hs, vfull_s = _gmlp(hs, g1, wuv16, vg, mix_s, bias_s, wo16, tm=n_s, period=s_len)
    tail = ((seq - 1) // CHUNK) * CHUNK
    v_prompt = vfull_p.reshape(bp, seq, D_MODEL)[:, tail:][None]
    v_sample = vfull_s.reshape(bs_, s_len, D_MODEL)[None]

    wr_pad = jnp.pad(w_router[0], ((0, 0), (0, LANES - N_EXPERTS)))
    gm = row(norm_moe[0])
    xn_p, sel_p = _router(hp, gm, wr_pad, tm_p)
    xn_s, sel_s = _router(hs, gm, wr_pad, n_s)
    mg, mu, md = (moe_w_gate[0].astype(BF16), moe_w_up[0].astype(BF16),
                  moe_w_down[0].astype(BF16))
    sc_align = SC_WORKERS * SC_CHUNK
    n_asg = 2 * (n_p + n_s)
    n_slots = -(-(n_asg + N_EXPERTS * (MOE_ROWS - 1)) // sc_align) * sc_align
    n_back = -(-n_asg // sc_align) * sc_align
    src, scale, tile_expert, n_used, pos = _route_plan(
        jnp.concatenate([sel_p, sel_s], axis=0), MOE_ROWS, n_slots)
    xs = _sc_gather(jnp.concatenate([xn_p, xn_s], axis=0), src)
    ys = _ffn_experts(tile_expert, n_used, xs, scale, mg, mu, md, tm=MOE_ROWS, fc=896)
    yb = _sc_gather(ys, jnp.pad(pos, (0, n_back - n_asg))).reshape(n_back // 2, 2 * D_MODEL)
    hp = _combine(hp, yb, 0, n_s)
    hs = _combine(hs, yb, n_p // n_s, n_s)

    return (hp.reshape(bp, seq, D_MODEL), hs.reshape(bs_, s_len, D_MODEL),
            conv_prompt, conv_sample, kv_prompt, kv_sample, v_prompt, v_sample)
```

```python
import functools

import jax
import jax.numpy as jnp
from jax import lax
from jax.experimental import pallas as pl
from jax.experimental.pallas import tpu as pltpu
from jax.experimental.pallas import tpu_sc as plsc

F32 = jnp.float32
BF16 = jnp.bfloat16

D_MODEL = 1024
D_A = 512
H_B = 8
HEAD_DIM = 64
D_B = H_B * HEAD_DIM
BRANCHES = ((128, 1), (512, 4), (2048, 16))
BAND = 128
ROPE_THETA = 10000.0
PAST_LEN = 16384
N_EXPERTS = 8
C_GROUPS = 4
CHUNK = 128
EPS = 1e-6
LANES = 128
VMEM_LIMIT = 56 * 1024 * 1024
SC_CORES = 2
SC_WORKERS = SC_CORES * 16
SC_CHUNK = 128
MOE_ROWS = 1024


def _params(n_axes):
    return pltpu.CompilerParams(dimension_semantics=("arbitrary",) * n_axes,
                                vmem_limit_bytes=VMEM_LIMIT)


def _rms_bf16(x, g):
    ms = jnp.mean(x * x, axis=-1, keepdims=True)
    return (x * lax.rsqrt(ms + EPS) * g).astype(BF16)


def _inproj_kernel(x_ref, g_ref, w_ref, cos_ref, sina_ref, sinb_ref, qg_ref, kg_ref, bd_ref,
                   z_ref, gb_ref, q_ref, k16_ref, v16_ref, kv_ref):
    xn = _rms_bf16(x_ref[...], g_ref[...])

    def col(j):
        return jnp.dot(xn, w_ref[:, j * 512:(j + 1) * 512], preferred_element_type=F32)

    z_ref[...] = col(1) * col(0)
    gb_ref[...] = col(2)
    cos, sina, sinb = cos_ref[...], sina_ref[...], sinb_ref[...]

    def head_norm_rope(t, gain):
        outs = []
        for c2 in range(2):
            tc = t[:, c2 * 256:(c2 + 1) * 256]
            ss = jnp.dot((tc * tc).astype(BF16), bd_ref[...], preferred_element_type=F32)
            tn = tc * lax.rsqrt(ss * (1.0 / HEAD_DIM) + EPS) * gain
            for c in range(2):
                u = tn[:, c * LANES:(c + 1) * LANES]
                outs.append(u * cos + pltpu.roll(u, LANES - 32, 1) * sina
                            + pltpu.roll(u, 32, 1) * sinb)
        return outs

    q = head_norm_rope(col(3), qg_ref[...])
    for c in range(4):
        q_ref[:, c * LANES:(c + 1) * LANES] = (q[c] * (HEAD_DIM ** -0.5)).astype(BF16)
    k = head_norm_rope(col(4), kg_ref[...])
    for c in range(4):
        kv_ref[:, c * LANES:(c + 1) * LANES] = k[c]
        k16_ref[:, c * LANES:(c + 1) * LANES] = k[c].astype(BF16)
    v = col(5)
    kv_ref[:, D_B:] = v
    v16_ref[...] = v.astype(BF16)


def _inproj(x, g, w16, tabs, qg, kg, bd, tm):
    n = x.shape[0]
    cos, sina, sinb = tabs
    tt = cos.shape[0] // tm
    row = lambda i: (i, 0)
    fix = lambda i: (0, 0)
    tab = lambda i: (i % tt, 0)
    return pl.pallas_call(
        _inproj_kernel,
        grid=(n // tm,),
        in_specs=[pl.BlockSpec((tm, D_MODEL), row), pl.BlockSpec((1, D_MODEL), fix),
                  pl.BlockSpec(w16.shape, fix),
                  pl.BlockSpec((tm, LANES), tab), pl.BlockSpec((tm, LANES), tab),
                  pl.BlockSpec((tm, LANES), tab),
                  pl.BlockSpec((1, 256), fix), pl.BlockSpec((1, 256), fix),
                  pl.BlockSpec((256, 256), fix)],
        out_specs=[pl.BlockSpec((tm, D_A), row), pl.BlockSpec((tm, D_A), row),
                   pl.BlockSpec((tm, D_B), row), pl.BlockSpec((tm, D_B), row),
                   pl.BlockSpec((tm, D_B), row), pl.BlockSpec((tm, 2 * D_B), row)],
        out_shape=[jax.ShapeDtypeStruct((n, D_A), F32), jax.ShapeDtypeStruct((n, D_A), F32),
                   jax.ShapeDtypeStruct((n, D_B), BF16), jax.ShapeDtypeStruct((n, D_B), BF16),
                   jax.ShapeDtypeStruct((n, D_B), BF16), jax.ShapeDtypeStruct((n, 2 * D_B), F32)],
        compiler_params=_params(1), name="inproj",
    )(x, g, w16, cos, sina, sinb, qg, kg, bd)


def _attn_prompt_kernel(q_ref, kc_ref, kp_ref, vc_ref, vp_ref, o_ref, lse_ref, *, qb):
    j = pl.program_id(2)
    nsub = qb // BAND
    q = q_ref[0]
    kcat = jnp.concatenate([kp_ref[0], kc_ref[0]], axis=0)
    vcat = jnp.concatenate([vp_ref[0], vc_ref[0]], axis=0)
    lane = lax.broadcasted_iota(jnp.int32, (1, LANES), 1)
    head0 = lane < HEAD_DIM
    qi = lax.broadcasted_iota(jnp.int32, (BAND, 2 * BAND), 0) + BAND
    kj = lax.broadcasted_iota(jnp.int32, (BAND, 2 * BAND), 1)
    dist = qi - kj
    band = (dist >= 0) & (dist <= BAND)
    for sub in range(nsub):
        qs = q[sub * BAND:(sub + 1) * BAND]
        ks = kcat[sub * BAND:(sub + 2) * BAND]
        vs = vcat[sub * BAND:(sub + 2) * BAND]
        if sub == 0:
            valid = band & ((kj >= BAND) | (j > 0))
        else:
            valid = band
        o_h, lse_h = [], []
        for hh in range(2):
            qm = jnp.where(head0 if hh == 0 else ~head0, qs, jnp.zeros_like(qs))
            s = lax.dot_general(qm, ks, (((1,), (1,)), ((), ())), preferred_element_type=F32)
            s = jnp.where(valid, s, -jnp.inf)
            mx = jnp.max(s, axis=-1, keepdims=True)
            p = jnp.exp(s - mx)
            l = jnp.sum(p, axis=-1, keepdims=True)
            o_h.append(jnp.dot(p.astype(BF16), vs, preferred_element_type=F32) / l)
            lse_h.append(mx + jnp.log(l))
        o_ref[0, sub * BAND:(sub + 1) * BAND, :] = jnp.where(head0, o_h[0], o_h[1]).astype(BF16)
        lse_ref[0, sub * BAND:(sub + 1) * BAND, :] = jnp.where(head0, lse_h[0], lse_h[1])


def _attn_prompt(q, k, v, qb):
    b, m, w = q.shape
    nsub = qb // BAND
    cur = lambda bi, hp, j: (bi, j, hp)
    prev = lambda bi, hp, j: (bi, jnp.maximum(j * nsub - 1, 0), hp)
    return pl.pallas_call(
        functools.partial(_attn_prompt_kernel, qb=qb),
        grid=(b, w // LANES, m // qb),
        in_specs=[pl.BlockSpec((1, qb, LANES), cur), pl.BlockSpec((1, qb, LANES), cur),
                  pl.BlockSpec((1, BAND, LANES), prev), pl.BlockSpec((1, qb, LANES), cur),
                  pl.BlockSpec((1, BAND, LANES), prev)],
        out_specs=[pl.BlockSpec((1, qb, LANES), cur), pl.BlockSpec((1, qb, LANES), cur)],
        out_shape=[jax.ShapeDtypeStruct((b, m, w), BF16), jax.ShapeDtypeStruct((b, m, w), F32)],
        compiler_params=_params(3), name="attn_prompt",
    )(q, k, k, v, v)


def _outproj_prompt_kernel(z_ref, zh_ref, gb_ref, cw_ref, o1_ref, o2_ref, o3_ref,
                           l1_ref, l2_ref, l3_ref, w_ref, h_ref, out_ref, *, tiles_per_batch):
    i = pl.program_id(0)
    z = z_ref[...]
    tm = z.shape[0]
    zh = jnp.where(i % tiles_per_batch == 0, 0.0, zh_ref[...])
    row = lax.broadcasted_iota(jnp.int32, (tm, 1), 0)
    zm1 = jnp.where(row >= 1, pltpu.roll(z, 1, 0), zh[7:8])
    zm2 = jnp.where(row >= 2, pltpu.roll(z, 2, 0), jnp.where(row == 1, zh[7:8], zh[6:7]))
    cw = cw_ref[...]
    a_out = gb_ref[...] * (cw[0:1] * zm2 + cw[1:2] * zm1 + cw[2:3] * z)
    l1, l2, l3 = l1_ref[...], l2_ref[...], l3_ref[...]
    lm = jnp.maximum(jnp.maximum(l1, l2), l3)
    e1, e2, e3 = jnp.exp(l1 - lm), jnp.exp(l2 - lm), jnp.exp(l3 - lm)
    o = (e1 * o1_ref[...].astype(F32) + e2 * o2_ref[...].astype(F32)
         + e3 * o3_ref[...].astype(F32)) / (e1 + e2 + e3)
    y = jnp.dot(a_out.astype(BF16), w_ref[:D_A, :], preferred_element_type=F32)
    y = y + jnp.dot(o.astype(BF16), w_ref[D_A:, :], preferred_element_type=F32)
    out_ref[...] = h_ref[...] + y


def _outproj_prompt(z, gb, cw, os_, ls_, w16, h, tm, seq):
    n = z.shape[0]
    row = lambda i: (i, 0)
    fix = lambda i: (0, 0)
    halo = lambda i: (jnp.maximum(i * (tm // 8) - 1, 0), 0)
    half = pl.BlockSpec((tm, D_A), row)
    return pl.pallas_call(
        functools.partial(_outproj_prompt_kernel, tiles_per_batch=seq // tm),
        grid=(n // tm,),
        in_specs=[half, pl.BlockSpec((8, D_A), halo), half, pl.BlockSpec((3, D_A), fix),
                  half, half, half, half, half, half,
                  pl.BlockSpec((D_MODEL, D_MODEL), fix), pl.BlockSpec((tm, D_MODEL), row)],
        out_specs=pl.BlockSpec((tm, D_MODEL), row),
        out_shape=jax.ShapeDtypeStruct((n, D_MODEL), F32),
        compiler_params=_params(1), name="outproj_prompt",
    )(z, z, gb, cw, *os_, *ls_, w16, h)


def _mix_sample_kernel(q_ref, kvn_ref, cache_ref, z_ref, gb_ref, st_ref, cw_ref, cat_ref,
                       *, lb):
    s_len = q_ref.shape[1]
    rows = H_B * s_len
    npad = LANES
    q = q_ref[0]
    r_i = lax.broadcasted_iota(jnp.int32, (rows, D_B), 0)
    c_i = lax.broadcasted_iota(jnp.int32, (rows, D_B), 1)
    qt = jnp.concatenate([q] * H_B, axis=0)
    qbd = jnp.where(r_i // s_len == c_i // HEAD_DIM, qt, 0.0).astype(BF16)
    kc = cache_ref[0, :, :D_B].astype(BF16)
    vc = cache_ref[0, :, D_B:].astype(BF16)
    kvn = jnp.concatenate([kvn_ref[0], jnp.zeros((npad - s_len, 2 * D_B), F32)], axis=0)
    kn = kvn[:, :D_B].astype(BF16)
    vn = kvn[:, D_B:].astype(BF16)
    nt = (((1,), (1,)), ((), ()))
    sc = lax.dot_general(qbd, kc, nt, preferred_element_type=F32)
    sn = lax.dot_general(qbd, kn, nt, preferred_element_type=F32)

    def multiplicity(delta):
        cnt = jnp.zeros(delta.shape, F32)
        for _, dil in BRANCHES:
            ok = (delta >= 0) & (delta <= BAND * dil) & ((delta & (dil - 1)) == 0)
            cnt = cnt + jnp.where(ok, 1.0, 0.0)
        return cnt

    tok_c = lax.broadcasted_iota(jnp.int32, (rows, lb), 0) % s_len
    cnt_c = multiplicity(lb + tok_c - lax.broadcasted_iota(jnp.int32, (rows, lb), 1))
    tok_n = lax.broadcasted_iota(jnp.int32, (rows, npad), 0) % s_len
    cnt_n = multiplicity(tok_n - lax.broadcasted_iota(jnp.int32, (rows, npad), 1))
    mx = jnp.maximum(
        jnp.max(jnp.where(cnt_c > 0, sc, -jnp.inf), axis=-1, keepdims=True),
        jnp.max(jnp.where(cnt_n > 0, sn, -jnp.inf), axis=-1, keepdims=True))
    pc = cnt_c * jnp.exp(jnp.where(cnt_c > 0, sc, -jnp.inf) - mx)
    pn = cnt_n * jnp.exp(jnp.where(cnt_n > 0, sn, -jnp.inf) - mx)
    den = jnp.sum(pc, axis=-1, keepdims=True) + jnp.sum(pn, axis=-1, keepdims=True)
    of = jnp.dot(pc.astype(BF16), vc, preferred_element_type=F32)
    of = (of + jnp.dot(pn.astype(BF16), vn, preferred_element_type=F32)) / den
    lane_head = lax.broadcasted_iota(jnp.int32, (s_len, D_B), 1) // HEAD_DIM
    o = jnp.zeros((s_len, D_B), F32)
    for h in range(H_B):
        o = o + jnp.where(lane_head == h, of[h * s_len:(h + 1) * s_len], 0.0)
    z = z_ref[0]
    st = st_ref[0]
    t = lax.broadcasted_iota(jnp.int32, (s_len, 1), 0)
    zm1 = jnp.where(t >= 1, pltpu.roll(z, 1, 0), st[1:2])
    zm2 = jnp.where(t >= 2, pltpu.roll(z, 2, 0), jnp.where(t == 1, st[1:2], st[0:1]))
    cw = cw_ref[...]
    cat_ref[0, :, :D_A] = gb_ref[0] * (cw[0:1] * zm2 + cw[1:2] * zm1 + cw[2:3] * z)
    cat_ref[0, :, D_A:] = o


def _mix_sample(q, kvn, cache, z, gb, st, cw):
    b, s_len, _ = q.shape
    lb = cache.shape[1]
    per = lambda i: (i, 0, 0)
    fix = lambda i: (0, 0)
    blk = lambda a: pl.BlockSpec((1,) + a.shape[1:], per)
    return pl.pallas_call(
        functools.partial(_mix_sample_kernel, lb=lb),
        grid=(b,),
        in_specs=[blk(q), blk(kvn), blk(cache), blk(z), blk(gb), blk(st),
                  pl.BlockSpec((3, D_A), fix)],
        out_specs=pl.BlockSpec((1, s_len, D_MODEL), per),
        out_shape=jax.ShapeDtypeStruct((b, s_len, D_MODEL), F32),
        compiler_params=_params(1), name="mix_sample",
    )(q, kvn, cache, z, gb, st, cw)


def _linear_res_kernel(x_ref, w_ref, h_ref, out_ref):
    out_ref[...] = h_ref[...] + jnp.dot(x_ref[...].astype(BF16), w_ref[...],
                                        preferred_element_type=F32)


def _linear_res(x, w16, h):
    n, k = x.shape
    full = lambda a: pl.BlockSpec(a.shape, lambda i: (0, 0))
    return pl.pallas_call(
        _linear_res_kernel, grid=(1,),
        in_specs=[full(x), full(w16), full(h)], out_specs=full(h),
        out_shape=jax.ShapeDtypeStruct(h.shape, F32),
        compiler_params=_params(1), name="linear_res",
    )(x, w16, h)


def _swiglu_step(xn, wg_ref, wu_ref, wd_ref, acc_sc):
    a = jnp.dot(xn, wg_ref[0], preferred_element_type=F32)
    b = jnp.dot(xn, wu_ref[0], preferred_element_type=F32)
    t = (jax.nn.silu(a) * b).astype(BF16)
    acc_sc[...] += jnp.dot(t, wd_ref[0], preferred_element_type=F32)


def _ffn_dense_kernel(x_ref, g_ref, wg_ref, wu_ref, wd_ref, out_ref, xn_sc, acc_sc):
    f = pl.program_id(1)

    @pl.when(f == 0)
    def _():
        xn_sc[...] = _rms_bf16(x_ref[...], g_ref[...])
        acc_sc[...] = jnp.zeros_like(acc_sc)

    _swiglu_step(xn_sc[...], wg_ref, wu_ref, wd_ref, acc_sc)

    @pl.when(f == pl.num_programs(1) - 1)
    def _():
        out_ref[...] = x_ref[...] + acc_sc[...]


def _ffn_dense(x, g, wg, wu, wd, *, tm, fc):
    n = x.shape[0]
    ff = wg.shape[2]
    row = lambda i, f: (i, 0)
    return pl.pallas_call(
        _ffn_dense_kernel, grid=(n // tm, ff // fc),
        in_specs=[pl.BlockSpec((tm, D_MODEL), row), pl.BlockSpec((1, D_MODEL), lambda i, f: (0, 0)),
                  pl.BlockSpec((1, D_MODEL, fc), lambda i, f: (0, 0, f)),
                  pl.BlockSpec((1, D_MODEL, fc), lambda i, f: (0, 0, f)),
                  pl.BlockSpec((1, fc, D_MODEL), lambda i, f: (0, f, 0))],
        out_specs=pl.BlockSpec((tm, D_MODEL), row),
        out_shape=jax.ShapeDtypeStruct((n, D_MODEL), F32),
        scratch_shapes=[pltpu.VMEM((tm, D_MODEL), BF16), pltpu.VMEM((tm, D_MODEL), F32)],
        compiler_params=_params(2), name="ffn_dense",
    )(x, g, wg, wu, wd)


def _pack_bf16_pairs(x):
    half = x.shape[1] // 2
    lo = lax.bitcast_convert_type(x[:, :half].astype(BF16).astype(F32), jnp.int32)
    hi = lax.bitcast_convert_type(x[:, half:].astype(BF16).astype(F32), jnp.int32)
    return (hi & -65536) | lax.shift_right_logical(lo, 16)


def _unpack_bf16_pairs(p):
    lo = lax.bitcast_convert_type(p << 16, F32)
    hi = lax.bitcast_convert_type(p & -65536, F32)
    return jnp.concatenate([lo, hi], axis=1)


def _ffn_experts_kernel(te_ref, tv_ref, x_ref, wg_ref, wu_ref, wd_ref, out_ref, xn_sc, acc_sc):
    del te_ref
    i, f = pl.program_id(0), pl.program_id(1)
    n_valid = tv_ref[i]

    @pl.when(n_valid > 0)
    def _():
        @pl.when(f == 0)
        def _():
            row = lax.broadcasted_iota(jnp.int32, x_ref.shape, 0)
            xn_sc[...] = _unpack_bf16_pairs(jnp.where(row < n_valid, x_ref[...], 0)).astype(BF16)
            acc_sc[...] = jnp.zeros_like(acc_sc)

        xn = xn_sc[...]
        a = jnp.dot(xn, wg_ref[0].astype(BF16), preferred_element_type=F32)
        b = jnp.dot(xn, wu_ref[0].astype(BF16), preferred_element_type=F32)
        t = (jax.nn.silu(a) * b).astype(BF16)
        acc_sc[...] += jnp.dot(t, wd_ref[0].astype(BF16), preferred_element_type=F32)

        @pl.when(f == pl.num_programs(1) - 1)
        def _():
            out_ref[...] = _pack_bf16_pairs(acc_sc[...])


def _ffn_experts(tile_expert, tile_valid, x, wg, wu, wd, *, tm, fc):
    n, wpk = x.shape
    ff = wg.shape[2]
    row = lambda i, f, te, tv: (i, 0)
    grid_spec = pltpu.PrefetchScalarGridSpec(
        num_scalar_prefetch=2, grid=(n // tm, ff // fc),
        in_specs=[pl.BlockSpec((tm, wpk), row),
                  pl.BlockSpec((1, D_MODEL, fc), lambda i, f, te, tv: (te[i], 0, f)),
                  pl.BlockSpec((1, D_MODEL, fc), lambda i, f, te, tv: (te[i], 0, f)),
                  pl.BlockSpec((1, fc, D_MODEL), lambda i, f, te, tv: (te[i], f, 0))],
        out_specs=pl.BlockSpec((tm, wpk), row),
        scratch_shapes=[pltpu.VMEM((tm, D_MODEL), BF16), pltpu.VMEM((tm, D_MODEL), F32)])
    return pl.pallas_call(
        _ffn_experts_kernel, grid_spec=grid_spec,
        out_shape=jax.ShapeDtypeStruct((n, wpk), jnp.int32),
        compiler_params=_params(2), name="ffn_experts",
    )(tile_expert, tile_valid, x, wg, wu, wd)


def _sc_worker_id():
    return lax.axis_index("s") * SC_CORES + lax.axis_index("c")


def _sc_dispatch(x_a, x_b, slots_a, slots_b, n_slots):
    w = x_a.shape[1]
    per_a = x_a.shape[0] // SC_WORKERS
    per_b = x_b.shape[0] // SC_WORKERS
    ch = min(SC_CHUNK, per_a)
    mesh = plsc.VectorSubcoreMesh(core_axis_name="c", subcore_axis_name="s")

    @functools.partial(
        pl.kernel, mesh=mesh, out_type=jax.ShapeDtypeStruct((n_slots, w), x_a.dtype),
        scratch_types=[pltpu.VMEM((ch,), jnp.int32), pltpu.VMEM((ch, w), x_a.dtype),
                       pltpu.VMEM((per_b,), jnp.int32), pltpu.VMEM((per_b, w), x_b.dtype)])
    def dispatch(xa_hbm, xb_hbm, a0_hbm, a1_hbm, b0_hbm, b1_hbm, out_hbm,
                 idx_v, rows_v, idxb_v, rowsb_v):
        wid = _sc_worker_id()

        @pl.loop(0, per_a // ch)
        def _(c):
            base = pl.multiple_of(wid * per_a + c * ch, 8)
            pltpu.sync_copy(xa_hbm.at[pl.ds(base, ch)], rows_v)
            for slots_hbm in (a0_hbm, a1_hbm):
                pltpu.sync_copy(slots_hbm.at[pl.ds(base, ch)], idx_v)
                pltpu.sync_copy(rows_v, out_hbm.at[idx_v])

        base_b = pl.multiple_of(wid * per_b, 8)
        pltpu.sync_copy(xb_hbm.at[pl.ds(base_b, per_b)], rowsb_v)
        for slots_hbm in (b0_hbm, b1_hbm):
            pltpu.sync_copy(slots_hbm.at[pl.ds(base_b, per_b)], idxb_v)
            pltpu.sync_copy(rowsb_v, out_hbm.at[idxb_v])

    return dispatch(x_a, x_b, *slots_a, *slots_b)


def _sc_gather(table, idx):
    _, w = table.shape
    b = idx.shape[0]
    per_w = b // SC_WORKERS
    ch = max(c for c in range(8, SC_CHUNK + 1, 8) if per_w % c == 0)
    assert per_w * SC_WORKERS == b
    mesh = plsc.VectorSubcoreMesh(core_axis_name="c", subcore_axis_name="s")

    @functools.partial(
        pl.kernel, mesh=mesh, out_type=jax.ShapeDtypeStruct((b, w), table.dtype),
        scratch_types=[pltpu.VMEM((ch,), jnp.int32), pltpu.VMEM((ch, w), table.dtype),
                       pltpu.SemaphoreType.DMA])
    def gather(table_hbm, idx_hbm, out_hbm, idx_v, rows_v, sem):
        wid = _sc_worker_id()

        @pl.loop(0, per_w // ch)
        def _(c):
            base = pl.multiple_of(wid * per_w + c * ch, 8)
            pltpu.sync_copy(idx_hbm.at[pl.ds(base, ch)], idx_v)
            pltpu.async_copy(table_hbm.at[idx_v], rows_v, sem).wait()
            pltpu.sync_copy(rows_v, out_hbm.at[pl.ds(base, ch)])

    return gather(table, idx)


def _combine_kernel(h_ref, sel_ref, y0_ref, y1_ref, out_ref):
    sel = sel_ref[...]
    out_ref[...] = (h_ref[...] + sel[:, 2:3] * _unpack_bf16_pairs(y0_ref[...])
                    + sel[:, 3:4] * _unpack_bf16_pairs(y1_ref[...]))


def _combine(h, sel, yb, blk_off, tm):
    n = h.shape[0]
    wpk = yb.shape[1]
    half_blocks = yb.shape[0] // 2 // tm
    return pl.pallas_call(
        _combine_kernel, grid=(n // tm,),
        in_specs=[pl.BlockSpec((tm, D_MODEL), lambda i: (i, 0)),
                  pl.BlockSpec((tm, LANES), lambda i: (i, 0)),
                  pl.BlockSpec((tm, wpk), lambda i: (i + blk_off, 0)),
                  pl.BlockSpec((tm, wpk), lambda i: (i + blk_off + half_blocks, 0))],
        out_specs=pl.BlockSpec((tm, D_MODEL), lambda i: (i, 0)),
        out_shape=jax.ShapeDtypeStruct((n, D_MODEL), F32),
        compiler_params=_params(1), name="combine",
    )(h, sel, yb, yb)


def _route_plan(sel, te_rows, n_tiles):
    e = sel[:, :2].astype(jnp.int32)
    n_tok = e.shape[0]
    experts = jnp.arange(N_EXPERTS, dtype=jnp.int32)
    onehot = (e.reshape(-1)[:, None] == experts[None, :]).astype(jnp.int32)
    csum = jnp.cumsum(onehot, axis=0)
    rank = jnp.sum((csum - onehot) * onehot, axis=1)
    counts = csum[-1]
    padded = (counts + te_rows - 1) // te_rows * te_rows
    ends = jnp.cumsum(padded)
    starts = ends - padded
    slots = (starts[e.reshape(-1)] + rank).reshape(n_tok, 2)
    tile_start = jnp.arange(n_tiles, dtype=jnp.int32) * te_rows
    tile_expert = jnp.minimum(jnp.sum(tile_start[:, None] >= ends[None, :], axis=1),
                              N_EXPERTS - 1).astype(jnp.int32)
    in_group = tile_start - starts[tile_expert]
    tile_valid = jnp.clip(counts[tile_expert] - in_group, 0, te_rows)
    tile_valid = jnp.where(tile_start < ends[-1], tile_valid, 0).astype(jnp.int32)
    return slots[:, 0], slots[:, 1], tile_expert, tile_valid


def _gmlp_kernel(h_ref, g_ref, wuv_ref, vg_ref, mix_ref, bs_ref, wo_ref, out_ref, v_ref, um_sc,
                 *, period):
    h = h_ref[...]
    tm = h.shape[0]
    ln = mix_ref.shape[1]
    gw = D_MODEL // C_GROUPS
    xn = _rms_bf16(h, g_ref[...])
    vv = jax.nn.gelu(jnp.dot(xn, wuv_ref[:, D_MODEL:], preferred_element_type=F32))
    v = vv * lax.rsqrt(jnp.mean(vv * vv, axis=-1, keepdims=True) + EPS) * vg_ref[...]
    v_ref[...] = v
    vb = v.astype(BF16)
    u = jax.nn.gelu(jnp.dot(xn, wuv_ref[:, :D_MODEL], preferred_element_type=F32))
    r = lax.broadcasted_iota(jnp.int32, (ln, ln), 0)
    c = lax.broadcasted_iota(jnp.int32, (ln, ln), 1)
    causal = (r // period == c // period) & (c % period <= r % period)
    for gi in range(C_GROUPS):
        wm = jnp.where(causal, mix_ref[gi], 0.0).astype(BF16)
        bias = bs_ref[:, gi:gi + 1]
        for ch in range(tm // ln):
            rs = slice(ch * ln, (ch + 1) * ln)
            cs = slice(gi * gw, (gi + 1) * gw)
            mixed = jnp.dot(wm, vb[rs, cs], preferred_element_type=F32) + bias
            um_sc[rs, cs] = (u[rs, cs] * mixed).astype(BF16)
    out_ref[...] = h + jnp.dot(um_sc[...], wo_ref[...], preferred_element_type=F32)


def _gmlp(h, g, wuv16, vg, mix, bs, wo16, *, tm, period):
    n = h.shape[0]
    row = lambda i: (i, 0)
    full = lambda a: pl.BlockSpec(a.shape, lambda i: (0,) * a.ndim)
    return pl.pallas_call(
        functools.partial(_gmlp_kernel, period=period),
        grid=(n // tm,),
        in_specs=[pl.BlockSpec((tm, D_MODEL), row), full(g), full(wuv16), full(vg), full(mix),
                  full(bs), full(wo16)],
        out_specs=[pl.BlockSpec((tm, D_MODEL), row), pl.BlockSpec((tm, D_MODEL), row)],
        out_shape=[jax.ShapeDtypeStruct((n, D_MODEL), F32),
                   jax.ShapeDtypeStruct((n, D_MODEL), F32)],
        scratch_shapes=[pltpu.VMEM((tm, D_MODEL), BF16)],
        compiler_params=_params(1), name="gmlp",
    )(h, g, wuv16, vg, mix, bs, wo16)


def _router_kernel(h_ref, g_ref, wr_ref, xn_ref, sel_ref):
    x = h_ref[...]
    xn = x * lax.rsqrt(jnp.mean(x * x, axis=-1, keepdims=True) + EPS) * g_ref[...]
    xn_ref[...] = _pack_bf16_pairs(xn)
    logits = jnp.dot(xn, wr_ref[...], preferred_element_type=F32,
                     precision=lax.Precision.HIGHEST)
    lane = lax.broadcasted_iota(jnp.int32, logits.shape, 1)
    lanef = lane.astype(F32)
    logits = jnp.where(lane < N_EXPERTS, logits, -jnp.inf)
    m1 = jnp.max(logits, axis=-1, keepdims=True)
    i1 = jnp.min(jnp.where(logits == m1, lanef, float(LANES)), axis=-1, keepdims=True)
    rest = jnp.where(lanef == i1, -jnp.inf, logits)
    m2 = jnp.max(rest, axis=-1, keepdims=True)
    i2 = jnp.min(jnp.where(rest == m2, lanef, float(LANES)), axis=-1, keepdims=True)
    e2 = jnp.exp(m2 - m1)
    w1 = 1.0 / (1.0 + e2)
    w2 = e2 / (1.0 + e2)
    sel_ref[...] = jnp.where(lane == 0, i1, jnp.where(lane == 1, i2,
                             jnp.where(lane == 2, w1, jnp.where(lane == 3, w2, 0.0))))


def _router(h, g, wr_pad, tm):
    n = h.shape[0]
    row = lambda i: (i, 0)
    fix = lambda i: (0, 0)
    return pl.pallas_call(
        _router_kernel, grid=(n // tm,),
        in_specs=[pl.BlockSpec((tm, D_MODEL), row), pl.BlockSpec((1, D_MODEL), fix),
                  pl.BlockSpec((D_MODEL, LANES), fix)],
        out_specs=[pl.BlockSpec((tm, D_MODEL // 2), row), pl.BlockSpec((tm, LANES), row)],
        out_shape=[jax.ShapeDtypeStruct((n, D_MODEL // 2), jnp.int32),
                   jax.ShapeDtypeStruct((n, LANES), F32)],
        compiler_params=_params(1), name="router",
    )(h, g, wr_pad)


def _rope_tables(pos):
    half = HEAD_DIM // 2
    inv = ROPE_THETA ** (-jnp.arange(half, dtype=F32) / half)
    ang = pos.astype(F32)[:, None] * inv[None, :]
    cos, sin = jnp.cos(ang), jnp.sin(ang)
    zero = jnp.zeros_like(sin)
    rep = lambda a, b: jnp.tile(jnp.concatenate([a, b], axis=-1), (1, LANES // HEAD_DIM))
    return rep(cos, cos), rep(-sin, zero), rep(zero, sin)


def _tile_for(n, pref):
    return pref if n % pref == 0 else n


def kernel(x_prompt, x_sample, state_conv, cache_kv_win, norm_mix_e, w_in_e, conv_w, q_gain,
           k_gain, w_out_e, norm_ffn_e, ffn_w_gate, ffn_w_up, ffn_w_down, norm_mix_o, w_uv,
           v_gain, w_s, b_s, w_out_o, norm_moe, w_router, moe_w_gate, moe_w_up, moe_w_down):
    bp, seq, _ = x_prompt.shape
    bs_, s_len, _ = x_sample.shape
    n_p, n_s = bp * seq, bs_ * s_len
    hp = x_prompt.reshape(n_p, D_MODEL)
    hs = x_sample.reshape(n_s, D_MODEL)
    tm_p = 512
    row = lambda a: a.reshape(1, -1)

    tabs_p = _rope_tables(jnp.arange(seq, dtype=jnp.int32))
    tabs_s = tuple(jnp.tile(t, (bs_, 1)) for t in
                   _rope_tables(PAST_LEN + jnp.arange(s_len, dtype=jnp.int32)))
    hd = jnp.arange(256) // HEAD_DIM
    bd = (hd[:, None] == hd[None, :]).astype(BF16)

    w_in16 = w_in_e[0].astype(BF16)
    w_out16 = w_out_e[0].astype(BF16)
    qg = jnp.tile(q_gain[0], 4).reshape(1, 256)
    kg = jnp.tile(k_gain[0], 4).reshape(1, 256)
    g0 = row(norm_mix_e[0])

    z_p, gb_p, q_p, k_p, v_p, kv_p = _inproj(hp, g0, w_in16, tabs_p, qg, kg, bd, tm_p)
    z_s, gb_s, q_s, k_s, v_s, kv_s = _inproj(hs, g0, w_in16, tabs_s, qg, kg, bd, n_s)

    os_, ls_ = [], []
    for _, dil in BRANCHES:
        m = seq // dil
        view = lambda a: a.reshape(bp, m, dil * D_B)
        o, lse = _attn_prompt(view(q_p), view(k_p), view(v_p), qb=min(256, m))
        os_.append(o.reshape(n_p, D_B))
        ls_.append(lse.reshape(n_p, D_B))
    hp = _outproj_prompt(z_p, gb_p, conv_w[0], os_, ls_, w_out16, hp, tm_p, seq)

    cat_s = _mix_sample(q_s.astype(F32).reshape(bs_, s_len, D_B),
                        kv_s.reshape(bs_, s_len, 2 * D_B),
                        cache_kv_win[0].reshape(bs_, -1, 2 * D_B),
                        z_s.reshape(bs_, s_len, D_A), gb_s.reshape(bs_, s_len, D_A),
                        state_conv[0], conv_w[0])
    hs = _linear_res(cat_s.reshape(n_s, D_MODEL), w_out16, hs)

    conv_prompt = z_p.reshape(bp, seq, D_A)[:, seq - 2:][None]
    conv_sample = z_s.reshape(bs_, s_len, D_A)[:, s_len - 2:][None]
    lw = min(2048, seq)
    kv_prompt = kv_p.reshape(bp, seq, 2, H_B, HEAD_DIM)[:, seq - lw:][None]
    kv_sample = kv_s.reshape(bs_, s_len, 2, H_B, HEAD_DIM)[None]

    fg, fu, fd = (ffn_w_gate.astype(BF16), ffn_w_up.astype(BF16), ffn_w_down.astype(BF16))
    gf = row(norm_ffn_e[0])
    hp = _ffn_dense(hp, gf, fg, fu, fd, tm=tm_p, fc=1408)
    hs = _ffn_dense(hs, gf, fg, fu, fd, tm=n_s, fc=1408)

    wuv16 = w_uv[0].astype(BF16)
    wo16 = w_out_o[0].astype(BF16)
    g1, vg = row(norm_mix_o[0]), row(v_gain[0])
    hp, vfull_p = _gmlp(hp, g1, wuv16, vg, w_s[0], b_s[0].T, wo16, tm=tm_p, period=CHUNK)
    mix_s = jnp.tile(w_s[0][:, :s_len, :s_len], (1, bs_, bs_))
    bias_s = jnp.tile(b_s[0][:, :s_len].T, (bs_, 1))
    hs, vfull_s = _gmlp(hs, g1, wuv16, vg, mix_s, bias_s, wo16, tm=n_s, period=s_len)
    tail = ((seq - 1) // CHUNK) * CHUNK
    v_prompt = vfull_p.reshape(bp, seq, D_MODEL)[:, tail:][None]
    v_sample = vfull_s.reshape(bs_, s_len, D_MODEL)[None]

    wr_pad = jnp.pad(w_router[0], ((0, 0), (0, LANES - N_EXPERTS)))
    gm = row(norm_moe[0])
    xn_p, sel_p = _router(hp, gm, wr_pad, tm_p)
    xn_s, sel_s = _router(hs, gm, wr_pad, n_s)
    n_tok = n_p + n_s
    n_tiles = -(-(2 * n_tok + N_EXPERTS * (MOE_ROWS - 1)) // MOE_ROWS)
    slot0, slot1, tile_expert, tile_valid = _route_plan(
        jnp.concatenate([sel_p, sel_s], axis=0), MOE_ROWS, n_tiles)
    xs = _sc_dispatch(xn_p, xn_s, (slot0[:n_p], slot1[:n_p]), (slot0[n_p:], slot1[n_p:]),
                      n_tiles * MOE_ROWS)
    ys = _ffn_experts(tile_expert, tile_valid, xs, moe_w_gate[0], moe_w_up[0], moe_w_down[0],
                      tm=MOE_ROWS, fc=512)
    yb = _sc_gather(ys, jnp.concatenate([slot0, slot1]))
    hp = _combine(hp, sel_p, yb, 0, n_s)
    hs = _combine(hs, sel_s, yb, n_p // n_s, n_s)

    return (hp.reshape(bp, seq, D_MODEL), hs.reshape(bs_, s_len, D_MODEL),
            conv_prompt, conv_sample, kv_prompt, kv_sample, v_prompt, v_sample)
```

```python
import functools

import jax
import jax.numpy as jnp
from jax import lax
from jax.experimental import pallas as pl
from jax.experimental.pallas import tpu as pltpu
from jax.experimental.pallas import tpu_sc as plsc

F32 = jnp.float32
BF16 = jnp.bfloat16

D_MODEL = 1024
D_A = 512
H_B = 8
HEAD_DIM = 64
D_B = H_B * HEAD_DIM
BRANCHES = ((128, 1), (512, 4), (2048, 16))
BAND = 128
ROPE_THETA = 10000.0
PAST_LEN = 16384
N_EXPERTS = 8
C_GROUPS = 4
CHUNK = 128
EPS = 1e-6
LANES = 128
VMEM_LIMIT = 56 * 1024 * 1024
SC_CORES = 2
SC_WORKERS = SC_CORES * 16
SC_CHUNK = 128
MOE_ROWS = 1024


def _params(n_axes):
    return pltpu.CompilerParams(dimension_semantics=("arbitrary",) * n_axes,
                                vmem_limit_bytes=VMEM_LIMIT)


def _rms_bf16(x, g):
    ms = jnp.mean(x * x, axis=-1, keepdims=True)
    return (x * lax.rsqrt(ms + EPS) * g).astype(BF16)


def _inproj_kernel(x_ref, g_ref, w_ref, cos_ref, sina_ref, sinb_ref, qg_ref, kg_ref, bd_ref,
                   z_ref, gb_ref, kv_ref, *rest, dils):
    if dils:
        planes, qkv_sc = rest[:-1], rest[-1]
    else:
        (q_out,) = rest
    xn = _rms_bf16(x_ref[...], g_ref[...])

    def col(j):
        return jnp.dot(xn, w_ref[:, j * 512:(j + 1) * 512], preferred_element_type=F32)

    z_ref[...] = col(1) * col(0)
    gb_ref[...] = col(2)
    cos, sina, sinb = cos_ref[...], sina_ref[...], sinb_ref[...]

    def head_norm_rope(t, gain):
        outs = []
        for c2 in range(2):
            tc = t[:, c2 * 256:(c2 + 1) * 256]
            ss = jnp.dot((tc * tc).astype(BF16), bd_ref[...], preferred_element_type=F32)
            tn = tc * lax.rsqrt(ss * (1.0 / HEAD_DIM) + EPS) * gain
            for c in range(2):
                u = tn[:, c * LANES:(c + 1) * LANES]
                outs.append(u * cos + pltpu.roll(u, LANES - 32, 1) * sina
                            + pltpu.roll(u, 32, 1) * sinb)
        return outs

    q = [t * (HEAD_DIM ** -0.5) for t in head_norm_rope(col(3), qg_ref[...])]
    k = head_norm_rope(col(4), kg_ref[...])
    v = col(5)
    for c in range(4):
        kv_ref[:, c * LANES:(c + 1) * LANES] = k[c]
    kv_ref[:, D_B:] = v
    if not dils:
        for c in range(4):
            q_out[:, c * LANES:(c + 1) * LANES] = q[c]
        return

    tm = x_ref.shape[0]
    for c in range(4):
        qkv_sc[0, c] = q[c]
        qkv_sc[1, c] = k[c]
        qkv_sc[2, c] = v[:, c * LANES:(c + 1) * LANES]
    for di, dil in enumerate(dils):
        for r in range(dil):
            rows = pl.ds(r, tm // dil, stride=dil) if dil > 1 else slice(None)
            for a in range(3):
                for c in range(4):
                    planes[3 * di + a][0, r, :, c * LANES:(c + 1) * LANES] = (
                        qkv_sc[a, c, rows, :].astype(BF16))


def _inproj(x, g, w16, tabs, qg, kg, bd, tm, seq, dils):
    n = x.shape[0]
    cos, sina, sinb = tabs
    tt = cos.shape[0] // tm
    tps = seq // tm
    row = lambda i: (i, 0)
    fix = lambda i: (0, 0)
    tab = lambda i: (i % tt, 0)
    out_specs = [pl.BlockSpec((tm, D_A), row), pl.BlockSpec((tm, D_A), row),
                 pl.BlockSpec((tm, 2 * D_B), row)]
    out_shape = [jax.ShapeDtypeStruct((n, D_A), F32), jax.ShapeDtypeStruct((n, D_A), F32),
                 jax.ShapeDtypeStruct((n, 2 * D_B), F32)]
    scratch = [pltpu.VMEM((3, D_B // LANES, tm, LANES), F32)]
    if not dils:
        out_specs.append(pl.BlockSpec((tm, D_B), row))
        out_shape.append(jax.ShapeDtypeStruct((n, D_B), F32))
        scratch = []
    for dil in dils:
        for _ in range(3):
            out_specs.append(pl.BlockSpec((1, dil, tm // dil, D_B),
                                          lambda i: (i // tps, 0, i % tps, 0)))
            out_shape.append(jax.ShapeDtypeStruct((n // seq, dil, seq // dil, D_B), BF16))
    return pl.pallas_call(
        functools.partial(_inproj_kernel, dils=dils),
        grid=(n // tm,),
        in_specs=[pl.BlockSpec((tm, D_MODEL), row), pl.BlockSpec((1, D_MODEL), fix),
                  pl.BlockSpec(w16.shape, fix),
                  pl.BlockSpec((tm, LANES), tab), pl.BlockSpec((tm, LANES), tab),
                  pl.BlockSpec((tm, LANES), tab),
                  pl.BlockSpec((1, 256), fix), pl.BlockSpec((1, 256), fix),
                  pl.BlockSpec((256, 256), fix)],
        out_specs=out_specs, out_shape=out_shape, scratch_shapes=scratch,
        compiler_params=_params(1), name="inproj",
    )(x, g, w16, cos, sina, sinb, qg, kg, bd)


def _attn_prompt_kernel(q_ref, kc_ref, kp_ref, vc_ref, vp_ref, o_ref, lse_ref, *, qb):
    j = pl.program_id(1)
    nsub = qb // BAND
    lane = lax.broadcasted_iota(jnp.int32, (1, LANES), 1)
    head0 = lane < HEAD_DIM
    qi = lax.broadcasted_iota(jnp.int32, (2 * BAND, 2 * BAND), 0) % BAND + BAND
    kj = lax.broadcasted_iota(jnp.int32, (2 * BAND, 2 * BAND), 1)
    dist = qi - kj
    band = (dist >= 0) & (dist <= BAND)
    first = band & ((kj >= BAND) | (j > 0))
    for hp in range(D_B // LANES):
        ls = slice(hp * LANES, (hp + 1) * LANES)
        kcat = jnp.concatenate([kp_ref[0, :, ls], kc_ref[0, :, ls]], axis=0)
        vcat = jnp.concatenate([vp_ref[0, :, ls], vc_ref[0, :, ls]], axis=0)
        for sub in range(nsub):
            qs = q_ref[0, sub * BAND:(sub + 1) * BAND, ls]
            ks = kcat[sub * BAND:(sub + 2) * BAND]
            vs = vcat[sub * BAND:(sub + 2) * BAND]
            q2 = jnp.concatenate([jnp.where(head0, qs, jnp.zeros_like(qs)),
                                  jnp.where(head0, jnp.zeros_like(qs), qs)], axis=0)
            s = lax.dot_general(q2, ks, (((1,), (1,)), ((), ())), preferred_element_type=F32)
            s = jnp.where(first if sub == 0 else band, s, -jnp.inf)
            mx = jnp.max(s, axis=-1, keepdims=True)
            p = jnp.exp(s - mx)
            l = jnp.sum(p, axis=-1, keepdims=True)
            pv = jnp.dot(p.astype(BF16), vs, preferred_element_type=F32) / l
            lse = mx + jnp.log(l)
            o_ref[0, sub * BAND:(sub + 1) * BAND, ls] = jnp.where(
                head0, pv[:BAND], pv[BAND:]).astype(BF16)
            lse_ref[0, sub * BAND:(sub + 1) * BAND, ls] = jnp.where(head0, lse[:BAND], lse[BAND:])


def _attn_prompt(q, k, v, qb):
    b, m, w = q.shape
    nsub = qb // BAND
    cur = lambda bi, j: (bi, j, 0)
    prev = lambda bi, j: (bi, jnp.maximum(j * nsub - 1, 0), 0)
    return pl.pallas_call(
        functools.partial(_attn_prompt_kernel, qb=qb),
        grid=(b, m // qb),
        in_specs=[pl.BlockSpec((1, qb, w), cur), pl.BlockSpec((1, qb, w), cur),
                  pl.BlockSpec((1, BAND, w), prev), pl.BlockSpec((1, qb, w), cur),
                  pl.BlockSpec((1, BAND, w), prev)],
        out_specs=[pl.BlockSpec((1, qb, w), cur), pl.BlockSpec((1, qb, w), cur)],
        out_shape=[jax.ShapeDtypeStruct((b, m, w), BF16), jax.ShapeDtypeStruct((b, m, w), F32)],
        compiler_params=_params(2), name="attn_prompt",
    )(q, k, k, v, v)


def _outproj_prompt_kernel(z_ref, zh_ref, gb_ref, cw_ref, *rest, tiles_per_batch, dils):
    nb = len(dils)
    o_refs, l_refs = rest[:nb], rest[nb:2 * nb]
    w_ref, h_ref, out_ref = rest[2 * nb:2 * nb + 3]
    o_scs, l_scs = rest[2 * nb + 3:3 * nb + 3], rest[3 * nb + 3:]
    i = pl.program_id(0)
    z = z_ref[...]
    tm = z.shape[0]
    for dil, o_ref, l_ref, o_sc, l_sc in zip(dils, o_refs, l_refs, o_scs, l_scs):
        for r in range(dil):
            rows = pl.ds(r, tm // dil, stride=dil) if dil > 1 else slice(None)
            for c in range(D_B // LANES):
                ls = slice(c * LANES, (c + 1) * LANES)
                o_sc[c, rows, :] = o_ref[0, r, :, ls].astype(F32)
                l_sc[c, rows, :] = l_ref[0, r, :, ls]
    unstage = lambda sc: jnp.concatenate([sc[c] for c in range(D_B // LANES)], axis=1)
    zh = jnp.where(i % tiles_per_batch == 0, 0.0, zh_ref[...])
    row = lax.broadcasted_iota(jnp.int32, (tm, 1), 0)
    zm1 = jnp.where(row >= 1, pltpu.roll(z, 1, 0), zh[7:8])
    zm2 = jnp.where(row >= 2, pltpu.roll(z, 2, 0), jnp.where(row == 1, zh[7:8], zh[6:7]))
    cw = cw_ref[...]
    a_out = gb_ref[...] * (cw[0:1] * zm2 + cw[1:2] * zm1 + cw[2:3] * z)
    lses = [unstage(l_sc) for l_sc in l_scs]
    lm = functools.reduce(jnp.maximum, lses)
    es = [jnp.exp(l - lm) for l in lses]
    o = sum(e * unstage(o_sc) for e, o_sc in zip(es, o_scs)) / sum(es)
    y = jnp.dot(a_out.astype(BF16), w_ref[:D_A, :], preferred_element_type=F32)
    y = y + jnp.dot(o.astype(BF16), w_ref[D_A:, :], preferred_element_type=F32)
    out_ref[...] = h_ref[...] + y


def _outproj_prompt(z, gb, cw, os_, ls_, w16, h, tm, seq, dils):
    n = z.shape[0]
    tps = seq // tm
    row = lambda i: (i, 0)
    fix = lambda i: (0, 0)
    halo = lambda i: (jnp.maximum(i * (tm // 8) - 1, 0), 0)
    half = pl.BlockSpec((tm, D_A), row)
    plane = [pl.BlockSpec((1, dil, tm // dil, D_B), lambda i: (i // tps, 0, i % tps, 0))
             for dil in dils]
    return pl.pallas_call(
        functools.partial(_outproj_prompt_kernel, tiles_per_batch=tps, dils=dils),
        grid=(n // tm,),
        in_specs=[half, pl.BlockSpec((8, D_A), halo), half, pl.BlockSpec((3, D_A), fix),
                  *plane, *plane,
                  pl.BlockSpec((D_MODEL, D_MODEL), fix), pl.BlockSpec((tm, D_MODEL), row)],
        out_specs=pl.BlockSpec((tm, D_MODEL), row),
        out_shape=jax.ShapeDtypeStruct((n, D_MODEL), F32),
        scratch_shapes=[pltpu.VMEM((D_B // LANES, tm, LANES), F32)] * (2 * len(dils)),
        compiler_params=_params(1), name="outproj_prompt",
    )(z, z, gb, cw, *os_, *ls_, w16, h)


def _mix_sample_kernel(q_ref, kvn_ref, cache_ref, z_ref, gb_ref, st_ref, cw_ref, cat_ref,
                       *, lb):
    s_len = q_ref.shape[1]
    rows = H_B * s_len
    npad = LANES
    q = q_ref[0]
    r_i = lax.broadcasted_iota(jnp.int32, (rows, D_B), 0)
    c_i = lax.broadcasted_iota(jnp.int32, (rows, D_B), 1)
    qt = jnp.concatenate([q] * H_B, axis=0)
    qbd = jnp.where(r_i // s_len == c_i // HEAD_DIM, qt, 0.0).astype(BF16)
    kc = cache_ref[0, :, :D_B].astype(BF16)
    vc = cache_ref[0, :, D_B:].astype(BF16)
    kvn = jnp.concatenate([kvn_ref[0], jnp.zeros((npad - s_len, 2 * D_B), F32)], axis=0)
    kn = kvn[:, :D_B].astype(BF16)
    vn = kvn[:, D_B:].astype(BF16)
    nt = (((1,), (1,)), ((), ()))
    sc = lax.dot_general(qbd, kc, nt, preferred_element_type=F32)
    sn = lax.dot_general(qbd, kn, nt, preferred_element_type=F32)

    def multiplicity(delta):
        cnt = jnp.zeros(delta.shape, F32)
        for _, dil in BRANCHES:
            ok = (delta >= 0) & (delta <= BAND * dil) & ((delta & (dil - 1)) == 0)
            cnt = cnt + jnp.where(ok, 1.0, 0.0)
        return cnt

    tok_c = lax.broadcasted_iota(jnp.int32, (rows, lb), 0) % s_len
    cnt_c = multiplicity(lb + tok_c - lax.broadcasted_iota(jnp.int32, (rows, lb), 1))
    tok_n = lax.broadcasted_iota(jnp.int32, (rows, npad), 0) % s_len
    cnt_n = multiplicity(tok_n - lax.broadcasted_iota(jnp.int32, (rows, npad), 1))
    mx = jnp.maximum(
        jnp.max(jnp.where(cnt_c > 0, sc, -jnp.inf), axis=-1, keepdims=True),
        jnp.max(jnp.where(cnt_n > 0, sn, -jnp.inf), axis=-1, keepdims=True))
    pc = cnt_c * jnp.exp(jnp.where(cnt_c > 0, sc, -jnp.inf) - mx)
    pn = cnt_n * jnp.exp(jnp.where(cnt_n > 0, sn, -jnp.inf) - mx)
    den = jnp.sum(pc, axis=-1, keepdims=True) + jnp.sum(pn, axis=-1, keepdims=True)
    of = jnp.dot(pc.astype(BF16), vc, preferred_element_type=F32)
    of = (of + jnp.dot(pn.astype(BF16), vn, preferred_element_type=F32)) / den
    lane_head = lax.broadcasted_iota(jnp.int32, (s_len, D_B), 1) // HEAD_DIM
    o = jnp.zeros((s_len, D_B), F32)
    for h in range(H_B):
        o = o + jnp.where(lane_head == h, of[h * s_len:(h + 1) * s_len], 0.0)
    z = z_ref[0]
    st = st_ref[0]
    t = lax.broadcasted_iota(jnp.int32, (s_len, 1), 0)
    zm1 = jnp.where(t >= 1, pltpu.roll(z, 1, 0), st[1:2])
    zm2 = jnp.where(t >= 2, pltpu.roll(z, 2, 0), jnp.where(t == 1, st[1:2], st[0:1]))
    cw = cw_ref[...]
    cat_ref[0, :, :D_A] = gb_ref[0] * (cw[0:1] * zm2 + cw[1:2] * zm1 + cw[2:3] * z)
    cat_ref[0, :, D_A:] = o


def _mix_sample(q, kvn, cache, z, gb, st, cw):
    b, s_len, _ = q.shape
    lb = cache.shape[1]
    per = lambda i: (i, 0, 0)
    fix = lambda i: (0, 0)
    blk = lambda a: pl.BlockSpec((1,) + a.shape[1:], per)
    return pl.pallas_call(
        functools.partial(_mix_sample_kernel, lb=lb),
        grid=(b,),
        in_specs=[blk(q), blk(kvn), blk(cache), blk(z), blk(gb), blk(st),
                  pl.BlockSpec((3, D_A), fix)],
        out_specs=pl.BlockSpec((1, s_len, D_MODEL), per),
        out_shape=jax.ShapeDtypeStruct((b, s_len, D_MODEL), F32),
        compiler_params=_params(1), name="mix_sample",
    )(q, kvn, cache, z, gb, st, cw)


def _linear_res_kernel(x_ref, w_ref, h_ref, out_ref):
    out_ref[...] = h_ref[...] + jnp.dot(x_ref[...].astype(BF16), w_ref[...],
                                        preferred_element_type=F32)


def _linear_res(x, w16, h):
    n, k = x.shape
    full = lambda a: pl.BlockSpec(a.shape, lambda i: (0, 0))
    return pl.pallas_call(
        _linear_res_kernel, grid=(1,),
        in_specs=[full(x), full(w16), full(h)], out_specs=full(h),
        out_shape=jax.ShapeDtypeStruct(h.shape, F32),
        compiler_params=_params(1), name="linear_res",
    )(x, w16, h)


def _swiglu_step(xn, wg_ref, wu_ref, wd_ref, acc_sc):
    a = jnp.dot(xn, wg_ref[0], preferred_element_type=F32)
    b = jnp.dot(xn, wu_ref[0], preferred_element_type=F32)
    t = (jax.nn.silu(a) * b).astype(BF16)
    acc_sc[...] += jnp.dot(t, wd_ref[0], preferred_element_type=F32)


def _ffn_dense_kernel(x_ref, g_ref, wg_ref, wu_ref, wd_ref, out_ref, xn_sc, acc_sc):
    f = pl.program_id(1)

    @pl.when(f == 0)
    def _():
        xn_sc[...] = _rms_bf16(x_ref[...], g_ref[...])
        acc_sc[...] = jnp.zeros_like(acc_sc)

    _swiglu_step(xn_sc[...], wg_ref, wu_ref, wd_ref, acc_sc)

    @pl.when(f == pl.num_programs(1) - 1)
    def _():
        out_ref[...] = x_ref[...] + acc_sc[...]


def _ffn_dense(x, g, wg, wu, wd, *, tm, fc):
    n = x.shape[0]
    ff = wg.shape[2]
    row = lambda i, f: (i, 0)
    return pl.pallas_call(
        _ffn_dense_kernel, grid=(n // tm, ff // fc),
        in_specs=[pl.BlockSpec((tm, D_MODEL), row), pl.BlockSpec((1, D_MODEL), lambda i, f: (0, 0)),
                  pl.BlockSpec((1, D_MODEL, fc), lambda i, f: (0, 0, f)),
                  pl.BlockSpec((1, D_MODEL, fc), lambda i, f: (0, 0, f)),
                  pl.BlockSpec((1, fc, D_MODEL), lambda i, f: (0, f, 0))],
        out_specs=pl.BlockSpec((tm, D_MODEL), row),
        out_shape=jax.ShapeDtypeStruct((n, D_MODEL), F32),
        scratch_shapes=[pltpu.VMEM((tm, D_MODEL), BF16), pltpu.VMEM((tm, D_MODEL), F32)],
        compiler_params=_params(2), name="ffn_dense",
    )(x, g, wg, wu, wd)


def _pack_bf16_pairs(x):
    half = x.shape[1] // 2
    lo = lax.bitcast_convert_type(x[:, :half].astype(BF16).astype(F32), jnp.int32)
    hi = lax.bitcast_convert_type(x[:, half:].astype(BF16).astype(F32), jnp.int32)
    return (hi & -65536) | lax.shift_right_logical(lo, 16)


def _unpack_bf16_pairs(p):
    lo = lax.bitcast_convert_type(p << 16, F32)
    hi = lax.bitcast_convert_type(p & -65536, F32)
    return jnp.concatenate([lo, hi], axis=1)


def _ffn_experts_kernel(te_ref, tv_ref, x_ref, wg_ref, wu_ref, wd_ref, out_ref, xn_sc, acc_sc):
    del te_ref
    i, f = pl.program_id(0), pl.program_id(1)
    n_valid = tv_ref[i]

    @pl.when(n_valid > 0)
    def _():
        @pl.when(f == 0)
        def _():
            row = lax.broadcasted_iota(jnp.int32, x_ref.shape, 0)
            xn_sc[...] = _unpack_bf16_pairs(jnp.where(row < n_valid, x_ref[...], 0)).astype(BF16)
            acc_sc[...] = jnp.zeros_like(acc_sc)

        xn = xn_sc[...]
        a = jnp.dot(xn, wg_ref[0].astype(BF16), preferred_element_type=F32)
        b = jnp.dot(xn, wu_ref[0].astype(BF16), preferred_element_type=F32)
        t = (jax.nn.silu(a) * b).astype(BF16)
        acc_sc[...] += jnp.dot(t, wd_ref[0].astype(BF16), preferred_element_type=F32)

        @pl.when(f == pl.num_programs(1) - 1)
        def _():
            out_ref[...] = _pack_bf16_pairs(acc_sc[...])


def _ffn_experts(tile_expert, tile_valid, x, wg, wu, wd, *, tm, fc):
    n, wpk = x.shape
    ff = wg.shape[2]
    row = lambda i, f, te, tv: (i, 0)
    grid_spec = pltpu.PrefetchScalarGridSpec(
        num_scalar_prefetch=2, grid=(n // tm, ff // fc),
        in_specs=[pl.BlockSpec((tm, wpk), row),
                  pl.BlockSpec((1, D_MODEL, fc), lambda i, f, te, tv: (te[i], 0, f)),
                  pl.BlockSpec((1, D_MODEL, fc), lambda i, f, te, tv: (te[i], 0, f)),
                  pl.BlockSpec((1, fc, D_MODEL), lambda i, f, te, tv: (te[i], f, 0))],
        out_specs=pl.BlockSpec((tm, wpk), row),
        scratch_shapes=[pltpu.VMEM((tm, D_MODEL), BF16), pltpu.VMEM((tm, D_MODEL), F32)])
    return pl.pallas_call(
        _ffn_experts_kernel, grid_spec=grid_spec,
        out_shape=jax.ShapeDtypeStruct((n, wpk), jnp.int32),
        compiler_params=_params(2), name="ffn_experts",
    )(tile_expert, tile_valid, x, wg, wu, wd)


def _sc_worker_id():
    return lax.axis_index("s") * SC_CORES + lax.axis_index("c")


def _sc_dispatch(x_a, x_b, slots_a, slots_b, n_slots):
    w = x_a.shape[1]
    per_a = x_a.shape[0] // SC_WORKERS
    per_b = x_b.shape[0] // SC_WORKERS
    ch = min(SC_CHUNK, per_a)
    mesh = plsc.VectorSubcoreMesh(core_axis_name="c", subcore_axis_name="s")

    @functools.partial(
        pl.kernel, mesh=mesh, out_type=jax.ShapeDtypeStruct((n_slots, w), x_a.dtype),
        scratch_types=[pltpu.VMEM((ch,), jnp.int32), pltpu.VMEM((ch, w), x_a.dtype),
                       pltpu.VMEM((per_b,), jnp.int32), pltpu.VMEM((per_b, w), x_b.dtype)])
    def dispatch(xa_hbm, xb_hbm, a0_hbm, a1_hbm, b0_hbm, b1_hbm, out_hbm,
                 idx_v, rows_v, idxb_v, rowsb_v):
        wid = _sc_worker_id()

        @pl.loop(0, per_a // ch)
        def _(c):
            base = pl.multiple_of(wid * per_a + c * ch, 8)
            pltpu.sync_copy(xa_hbm.at[pl.ds(base, ch)], rows_v)
            for slots_hbm in (a0_hbm, a1_hbm):
                pltpu.sync_copy(slots_hbm.at[pl.ds(base, ch)], idx_v)
                pltpu.sync_copy(rows_v, out_hbm.at[idx_v])

        base_b = pl.multiple_of(wid * per_b, 8)
        pltpu.sync_copy(xb_hbm.at[pl.ds(base_b, per_b)], rowsb_v)
        for slots_hbm in (b0_hbm, b1_hbm):
            pltpu.sync_copy(slots_hbm.at[pl.ds(base_b, per_b)], idxb_v)
            pltpu.sync_copy(rowsb_v, out_hbm.at[idxb_v])

    return dispatch(x_a, x_b, *slots_a, *slots_b)


def _sc_gather(table, idx):
    _, w = table.shape
    b = idx.shape[0]
    per_w = b // SC_WORKERS
    ch = max(c for c in range(8, SC_CHUNK + 1, 8) if per_w % c == 0)
    assert per_w * SC_WORKERS == b
    mesh = plsc.VectorSubcoreMesh(core_axis_name="c", subcore_axis_name="s")

    @functools.partial(
        pl.kernel, mesh=mesh, out_type=jax.ShapeDtypeStruct((b, w), table.dtype),
        scratch_types=[pltpu.VMEM((ch,), jnp.int32), pltpu.VMEM((ch, w), table.dtype),
                       pltpu.SemaphoreType.DMA])
    def gather(table_hbm, idx_hbm, out_hbm, idx_v, rows_v, sem):
        wid = _sc_worker_id()

        @pl.loop(0, per_w // ch)
        def _(c):
            base = pl.multiple_of(wid * per_w + c * ch, 8)
            pltpu.sync_copy(idx_hbm.at[pl.ds(base, ch)], idx_v)
            pltpu.async_copy(table_hbm.at[idx_v], rows_v, sem).wait()
            pltpu.sync_copy(rows_v, out_hbm.at[pl.ds(base, ch)])

    return gather(table, idx)


def _combine_kernel(h_ref, sel_ref, y0_ref, y1_ref, out_ref):
    sel = sel_ref[...]
    out_ref[...] = (h_ref[...] + sel[:, 2:3] * _unpack_bf16_pairs(y0_ref[...])
                    + sel[:, 3:4] * _unpack_bf16_pairs(y1_ref[...]))


def _combine(h, sel, yb, blk_off, tm):
    n = h.shape[0]
    wpk = yb.shape[1]
    half_blocks = yb.shape[0] // 2 // tm
    return pl.pallas_call(
        _combine_kernel, grid=(n // tm,),
        in_specs=[pl.BlockSpec((tm, D_MODEL), lambda i: (i, 0)),
                  pl.BlockSpec((tm, LANES), lambda i: (i, 0)),
                  pl.BlockSpec((tm, wpk), lambda i: (i + blk_off, 0)),
                  pl.BlockSpec((tm, wpk), lambda i: (i + blk_off + half_blocks, 0))],
        out_specs=pl.BlockSpec((tm, D_MODEL), lambda i: (i, 0)),
        out_shape=jax.ShapeDtypeStruct((n, D_MODEL), F32),
        compiler_params=_params(1), name="combine",
    )(h, sel, yb, yb)


def _route_plan(sel, te_rows, n_tiles):
    e = sel[:, :2].astype(jnp.int32)
    n_tok = e.shape[0]
    experts = jnp.arange(N_EXPERTS, dtype=jnp.int32)
    onehot = (e.reshape(-1)[:, None] == experts[None, :]).astype(jnp.int32)
    csum = jnp.cumsum(onehot, axis=0)
    rank = jnp.sum((csum - onehot) * onehot, axis=1)
    counts = csum[-1]
    padded = (counts + te_rows - 1) // te_rows * te_rows
    ends = jnp.cumsum(padded)
    starts = ends - padded
    slots = (starts[e.reshape(-1)] + rank).reshape(n_tok, 2)
    tile_start = jnp.arange(n_tiles, dtype=jnp.int32) * te_rows
    tile_expert = jnp.minimum(jnp.sum(tile_start[:, None] >= ends[None, :], axis=1),
                              N_EXPERTS - 1).astype(jnp.int32)
    in_group = tile_start - starts[tile_expert]
    tile_valid = jnp.clip(counts[tile_expert] - in_group, 0, te_rows)
    tile_valid = jnp.where(tile_start < ends[-1], tile_valid, 0).astype(jnp.int32)
    return slots[:, 0], slots[:, 1], tile_expert, tile_valid


def _gmlp_kernel(h_ref, g_ref, wuv_ref, vg_ref, mix_ref, bs_ref, wo_ref, out_ref, v_ref, um_sc,
                 *, period):
    h = h_ref[...]
    tm = h.shape[0]
    ln = mix_ref.shape[1]
    gw = D_MODEL // C_GROUPS
    xn = _rms_bf16(h, g_ref[...])
    vv = jax.nn.gelu(jnp.dot(xn, wuv_ref[:, D_MODEL:], preferred_element_type=F32))
    v = vv * lax.rsqrt(jnp.mean(vv * vv, axis=-1, keepdims=True) + EPS) * vg_ref[...]
    v_ref[...] = v
    vb = v.astype(BF16)
    u = jax.nn.gelu(jnp.dot(xn, wuv_ref[:, :D_MODEL], preferred_element_type=F32))
    r = lax.broadcasted_iota(jnp.int32, (ln, ln), 0)
    c = lax.broadcasted_iota(jnp.int32, (ln, ln), 1)
    causal = (r // period == c // period) & (c % period <= r % period)
    for gi in range(C_GROUPS):
        wm = jnp.where(causal, mix_ref[gi], 0.0).astype(BF16)
        bias = bs_ref[:, gi:gi + 1]
        for ch in range(tm // ln):
            rs = slice(ch * ln, (ch + 1) * ln)
            cs = slice(gi * gw, (gi + 1) * gw)
            mixed = jnp.dot(wm, vb[rs, cs], preferred_element_type=F32) + bias
            um_sc[rs, cs] = (u[rs, cs] * mixed).astype(BF16)
    out_ref[...] = h + jnp.dot(um_sc[...], wo_ref[...], preferred_element_type=F32)


def _gmlp(h, g, wuv16, vg, mix, bs, wo16, *, tm, period):
    n = h.shape[0]
    row = lambda i: (i, 0)
    full = lambda a: pl.BlockSpec(a.shape, lambda i: (0,) * a.ndim)
    return pl.pallas_call(
        functools.partial(_gmlp_kernel, period=period),
        grid=(n // tm,),
        in_specs=[pl.BlockSpec((tm, D_MODEL), row), full(g), full(wuv16), full(vg), full(mix),
                  full(bs), full(wo16)],
        out_specs=[pl.BlockSpec((tm, D_MODEL), row), pl.BlockSpec((tm, D_MODEL), row)],
        out_shape=[jax.ShapeDtypeStruct((n, D_MODEL), F32),
                   jax.ShapeDtypeStruct((n, D_MODEL), F32)],
        scratch_shapes=[pltpu.VMEM((tm, D_MODEL), BF16)],
        compiler_params=_params(1), name="gmlp",
    )(h, g, wuv16, vg, mix, bs, wo16)


def _router_kernel(h_ref, g_ref, wr_ref, xn_ref, sel_ref):
    x = h_ref[...]
    xn = x * lax.rsqrt(jnp.mean(x * x, axis=-1, keepdims=True) + EPS) * g_ref[...]
    xn_ref[...] = _pack_bf16_pairs(xn)
    logits = jnp.dot(xn, wr_ref[...], preferred_element_type=F32,
                     precision=lax.Precision.HIGHEST)
    lane = lax.broadcasted_iota(jnp.int32, logits.shape, 1)
    lanef = lane.astype(F32)
    logits = jnp.where(lane < N_EXPERTS, logits, -jnp.inf)
    m1 = jnp.max(logits, axis=-1, keepdims=True)
    i1 = jnp.min(jnp.where(logits == m1, lanef, float(LANES)), axis=-1, keepdims=True)
    rest = jnp.where(lanef == i1, -jnp.inf, logits)
    m2 = jnp.max(rest, axis=-1, keepdims=True)
    i2 = jnp.min(jnp.where(rest == m2, lanef, float(LANES)), axis=-1, keepdims=True)
    e2 = jnp.exp(m2 - m1)
    w1 = 1.0 / (1.0 + e2)
    w2 = e2 / (1.0 + e2)
    sel_ref[...] = jnp.where(lane == 0, i1, jnp.where(lane == 1, i2,
                             jnp.where(lane == 2, w1, jnp.where(lane == 3, w2, 0.0))))


def _router(h, g, wr_pad, tm):
    n = h.shape[0]
    row = lambda i: (i, 0)
    fix = lambda i: (0, 0)
    return pl.pallas_call(
        _router_kernel, grid=(n // tm,),
        in_specs=[pl.BlockSpec((tm, D_MODEL), row), pl.BlockSpec((1, D_MODEL), fix),
                  pl.BlockSpec((D_MODEL, LANES), fix)],
        out_specs=[pl.BlockSpec((tm, D_MODEL // 2), row), pl.BlockSpec((tm, LANES), row)],
        out_shape=[jax.ShapeDtypeStruct((n, D_MODEL // 2), jnp.int32),
                   jax.ShapeDtypeStruct((n, LANES), F32)],
        compiler_params=_params(1), name="router",
    )(h, g, wr_pad)


def _rope_tables(pos):
    half = HEAD_DIM // 2
    inv = ROPE_THETA ** (-jnp.arange(half, dtype=F32) / half)
    ang = pos.astype(F32)[:, None] * inv[None, :]
    cos, sin = jnp.cos(ang), jnp.sin(ang)
    zero = jnp.zeros_like(sin)
    rep = lambda a, b: jnp.tile(jnp.concatenate([a, b], axis=-1), (1, LANES // HEAD_DIM))
    return rep(cos, cos), rep(-sin, zero), rep(zero, sin)


def _tile_for(n, pref):
    return pref if n % pref == 0 else n


def kernel(x_prompt, x_sample, state_conv, cache_kv_win, norm_mix_e, w_in_e, conv_w, q_gain,
           k_gain, w_out_e, norm_ffn_e, ffn_w_gate, ffn_w_up, ffn_w_down, norm_mix_o, w_uv,
           v_gain, w_s, b_s, w_out_o, norm_moe, w_router, moe_w_gate, moe_w_up, moe_w_down):
    bp, seq, _ = x_prompt.shape
    bs_, s_len, _ = x_sample.shape
    n_p, n_s = bp * seq, bs_ * s_len
    hp = x_prompt.reshape(n_p, D_MODEL)
    hs = x_sample.reshape(n_s, D_MODEL)
    tm_p = 512
    row = lambda a: a.reshape(1, -1)

    tabs_p = _rope_tables(jnp.arange(seq, dtype=jnp.int32))
    tabs_s = tuple(jnp.tile(t, (bs_, 1)) for t in
                   _rope_tables(PAST_LEN + jnp.arange(s_len, dtype=jnp.int32)))
    hd = jnp.arange(256) // HEAD_DIM
    bd = (hd[:, None] == hd[None, :]).astype(BF16)

    w_in16 = w_in_e[0].astype(BF16)
    w_out16 = w_out_e[0].astype(BF16)
    qg = jnp.tile(q_gain[0], 4).reshape(1, 256)
    kg = jnp.tile(k_gain[0], 4).reshape(1, 256)
    g0 = row(norm_mix_e[0])

    dils = tuple(dil for _, dil in BRANCHES)
    z_p, gb_p, kv_p, *planes = _inproj(hp, g0, w_in16, tabs_p, qg, kg, bd, tm_p, seq, dils)
    z_s, gb_s, kv_s, q_s = _inproj(hs, g0, w_in16, tabs_s, qg, kg, bd, n_s, n_s, ())

    os_, ls_ = [], []
    for di, dil in enumerate(dils):
        m = seq // dil
        q_d, k_d, v_d = (a.reshape(bp * dil, m, D_B) for a in planes[3 * di:3 * di + 3])
        o, lse = _attn_prompt(q_d, k_d, v_d, qb=min(256, m))
        os_.append(o.reshape(bp, dil, m, D_B))
        ls_.append(lse.reshape(bp, dil, m, D_B))
    hp = _outproj_prompt(z_p, gb_p, conv_w[0], os_, ls_, w_out16, hp, tm_p, seq, dils)

    cat_s = _mix_sample(q_s.reshape(bs_, s_len, D_B),
                        kv_s.reshape(bs_, s_len, 2 * D_B),
                        cache_kv_win[0].reshape(bs_, -1, 2 * D_B),
                        z_s.reshape(bs_, s_len, D_A), gb_s.reshape(bs_, s_len, D_A),
                        state_conv[0], conv_w[0])
    hs = _linear_res(cat_s.reshape(n_s, D_MODEL), w_out16, hs)

    conv_prompt = z_p.reshape(bp, seq, D_A)[:, seq - 2:][None]
    conv_sample = z_s.reshape(bs_, s_len, D_A)[:, s_len - 2:][None]
    lw = min(2048, seq)
    kv_prompt = kv_p.reshape(bp, seq, 2, H_B, HEAD_DIM)[:, seq - lw:][None]
    kv_sample = kv_s.reshape(bs_, s_len, 2, H_B, HEAD_DIM)[None]

    fg, fu, fd = (ffn_w_gate.astype(BF16), ffn_w_up.astype(BF16), ffn_w_down.astype(BF16))
    gf = row(norm_ffn_e[0])
    hp = _ffn_dense(hp, gf, fg, fu, fd, tm=tm_p, fc=1408)
    hs = _ffn_dense(hs, gf, fg, fu, fd, tm=n_s, fc=1408)

    wuv16 = w_uv[0].astype(BF16)
    wo16 = w_out_o[0].astype(BF16)
    g1, vg = row(norm_mix_o[0]), row(v_gain[0])
    hp, vfull_p = _gmlp(hp, g1, wuv16, vg, w_s[0], b_s[0].T, wo16, tm=tm_p, period=CHUNK)
    mix_s = jnp.tile(w_s[0][:, :s_len, :s_len], (1, bs_, bs_))
    bias_s = jnp.tile(b_s[0][:, :s_len].T, (bs_, 1))
    hs, vfull_s = _gmlp(hs, g1, wuv16, vg, mix_s, bias_s, wo16, tm=n_s, period=s_len)
    tail = ((seq - 1) // CHUNK) * CHUNK
    v_prompt = vfull_p.reshape(bp, seq, D_MODEL)[:, tail:][None]
    v_sample = vfull_s.reshape(bs_, s_len, D_MODEL)[None]

    wr_pad = jnp.pad(w_router[0], ((0, 0), (0, LANES - N_EXPERTS)))
    gm = row(norm_moe[0])
    xn_p, sel_p = _router(hp, gm, wr_pad, tm_p)
    xn_s, sel_s = _router(hs, gm, wr_pad, n_s)
    n_tok = n_p + n_s
    n_tiles = -(-(2 * n_tok + N_EXPERTS * (MOE_ROWS - 1)) // MOE_ROWS)
    slot0, slot1, tile_expert, tile_valid = _route_plan(
        jnp.concatenate([sel_p, sel_s], axis=0), MOE_ROWS, n_tiles)
    xs = _sc_dispatch(xn_p, xn_s, (slot0[:n_p], slot1[:n_p]), (slot0[n_p:], slot1[n_p:]),
                      n_tiles * MOE_ROWS)
    ys = _ffn_experts(tile_expert, tile_valid, xs, moe_w_gate[0], moe_w_up[0], moe_w_down[0],
                      tm=MOE_ROWS, fc=512)
    yb = _sc_gather(ys, jnp.concatenate([slot0, slot1]))
    hp = _combine(hp, sel_p, yb, 0, n_s)
    hs = _combine(hs, sel_s, yb, n_p // n_s, n_s)

    return (hp.reshape(bp, seq, D_MODEL), hs.reshape(bs_, s_len, D_MODEL),
            conv_prompt, conv_sample, kv_prompt, kv_sample, v_prompt, v_sample)
```

```python
import functools

import jax
import jax.numpy as jnp
from jax import lax
from jax.experimental import pallas as pl
from jax.experimental.pallas import tpu as pltpu
from jax.experimental.pallas import tpu_sc as plsc

F32 = jnp.float32
BF16 = jnp.bfloat16

D_MODEL = 1024
D_A = 512
H_B = 8
HEAD_DIM = 64
D_B = H_B * HEAD_DIM
BRANCHES = ((128, 1), (512, 4), (2048, 16))
BAND = 128
ROPE_THETA = 10000.0
PAST_LEN = 16384
N_EXPERTS = 8
C_GROUPS = 4
CHUNK = 128
EPS = 1e-6
LANES = 128
VMEM_LIMIT = 56 * 1024 * 1024
SC_CORES = 2
SC_WORKERS = SC_CORES * 16
SC_CHUNK = 128
MOE_ROWS = 1024


def _params(n_axes):
    return pltpu.CompilerParams(dimension_semantics=("arbitrary",) * n_axes,
                                vmem_limit_bytes=VMEM_LIMIT)


def _rms_bf16(x, g):
    ms = jnp.mean(x * x, axis=-1, keepdims=True)
    return (x * lax.rsqrt(ms + EPS) * g).astype(BF16)


def _inproj_kernel(x_ref, g_ref, w_ref, cos_ref, sina_ref, sinb_ref, qg_ref, kg_ref, bd_ref,
                   z_ref, gb_ref, kv_ref, *rest, dils):
    if dils:
        planes, qkv_sc = rest[:-1], rest[-1]
    else:
        (q_out,) = rest
    xn = _rms_bf16(x_ref[...], g_ref[...])

    def col(j):
        return jnp.dot(xn, w_ref[:, j * 512:(j + 1) * 512], preferred_element_type=F32)

    z_ref[...] = col(1) * col(0)
    gb_ref[...] = col(2)
    cos, sina, sinb = cos_ref[...], sina_ref[...], sinb_ref[...]

    def head_norm_rope(t, gain):
        outs = []
        for c2 in range(2):
            tc = t[:, c2 * 256:(c2 + 1) * 256]
            ss = jnp.dot((tc * tc).astype(BF16), bd_ref[...], preferred_element_type=F32)
            tn = tc * lax.rsqrt(ss * (1.0 / HEAD_DIM) + EPS) * gain
            for c in range(2):
                u = tn[:, c * LANES:(c + 1) * LANES]
                outs.append(u * cos + pltpu.roll(u, LANES - 32, 1) * sina
                            + pltpu.roll(u, 32, 1) * sinb)
        return outs

    q = [t * (HEAD_DIM ** -0.5) for t in head_norm_rope(col(3), qg_ref[...])]
    k = head_norm_rope(col(4), kg_ref[...])
    v = col(5)
    for c in range(4):
        kv_ref[:, c * LANES:(c + 1) * LANES] = k[c]
    kv_ref[:, D_B:] = v
    if not dils:
        for c in range(4):
            q_out[:, c * LANES:(c + 1) * LANES] = q[c]
        return

    tm = x_ref.shape[0]
    for c in range(4):
        qkv_sc[0, c] = q[c]
        qkv_sc[1, c] = k[c]
        qkv_sc[2, c] = v[:, c * LANES:(c + 1) * LANES]
    for di, dil in enumerate(dils):
        for r in range(dil):
            rows = pl.ds(r, tm // dil, stride=dil) if dil > 1 else slice(None)
            for a in range(3):
                for c in range(4):
                    planes[3 * di + a][0, r, :, c * LANES:(c + 1) * LANES] = (
                        qkv_sc[a, c, rows, :].astype(BF16))


def _inproj(x, g, w16, tabs, qg, kg, bd, tm, seq, dils):
    n = x.shape[0]
    cos, sina, sinb = tabs
    tt = cos.shape[0] // tm
    tps = seq // tm
    row = lambda i: (i, 0)
    fix = lambda i: (0, 0)
    tab = lambda i: (i % tt, 0)
    out_specs = [pl.BlockSpec((tm, D_A), row), pl.BlockSpec((tm, D_A), row),
                 pl.BlockSpec((tm, 2 * D_B), row)]
    out_shape = [jax.ShapeDtypeStruct((n, D_A), F32), jax.ShapeDtypeStruct((n, D_A), F32),
                 jax.ShapeDtypeStruct((n, 2 * D_B), F32)]
    scratch = [pltpu.VMEM((3, D_B // LANES, tm, LANES), F32)]
    if not dils:
        out_specs.append(pl.BlockSpec((tm, D_B), row))
        out_shape.append(jax.ShapeDtypeStruct((n, D_B), F32))
        scratch = []
    for dil in dils:
        for _ in range(3):
            out_specs.append(pl.BlockSpec((1, dil, tm // dil, D_B),
                                          lambda i: (i // tps, 0, i % tps, 0)))
            out_shape.append(jax.ShapeDtypeStruct((n // seq, dil, seq // dil, D_B), BF16))
    return pl.pallas_call(
        functools.partial(_inproj_kernel, dils=dils),
        grid=(n // tm,),
        in_specs=[pl.BlockSpec((tm, D_MODEL), row), pl.BlockSpec((1, D_MODEL), fix),
                  pl.BlockSpec(w16.shape, fix),
                  pl.BlockSpec((tm, LANES), tab), pl.BlockSpec((tm, LANES), tab),
                  pl.BlockSpec((tm, LANES), tab),
                  pl.BlockSpec((1, 256), fix), pl.BlockSpec((1, 256), fix),
                  pl.BlockSpec((256, 256), fix)],
        out_specs=out_specs, out_shape=out_shape, scratch_shapes=scratch,
        compiler_params=_params(1), name="inproj",
    )(x, g, w16, cos, sina, sinb, qg, kg, bd)


def _attn_prompt_kernel(q_ref, kc_ref, kp_ref, vc_ref, vp_ref, o_ref, lse_ref, *, qb):
    j = pl.program_id(1)
    nsub = qb // BAND
    lane = lax.broadcasted_iota(jnp.int32, (1, LANES), 1)
    head0 = lane < HEAD_DIM
    qi = lax.broadcasted_iota(jnp.int32, (2 * BAND, 2 * BAND), 0) % BAND + BAND
    kj = lax.broadcasted_iota(jnp.int32, (2 * BAND, 2 * BAND), 1)
    dist = qi - kj
    band = (dist >= 0) & (dist <= BAND)
    first = band & ((kj >= BAND) | (j > 0))
    for hp in range(D_B // LANES):
        ls = slice(hp * LANES, (hp + 1) * LANES)
        kcat = jnp.concatenate([kp_ref[0, :, ls], kc_ref[0, :, ls]], axis=0)
        vcat = jnp.concatenate([vp_ref[0, :, ls], vc_ref[0, :, ls]], axis=0)
        for sub in range(nsub):
            qs = q_ref[0, sub * BAND:(sub + 1) * BAND, ls]
            ks = kcat[sub * BAND:(sub + 2) * BAND]
            vs = vcat[sub * BAND:(sub + 2) * BAND]
            q2 = jnp.concatenate([jnp.where(head0, qs, jnp.zeros_like(qs)),
                                  jnp.where(head0, jnp.zeros_like(qs), qs)], axis=0)
            s = lax.dot_general(q2, ks, (((1,), (1,)), ((), ())), preferred_element_type=F32)
            s = jnp.where(first if sub == 0 else band, s, -jnp.inf)
            mx = jnp.max(s, axis=-1, keepdims=True)
            p = jnp.exp(s - mx)
            l = jnp.sum(p, axis=-1, keepdims=True)
            pv = jnp.dot(p.astype(BF16), vs, preferred_element_type=F32) / l
            lse = mx + jnp.log(l)
            o_ref[0, sub * BAND:(sub + 1) * BAND, ls] = jnp.where(
                head0, pv[:BAND], pv[BAND:]).astype(BF16)
            lse_ref[0, sub * BAND:(sub + 1) * BAND, ls] = jnp.where(head0, lse[:BAND], lse[BAND:])


def _attn_prompt(q, k, v, qb):
    b, m, w = q.shape
    nsub = qb // BAND
    cur = lambda bi, j: (bi, j, 0)
    prev = lambda bi, j: (bi, jnp.maximum(j * nsub - 1, 0), 0)
    return pl.pallas_call(
        functools.partial(_attn_prompt_kernel, qb=qb),
        grid=(b, m // qb),
        in_specs=[pl.BlockSpec((1, qb, w), cur), pl.BlockSpec((1, qb, w), cur),
                  pl.BlockSpec((1, BAND, w), prev), pl.BlockSpec((1, qb, w), cur),
                  pl.BlockSpec((1, BAND, w), prev)],
        out_specs=[pl.BlockSpec((1, qb, w), cur), pl.BlockSpec((1, qb, w), cur)],
        out_shape=[jax.ShapeDtypeStruct((b, m, w), BF16), jax.ShapeDtypeStruct((b, m, w), F32)],
        compiler_params=_params(2), name="attn_prompt",
    )(q, k, k, v, v)


def _outproj_prompt_kernel(z_ref, zh_ref, gb_ref, cw_ref, *rest, tiles_per_batch, dils):
    nb = len(dils)
    o_refs, l_refs = rest[:nb], rest[nb:2 * nb]
    w_ref, h_ref, out_ref = rest[2 * nb:2 * nb + 3]
    o_scs, l_scs = rest[2 * nb + 3:3 * nb + 3], rest[3 * nb + 3:]
    i = pl.program_id(0)
    z = z_ref[...]
    tm = z.shape[0]
    for dil, o_ref, l_ref, o_sc, l_sc in zip(dils, o_refs, l_refs, o_scs, l_scs):
        for r in range(dil):
            rows = pl.ds(r, tm // dil, stride=dil) if dil > 1 else slice(None)
            for c in range(D_B // LANES):
                ls = slice(c * LANES, (c + 1) * LANES)
                o_sc[c, rows, :] = o_ref[0, r, :, ls].astype(F32)
                l_sc[c, rows, :] = l_ref[0, r, :, ls]
    unstage = lambda sc: jnp.concatenate([sc[c] for c in range(D_B // LANES)], axis=1)
    zh = jnp.where(i % tiles_per_batch == 0, 0.0, zh_ref[...])
    row = lax.broadcasted_iota(jnp.int32, (tm, 1), 0)
    zm1 = jnp.where(row >= 1, pltpu.roll(z, 1, 0), zh[7:8])
    zm2 = jnp.where(row >= 2, pltpu.roll(z, 2, 0), jnp.where(row == 1, zh[7:8], zh[6:7]))
    cw = cw_ref[...]
    a_out = gb_ref[...] * (cw[0:1] * zm2 + cw[1:2] * zm1 + cw[2:3] * z)
    lses = [unstage(l_sc) for l_sc in l_scs]
    lm = functools.reduce(jnp.maximum, lses)
    es = [jnp.exp(l - lm) for l in lses]
    o = sum(e * unstage(o_sc) for e, o_sc in zip(es, o_scs)) / sum(es)
    y = jnp.dot(a_out.astype(BF16), w_ref[:D_A, :], preferred_element_type=F32)
    y = y + jnp.dot(o.astype(BF16), w_ref[D_A:, :], preferred_element_type=F32)
    out_ref[...] = h_ref[...] + y


def _outproj_prompt(z, gb, cw, os_, ls_, w16, h, tm, seq, dils):
    n = z.shape[0]
    tps = seq // tm
    row = lambda i: (i, 0)
    fix = lambda i: (0, 0)
    halo = lambda i: (jnp.maximum(i * (tm // 8) - 1, 0), 0)
    half = pl.BlockSpec((tm, D_A), row)
    plane = [pl.BlockSpec((1, dil, tm // dil, D_B), lambda i: (i // tps, 0, i % tps, 0))
             for dil in dils]
    return pl.pallas_call(
        functools.partial(_outproj_prompt_kernel, tiles_per_batch=tps, dils=dils),
        grid=(n // tm,),
        in_specs=[half, pl.BlockSpec((8, D_A), halo), half, pl.BlockSpec((3, D_A), fix),
                  *plane, *plane,
                  pl.BlockSpec((D_MODEL, D_MODEL), fix), pl.BlockSpec((tm, D_MODEL), row)],
        out_specs=pl.BlockSpec((tm, D_MODEL), row),
        out_shape=jax.ShapeDtypeStruct((n, D_MODEL), F32),
        scratch_shapes=[pltpu.VMEM((D_B // LANES, tm, LANES), F32)] * (2 * len(dils)),
        compiler_params=_params(1), name="outproj_prompt",
    )(z, z, gb, cw, *os_, *ls_, w16, h)


def _mix_sample_kernel(q_ref, kvn_ref, ca_ref, cb_ref, z_ref, gb_ref, st_ref, cw_ref, cat_ref,
                       *, group, keep, tail):
    s_len = q_ref.shape[1]
    rows = H_B * s_len
    npad = LANES
    q = q_ref[0]
    r_i = lax.broadcasted_iota(jnp.int32, (rows, D_B), 0)
    c_i = lax.broadcasted_iota(jnp.int32, (rows, D_B), 1)
    qt = jnp.concatenate([q] * H_B, axis=0)
    qbd = jnp.where(r_i // s_len == c_i // HEAD_DIM, qt, 0.0).astype(BF16)
    kvn = jnp.concatenate([kvn_ref[0], jnp.zeros((npad - s_len, 2 * D_B), F32)], axis=0)

    def multiplicity(delta):
        cnt = jnp.zeros(delta.shape, F32)
        for _, dil in BRANCHES:
            ok = (delta >= 0) & (delta <= BAND * dil) & ((delta & (dil - 1)) == 0)
            cnt = cnt + jnp.where(ok, 1.0, 0.0)
        return cnt

    na, nb = ca_ref.shape[1], cb_ref.shape[1]
    ia = lax.broadcasted_iota(jnp.int32, (rows, na), 1)
    ib = lax.broadcasted_iota(jnp.int32, (rows, nb), 1)
    rest = group - keep
    segments = (
        (ca_ref[0], (ia // keep) * group + ia % keep - (na // keep) * group),
        (cb_ref[0], (ib // rest) * group + keep + ib % rest - tail),
        (kvn, lax.broadcasted_iota(jnp.int32, (rows, npad), 1)),
    )
    nt = (((1,), (1,)), ((), ()))
    scores, counts, values = [], [], []
    for kv, rel in segments:
        tok = lax.broadcasted_iota(jnp.int32, rel.shape, 0) % s_len
        cnt = multiplicity(tok - rel)
        s = lax.dot_general(qbd, kv[:, :D_B].astype(BF16), nt, preferred_element_type=F32)
        scores.append(jnp.where(cnt > 0, s, -jnp.inf))
        counts.append(cnt)
        values.append(kv[:, D_B:].astype(BF16))
    mx = functools.reduce(jnp.maximum, [jnp.max(s, axis=-1, keepdims=True) for s in scores])
    probs = [c * jnp.exp(s - mx) for c, s in zip(counts, scores)]
    den = sum(jnp.sum(p, axis=-1, keepdims=True) for p in probs)
    of = sum(jnp.dot(p.astype(BF16), v, preferred_element_type=F32)
             for p, v in zip(probs, values)) / den
    lane_head = lax.broadcasted_iota(jnp.int32, (s_len, D_B), 1) // HEAD_DIM
    o = jnp.zeros((s_len, D_B), F32)
    for h in range(H_B):
        o = o + jnp.where(lane_head == h, of[h * s_len:(h + 1) * s_len], 0.0)
    z = z_ref[0]
    st = st_ref[0]
    t = lax.broadcasted_iota(jnp.int32, (s_len, 1), 0)
    zm1 = jnp.where(t >= 1, pltpu.roll(z, 1, 0), st[1:2])
    zm2 = jnp.where(t >= 2, pltpu.roll(z, 2, 0), jnp.where(t == 1, st[1:2], st[0:1]))
    cw = cw_ref[...]
    cat_ref[0, :, :D_A] = gb_ref[0] * (cw[0:1] * zm2 + cw[1:2] * zm1 + cw[2:3] * z)
    cat_ref[0, :, D_A:] = o


def _mix_sample(q, kvn, cache, z, gb, st, cw):
    b, s_len, _ = q.shape
    lb = cache.shape[1]
    group = max(dil for _, dil in BRANCHES)
    tail = max(wnd for wnd, dil in BRANCHES if dil < group)
    assert lb % group == 0 and s_len <= group and tail % group == 0 and tail <= lb
    grouped = cache.reshape(b, lb // group, group, 2 * D_B)
    ca = grouped[:, :, :s_len].reshape(b, lb // group * s_len, 2 * D_B)
    cb = grouped[:, (lb - tail) // group:, s_len:].reshape(b, tail // group * (group - s_len),
                                                         2 * D_B)
    per = lambda i: (i, 0, 0)
    fix = lambda i: (0, 0)
    blk = lambda a: pl.BlockSpec((1,) + a.shape[1:], per)
    return pl.pallas_call(
        functools.partial(_mix_sample_kernel, group=group, keep=s_len, tail=tail),
        grid=(b,),
        in_specs=[blk(q), blk(kvn), blk(ca), blk(cb), blk(z), blk(gb), blk(st),
                  pl.BlockSpec((3, D_A), fix)],
        out_specs=pl.BlockSpec((1, s_len, D_MODEL), per),
        out_shape=jax.ShapeDtypeStruct((b, s_len, D_MODEL), F32),
        compiler_params=_params(1), name="mix_sample",
    )(q, kvn, ca, cb, z, gb, st, cw)


def _linear_res_kernel(x_ref, w_ref, h_ref, out_ref):
    out_ref[...] = h_ref[...] + jnp.dot(x_ref[...].astype(BF16), w_ref[...],
                                        preferred_element_type=F32)


def _linear_res(x, w16, h):
    n, k = x.shape
    full = lambda a: pl.BlockSpec(a.shape, lambda i: (0, 0))
    return pl.pallas_call(
        _linear_res_kernel, grid=(1,),
        in_specs=[full(x), full(w16), full(h)], out_specs=full(h),
        out_shape=jax.ShapeDtypeStruct(h.shape, F32),
        compiler_params=_params(1), name="linear_res",
    )(x, w16, h)


def _swiglu_step(xn, wg_ref, wu_ref, wd_ref, acc_sc):
    a = jnp.dot(xn, wg_ref[0], preferred_element_type=F32)
    b = jnp.dot(xn, wu_ref[0], preferred_element_type=F32)
    t = (jax.nn.silu(a) * b).astype(BF16)
    acc_sc[...] += jnp.dot(t, wd_ref[0], preferred_element_type=F32)


def _ffn_dense_kernel(x_ref, g_ref, wg_ref, wu_ref, wd_ref, out_ref, xn_sc, acc_sc):
    f = pl.program_id(1)

    @pl.when(f == 0)
    def _():
        xn_sc[...] = _rms_bf16(x_ref[...], g_ref[...])
        acc_sc[...] = jnp.zeros_like(acc_sc)

    _swiglu_step(xn_sc[...], wg_ref, wu_ref, wd_ref, acc_sc)

    @pl.when(f == pl.num_programs(1) - 1)
    def _():
        out_ref[...] = x_ref[...] + acc_sc[...]


def _ffn_dense(x, g, wg, wu, wd, *, tm, fc):
    n = x.shape[0]
    ff = wg.shape[2]
    row = lambda i, f: (i, 0)
    return pl.pallas_call(
        _ffn_dense_kernel, grid=(n // tm, ff // fc),
        in_specs=[pl.BlockSpec((tm, D_MODEL), row), pl.BlockSpec((1, D_MODEL), lambda i, f: (0, 0)),
                  pl.BlockSpec((1, D_MODEL, fc), lambda i, f: (0, 0, f)),
                  pl.BlockSpec((1, D_MODEL, fc), lambda i, f: (0, 0, f)),
                  pl.BlockSpec((1, fc, D_MODEL), lambda i, f: (0, f, 0))],
        out_specs=pl.BlockSpec((tm, D_MODEL), row),
        out_shape=jax.ShapeDtypeStruct((n, D_MODEL), F32),
        scratch_shapes=[pltpu.VMEM((tm, D_MODEL), BF16), pltpu.VMEM((tm, D_MODEL), F32)],
        compiler_params=_params(2), name="ffn_dense",
    )(x, g, wg, wu, wd)


def _pack_bf16_pairs(x):
    half = x.shape[1] // 2
    lo = lax.bitcast_convert_type(x[:, :half].astype(BF16).astype(F32), jnp.int32)
    hi = lax.bitcast_convert_type(x[:, half:].astype(BF16).astype(F32), jnp.int32)
    return (hi & -65536) | lax.shift_right_logical(lo, 16)


def _unpack_bf16_pairs(p):
    lo = lax.bitcast_convert_type(p << 16, F32)
    hi = lax.bitcast_convert_type(p & -65536, F32)
    return jnp.concatenate([lo, hi], axis=1)


def _ffn_experts_kernel(te_ref, tv_ref, x_ref, wg_ref, wu_ref, wd_ref, out_ref, xn_sc, acc_sc):
    del te_ref
    i, f = pl.program_id(0), pl.program_id(1)
    n_valid = tv_ref[i]

    @pl.when(n_valid > 0)
    def _():
        @pl.when(f == 0)
        def _():
            row = lax.broadcasted_iota(jnp.int32, x_ref.shape, 0)
            xn_sc[...] = _unpack_bf16_pairs(jnp.where(row < n_valid, x_ref[...], 0)).astype(BF16)
            acc_sc[...] = jnp.zeros_like(acc_sc)

        def step(rows):
            xn = xn_sc[:rows]
            a = jnp.dot(xn, wg_ref[0].astype(BF16), preferred_element_type=F32)
            b = jnp.dot(xn, wu_ref[0].astype(BF16), preferred_element_type=F32)
            t = (jax.nn.silu(a) * b).astype(BF16)
            acc_sc[:rows] += jnp.dot(t, wd_ref[0].astype(BF16), preferred_element_type=F32)

        tm = xn_sc.shape[0]
        pl.when(n_valid > tm // 2)(lambda: step(tm))
        pl.when(n_valid <= tm // 2)(lambda: step(tm // 2))

        @pl.when(f == pl.num_programs(1) - 1)
        def _():
            out_ref[...] = _pack_bf16_pairs(acc_sc[...])


def _ffn_experts(tile_expert, tile_valid, x, wg, wu, wd, *, tm, fc):
    n, wpk = x.shape
    ff = wg.shape[2]
    row = lambda i, f, te, tv: (i, 0)
    grid_spec = pltpu.PrefetchScalarGridSpec(
        num_scalar_prefetch=2, grid=(n // tm, ff // fc),
        in_specs=[pl.BlockSpec((tm, wpk), row),
                  pl.BlockSpec((1, D_MODEL, fc), lambda i, f, te, tv: (te[i], 0, f)),
                  pl.BlockSpec((1, D_MODEL, fc), lambda i, f, te, tv: (te[i], 0, f)),
                  pl.BlockSpec((1, fc, D_MODEL), lambda i, f, te, tv: (te[i], f, 0))],
        out_specs=pl.BlockSpec((tm, wpk), row),
        scratch_shapes=[pltpu.VMEM((tm, D_MODEL), BF16), pltpu.VMEM((tm, D_MODEL), F32)])
    return pl.pallas_call(
        _ffn_experts_kernel, grid_spec=grid_spec,
        out_shape=jax.ShapeDtypeStruct((n, wpk), jnp.int32),
        compiler_params=_params(2), name="ffn_experts",
    )(tile_expert, tile_valid, x, wg, wu, wd)


def _sc_worker_id():
    return lax.axis_index("s") * SC_CORES + lax.axis_index("c")


def _sc_dispatch(x_a, x_b, slots_a, slots_b, n_slots):
    w = x_a.shape[1]
    per_a = x_a.shape[0] // SC_WORKERS
    per_b = x_b.shape[0] // SC_WORKERS
    ch = min(SC_CHUNK, per_a)
    mesh = plsc.VectorSubcoreMesh(core_axis_name="c", subcore_axis_name="s")

    @functools.partial(
        pl.kernel, mesh=mesh, out_type=jax.ShapeDtypeStruct((n_slots, w), x_a.dtype),
        scratch_types=[pltpu.VMEM((ch,), jnp.int32), pltpu.VMEM((ch, w), x_a.dtype),
                       pltpu.VMEM((per_b,), jnp.int32), pltpu.VMEM((per_b, w), x_b.dtype)])
    def dispatch(xa_hbm, xb_hbm, a0_hbm, a1_hbm, b0_hbm, b1_hbm, out_hbm,
                 idx_v, rows_v, idxb_v, rowsb_v):
        wid = _sc_worker_id()

        @pl.loop(0, per_a // ch)
        def _(c):
            base = pl.multiple_of(wid * per_a + c * ch, 8)
            pltpu.sync_copy(xa_hbm.at[pl.ds(base, ch)], rows_v)
            for slots_hbm in (a0_hbm, a1_hbm):
                pltpu.sync_copy(slots_hbm.at[pl.ds(base, ch)], idx_v)
                pltpu.sync_copy(rows_v, out_hbm.at[idx_v])

        base_b = pl.multiple_of(wid * per_b, 8)
        pltpu.sync_copy(xb_hbm.at[pl.ds(base_b, per_b)], rowsb_v)
        for slots_hbm in (b0_hbm, b1_hbm):
            pltpu.sync_copy(slots_hbm.at[pl.ds(base_b, per_b)], idxb_v)
            pltpu.sync_copy(rowsb_v, out_hbm.at[idxb_v])

    return dispatch(x_a, x_b, *slots_a, *slots_b)


def _sc_gather(table, idx):
    _, w = table.shape
    b = idx.shape[0]
    per_w = b // SC_WORKERS
    ch = max(c for c in range(8, SC_CHUNK + 1, 8) if per_w % c == 0)
    assert per_w * SC_WORKERS == b
    mesh = plsc.VectorSubcoreMesh(core_axis_name="c", subcore_axis_name="s")

    @functools.partial(
        pl.kernel, mesh=mesh, out_type=jax.ShapeDtypeStruct((b, w), table.dtype),
        scratch_types=[pltpu.VMEM((ch,), jnp.int32), pltpu.VMEM((ch, w), table.dtype),
                       pltpu.SemaphoreType.DMA])
    def gather(table_hbm, idx_hbm, out_hbm, idx_v, rows_v, sem):
        wid = _sc_worker_id()

        @pl.loop(0, per_w // ch)
        def _(c):
            base = pl.multiple_of(wid * per_w + c * ch, 8)
            pltpu.sync_copy(idx_hbm.at[pl.ds(base, ch)], idx_v)
            pltpu.async_copy(table_hbm.at[idx_v], rows_v, sem).wait()
            pltpu.sync_copy(rows_v, out_hbm.at[pl.ds(base, ch)])

    return gather(table, idx)


def _combine_kernel(h_ref, sel_ref, y0_ref, y1_ref, out_ref):
    sel = sel_ref[...]
    out_ref[...] = (h_ref[...] + sel[:, 2:3] * _unpack_bf16_pairs(y0_ref[...])
                    + sel[:, 3:4] * _unpack_bf16_pairs(y1_ref[...]))


def _combine(h, sel, yb, blk_off, tm):
    n = h.shape[0]
    wpk = yb.shape[1]
    half_blocks = yb.shape[0] // 2 // tm
    return pl.pallas_call(
        _combine_kernel, grid=(n // tm,),
        in_specs=[pl.BlockSpec((tm, D_MODEL), lambda i: (i, 0)),
                  pl.BlockSpec((tm, LANES), lambda i: (i, 0)),
                  pl.BlockSpec((tm, wpk), lambda i: (i + blk_off, 0)),
                  pl.BlockSpec((tm, wpk), lambda i: (i + blk_off + half_blocks, 0))],
        out_specs=pl.BlockSpec((tm, D_MODEL), lambda i: (i, 0)),
        out_shape=jax.ShapeDtypeStruct((n, D_MODEL), F32),
        compiler_params=_params(1), name="combine",
    )(h, sel, yb, yb)


def _route_plan(sel, te_rows, n_tiles):
    e = sel[:, :2].astype(jnp.int32)
    n_tok = e.shape[0]
    experts = jnp.arange(N_EXPERTS, dtype=jnp.int32)
    onehot = (e.reshape(-1)[:, None] == experts[None, :]).astype(jnp.int32)
    csum = jnp.cumsum(onehot, axis=0)
    rank = jnp.sum((csum - onehot) * onehot, axis=1)
    counts = csum[-1]
    padded = (counts + te_rows - 1) // te_rows * te_rows
    ends = jnp.cumsum(padded)
    starts = ends - padded
    slots = (starts[e.reshape(-1)] + rank).reshape(n_tok, 2)
    tile_start = jnp.arange(n_tiles, dtype=jnp.int32) * te_rows
    tile_expert = jnp.minimum(jnp.sum(tile_start[:, None] >= ends[None, :], axis=1),
                              N_EXPERTS - 1).astype(jnp.int32)
    in_group = tile_start - starts[tile_expert]
    tile_valid = jnp.clip(counts[tile_expert] - in_group, 0, te_rows)
    tile_valid = jnp.where(tile_start < ends[-1], tile_valid, 0).astype(jnp.int32)
    return slots[:, 0], slots[:, 1], tile_expert, tile_valid


def _gmlp_kernel(h_ref, g_ref, wuv_ref, vg_ref, mix_ref, bs_ref, wo_ref, out_ref, v_ref, um_sc,
                 *, period):
    h = h_ref[...]
    tm = h.shape[0]
    ln = mix_ref.shape[1]
    gw = D_MODEL // C_GROUPS
    xn = _rms_bf16(h, g_ref[...])
    vv = jax.nn.gelu(jnp.dot(xn, wuv_ref[:, D_MODEL:], preferred_element_type=F32))
    v = vv * lax.rsqrt(jnp.mean(vv * vv, axis=-1, keepdims=True) + EPS) * vg_ref[...]
    v_ref[...] = v
    vb = v.astype(BF16)
    u = jax.nn.gelu(jnp.dot(xn, wuv_ref[:, :D_MODEL], preferred_element_type=F32))
    r = lax.broadcasted_iota(jnp.int32, (ln, ln), 0)
    c = lax.broadcasted_iota(jnp.int32, (ln, ln), 1)
    causal = (r // period == c // period) & (c % period <= r % period)
    for gi in range(C_GROUPS):
        wm = jnp.where(causal, mix_ref[gi], 0.0).astype(BF16)
        bias = bs_ref[:, gi:gi + 1]
        for ch in range(tm // ln):
            rs = slice(ch * ln, (ch + 1) * ln)
            cs = slice(gi * gw, (gi + 1) * gw)
            mixed = jnp.dot(wm, vb[rs, cs], preferred_element_type=F32) + bias
            um_sc[rs, cs] = (u[rs, cs] * mixed).astype(BF16)
    out_ref[...] = h + jnp.dot(um_sc[...], wo_ref[...], preferred_element_type=F32)


def _gmlp(h, g, wuv16, vg, mix, bs, wo16, *, tm, period):
    n = h.shape[0]
    row = lambda i: (i, 0)
    full = lambda a: pl.BlockSpec(a.shape, lambda i: (0,) * a.ndim)
    return pl.pallas_call(
        functools.partial(_gmlp_kernel, period=period),
        grid=(n // tm,),
        in_specs=[pl.BlockSpec((tm, D_MODEL), row), full(g), full(wuv16), full(vg), full(mix),
                  full(bs), full(wo16)],
        out_specs=[pl.BlockSpec((tm, D_MODEL), row), pl.BlockSpec((tm, D_MODEL), row)],
        out_shape=[jax.ShapeDtypeStruct((n, D_MODEL), F32),
                   jax.ShapeDtypeStruct((n, D_MODEL), F32)],
        scratch_shapes=[pltpu.VMEM((tm, D_MODEL), BF16)],
        compiler_params=_params(1), name="gmlp",
    )(h, g, wuv16, vg, mix, bs, wo16)


def _router_kernel(h_ref, g_ref, wr_ref, xn_ref, sel_ref):
    x = h_ref[...]
    xn = x * lax.rsqrt(jnp.mean(x * x, axis=-1, keepdims=True) + EPS) * g_ref[...]
    xn_ref[...] = _pack_bf16_pairs(xn)
    logits = jnp.dot(xn, wr_ref[...], preferred_element_type=F32,
                     precision=lax.Precision.HIGHEST)
    lane = lax.broadcasted_iota(jnp.int32, logits.shape, 1)
    lanef = lane.astype(F32)
    logits = jnp.where(lane < N_EXPERTS, logits, -jnp.inf)
    m1 = jnp.max(logits, axis=-1, keepdims=True)
    i1 = jnp.min(jnp.where(logits == m1, lanef, float(LANES)), axis=-1, keepdims=True)
    rest = jnp.where(lanef == i1, -jnp.inf, logits)
    m2 = jnp.max(rest, axis=-1, keepdims=True)
    i2 = jnp.min(jnp.where(rest == m2, lanef, float(LANES)), axis=-1, keepdims=True)
    e2 = jnp.exp(m2 - m1)
    w1 = 1.0 / (1.0 + e2)
    w2 = e2 / (1.0 + e2)
    sel_ref[...] = jnp.where(lane == 0, i1, jnp.where(lane == 1, i2,
                             jnp.where(lane == 2, w1, jnp.where(lane == 3, w2, 0.0))))


def _router(h, g, wr_pad, tm):
    n = h.shape[0]
    row = lambda i: (i, 0)
    fix = lambda i: (0, 0)
    return pl.pallas_call(
        _router_kernel, grid=(n // tm,),
        in_specs=[pl.BlockSpec((tm, D_MODEL), row), pl.BlockSpec((1, D_MODEL), fix),
                  pl.BlockSpec((D_MODEL, LANES), fix)],
        out_specs=[pl.BlockSpec((tm, D_MODEL // 2), row), pl.BlockSpec((tm, LANES), row)],
        out_shape=[jax.ShapeDtypeStruct((n, D_MODEL // 2), jnp.int32),
                   jax.ShapeDtypeStruct((n, LANES), F32)],
        compiler_params=_params(1), name="router",
    )(h, g, wr_pad)


def _rope_tables(pos):
    half = HEAD_DIM // 2
    inv = ROPE_THETA ** (-jnp.arange(half, dtype=F32) / half)
    ang = pos.astype(F32)[:, None] * inv[None, :]
    cos, sin = jnp.cos(ang), jnp.sin(ang)
    zero = jnp.zeros_like(sin)
    rep = lambda a, b: jnp.tile(jnp.concatenate([a, b], axis=-1), (1, LANES // HEAD_DIM))
    return rep(cos, cos), rep(-sin, zero), rep(zero, sin)


def _tile_for(n, pref):
    return pref if n % pref == 0 else n


def kernel(x_prompt, x_sample, state_conv, cache_kv_win, norm_mix_e, w_in_e, conv_w, q_gain,
           k_gain, w_out_e, norm_ffn_e, ffn_w_gate, ffn_w_up, ffn_w_down, norm_mix_o, w_uv,
           v_gain, w_s, b_s, w_out_o, norm_moe, w_router, moe_w_gate, moe_w_up, moe_w_down):
    bp, seq, _ = x_prompt.shape
    bs_, s_len, _ = x_sample.shape
    n_p, n_s = bp * seq, bs_ * s_len
    hp = x_prompt.reshape(n_p, D_MODEL)
    hs = x_sample.reshape(n_s, D_MODEL)
    tm_p = 512
    row = lambda a: a.reshape(1, -1)

    tabs_p = _rope_tables(jnp.arange(seq, dtype=jnp.int32))
    tabs_s = tuple(jnp.tile(t, (bs_, 1)) for t in
                   _rope_tables(PAST_LEN + jnp.arange(s_len, dtype=jnp.int32)))
    hd = jnp.arange(256) // HEAD_DIM
    bd = (hd[:, None] == hd[None, :]).astype(BF16)

    w_in16 = w_in_e[0].astype(BF16)
    w_out16 = w_out_e[0].astype(BF16)
    qg = jnp.tile(q_gain[0], 4).reshape(1, 256)
    kg = jnp.tile(k_gain[0], 4).reshape(1, 256)
    g0 = row(norm_mix_e[0])

    dils = tuple(dil for _, dil in BRANCHES)
    z_p, gb_p, kv_p, *planes = _inproj(hp, g0, w_in16, tabs_p, qg, kg, bd, tm_p, seq, dils)
    z_s, gb_s, kv_s, q_s = _inproj(hs, g0, w_in16, tabs_s, qg, kg, bd, n_s, n_s, ())

    os_, ls_ = [], []
    for di, dil in enumerate(dils):
        m = seq // dil
        q_d, k_d, v_d = (a.reshape(bp * dil, m, D_B) for a in planes[3 * di:3 * di + 3])
        o, lse = _attn_prompt(q_d, k_d, v_d, qb=min(256, m))
        os_.append(o.reshape(bp, dil, m, D_B))
        ls_.append(lse.reshape(bp, dil, m, D_B))
    hp = _outproj_prompt(z_p, gb_p, conv_w[0], os_, ls_, w_out16, hp, tm_p, seq, dils)

    cat_s = _mix_sample(q_s.reshape(bs_, s_len, D_B),
                        kv_s.reshape(bs_, s_len, 2 * D_B),
                        cache_kv_win[0],
                        z_s.reshape(bs_, s_len, D_A), gb_s.reshape(bs_, s_len, D_A),
                        state_conv[0], conv_w[0])
    hs = _linear_res(cat_s.reshape(n_s, D_MODEL), w_out16, hs)

    conv_prompt = z_p.reshape(bp, seq, D_A)[:, seq - 2:][None]
    conv_sample = z_s.reshape(bs_, s_len, D_A)[:, s_len - 2:][None]
    lw = min(2048, seq)
    kv_prompt = kv_p.reshape(bp, seq, 2, H_B, HEAD_DIM)[:, seq - lw:][None]
    kv_sample = kv_s.reshape(bs_, s_len, 2, H_B, HEAD_DIM)[None]

    fg, fu, fd = (ffn_w_gate.astype(BF16), ffn_w_up.astype(BF16), ffn_w_down.astype(BF16))
    gf = row(norm_ffn_e[0])
    hp = _ffn_dense(hp, gf, fg, fu, fd, tm=tm_p, fc=1408)
    hs = _ffn_dense(hs, gf, fg, fu, fd, tm=n_s, fc=1408)

    wuv16 = w_uv[0].astype(BF16)
    wo16 = w_out_o[0].astype(BF16)
    g1, vg = row(norm_mix_o[0]), row(v_gain[0])
    hp, vfull_p = _gmlp(hp, g1, wuv16, vg, w_s[0], b_s[0].T, wo16, tm=tm_p, period=CHUNK)
    mix_s = jnp.tile(w_s[0][:, :s_len, :s_len], (1, bs_, bs_))
    bias_s = jnp.tile(b_s[0][:, :s_len].T, (bs_, 1))
    hs, vfull_s = _gmlp(hs, g1, wuv16, vg, mix_s, bias_s, wo16, tm=n_s, period=s_len)
    tail = ((seq - 1) // CHUNK) * CHUNK
    v_prompt = vfull_p.reshape(bp, seq, D_MODEL)[:, tail:][None]
    v_sample = vfull_s.reshape(bs_, s_len, D_MODEL)[None]

    wr_pad = jnp.pad(w_router[0], ((0, 0), (0, LANES - N_EXPERTS)))
    gm = row(norm_moe[0])
    xn_p, sel_p = _router(hp, gm, wr_pad, tm_p)
    xn_s, sel_s = _router(hs, gm, wr_pad, n_s)
    n_tok = n_p + n_s
    n_tiles = -(-(2 * n_tok + N_EXPERTS * (MOE_ROWS - 1)) // MOE_ROWS)
    slot0, slot1, tile_expert, tile_valid = _route_plan(
        jnp.concatenate([sel_p, sel_s], axis=0), MOE_ROWS, n_tiles)
    xs = _sc_dispatch(xn_p, xn_s, (slot0[:n_p], slot1[:n_p]), (slot0[n_p:], slot1[n_p:]),
                      n_tiles * MOE_ROWS)
    ys = _ffn_experts(tile_expert, tile_valid, xs, moe_w_gate[0], moe_w_up[0], moe_w_down[0],
                      tm=MOE_ROWS, fc=896)
    yb = _sc_gather(ys, jnp.concatenate([slot0, slot1]))
    hp = _combine(hp, sel_p, yb, 0, n_s)
    hs = _combine(hs, sel_s, yb, n_p // n_s, n_s)

    return (hp.reshape(bp, seq, D_MODEL), hs.reshape(bs_, s_len, D_MODEL),
            conv_prompt, conv_sample, kv_prompt, kv_sample, v_prompt, v_sample)
```

```python
import functools

import jax
import jax.numpy as jnp
from jax import lax
from jax.experimental import pallas as pl
from jax.experimental.pallas import tpu as pltpu
from jax.experimental.pallas import tpu_sc as plsc

F32 = jnp.float32
BF16 = jnp.bfloat16

D_MODEL = 1024
D_A = 512
H_B = 8
HEAD_DIM = 64
D_B = H_B * HEAD_DIM
BRANCHES = ((128, 1), (512, 4), (2048, 16))
BAND = 128
ROPE_THETA = 10000.0
PAST_LEN = 16384
N_EXPERTS = 8
C_GROUPS = 4
CHUNK = 128
EPS = 1e-6
LANES = 128
VMEM_LIMIT = 56 * 1024 * 1024
SC_CORES = 2
SC_WORKERS = SC_CORES * 16
SC_CHUNK = 128
MOE_ROWS = 1024


def _params(n_axes):
    return pltpu.CompilerParams(dimension_semantics=("arbitrary",) * n_axes,
                                vmem_limit_bytes=VMEM_LIMIT)


def _rms_bf16(x, g):
    ms = jnp.mean(x * x, axis=-1, keepdims=True)
    return (x * lax.rsqrt(ms + EPS) * g).astype(BF16)


def _inproj_kernel(x_ref, g_ref, w_ref, cos_ref, sina_ref, sinb_ref, qg_ref, kg_ref, bd_ref,
                   z_ref, gb_ref, kv_ref, *rest, dils, win_tiles, tiles_per_seq):
    if dils:
        planes, qkv_sc = rest[:-1], rest[-1]
    else:
        (q_out,) = rest
    xn = _rms_bf16(x_ref[...], g_ref[...])

    def col(j):
        return jnp.dot(xn, w_ref[:, j * 512:(j + 1) * 512], preferred_element_type=F32)

    z_ref[...] = col(1) * col(0)
    gb_ref[...] = col(2)
    cos, sina, sinb = cos_ref[...], sina_ref[...], sinb_ref[...]

    def head_norm_rope(t, gain):
        outs = []
        for c2 in range(2):
            tc = t[:, c2 * 256:(c2 + 1) * 256]
            ss = jnp.dot((tc * tc).astype(BF16), bd_ref[...], preferred_element_type=F32)
            tn = tc * lax.rsqrt(ss * (1.0 / HEAD_DIM) + EPS) * gain
            for c in range(2):
                u = tn[:, c * LANES:(c + 1) * LANES]
                outs.append(u * cos + pltpu.roll(u, LANES - 32, 1) * sina
                            + pltpu.roll(u, 32, 1) * sinb)
        return outs

    q = [t * (HEAD_DIM ** -0.5) for t in head_norm_rope(col(3), qg_ref[...])]
    k = head_norm_rope(col(4), kg_ref[...])
    v = col(5)
    if not dils:
        for c in range(4):
            kv_ref[:, c * LANES:(c + 1) * LANES] = k[c]
            q_out[:, c * LANES:(c + 1) * LANES] = q[c]
        kv_ref[:, D_B:] = v
        return

    @pl.when(pl.program_id(0) % tiles_per_seq >= tiles_per_seq - win_tiles)
    def _():
        for c in range(4):
            kv_ref[0, c * LANES:(c + 1) * LANES, :] = k[c].T
            kv_ref[0, D_B + c * LANES:D_B + (c + 1) * LANES, :] = v[:, c * LANES:(c + 1) * LANES].T

    tm = x_ref.shape[0]
    for c in range(4):
        qkv_sc[0, c] = q[c]
        qkv_sc[1, c] = k[c]
        qkv_sc[2, c] = v[:, c * LANES:(c + 1) * LANES]
    for di, dil in enumerate(dils):
        for r in range(dil):
            rows = pl.ds(r, tm // dil, stride=dil) if dil > 1 else slice(None)
            for a in range(3):
                for c in range(4):
                    planes[3 * di + a][0, r, :, c * LANES:(c + 1) * LANES] = (
                        qkv_sc[a, c, rows, :].astype(BF16))


def _inproj(x, g, w16, tabs, qg, kg, bd, tm, seq, dils, win=0):
    n = x.shape[0]
    cos, sina, sinb = tabs
    tt = cos.shape[0] // tm
    tps = seq // tm
    win_tiles = win // tm
    row = lambda i: (i, 0)
    fix = lambda i: (0, 0)
    tab = lambda i: (i % tt, 0)
    out_specs = [pl.BlockSpec((tm, D_A), row), pl.BlockSpec((tm, D_A), row)]
    out_shape = [jax.ShapeDtypeStruct((n, D_A), F32), jax.ShapeDtypeStruct((n, D_A), F32)]
    if dils:
        out_specs.append(pl.BlockSpec(
            (1, 2 * D_B, tm),
            lambda i: (i // tps, 0, jnp.maximum(i % tps - (tps - win_tiles), 0))))
        out_shape.append(jax.ShapeDtypeStruct((n // seq, 2 * D_B, win), F32))
        scratch = [pltpu.VMEM((3, D_B // LANES, tm, LANES), F32)]
    else:
        out_specs += [pl.BlockSpec((tm, 2 * D_B), row), pl.BlockSpec((tm, D_B), row)]
        out_shape += [jax.ShapeDtypeStruct((n, 2 * D_B), F32),
                      jax.ShapeDtypeStruct((n, D_B), F32)]
        scratch = []
    for dil in dils:
        for _ in range(3):
            out_specs.append(pl.BlockSpec((1, dil, tm // dil, D_B),
                                          lambda i: (i // tps, 0, i % tps, 0)))
            out_shape.append(jax.ShapeDtypeStruct((n // seq, dil, seq // dil, D_B), BF16))
    return pl.pallas_call(
        functools.partial(_inproj_kernel, dils=dils, win_tiles=win_tiles, tiles_per_seq=tps),
        grid=(n // tm,),
        in_specs=[pl.BlockSpec((tm, D_MODEL), row), pl.BlockSpec((1, D_MODEL), fix),
                  pl.BlockSpec(w16.shape, fix),
                  pl.BlockSpec((tm, LANES), tab), pl.BlockSpec((tm, LANES), tab),
                  pl.BlockSpec((tm, LANES), tab),
                  pl.BlockSpec((1, 256), fix), pl.BlockSpec((1, 256), fix),
                  pl.BlockSpec((256, 256), fix)],
        out_specs=out_specs, out_shape=out_shape, scratch_shapes=scratch,
        compiler_params=_params(1), name="inproj",
    )(x, g, w16, cos, sina, sinb, qg, kg, bd)


def _attn_prompt_kernel(q_ref, kc_ref, kp_ref, vc_ref, vp_ref, o_ref, lse_ref, *, qb):
    j = pl.program_id(1)
    nsub = qb // BAND
    lane = lax.broadcasted_iota(jnp.int32, (1, LANES), 1)
    head0 = lane < HEAD_DIM
    qi = lax.broadcasted_iota(jnp.int32, (2 * BAND, 2 * BAND), 0) % BAND + BAND
    kj = lax.broadcasted_iota(jnp.int32, (2 * BAND, 2 * BAND), 1)
    dist = qi - kj
    band = (dist >= 0) & (dist <= BAND)
    first = band & ((kj >= BAND) | (j > 0))
    for hp in range(D_B // LANES):
        ls = slice(hp * LANES, (hp + 1) * LANES)
        kcat = jnp.concatenate([kp_ref[0, :, ls], kc_ref[0, :, ls]], axis=0)
        vcat = jnp.concatenate([vp_ref[0, :, ls], vc_ref[0, :, ls]], axis=0)
        for sub in range(nsub):
            qs = q_ref[0, sub * BAND:(sub + 1) * BAND, ls]
            ks = kcat[sub * BAND:(sub + 2) * BAND]
            vs = vcat[sub * BAND:(sub + 2) * BAND]
            q2 = jnp.concatenate([jnp.where(head0, qs, jnp.zeros_like(qs)),
                                  jnp.where(head0, jnp.zeros_like(qs), qs)], axis=0)
            s = lax.dot_general(q2, ks, (((1,), (1,)), ((), ())), preferred_element_type=F32)
            s = jnp.where(first if sub == 0 else band, s, -jnp.inf)
            mx = jnp.max(s, axis=-1, keepdims=True)
            p = jnp.exp(s - mx)
            l = jnp.sum(p, axis=-1, keepdims=True)
            pv = jnp.dot(p.astype(BF16), vs, preferred_element_type=F32) / l
            lse = mx + jnp.log(l)
            o_ref[0, sub * BAND:(sub + 1) * BAND, ls] = jnp.where(
                head0, pv[:BAND], pv[BAND:]).astype(BF16)
            lse_ref[0, sub * BAND:(sub + 1) * BAND, ls] = jnp.where(head0, lse[:BAND], lse[BAND:])


def _attn_prompt(q, k, v, qb):
    b, m, w = q.shape
    nsub = qb // BAND
    cur = lambda bi, j: (bi, j, 0)
    prev = lambda bi, j: (bi, jnp.maximum(j * nsub - 1, 0), 0)
    return pl.pallas_call(
        functools.partial(_attn_prompt_kernel, qb=qb),
        grid=(b, m // qb),
        in_specs=[pl.BlockSpec((1, qb, w), cur), pl.BlockSpec((1, qb, w), cur),
                  pl.BlockSpec((1, BAND, w), prev), pl.BlockSpec((1, qb, w), cur),
                  pl.BlockSpec((1, BAND, w), prev)],
        out_specs=[pl.BlockSpec((1, qb, w), cur), pl.BlockSpec((1, qb, w), cur)],
        out_shape=[jax.ShapeDtypeStruct((b, m, w), BF16), jax.ShapeDtypeStruct((b, m, w), F32)],
        compiler_params=_params(2), name="attn_prompt",
    )(q, k, k, v, v)


def _outproj_prompt_kernel(z_ref, zh_ref, gb_ref, cw_ref, *rest, tiles_per_batch, dils):
    nb = len(dils)
    o_refs, l_refs = rest[:nb], rest[nb:2 * nb]
    w_ref, h_ref, out_ref = rest[2 * nb:2 * nb + 3]
    o_scs, l_scs = rest[2 * nb + 3:3 * nb + 3], rest[3 * nb + 3:]
    i = pl.program_id(0)
    z = z_ref[...]
    tm = z.shape[0]
    for dil, o_ref, l_ref, o_sc, l_sc in zip(dils, o_refs, l_refs, o_scs, l_scs):
        for r in range(dil):
            rows = pl.ds(r, tm // dil, stride=dil) if dil > 1 else slice(None)
            for c in range(D_B // LANES):
                ls = slice(c * LANES, (c + 1) * LANES)
                o_sc[c, rows, :] = o_ref[0, r, :, ls].astype(F32)
                l_sc[c, rows, :] = l_ref[0, r, :, ls]
    unstage = lambda sc: jnp.concatenate([sc[c] for c in range(D_B // LANES)], axis=1)
    zh = jnp.where(i % tiles_per_batch == 0, 0.0, zh_ref[...])
    row = lax.broadcasted_iota(jnp.int32, (tm, 1), 0)
    zm1 = jnp.where(row >= 1, pltpu.roll(z, 1, 0), zh[7:8])
    zm2 = jnp.where(row >= 2, pltpu.roll(z, 2, 0), jnp.where(row == 1, zh[7:8], zh[6:7]))
    cw = cw_ref[...]
    a_out = gb_ref[...] * (cw[0:1] * zm2 + cw[1:2] * zm1 + cw[2:3] * z)
    lses = [unstage(l_sc) for l_sc in l_scs]
    lm = functools.reduce(jnp.maximum, lses)
    es = [jnp.exp(l - lm) for l in lses]
    o = sum(e * unstage(o_sc) for e, o_sc in zip(es, o_scs)) / sum(es)
    y = jnp.dot(a_out.astype(BF16), w_ref[:D_A, :], preferred_element_type=F32)
    y = y + jnp.dot(o.astype(BF16), w_ref[D_A:, :], preferred_element_type=F32)
    out_ref[...] = h_ref[...] + y


def _outproj_prompt(z, gb, cw, os_, ls_, w16, h, tm, seq, dils):
    n = z.shape[0]
    tps = seq // tm
    row = lambda i: (i, 0)
    fix = lambda i: (0, 0)
    halo = lambda i: (jnp.maximum(i * (tm // 8) - 1, 0), 0)
    half = pl.BlockSpec((tm, D_A), row)
    plane = [pl.BlockSpec((1, dil, tm // dil, D_B), lambda i: (i // tps, 0, i % tps, 0))
             for dil in dils]
    return pl.pallas_call(
        functools.partial(_outproj_prompt_kernel, tiles_per_batch=tps, dils=dils),
        grid=(n // tm,),
        in_specs=[half, pl.BlockSpec((8, D_A), halo), half, pl.BlockSpec((3, D_A), fix),
                  *plane, *plane,
                  pl.BlockSpec((D_MODEL, D_MODEL), fix), pl.BlockSpec((tm, D_MODEL), row)],
        out_specs=pl.BlockSpec((tm, D_MODEL), row),
        out_shape=jax.ShapeDtypeStruct((n, D_MODEL), F32),
        scratch_shapes=[pltpu.VMEM((D_B // LANES, tm, LANES), F32)] * (2 * len(dils)),
        compiler_params=_params(1), name="outproj_prompt",
    )(z, z, gb, cw, *os_, *ls_, w16, h)


def _mix_sample_kernel(q_ref, kvn_ref, win_ref, z_ref, gb_ref, st_ref, cw_ref, cat_ref):
    s_len = q_ref.shape[1]
    rows = H_B * s_len
    npad = LANES
    q = q_ref[0]
    r_i = lax.broadcasted_iota(jnp.int32, (rows, D_B), 0)
    c_i = lax.broadcasted_iota(jnp.int32, (rows, D_B), 1)
    qt = jnp.concatenate([q] * H_B, axis=0)
    qbd = jnp.where(r_i // s_len == c_i // HEAD_DIM, qt, 0.0).astype(BF16)
    kvn = jnp.concatenate([kvn_ref[0], jnp.zeros((npad - s_len, 2 * D_B), F32)], axis=0)

    def multiplicity(delta):
        cnt = jnp.zeros(delta.shape, F32)
        for _, dil in BRANCHES:
            ok = (delta >= 0) & (delta <= BAND * dil) & ((delta & (dil - 1)) == 0)
            cnt = cnt + jnp.where(ok, 1.0, 0.0)
        return cnt

    lb = win_ref.shape[3]
    nt = (((1,), (1,)), ((), ()))
    tok_c = lax.broadcasted_iota(jnp.int32, (rows, lb), 0) % s_len
    cnt_c = multiplicity(tok_c + lb - lax.broadcasted_iota(jnp.int32, (rows, lb), 1))
    tok_n = lax.broadcasted_iota(jnp.int32, (rows, npad), 0) % s_len
    cnt_n = multiplicity(tok_n - lax.broadcasted_iota(jnp.int32, (rows, npad), 1))
    sc = jnp.dot(qbd, win_ref[0, 0].astype(BF16), preferred_element_type=F32)
    sn = lax.dot_general(qbd, kvn[:, :D_B].astype(BF16), nt, preferred_element_type=F32)
    sc = jnp.where(cnt_c > 0, sc, -jnp.inf)
    sn = jnp.where(cnt_n > 0, sn, -jnp.inf)
    mx = jnp.maximum(jnp.max(sc, axis=-1, keepdims=True), jnp.max(sn, axis=-1, keepdims=True))
    pc = cnt_c * jnp.exp(sc - mx)
    pn = cnt_n * jnp.exp(sn - mx)
    den = jnp.sum(pc, axis=-1, keepdims=True) + jnp.sum(pn, axis=-1, keepdims=True)
    of = lax.dot_general(pc.astype(BF16), win_ref[0, 1].astype(BF16), nt,
                         preferred_element_type=F32)
    of = (of + jnp.dot(pn.astype(BF16), kvn[:, D_B:].astype(BF16),
                       preferred_element_type=F32)) / den
    lane_head = lax.broadcasted_iota(jnp.int32, (s_len, D_B), 1) // HEAD_DIM
    o = jnp.zeros((s_len, D_B), F32)
    for h in range(H_B):
        o = o + jnp.where(lane_head == h, of[h * s_len:(h + 1) * s_len], 0.0)
    z = z_ref[0]
    st = st_ref[0]
    t = lax.broadcasted_iota(jnp.int32, (s_len, 1), 0)
    zm1 = jnp.where(t >= 1, pltpu.roll(z, 1, 0), st[1:2])
    zm2 = jnp.where(t >= 2, pltpu.roll(z, 2, 0), jnp.where(t == 1, st[1:2], st[0:1]))
    cw = cw_ref[...]
    cat_ref[0, :, :D_A] = gb_ref[0] * (cw[0:1] * zm2 + cw[1:2] * zm1 + cw[2:3] * z)
    cat_ref[0, :, D_A:] = o


def _mix_sample(q, kvn, win, z, gb, st, cw):
    b, s_len, _ = q.shape
    per = lambda i: (i,) + (0,) * 2
    blk = lambda a: pl.BlockSpec((1,) + a.shape[1:], per)
    return pl.pallas_call(
        _mix_sample_kernel,
        grid=(b,),
        in_specs=[blk(q), blk(kvn),
                  pl.BlockSpec((1,) + win.shape[1:], lambda i: (i, 0, 0, 0)),
                  blk(z), blk(gb), blk(st), pl.BlockSpec((3, D_A), lambda i: (0, 0))],
        out_specs=pl.BlockSpec((1, s_len, D_MODEL), per),
        out_shape=jax.ShapeDtypeStruct((b, s_len, D_MODEL), F32),
        compiler_params=_params(1), name="mix_sample",
    )(q, kvn, win, z, gb, st, cw)


def _linear_res_kernel(x_ref, w_ref, h_ref, out_ref):
    out_ref[...] = h_ref[...] + jnp.dot(x_ref[...].astype(BF16), w_ref[...],
                                        preferred_element_type=F32)


def _linear_res(x, w16, h):
    n, k = x.shape
    full = lambda a: pl.BlockSpec(a.shape, lambda i: (0, 0))
    return pl.pallas_call(
        _linear_res_kernel, grid=(1,),
        in_specs=[full(x), full(w16), full(h)], out_specs=full(h),
        out_shape=jax.ShapeDtypeStruct(h.shape, F32),
        compiler_params=_params(1), name="linear_res",
    )(x, w16, h)


def _swiglu_step(xn, wg_ref, wu_ref, wd_ref, acc_sc):
    a = jnp.dot(xn, wg_ref[0], preferred_element_type=F32)
    b = jnp.dot(xn, wu_ref[0], preferred_element_type=F32)
    t = (jax.nn.silu(a) * b).astype(BF16)
    acc_sc[...] += jnp.dot(t, wd_ref[0], preferred_element_type=F32)


def _ffn_dense_kernel(x_ref, g_ref, wg_ref, wu_ref, wd_ref, out_ref, xn_sc, acc_sc):
    f = pl.program_id(1)

    @pl.when(f == 0)
    def _():
        xn_sc[...] = _rms_bf16(x_ref[...], g_ref[...])
        acc_sc[...] = jnp.zeros_like(acc_sc)

    _swiglu_step(xn_sc[...], wg_ref, wu_ref, wd_ref, acc_sc)

    @pl.when(f == pl.num_programs(1) - 1)
    def _():
        out_ref[...] = x_ref[...] + acc_sc[...]


def _ffn_dense(x, g, wg, wu, wd, *, tm, fc):
    n = x.shape[0]
    ff = wg.shape[2]
    row = lambda i, f: (i, 0)
    return pl.pallas_call(
        _ffn_dense_kernel, grid=(n // tm, ff // fc),
        in_specs=[pl.BlockSpec((tm, D_MODEL), row), pl.BlockSpec((1, D_MODEL), lambda i, f: (0, 0)),
                  pl.BlockSpec((1, D_MODEL, fc), lambda i, f: (0, 0, f)),
                  pl.BlockSpec((1, D_MODEL, fc), lambda i, f: (0, 0, f)),
                  pl.BlockSpec((1, fc, D_MODEL), lambda i, f: (0, f, 0))],
        out_specs=pl.BlockSpec((tm, D_MODEL), row),
        out_shape=jax.ShapeDtypeStruct((n, D_MODEL), F32),
        scratch_shapes=[pltpu.VMEM((tm, D_MODEL), BF16), pltpu.VMEM((tm, D_MODEL), F32)],
        compiler_params=_params(2), name="ffn_dense",
    )(x, g, wg, wu, wd)


def _pack_bf16_pairs(x):
    half = x.shape[1] // 2
    lo = lax.bitcast_convert_type(x[:, :half].astype(BF16).astype(F32), jnp.int32)
    hi = lax.bitcast_convert_type(x[:, half:].astype(BF16).astype(F32), jnp.int32)
    return (hi & -65536) | lax.shift_right_logical(lo, 16)


def _unpack_bf16_pairs(p):
    lo = lax.bitcast_convert_type(p << 16, F32)
    hi = lax.bitcast_convert_type(p & -65536, F32)
    return jnp.concatenate([lo, hi], axis=1)


def _ffn_experts_kernel(te_ref, tv_ref, x_ref, wg_ref, wu_ref, wd_ref, out_ref, xn_sc, acc_sc):
    del te_ref
    i, f = pl.program_id(0), pl.program_id(1)
    n_valid = tv_ref[i]

    @pl.when(n_valid > 0)
    def _():
        @pl.when(f == 0)
        def _():
            row = lax.broadcasted_iota(jnp.int32, x_ref.shape, 0)
            xn_sc[...] = _unpack_bf16_pairs(jnp.where(row < n_valid, x_ref[...], 0)).astype(BF16)
            acc_sc[...] = jnp.zeros_like(acc_sc)

        def step(rows):
            xn = xn_sc[:rows]
            a = jnp.dot(xn, wg_ref[0].astype(BF16), preferred_element_type=F32)
            b = jnp.dot(xn, wu_ref[0].astype(BF16), preferred_element_type=F32)
            t = (jax.nn.silu(a) * b).astype(BF16)
            acc_sc[:rows] += jnp.dot(t, wd_ref[0].astype(BF16), preferred_element_type=F32)

        tm = xn_sc.shape[0]
        pl.when(n_valid > tm // 2)(lambda: step(tm))
        pl.when(n_valid <= tm // 2)(lambda: step(tm // 2))

        @pl.when(f == pl.num_programs(1) - 1)
        def _():
            out_ref[...] = _pack_bf16_pairs(acc_sc[...])


def _ffn_experts(tile_expert, tile_valid, x, wg, wu, wd, *, tm, fc):
    n, wpk = x.shape
    ff = wg.shape[2]
    row = lambda i, f, te, tv: (i, 0)
    grid_spec = pltpu.PrefetchScalarGridSpec(
        num_scalar_prefetch=2, grid=(n // tm, ff // fc),
        in_specs=[pl.BlockSpec((tm, wpk), row),
                  pl.BlockSpec((1, D_MODEL, fc), lambda i, f, te, tv: (te[i], 0, f)),
                  pl.BlockSpec((1, D_MODEL, fc), lambda i, f, te, tv: (te[i], 0, f)),
                  pl.BlockSpec((1, fc, D_MODEL), lambda i, f, te, tv: (te[i], f, 0))],
        out_specs=pl.BlockSpec((tm, wpk), row),
        scratch_shapes=[pltpu.VMEM((tm, D_MODEL), BF16), pltpu.VMEM((tm, D_MODEL), F32)])
    return pl.pallas_call(
        _ffn_experts_kernel, grid_spec=grid_spec,
        out_shape=jax.ShapeDtypeStruct((n, wpk), jnp.int32),
        compiler_params=_params(2), name="ffn_experts",
    )(tile_expert, tile_valid, x, wg, wu, wd)


def _sc_worker_id():
    return lax.axis_index("s") * SC_CORES + lax.axis_index("c")


def _sc_dispatch(x_a, x_b, slots_a, slots_b, n_slots):
    w = x_a.shape[1]
    per_a = x_a.shape[0] // SC_WORKERS
    per_b = x_b.shape[0] // SC_WORKERS
    ch = min(SC_CHUNK, per_a)
    mesh = plsc.VectorSubcoreMesh(core_axis_name="c", subcore_axis_name="s")

    @functools.partial(
        pl.kernel, mesh=mesh, out_type=jax.ShapeDtypeStruct((n_slots, w), x_a.dtype),
        scratch_types=[pltpu.VMEM((ch,), jnp.int32), pltpu.VMEM((ch, w), x_a.dtype),
                       pltpu.VMEM((per_b,), jnp.int32), pltpu.VMEM((per_b, w), x_b.dtype)])
    def dispatch(xa_hbm, xb_hbm, a0_hbm, a1_hbm, b0_hbm, b1_hbm, out_hbm,
                 idx_v, rows_v, idxb_v, rowsb_v):
        wid = _sc_worker_id()

        @pl.loop(0, per_a // ch)
        def _(c):
            base = pl.multiple_of(wid * per_a + c * ch, 8)
            pltpu.sync_copy(xa_hbm.at[pl.ds(base, ch)], rows_v)
            for slots_hbm in (a0_hbm, a1_hbm):
                pltpu.sync_copy(slots_hbm.at[pl.ds(base, ch)], idx_v)
                pltpu.sync_copy(rows_v, out_hbm.at[idx_v])

        base_b = pl.multiple_of(wid * per_b, 8)
        pltpu.sync_copy(xb_hbm.at[pl.ds(base_b, per_b)], rowsb_v)
        for slots_hbm in (b0_hbm, b1_hbm):
            pltpu.sync_copy(slots_hbm.at[pl.ds(base_b, per_b)], idxb_v)
            pltpu.sync_copy(rowsb_v, out_hbm.at[idxb_v])

    return dispatch(x_a, x_b, *slots_a, *slots_b)


def _sc_gather(table, idx):
    _, w = table.shape
    b = idx.shape[0]
    per_w = b // SC_WORKERS
    ch = max(c for c in range(8, SC_CHUNK + 1, 8) if per_w % c == 0)
    assert per_w * SC_WORKERS == b
    mesh = plsc.VectorSubcoreMesh(core_axis_name="c", subcore_axis_name="s")

    @functools.partial(
        pl.kernel, mesh=mesh, out_type=jax.ShapeDtypeStruct((b, w), table.dtype),
        scratch_types=[pltpu.VMEM((ch,), jnp.int32), pltpu.VMEM((ch, w), table.dtype),
                       pltpu.SemaphoreType.DMA])
    def gather(table_hbm, idx_hbm, out_hbm, idx_v, rows_v, sem):
        wid = _sc_worker_id()

        @pl.loop(0, per_w // ch)
        def _(c):
            base = pl.multiple_of(wid * per_w + c * ch, 8)
            pltpu.sync_copy(idx_hbm.at[pl.ds(base, ch)], idx_v)
            pltpu.async_copy(table_hbm.at[idx_v], rows_v, sem).wait()
            pltpu.sync_copy(rows_v, out_hbm.at[pl.ds(base, ch)])

    return gather(table, idx)


def _combine_kernel(h_ref, sel_ref, y0_ref, y1_ref, out_ref):
    sel = sel_ref[...]
    out_ref[...] = (h_ref[...] + sel[:, 2:3] * _unpack_bf16_pairs(y0_ref[...])
                    + sel[:, 3:4] * _unpack_bf16_pairs(y1_ref[...]))


def _combine(h, sel, yb, blk_off, tm):
    n = h.shape[0]
    wpk = yb.shape[1]
    half_blocks = yb.shape[0] // 2 // tm
    return pl.pallas_call(
        _combine_kernel, grid=(n // tm,),
        in_specs=[pl.BlockSpec((tm, D_MODEL), lambda i: (i, 0)),
                  pl.BlockSpec((tm, LANES), lambda i: (i, 0)),
                  pl.BlockSpec((tm, wpk), lambda i: (i + blk_off, 0)),
                  pl.BlockSpec((tm, wpk), lambda i: (i + blk_off + half_blocks, 0))],
        out_specs=pl.BlockSpec((tm, D_MODEL), lambda i: (i, 0)),
        out_shape=jax.ShapeDtypeStruct((n, D_MODEL), F32),
        compiler_params=_params(1), name="combine",
    )(h, sel, yb, yb)


def _route_plan(sel, te_rows, n_tiles):
    e = sel[:, :2].astype(jnp.int32)
    n_tok = e.shape[0]
    experts = jnp.arange(N_EXPERTS, dtype=jnp.int32)
    onehot = (e.reshape(-1)[:, None] == experts[None, :]).astype(jnp.int32)
    csum = jnp.cumsum(onehot, axis=0)
    rank = jnp.sum((csum - onehot) * onehot, axis=1)
    counts = csum[-1]
    padded = (counts + te_rows - 1) // te_rows * te_rows
    ends = jnp.cumsum(padded)
    starts = ends - padded
    slots = (starts[e.reshape(-1)] + rank).reshape(n_tok, 2)
    tile_start = jnp.arange(n_tiles, dtype=jnp.int32) * te_rows
    tile_expert = jnp.minimum(jnp.sum(tile_start[:, None] >= ends[None, :], axis=1),
                              N_EXPERTS - 1).astype(jnp.int32)
    in_group = tile_start - starts[tile_expert]
    tile_valid = jnp.clip(counts[tile_expert] - in_group, 0, te_rows)
    tile_valid = jnp.where(tile_start < ends[-1], tile_valid, 0).astype(jnp.int32)
    return slots[:, 0], slots[:, 1], tile_expert, tile_valid


def _gmlp_kernel(h_ref, g_ref, wuv_ref, vg_ref, mix_ref, bs_ref, wo_ref, out_ref, v_ref, um_sc,
                 *, period):
    h = h_ref[...]
    tm = h.shape[0]
    ln = mix_ref.shape[1]
    gw = D_MODEL // C_GROUPS
    xn = _rms_bf16(h, g_ref[...])
    vv = jax.nn.gelu(jnp.dot(xn, wuv_ref[:, D_MODEL:], preferred_element_type=F32))
    v = vv * lax.rsqrt(jnp.mean(vv * vv, axis=-1, keepdims=True) + EPS) * vg_ref[...]
    v_ref[...] = v
    vb = v.astype(BF16)
    u = jax.nn.gelu(jnp.dot(xn, wuv_ref[:, :D_MODEL], preferred_element_type=F32))
    r = lax.broadcasted_iota(jnp.int32, (ln, ln), 0)
    c = lax.broadcasted_iota(jnp.int32, (ln, ln), 1)
    causal = (r // period == c // period) & (c % period <= r % period)
    for gi in range(C_GROUPS):
        wm = jnp.where(causal, mix_ref[gi], 0.0).astype(BF16)
        bias = bs_ref[:, gi:gi + 1]
        for ch in range(tm // ln):
            rs = slice(ch * ln, (ch + 1) * ln)
            cs = slice(gi * gw, (gi + 1) * gw)
            mixed = jnp.dot(wm, vb[rs, cs], preferred_element_type=F32) + bias
            um_sc[rs, cs] = (u[rs, cs] * mixed).astype(BF16)
    out_ref[...] = h + jnp.dot(um_sc[...], wo_ref[...], preferred_element_type=F32)


def _gmlp(h, g, wuv16, vg, mix, bs, wo16, *, tm, period):
    n = h.shape[0]
    row = lambda i: (i, 0)
    full = lambda a: pl.BlockSpec(a.shape, lambda i: (0,) * a.ndim)
    return pl.pallas_call(
        functools.partial(_gmlp_kernel, period=period),
        grid=(n // tm,),
        in_specs=[pl.BlockSpec((tm, D_MODEL), row), full(g), full(wuv16), full(vg), full(mix),
                  full(bs), full(wo16)],
        out_specs=[pl.BlockSpec((tm, D_MODEL), row), pl.BlockSpec((tm, D_MODEL), row)],
        out_shape=[jax.ShapeDtypeStruct((n, D_MODEL), F32),
                   jax.ShapeDtypeStruct((n, D_MODEL), F32)],
        scratch_shapes=[pltpu.VMEM((tm, D_MODEL), BF16)],
        compiler_params=_params(1), name="gmlp",
    )(h, g, wuv16, vg, mix, bs, wo16)


def _router_kernel(h_ref, g_ref, wr_ref, xn_ref, sel_ref):
    x = h_ref[...]
    xn = x * lax.rsqrt(jnp.mean(x * x, axis=-1, keepdims=True) + EPS) * g_ref[...]
    xn_ref[...] = _pack_bf16_pairs(xn)
    logits = jnp.dot(xn, wr_ref[...], preferred_element_type=F32,
                     precision=lax.Precision.HIGHEST)
    lane = lax.broadcasted_iota(jnp.int32, logits.shape, 1)
    lanef = lane.astype(F32)
    logits = jnp.where(lane < N_EXPERTS, logits, -jnp.inf)
    m1 = jnp.max(logits, axis=-1, keepdims=True)
    i1 = jnp.min(jnp.where(logits == m1, lanef, float(LANES)), axis=-1, keepdims=True)
    rest = jnp.where(lanef == i1, -jnp.inf, logits)
    m2 = jnp.max(rest, axis=-1, keepdims=True)
    i2 = jnp.min(jnp.where(rest == m2, lanef, float(LANES)), axis=-1, keepdims=True)
    e2 = jnp.exp(m2 - m1)
    w1 = 1.0 / (1.0 + e2)
    w2 = e2 / (1.0 + e2)
    sel_ref[...] = jnp.where(lane == 0, i1, jnp.where(lane == 1, i2,
                             jnp.where(lane == 2, w1, jnp.where(lane == 3, w2, 0.0))))


def _router(h, g, wr_pad, tm):
    n = h.shape[0]
    row = lambda i: (i, 0)
    fix = lambda i: (0, 0)
    return pl.pallas_call(
        _router_kernel, grid=(n // tm,),
        in_specs=[pl.BlockSpec((tm, D_MODEL), row), pl.BlockSpec((1, D_MODEL), fix),
                  pl.BlockSpec((D_MODEL, LANES), fix)],
        out_specs=[pl.BlockSpec((tm, D_MODEL // 2), row), pl.BlockSpec((tm, LANES), row)],
        out_shape=[jax.ShapeDtypeStruct((n, D_MODEL // 2), jnp.int32),
                   jax.ShapeDtypeStruct((n, LANES), F32)],
        compiler_params=_params(1), name="router",
    )(h, g, wr_pad)


def _rope_tables(pos):
    half = HEAD_DIM // 2
    inv = ROPE_THETA ** (-jnp.arange(half, dtype=F32) / half)
    ang = pos.astype(F32)[:, None] * inv[None, :]
    cos, sin = jnp.cos(ang), jnp.sin(ang)
    zero = jnp.zeros_like(sin)
    rep = lambda a, b: jnp.tile(jnp.concatenate([a, b], axis=-1), (1, LANES // HEAD_DIM))
    return rep(cos, cos), rep(-sin, zero), rep(zero, sin)


def _tile_for(n, pref):
    return pref if n % pref == 0 else n


def kernel(x_prompt, x_sample, state_conv, cache_kv_win, norm_mix_e, w_in_e, conv_w, q_gain,
           k_gain, w_out_e, norm_ffn_e, ffn_w_gate, ffn_w_up, ffn_w_down, norm_mix_o, w_uv,
           v_gain, w_s, b_s, w_out_o, norm_moe, w_router, moe_w_gate, moe_w_up, moe_w_down):
    bp, seq, _ = x_prompt.shape
    bs_, s_len, _ = x_sample.shape
    n_p, n_s = bp * seq, bs_ * s_len
    hp = x_prompt.reshape(n_p, D_MODEL)
    hs = x_sample.reshape(n_s, D_MODEL)
    tm_p = 512
    row = lambda a: a.reshape(1, -1)

    tabs_p = _rope_tables(jnp.arange(seq, dtype=jnp.int32))
    tabs_s = tuple(jnp.tile(t, (bs_, 1)) for t in
                   _rope_tables(PAST_LEN + jnp.arange(s_len, dtype=jnp.int32)))
    hd = jnp.arange(256) // HEAD_DIM
    bd = (hd[:, None] == hd[None, :]).astype(BF16)

    w_in16 = w_in_e[0].astype(BF16)
    w_out16 = w_out_e[0].astype(BF16)
    qg = jnp.tile(q_gain[0], 4).reshape(1, 256)
    kg = jnp.tile(k_gain[0], 4).reshape(1, 256)
    g0 = row(norm_mix_e[0])

    dils = tuple(dil for _, dil in BRANCHES)
    lw = min(2048, seq)
    z_p, gb_p, kvt_p, *planes = _inproj(hp, g0, w_in16, tabs_p, qg, kg, bd, tm_p, seq, dils, lw)
    z_s, gb_s, kv_s, q_s = _inproj(hs, g0, w_in16, tabs_s, qg, kg, bd, n_s, n_s, ())

    os_, ls_ = [], []
    for di, dil in enumerate(dils):
        m = seq // dil
        q_d, k_d, v_d = (a.reshape(bp * dil, m, D_B) for a in planes[3 * di:3 * di + 3])
        o, lse = _attn_prompt(q_d, k_d, v_d, qb=min(256, m))
        os_.append(o.reshape(bp, dil, m, D_B))
        ls_.append(lse.reshape(bp, dil, m, D_B))
    hp = _outproj_prompt(z_p, gb_p, conv_w[0], os_, ls_, w_out16, hp, tm_p, seq, dils)

    cat_s = _mix_sample(q_s.reshape(bs_, s_len, D_B),
                        kv_s.reshape(bs_, s_len, 2 * D_B),
                        jnp.transpose(cache_kv_win[0], (0, 2, 3, 4, 1)).reshape(
                            bs_, 2, D_B, -1),
                        z_s.reshape(bs_, s_len, D_A), gb_s.reshape(bs_, s_len, D_A),
                        state_conv[0], conv_w[0])
    hs = _linear_res(cat_s.reshape(n_s, D_MODEL), w_out16, hs)

    conv_prompt = z_p.reshape(bp, seq, D_A)[:, seq - 2:][None]
    conv_sample = z_s.reshape(bs_, s_len, D_A)[:, s_len - 2:][None]
    kv_prompt = jnp.transpose(kvt_p.reshape(bp, 2, H_B, HEAD_DIM, lw), (0, 4, 1, 2, 3))[None]
    kv_sample = kv_s.reshape(bs_, s_len, 2, H_B, HEAD_DIM)[None]

    fg, fu, fd = (ffn_w_gate.astype(BF16), ffn_w_up.astype(BF16), ffn_w_down.astype(BF16))
    gf = row(norm_ffn_e[0])
    hp = _ffn_dense(hp, gf, fg, fu, fd, tm=tm_p, fc=1408)
    hs = _ffn_dense(hs, gf, fg, fu, fd, tm=n_s, fc=1408)

    wuv16 = w_uv[0].astype(BF16)
    wo16 = w_out_o[0].astype(BF16)
    g1, vg = row(norm_mix_o[0]), row(v_gain[0])
    hp, vfull_p = _gmlp(hp, g1, wuv16, vg, w_s[0], b_s[0].T, wo16, tm=tm_p, period=CHUNK)
    mix_s = jnp.tile(w_s[0][:, :s_len, :s_len], (1, bs_, bs_))
    bias_s = jnp.tile(b_s[0][:, :s_len].T, (bs_, 1))
    hs, vfull_s = _gmlp(hs, g1, wuv16, vg, mix_s, bias_s, wo16, tm=n_s, period=s_len)
    tail = ((seq - 1) // CHUNK) * CHUNK
    v_prompt = vfull_p.reshape(bp, seq, D_MODEL)[:, tail:][None]
    v_sample = vfull_s.reshape(bs_, s_len, D_MODEL)[None]

    wr_pad = jnp.pad(w_router[0], ((0, 0), (0, LANES - N_EXPERTS)))
    gm = row(norm_moe[0])
    xn_p, sel_p = _router(hp, gm, wr_pad, tm_p)
    xn_s, sel_s = _router(hs, gm, wr_pad, n_s)
    n_tok = n_p + n_s
    n_tiles = -(-(2 * n_tok + N_EXPERTS * (MOE_ROWS - 1)) // MOE_ROWS)
    slot0, slot1, tile_expert, tile_valid = _route_plan(
        jnp.concatenate([sel_p, sel_s], axis=0), MOE_ROWS, n_tiles)
    xs = _sc_dispatch(xn_p, xn_s, (slot0[:n_p], slot1[:n_p]), (slot0[n_p:], slot1[n_p:]),
                      n_tiles * MOE_ROWS)
    ys = _ffn_experts(tile_expert, tile_valid, xs, moe_w_gate[0], moe_w_up[0], moe_w_down[0],
                      tm=MOE_ROWS, fc=896)
    yb = _sc_gather(ys, jnp.concatenate([slot0, slot1]))
    hp = _combine(hp, sel_p, yb, 0, n_s)
    hs = _combine(hs, sel_s, yb, n_p // n_s, n_s)

    return (hp.reshape(bp, seq, D_MODEL), hs.reshape(bs_, s_len, D_MODEL),
            conv_prompt, conv_sample, kv_prompt, kv_sample, v_prompt, v_sample)
```

```python
import functools

import jax
import jax.numpy as jnp
from jax import lax
from jax.experimental import pallas as pl
from jax.experimental.pallas import tpu as pltpu
from jax.experimental.pallas import tpu_sc as plsc

F32 = jnp.float32
BF16 = jnp.bfloat16

D_MODEL = 1024
D_A = 512
H_B = 8
HEAD_DIM = 64
D_B = H_B * HEAD_DIM
BRANCHES = ((128, 1), (512, 4), (2048, 16))
BAND = 128
ROPE_THETA = 10000.0
PAST_LEN = 16384
N_EXPERTS = 8
C_GROUPS = 4
CHUNK = 128
EPS = 1e-6
LANES = 128
VMEM_LIMIT = 56 * 1024 * 1024
SC_CORES = 2
SC_WORKERS = SC_CORES * 16
SC_CHUNK = 128
MOE_ROWS = 1024


def _params(n_axes):
    return pltpu.CompilerParams(dimension_semantics=("arbitrary",) * n_axes,
                                vmem_limit_bytes=VMEM_LIMIT)


def _rms_bf16(x, g):
    ms = jnp.mean(x * x, axis=-1, keepdims=True)
    return (x * lax.rsqrt(ms + EPS) * g).astype(BF16)


def _inproj_kernel(x_ref, g_ref, w_ref, cos_ref, sina_ref, sinb_ref, qg_ref, kg_ref, bd_ref,
                   z_ref, gb_ref, kv_ref, *rest, dils, win_tiles, tiles_per_seq):
    if dils:
        planes, qkv_sc = rest[:-1], rest[-1]
    else:
        (q_out,) = rest
    xn = _rms_bf16(x_ref[...], g_ref[...])

    def col(j):
        return jnp.dot(xn, w_ref[:, j * 512:(j + 1) * 512], preferred_element_type=F32)

    z_ref[...] = col(1) * col(0)
    gb_ref[...] = col(2)
    cos, sina, sinb = cos_ref[...], sina_ref[...], sinb_ref[...]

    def head_norm_rope(t, gain):
        outs = []
        for c2 in range(2):
            tc = t[:, c2 * 256:(c2 + 1) * 256]
            ss = jnp.dot((tc * tc).astype(BF16), bd_ref[...], preferred_element_type=F32)
            tn = tc * lax.rsqrt(ss * (1.0 / HEAD_DIM) + EPS) * gain
            for c in range(2):
                u = tn[:, c * LANES:(c + 1) * LANES]
                outs.append(u * cos + pltpu.roll(u, LANES - 32, 1) * sina
                            + pltpu.roll(u, 32, 1) * sinb)
        return outs

    q = [t * (HEAD_DIM ** -0.5) for t in head_norm_rope(col(3), qg_ref[...])]
    k = head_norm_rope(col(4), kg_ref[...])
    v = col(5)
    if not dils:
        for c in range(4):
            kv_ref[:, c * LANES:(c + 1) * LANES] = k[c]
            q_out[:, c * LANES:(c + 1) * LANES] = q[c]
        kv_ref[:, D_B:] = v
        return

    @pl.when(pl.program_id(0) % tiles_per_seq >= tiles_per_seq - win_tiles)
    def _():
        for c in range(4):
            kv_ref[0, c * LANES:(c + 1) * LANES, :] = k[c].T
            kv_ref[0, D_B + c * LANES:D_B + (c + 1) * LANES, :] = v[:, c * LANES:(c + 1) * LANES].T

    tm = x_ref.shape[0]
    for c in range(4):
        qkv_sc[0, c] = q[c]
        qkv_sc[1, c] = k[c]
        qkv_sc[2, c] = v[:, c * LANES:(c + 1) * LANES]
    for di, dil in enumerate(dils):
        for r in range(dil):
            rows = pl.ds(r, tm // dil, stride=dil) if dil > 1 else slice(None)
            for a in range(3):
                for c in range(4):
                    planes[3 * di + a][0, r, :, c * LANES:(c + 1) * LANES] = (
                        qkv_sc[a, c, rows, :].astype(BF16))


def _inproj(x, g, w16, tabs, qg, kg, bd, tm, seq, dils, win=0):
    n = x.shape[0]
    cos, sina, sinb = tabs
    tt = cos.shape[0] // tm
    tps = seq // tm
    win_tiles = win // tm
    row = lambda i: (i, 0)
    fix = lambda i: (0, 0)
    tab = lambda i: (i % tt, 0)
    out_specs = [pl.BlockSpec((tm, D_A), row), pl.BlockSpec((tm, D_A), row)]
    out_shape = [jax.ShapeDtypeStruct((n, D_A), F32), jax.ShapeDtypeStruct((n, D_A), F32)]
    if dils:
        out_specs.append(pl.BlockSpec(
            (1, 2 * D_B, tm),
            lambda i: (i // tps, 0, jnp.maximum(i % tps - (tps - win_tiles), 0))))
        out_shape.append(jax.ShapeDtypeStruct((n // seq, 2 * D_B, win), F32))
        scratch = [pltpu.VMEM((3, D_B // LANES, tm, LANES), F32)]
    else:
        out_specs += [pl.BlockSpec((tm, 2 * D_B), row), pl.BlockSpec((tm, D_B), row)]
        out_shape += [jax.ShapeDtypeStruct((n, 2 * D_B), F32),
                      jax.ShapeDtypeStruct((n, D_B), F32)]
        scratch = []
    for dil in dils:
        for _ in range(3):
            out_specs.append(pl.BlockSpec((1, dil, tm // dil, D_B),
                                          lambda i: (i // tps, 0, i % tps, 0)))
            out_shape.append(jax.ShapeDtypeStruct((n // seq, dil, seq // dil, D_B), BF16))
    return pl.pallas_call(
        functools.partial(_inproj_kernel, dils=dils, win_tiles=win_tiles, tiles_per_seq=tps),
        grid=(n // tm,),
        in_specs=[pl.BlockSpec((tm, D_MODEL), row), pl.BlockSpec((1, D_MODEL), fix),
                  pl.BlockSpec(w16.shape, fix),
                  pl.BlockSpec((tm, LANES), tab), pl.BlockSpec((tm, LANES), tab),
                  pl.BlockSpec((tm, LANES), tab),
                  pl.BlockSpec((1, 256), fix), pl.BlockSpec((1, 256), fix),
                  pl.BlockSpec((256, 256), fix)],
        out_specs=out_specs, out_shape=out_shape, scratch_shapes=scratch,
        compiler_params=_params(1), name="inproj",
    )(x, g, w16, cos, sina, sinb, qg, kg, bd)


def _attn_prompt_kernel(q_ref, kc_ref, kp_ref, vc_ref, vp_ref, o_ref, lse_ref, *, qb):
    j = pl.program_id(1)
    nsub = qb // BAND
    lane = lax.broadcasted_iota(jnp.int32, (1, LANES), 1)
    head0 = lane < HEAD_DIM
    qi = lax.broadcasted_iota(jnp.int32, (2 * BAND, 2 * BAND), 0) % BAND + BAND
    kj = lax.broadcasted_iota(jnp.int32, (2 * BAND, 2 * BAND), 1)
    dist = qi - kj
    band = (dist >= 0) & (dist <= BAND)
    first = band & ((kj >= BAND) | (j > 0))
    for hp in range(D_B // LANES):
        ls = slice(hp * LANES, (hp + 1) * LANES)
        kcat = jnp.concatenate([kp_ref[0, :, ls], kc_ref[0, :, ls]], axis=0)
        vcat = jnp.concatenate([vp_ref[0, :, ls], vc_ref[0, :, ls]], axis=0)
        for sub in range(nsub):
            qs = q_ref[0, sub * BAND:(sub + 1) * BAND, ls]
            ks = kcat[sub * BAND:(sub + 2) * BAND]
            vs = vcat[sub * BAND:(sub + 2) * BAND]
            q2 = jnp.concatenate([jnp.where(head0, qs, jnp.zeros_like(qs)),
                                  jnp.where(head0, jnp.zeros_like(qs), qs)], axis=0)
            s = lax.dot_general(q2, ks, (((1,), (1,)), ((), ())), preferred_element_type=F32)
            s = jnp.where(first if sub == 0 else band, s, -jnp.inf)
            mx = jnp.max(s, axis=-1, keepdims=True)
            p = jnp.exp(s - mx)
            l = jnp.sum(p, axis=-1, keepdims=True)
            pv = jnp.dot(p.astype(BF16), vs, preferred_element_type=F32) / l
            lse = mx + jnp.log(l)
            o_ref[0, sub * BAND:(sub + 1) * BAND, ls] = jnp.where(
                head0, pv[:BAND], pv[BAND:]).astype(BF16)
            lse_ref[0, sub * BAND:(sub + 1) * BAND, ls] = jnp.where(head0, lse[:BAND], lse[BAND:])


def _attn_prompt(q, k, v, qb):
    b, m, w = q.shape
    nsub = qb // BAND
    cur = lambda bi, j: (bi, j, 0)
    prev = lambda bi, j: (bi, jnp.maximum(j * nsub - 1, 0), 0)
    return pl.pallas_call(
        functools.partial(_attn_prompt_kernel, qb=qb),
        grid=(b, m // qb),
        in_specs=[pl.BlockSpec((1, qb, w), cur), pl.BlockSpec((1, qb, w), cur),
                  pl.BlockSpec((1, BAND, w), prev), pl.BlockSpec((1, qb, w), cur),
                  pl.BlockSpec((1, BAND, w), prev)],
        out_specs=[pl.BlockSpec((1, qb, w), cur), pl.BlockSpec((1, qb, w), cur)],
        out_shape=[jax.ShapeDtypeStruct((b, m, w), BF16), jax.ShapeDtypeStruct((b, m, w), F32)],
        compiler_params=_params(2), name="attn_prompt",
    )(q, k, k, v, v)


def _outproj_prompt_kernel(z_ref, zh_ref, gb_ref, cw_ref, *rest, tiles_per_batch, dils):
    nb = len(dils)
    o_refs, l_refs = rest[:nb], rest[nb:2 * nb]
    w_ref, h_ref, out_ref = rest[2 * nb:2 * nb + 3]
    o_scs, l_scs = rest[2 * nb + 3:3 * nb + 3], rest[3 * nb + 3:]
    i = pl.program_id(0)
    z = z_ref[...]
    tm = z.shape[0]
    for dil, o_ref, l_ref, o_sc, l_sc in zip(dils, o_refs, l_refs, o_scs, l_scs):
        for r in range(dil):
            rows = pl.ds(r, tm // dil, stride=dil) if dil > 1 else slice(None)
            for c in range(D_B // LANES):
                ls = slice(c * LANES, (c + 1) * LANES)
                o_sc[c, rows, :] = o_ref[0, r, :, ls].astype(F32)
                l_sc[c, rows, :] = l_ref[0, r, :, ls]
    unstage = lambda sc: jnp.concatenate([sc[c] for c in range(D_B // LANES)], axis=1)
    zh = jnp.where(i % tiles_per_batch == 0, 0.0, zh_ref[...])
    row = lax.broadcasted_iota(jnp.int32, (tm, 1), 0)
    zm1 = jnp.where(row >= 1, pltpu.roll(z, 1, 0), zh[7:8])
    zm2 = jnp.where(row >= 2, pltpu.roll(z, 2, 0), jnp.where(row == 1, zh[7:8], zh[6:7]))
    cw = cw_ref[...]
    a_out = gb_ref[...] * (cw[0:1] * zm2 + cw[1:2] * zm1 + cw[2:3] * z)
    lses = [unstage(l_sc) for l_sc in l_scs]
    lm = functools.reduce(jnp.maximum, lses)
    es = [jnp.exp(l - lm) for l in lses]
    o = sum(e * unstage(o_sc) for e, o_sc in zip(es, o_scs)) / sum(es)
    y = jnp.dot(a_out.astype(BF16), w_ref[:D_A, :], preferred_element_type=F32)
    y = y + jnp.dot(o.astype(BF16), w_ref[D_A:, :], preferred_element_type=F32)
    out_ref[...] = h_ref[...] + y


def _outproj_prompt(z, gb, cw, os_, ls_, w16, h, tm, seq, dils):
    n = z.shape[0]
    tps = seq // tm
    row = lambda i: (i, 0)
    fix = lambda i: (0, 0)
    halo = lambda i: (jnp.maximum(i * (tm // 8) - 1, 0), 0)
    half = pl.BlockSpec((tm, D_A), row)
    plane = [pl.BlockSpec((1, dil, tm // dil, D_B), lambda i: (i // tps, 0, i % tps, 0))
             for dil in dils]
    return pl.pallas_call(
        functools.partial(_outproj_prompt_kernel, tiles_per_batch=tps, dils=dils),
        grid=(n // tm,),
        in_specs=[half, pl.BlockSpec((8, D_A), halo), half, pl.BlockSpec((3, D_A), fix),
                  *plane, *plane,
                  pl.BlockSpec((D_MODEL, D_MODEL), fix), pl.BlockSpec((tm, D_MODEL), row)],
        out_specs=pl.BlockSpec((tm, D_MODEL), row),
        out_shape=jax.ShapeDtypeStruct((n, D_MODEL), F32),
        scratch_shapes=[pltpu.VMEM((D_B // LANES, tm, LANES), F32)] * (2 * len(dils)),
        compiler_params=_params(1), name="outproj_prompt",
    )(z, z, gb, cw, *os_, *ls_, w16, h)


def _mix_sample_kernel(q_ref, kvn_ref, win_ref, z_ref, gb_ref, st_ref, cw_ref, cat_ref):
    s_len = q_ref.shape[1]
    rows = H_B * s_len
    npad = LANES
    q = q_ref[0]
    r_i = lax.broadcasted_iota(jnp.int32, (rows, D_B), 0)
    c_i = lax.broadcasted_iota(jnp.int32, (rows, D_B), 1)
    qt = jnp.concatenate([q] * H_B, axis=0)
    qbd = jnp.where(r_i // s_len == c_i // HEAD_DIM, qt, 0.0).astype(BF16)
    kvn = jnp.concatenate([kvn_ref[0], jnp.zeros((npad - s_len, 2 * D_B), F32)], axis=0)

    def multiplicity(delta):
        cnt = jnp.zeros(delta.shape, F32)
        for _, dil in BRANCHES:
            ok = (delta >= 0) & (delta <= BAND * dil) & ((delta & (dil - 1)) == 0)
            cnt = cnt + jnp.where(ok, 1.0, 0.0)
        return cnt

    lb = win_ref.shape[3]
    nt = (((1,), (1,)), ((), ()))
    tok_c = lax.broadcasted_iota(jnp.int32, (rows, lb), 0) % s_len
    cnt_c = multiplicity(tok_c + lb - lax.broadcasted_iota(jnp.int32, (rows, lb), 1))
    tok_n = lax.broadcasted_iota(jnp.int32, (rows, npad), 0) % s_len
    cnt_n = multiplicity(tok_n - lax.broadcasted_iota(jnp.int32, (rows, npad), 1))
    sc = jnp.dot(qbd, win_ref[0, 0].astype(BF16), preferred_element_type=F32)
    sn = lax.dot_general(qbd, kvn[:, :D_B].astype(BF16), nt, preferred_element_type=F32)
    sc = jnp.where(cnt_c > 0, sc, -jnp.inf)
    sn = jnp.where(cnt_n > 0, sn, -jnp.inf)
    mx = jnp.maximum(jnp.max(sc, axis=-1, keepdims=True), jnp.max(sn, axis=-1, keepdims=True))
    pc = cnt_c * jnp.exp(sc - mx)
    pn = cnt_n * jnp.exp(sn - mx)
    den = jnp.sum(pc, axis=-1, keepdims=True) + jnp.sum(pn, axis=-1, keepdims=True)
    of = lax.dot_general(pc.astype(BF16), win_ref[0, 1].astype(BF16), nt,
                         preferred_element_type=F32)
    of = (of + jnp.dot(pn.astype(BF16), kvn[:, D_B:].astype(BF16),
                       preferred_element_type=F32)) / den
    lane_head = lax.broadcasted_iota(jnp.int32, (s_len, D_B), 1) // HEAD_DIM
    o = jnp.zeros((s_len, D_B), F32)
    for h in range(H_B):
        o = o + jnp.where(lane_head == h, of[h * s_len:(h + 1) * s_len], 0.0)
    z = z_ref[0]
    st = st_ref[0]
    t = lax.broadcasted_iota(jnp.int32, (s_len, 1), 0)
    zm1 = jnp.where(t >= 1, pltpu.roll(z, 1, 0), st[1:2])
    zm2 = jnp.where(t >= 2, pltpu.roll(z, 2, 0), jnp.where(t == 1, st[1:2], st[0:1]))
    cw = cw_ref[...]
    cat_ref[0, :, :D_A] = gb_ref[0] * (cw[0:1] * zm2 + cw[1:2] * zm1 + cw[2:3] * z)
    cat_ref[0, :, D_A:] = o


def _mix_sample(q, kvn, win, z, gb, st, cw):
    b, s_len, _ = q.shape
    per = lambda i: (i,) + (0,) * 2
    blk = lambda a: pl.BlockSpec((1,) + a.shape[1:], per)
    return pl.pallas_call(
        _mix_sample_kernel,
        grid=(b,),
        in_specs=[blk(q), blk(kvn),
                  pl.BlockSpec((1,) + win.shape[1:], lambda i: (i, 0, 0, 0)),
                  blk(z), blk(gb), blk(st), pl.BlockSpec((3, D_A), lambda i: (0, 0))],
        out_specs=pl.BlockSpec((1, s_len, D_MODEL), per),
        out_shape=jax.ShapeDtypeStruct((b, s_len, D_MODEL), F32),
        compiler_params=_params(1), name="mix_sample",
    )(q, kvn, win, z, gb, st, cw)


def _linear_res_kernel(x_ref, w_ref, h_ref, out_ref):
    out_ref[...] = h_ref[...] + jnp.dot(x_ref[...].astype(BF16), w_ref[...],
                                        preferred_element_type=F32)


def _linear_res(x, w16, h):
    n, k = x.shape
    full = lambda a: pl.BlockSpec(a.shape, lambda i: (0, 0))
    return pl.pallas_call(
        _linear_res_kernel, grid=(1,),
        in_specs=[full(x), full(w16), full(h)], out_specs=full(h),
        out_shape=jax.ShapeDtypeStruct(h.shape, F32),
        compiler_params=_params(1), name="linear_res",
    )(x, w16, h)


def _swiglu_step(xn, wg_ref, wu_ref, wd_ref, acc_sc):
    a = jnp.dot(xn, wg_ref[0], preferred_element_type=F32)
    b = jnp.dot(xn, wu_ref[0], preferred_element_type=F32)
    t = (jax.nn.silu(a) * b).astype(BF16)
    acc_sc[...] += jnp.dot(t, wd_ref[0], preferred_element_type=F32)


def _ffn_dense_kernel(x_ref, g_ref, wg_ref, wu_ref, wd_ref, out_ref, xn_sc, acc_sc):
    f = pl.program_id(1)

    @pl.when(f == 0)
    def _():
        xn_sc[...] = _rms_bf16(x_ref[...], g_ref[...])
        acc_sc[...] = jnp.zeros_like(acc_sc)

    _swiglu_step(xn_sc[...], wg_ref, wu_ref, wd_ref, acc_sc)

    @pl.when(f == pl.num_programs(1) - 1)
    def _():
        out_ref[...] = x_ref[...] + acc_sc[...]


def _ffn_dense(x, g, wg, wu, wd, *, tm, fc):
    n = x.shape[0]
    ff = wg.shape[2]
    row = lambda i, f: (i, 0)
    return pl.pallas_call(
        _ffn_dense_kernel, grid=(n // tm, ff // fc),
        in_specs=[pl.BlockSpec((tm, D_MODEL), row), pl.BlockSpec((1, D_MODEL), lambda i, f: (0, 0)),
                  pl.BlockSpec((1, D_MODEL, fc), lambda i, f: (0, 0, f)),
                  pl.BlockSpec((1, D_MODEL, fc), lambda i, f: (0, 0, f)),
                  pl.BlockSpec((1, fc, D_MODEL), lambda i, f: (0, f, 0))],
        out_specs=pl.BlockSpec((tm, D_MODEL), row),
        out_shape=jax.ShapeDtypeStruct((n, D_MODEL), F32),
        scratch_shapes=[pltpu.VMEM((tm, D_MODEL), BF16), pltpu.VMEM((tm, D_MODEL), F32)],
        compiler_params=_params(2), name="ffn_dense",
    )(x, g, wg, wu, wd)


def _pack_bf16_pairs(x):
    half = x.shape[1] // 2
    lo = lax.bitcast_convert_type(x[:, :half].astype(BF16).astype(F32), jnp.int32)
    hi = lax.bitcast_convert_type(x[:, half:].astype(BF16).astype(F32), jnp.int32)
    return (hi & -65536) | lax.shift_right_logical(lo, 16)


def _unpack_bf16_pairs(p):
    lo = lax.bitcast_convert_type(p << 16, F32)
    hi = lax.bitcast_convert_type(p & -65536, F32)
    return jnp.concatenate([lo, hi], axis=1)


def _ffn_experts_kernel(te_ref, tv_ref, x_ref, wg_ref, wu_ref, wd_ref, out_ref, xn_sc, acc_sc):
    del te_ref
    i, f = pl.program_id(0), pl.program_id(1)
    n_valid = tv_ref[i]

    @pl.when(n_valid > 0)
    def _():
        @pl.when(f == 0)
        def _():
            row = lax.broadcasted_iota(jnp.int32, x_ref.shape, 0)
            xn_sc[...] = _unpack_bf16_pairs(jnp.where(row < n_valid, x_ref[...], 0)).astype(BF16)
            acc_sc[...] = jnp.zeros_like(acc_sc)

        def step(rows):
            xn = xn_sc[:rows]
            a = jnp.dot(xn, wg_ref[0].astype(BF16), preferred_element_type=F32)
            b = jnp.dot(xn, wu_ref[0].astype(BF16), preferred_element_type=F32)
            t = (jax.nn.silu(a) * b).astype(BF16)
            acc_sc[:rows] += jnp.dot(t, wd_ref[0].astype(BF16), preferred_element_type=F32)

        tm = xn_sc.shape[0]
        pl.when(n_valid > tm // 2)(lambda: step(tm))
        pl.when(n_valid <= tm // 2)(lambda: step(tm // 2))

        @pl.when(f == pl.num_programs(1) - 1)
        def _():
            out_ref[...] = _pack_bf16_pairs(acc_sc[...])


def _ffn_experts(tile_expert, tile_valid, x, wg, wu, wd, *, tm, fc):
    n, wpk = x.shape
    ff = wg.shape[2]
    row = lambda i, f, te, tv: (i, 0)
    grid_spec = pltpu.PrefetchScalarGridSpec(
        num_scalar_prefetch=2, grid=(n // tm, ff // fc),
        in_specs=[pl.BlockSpec((tm, wpk), row),
                  pl.BlockSpec((1, D_MODEL, fc), lambda i, f, te, tv: (te[i], 0, f)),
                  pl.BlockSpec((1, D_MODEL, fc), lambda i, f, te, tv: (te[i], 0, f)),
                  pl.BlockSpec((1, fc, D_MODEL), lambda i, f, te, tv: (te[i], f, 0))],
        out_specs=pl.BlockSpec((tm, wpk), row),
        scratch_shapes=[pltpu.VMEM((tm, D_MODEL), BF16), pltpu.VMEM((tm, D_MODEL), F32)])
    return pl.pallas_call(
        _ffn_experts_kernel, grid_spec=grid_spec,
        out_shape=jax.ShapeDtypeStruct((n, wpk), jnp.int32),
        compiler_params=_params(2), name="ffn_experts",
    )(tile_expert, tile_valid, x, wg, wu, wd)


def _sc_worker_id():
    return lax.axis_index("s") * SC_CORES + lax.axis_index("c")


def _sc_dispatch(x_a, x_b, slots_a, slots_b, n_slots):
    w = x_a.shape[1]
    per_a = x_a.shape[0] // SC_WORKERS
    per_b = x_b.shape[0] // SC_WORKERS
    ch = min(SC_CHUNK, per_a)
    mesh = plsc.VectorSubcoreMesh(core_axis_name="c", subcore_axis_name="s")

    @functools.partial(
        pl.kernel, mesh=mesh, out_type=jax.ShapeDtypeStruct((n_slots, w), x_a.dtype),
        scratch_types=[pltpu.VMEM((ch,), jnp.int32), pltpu.VMEM((ch, w), x_a.dtype),
                       pltpu.VMEM((per_b,), jnp.int32), pltpu.VMEM((per_b, w), x_b.dtype)])
    def dispatch(xa_hbm, xb_hbm, a0_hbm, a1_hbm, b0_hbm, b1_hbm, out_hbm,
                 idx_v, rows_v, idxb_v, rowsb_v):
        wid = _sc_worker_id()

        @pl.loop(0, per_a // ch)
        def _(c):
            base = pl.multiple_of(wid * per_a + c * ch, 8)
            pltpu.sync_copy(xa_hbm.at[pl.ds(base, ch)], rows_v)
            for slots_hbm in (a0_hbm, a1_hbm):
                pltpu.sync_copy(slots_hbm.at[pl.ds(base, ch)], idx_v)
                pltpu.sync_copy(rows_v, out_hbm.at[idx_v])

        base_b = pl.multiple_of(wid * per_b, 8)
        pltpu.sync_copy(xb_hbm.at[pl.ds(base_b, per_b)], rowsb_v)
        for slots_hbm in (b0_hbm, b1_hbm):
            pltpu.sync_copy(slots_hbm.at[pl.ds(base_b, per_b)], idxb_v)
            pltpu.sync_copy(rowsb_v, out_hbm.at[idxb_v])

    return dispatch(x_a, x_b, *slots_a, *slots_b)


def _sc_gather(table, idx):
    _, w = table.shape
    b = idx.shape[0]
    per_w = b // SC_WORKERS
    ch = max(c for c in range(8, SC_CHUNK + 1, 8) if per_w % c == 0)
    assert per_w * SC_WORKERS == b
    mesh = plsc.VectorSubcoreMesh(core_axis_name="c", subcore_axis_name="s")

    @functools.partial(
        pl.kernel, mesh=mesh, out_type=jax.ShapeDtypeStruct((b, w), table.dtype),
        scratch_types=[pltpu.VMEM((ch,), jnp.int32), pltpu.VMEM((ch, w), table.dtype),
                       pltpu.SemaphoreType.DMA])
    def gather(table_hbm, idx_hbm, out_hbm, idx_v, rows_v, sem):
        wid = _sc_worker_id()

        @pl.loop(0, per_w // ch)
        def _(c):
            base = pl.multiple_of(wid * per_w + c * ch, 8)
            pltpu.sync_copy(idx_hbm.at[pl.ds(base, ch)], idx_v)
            pltpu.async_copy(table_hbm.at[idx_v], rows_v, sem).wait()
            pltpu.sync_copy(rows_v, out_hbm.at[pl.ds(base, ch)])

    return gather(table, idx)


def _combine_kernel(h_ref, sel_ref, y0_ref, y1_ref, out_ref):
    sel = sel_ref[...]
    out_ref[...] = (h_ref[...] + sel[:, 2:3] * _unpack_bf16_pairs(y0_ref[...])
                    + sel[:, 3:4] * _unpack_bf16_pairs(y1_ref[...]))


def _combine(h, sel, yb, blk_off, tm):
    n = h.shape[0]
    wpk = yb.shape[1]
    half_blocks = yb.shape[0] // 2 // tm
    return pl.pallas_call(
        _combine_kernel, grid=(n // tm,),
        in_specs=[pl.BlockSpec((tm, D_MODEL), lambda i: (i, 0)),
                  pl.BlockSpec((tm, LANES), lambda i: (i, 0)),
                  pl.BlockSpec((tm, wpk), lambda i: (i + blk_off, 0)),
                  pl.BlockSpec((tm, wpk), lambda i: (i + blk_off + half_blocks, 0))],
        out_specs=pl.BlockSpec((tm, D_MODEL), lambda i: (i, 0)),
        out_shape=jax.ShapeDtypeStruct((n, D_MODEL), F32),
        compiler_params=_params(1), name="combine",
    )(h, sel, yb, yb)


def _route_plan(sels, counts, te_rows, n_tiles):
    counts = counts[0, :N_EXPERTS].astype(jnp.int32)
    padded = (counts + te_rows - 1) // te_rows * te_rows
    ends = jnp.cumsum(padded)
    starts = ends - padded
    experts = jnp.arange(N_EXPERTS, dtype=jnp.int32)

    def slots_of(sel, k):
        e = sel[:, k].astype(jnp.int32)
        start = jnp.sum(jnp.where(e[:, None] == experts[None, :], starts[None, :], 0), axis=1)
        return start + sel[:, 4 + k].astype(jnp.int32)

    slots = [(slots_of(sel, 0), slots_of(sel, 1)) for sel in sels]
    tile_start = jnp.arange(n_tiles, dtype=jnp.int32) * te_rows
    tile_expert = jnp.minimum(jnp.sum(tile_start[:, None] >= ends[None, :], axis=1),
                              N_EXPERTS - 1).astype(jnp.int32)
    in_group = tile_start - starts[tile_expert]
    tile_valid = jnp.clip(counts[tile_expert] - in_group, 0, te_rows)
    tile_valid = jnp.where(tile_start < ends[-1], tile_valid, 0).astype(jnp.int32)
    return slots, tile_expert, tile_valid


def _gmlp_kernel(h_ref, g_ref, wuv_ref, vg_ref, mix_ref, bs_ref, wo_ref, out_ref, v_ref, um_sc,
                 *, period):
    h = h_ref[...]
    tm = h.shape[0]
    ln = mix_ref.shape[1]
    gw = D_MODEL // C_GROUPS
    xn = _rms_bf16(h, g_ref[...])
    vv = jax.nn.gelu(jnp.dot(xn, wuv_ref[:, D_MODEL:], preferred_element_type=F32))
    v = vv * lax.rsqrt(jnp.mean(vv * vv, axis=-1, keepdims=True) + EPS) * vg_ref[...]
    v_ref[0] = v[tm - v_ref.shape[1]:]
    vb = v.astype(BF16)
    u = jax.nn.gelu(jnp.dot(xn, wuv_ref[:, :D_MODEL], preferred_element_type=F32))
    r = lax.broadcasted_iota(jnp.int32, (ln, ln), 0)
    c = lax.broadcasted_iota(jnp.int32, (ln, ln), 1)
    causal = (r // period == c // period) & (c % period <= r % period)
    for gi in range(C_GROUPS):
        wm = jnp.where(causal, mix_ref[gi], 0.0).astype(BF16)
        bias = bs_ref[:, gi:gi + 1]
        for ch in range(tm // ln):
            rs = slice(ch * ln, (ch + 1) * ln)
            cs = slice(gi * gw, (gi + 1) * gw)
            mixed = jnp.dot(wm, vb[rs, cs], preferred_element_type=F32) + bias
            um_sc[rs, cs] = (u[rs, cs] * mixed).astype(BF16)
    out_ref[...] = h + jnp.dot(um_sc[...], wo_ref[...], preferred_element_type=F32)


def _gmlp(h, g, wuv16, vg, mix, bs, wo16, *, tm, period, seq, tail):
    n = h.shape[0]
    tps = seq // tm
    row = lambda i: (i, 0)
    full = lambda a: pl.BlockSpec(a.shape, lambda i: (0,) * a.ndim)
    return pl.pallas_call(
        functools.partial(_gmlp_kernel, period=period),
        grid=(n // tm,),
        in_specs=[pl.BlockSpec((tm, D_MODEL), row), full(g), full(wuv16), full(vg), full(mix),
                  full(bs), full(wo16)],
        out_specs=[pl.BlockSpec((tm, D_MODEL), row),
                   pl.BlockSpec((1, tail, D_MODEL), lambda i: (i // tps, 0, 0))],
        out_shape=[jax.ShapeDtypeStruct((n, D_MODEL), F32),
                   jax.ShapeDtypeStruct((n // seq, tail, D_MODEL), F32)],
        scratch_shapes=[pltpu.VMEM((tm, D_MODEL), BF16)],
        compiler_params=_params(1), name="gmlp",
    )(h, g, wuv16, vg, mix, bs, wo16)


def _router_kernel(h_ref, g_ref, wr_ref, cin_ref, xn_ref, sel_ref, cnt_ref):
    @pl.when(pl.program_id(0) == 0)
    def _():
        cnt_ref[...] = cin_ref[...]

    x = h_ref[...]
    xn = x * lax.rsqrt(jnp.mean(x * x, axis=-1, keepdims=True) + EPS) * g_ref[...]
    xn_ref[...] = _pack_bf16_pairs(xn)
    logits = jnp.dot(xn, wr_ref[...], preferred_element_type=F32,
                     precision=lax.Precision.HIGHEST)
    lane = lax.broadcasted_iota(jnp.int32, logits.shape, 1)
    lanef = lane.astype(F32)
    logits = jnp.where(lane < N_EXPERTS, logits, -jnp.inf)
    m1 = jnp.max(logits, axis=-1, keepdims=True)
    i1 = jnp.min(jnp.where(logits == m1, lanef, float(LANES)), axis=-1, keepdims=True)
    rest = jnp.where(lanef == i1, -jnp.inf, logits)
    m2 = jnp.max(rest, axis=-1, keepdims=True)
    i2 = jnp.min(jnp.where(rest == m2, lanef, float(LANES)), axis=-1, keepdims=True)
    e2 = jnp.exp(m2 - m1)
    w1 = 1.0 / (1.0 + e2)
    w2 = e2 / (1.0 + e2)
    tm = x.shape[0]
    hit1, hit2 = lanef == i1, lanef == i2
    picked = jnp.where(hit1 | hit2, 1.0, 0.0)
    r_i = lax.broadcasted_iota(jnp.int32, (tm, tm), 0)
    c_i = lax.broadcasted_iota(jnp.int32, (tm, tm), 1)
    below = jnp.where(c_i < r_i, 1.0, 0.0).astype(BF16)
    before = jnp.dot(below, picked.astype(BF16), preferred_element_type=F32) + cnt_ref[...]
    rank1 = jnp.sum(jnp.where(hit1, before, 0.0), axis=-1, keepdims=True)
    rank2 = jnp.sum(jnp.where(hit2, before, 0.0), axis=-1, keepdims=True)
    cnt_ref[...] += jnp.sum(picked, axis=0, keepdims=True)
    vals = (i1, i2, w1, w2, rank1, rank2)
    sel = jnp.zeros_like(logits)
    for li, val in enumerate(vals):
        sel = jnp.where(lane == li, val, sel)
    sel_ref[...] = sel


def _router(h, g, wr_pad, counts_in, tm):
    n = h.shape[0]
    row = lambda i: (i, 0)
    fix = lambda i: (0, 0)
    return pl.pallas_call(
        _router_kernel, grid=(n // tm,),
        in_specs=[pl.BlockSpec((tm, D_MODEL), row), pl.BlockSpec((1, D_MODEL), fix),
                  pl.BlockSpec((D_MODEL, LANES), fix), pl.BlockSpec((1, LANES), fix)],
        out_specs=[pl.BlockSpec((tm, D_MODEL // 2), row), pl.BlockSpec((tm, LANES), row),
                   pl.BlockSpec((1, LANES), fix)],
        out_shape=[jax.ShapeDtypeStruct((n, D_MODEL // 2), jnp.int32),
                   jax.ShapeDtypeStruct((n, LANES), F32),
                   jax.ShapeDtypeStruct((1, LANES), F32)],
        compiler_params=_params(1), name="router",
    )(h, g, wr_pad, counts_in)


def _rope_tables(pos):
    half = HEAD_DIM // 2
    inv = ROPE_THETA ** (-jnp.arange(half, dtype=F32) / half)
    ang = pos.astype(F32)[:, None] * inv[None, :]
    cos, sin = jnp.cos(ang), jnp.sin(ang)
    zero = jnp.zeros_like(sin)
    rep = lambda a, b: jnp.tile(jnp.concatenate([a, b], axis=-1), (1, LANES // HEAD_DIM))
    return rep(cos, cos), rep(-sin, zero), rep(zero, sin)


def _tile_for(n, pref):
    return pref if n % pref == 0 else n


def kernel(x_prompt, x_sample, state_conv, cache_kv_win, norm_mix_e, w_in_e, conv_w, q_gain,
           k_gain, w_out_e, norm_ffn_e, ffn_w_gate, ffn_w_up, ffn_w_down, norm_mix_o, w_uv,
           v_gain, w_s, b_s, w_out_o, norm_moe, w_router, moe_w_gate, moe_w_up, moe_w_down):
    bp, seq, _ = x_prompt.shape
    bs_, s_len, _ = x_sample.shape
    n_p, n_s = bp * seq, bs_ * s_len
    hp = x_prompt.reshape(n_p, D_MODEL)
    hs = x_sample.reshape(n_s, D_MODEL)
    tm_p = 512
    row = lambda a: a.reshape(1, -1)

    tabs_p = _rope_tables(jnp.arange(seq, dtype=jnp.int32))
    tabs_s = tuple(jnp.tile(t, (bs_, 1)) for t in
                   _rope_tables(PAST_LEN + jnp.arange(s_len, dtype=jnp.int32)))
    hd = jnp.arange(256) // HEAD_DIM
    bd = (hd[:, None] == hd[None, :]).astype(BF16)

    w_in16 = w_in_e[0].astype(BF16)
    w_out16 = w_out_e[0].astype(BF16)
    qg = jnp.tile(q_gain[0], 4).reshape(1, 256)
    kg = jnp.tile(k_gain[0], 4).reshape(1, 256)
    g0 = row(norm_mix_e[0])

    dils = tuple(dil for _, dil in BRANCHES)
    lw = min(2048, seq)
    z_p, gb_p, kvt_p, *planes = _inproj(hp, g0, w_in16, tabs_p, qg, kg, bd, tm_p, seq, dils, lw)
    z_s, gb_s, kv_s, q_s = _inproj(hs, g0, w_in16, tabs_s, qg, kg, bd, n_s, n_s, ())

    os_, ls_ = [], []
    for di, dil in enumerate(dils):
        m = seq // dil
        q_d, k_d, v_d = (a.reshape(bp * dil, m, D_B) for a in planes[3 * di:3 * di + 3])
        o, lse = _attn_prompt(q_d, k_d, v_d, qb=min(256, m))
        os_.append(o.reshape(bp, dil, m, D_B))
        ls_.append(lse.reshape(bp, dil, m, D_B))
    hp = _outproj_prompt(z_p, gb_p, conv_w[0], os_, ls_, w_out16, hp, tm_p, seq, dils)

    cat_s = _mix_sample(q_s.reshape(bs_, s_len, D_B),
                        kv_s.reshape(bs_, s_len, 2 * D_B),
                        jnp.transpose(cache_kv_win[0], (0, 2, 3, 4, 1)).reshape(
                            bs_, 2, D_B, -1),
                        z_s.reshape(bs_, s_len, D_A), gb_s.reshape(bs_, s_len, D_A),
                        state_conv[0], conv_w[0])
    hs = _linear_res(cat_s.reshape(n_s, D_MODEL), w_out16, hs)

    conv_prompt = z_p.reshape(bp, seq, D_A)[:, seq - 2:][None]
    conv_sample = z_s.reshape(bs_, s_len, D_A)[:, s_len - 2:][None]
    kv_prompt = jnp.transpose(kvt_p.reshape(bp, 2, H_B, HEAD_DIM, lw), (0, 4, 1, 2, 3))[None]
    kv_sample = kv_s.reshape(bs_, s_len, 2, H_B, HEAD_DIM)[None]

    fg, fu, fd = (ffn_w_gate.astype(BF16), ffn_w_up.astype(BF16), ffn_w_down.astype(BF16))
    gf = row(norm_ffn_e[0])
    hp = _ffn_dense(hp, gf, fg, fu, fd, tm=tm_p, fc=1408)
    hs = _ffn_dense(hs, gf, fg, fu, fd, tm=n_s, fc=1408)

    wuv16 = w_uv[0].astype(BF16)
    wo16 = w_out_o[0].astype(BF16)
    g1, vg = row(norm_mix_o[0]), row(v_gain[0])
    tail = seq - ((seq - 1) // CHUNK) * CHUNK
    hp, v_tail_p = _gmlp(hp, g1, wuv16, vg, w_s[0], b_s[0].T, wo16, tm=tm_p, period=CHUNK,
                         seq=seq, tail=tail)
    mix_s = jnp.tile(w_s[0][:, :s_len, :s_len], (1, bs_, bs_))
    bias_s = jnp.tile(b_s[0][:, :s_len].T, (bs_, 1))
    hs, v_all_s = _gmlp(hs, g1, wuv16, vg, mix_s, bias_s, wo16, tm=n_s, period=s_len,
                        seq=n_s, tail=n_s)
    v_prompt = v_tail_p[None]
    v_sample = v_all_s.reshape(bs_, s_len, D_MODEL)[None]

    wr_pad = jnp.pad(w_router[0], ((0, 0), (0, LANES - N_EXPERTS)))
    gm = row(norm_moe[0])
    xn_p, sel_p, counts = _router(hp, gm, wr_pad, jnp.zeros((1, LANES), F32), tm_p)
    xn_s, sel_s, counts = _router(hs, gm, wr_pad, counts, n_s)
    n_tok = n_p + n_s
    n_tiles = -(-(2 * n_tok + N_EXPERTS * (MOE_ROWS - 1)) // MOE_ROWS)
    (slots_p, slots_s), tile_expert, tile_valid = _route_plan(
        (sel_p, sel_s), counts, MOE_ROWS, n_tiles)
    slot0 = jnp.concatenate([slots_p[0], slots_s[0]])
    slot1 = jnp.concatenate([slots_p[1], slots_s[1]])
    xs = _sc_dispatch(xn_p, xn_s, slots_p, slots_s, n_tiles * MOE_ROWS)
    ys = _ffn_experts(tile_expert, tile_valid, xs, moe_w_gate[0], moe_w_up[0], moe_w_down[0],
                      tm=MOE_ROWS, fc=896)
    yb = _sc_gather(ys, jnp.concatenate([slot0, slot1]))
    hp = _combine(hp, sel_p, yb, 0, n_s)
    hs = _combine(hs, sel_s, yb, n_p // n_s, n_s)

    return (hp.reshape(bp, seq, D_MODEL), hs.reshape(bs_, s_len, D_MODEL),
            conv_prompt, conv_sample, kv_prompt, kv_sample, v_prompt, v_sample)
```

```python
import functools

import jax
import jax.numpy as jnp
from jax import lax
from jax.experimental import pallas as pl
from jax.experimental.pallas import tpu as pltpu
from jax.experimental.pallas import tpu_sc as plsc

F32 = jnp.float32
BF16 = jnp.bfloat16

D_MODEL = 1024
D_A = 512
H_B = 8
HEAD_DIM = 64
D_B = H_B * HEAD_DIM
BRANCHES = ((128, 1), (512, 4), (2048, 16))
BAND = 128
ROPE_THETA = 10000.0
PAST_LEN = 16384
N_EXPERTS = 8
C_GROUPS = 4
CHUNK = 128
EPS = 1e-6
LANES = 128
VMEM_LIMIT = 56 * 1024 * 1024
SC_CORES = 2
SC_WORKERS = SC_CORES * 16
SC_CHUNK = 128
MOE_ROWS = 1024


def _params(n_axes):
    return pltpu.CompilerParams(dimension_semantics=("arbitrary",) * n_axes,
                                vmem_limit_bytes=VMEM_LIMIT)


def _rms_bf16(x, g):
    ms = jnp.mean(x * x, axis=-1, keepdims=True)
    return (x * lax.rsqrt(ms + EPS) * g).astype(BF16)


def _inproj_kernel(x_ref, g_ref, w_ref, cos_ref, sina_ref, sinb_ref, qg_ref, kg_ref, bd_ref,
                   z_ref, gb_ref, kv_ref, *rest, dils, win_tiles, tiles_per_seq):
    if dils:
        planes, qkv_sc = rest[:-1], rest[-1]
    else:
        (q_out,) = rest
    xn = _rms_bf16(x_ref[...], g_ref[...])

    def col(j):
        return jnp.dot(xn, w_ref[:, j * 512:(j + 1) * 512], preferred_element_type=F32)

    z_ref[...] = col(1) * col(0)
    gb_ref[...] = col(2)
    cos, sina, sinb = cos_ref[...], sina_ref[...], sinb_ref[...]

    def head_norm_rope(t, gain):
        outs = []
        for c2 in range(2):
            tc = t[:, c2 * 256:(c2 + 1) * 256]
            ss = jnp.dot((tc * tc).astype(BF16), bd_ref[...], preferred_element_type=F32)
            tn = tc * lax.rsqrt(ss * (1.0 / HEAD_DIM) + EPS) * gain
            for c in range(2):
                u = tn[:, c * LANES:(c + 1) * LANES]
                outs.append(u * cos + pltpu.roll(u, LANES - 32, 1) * sina
                            + pltpu.roll(u, 32, 1) * sinb)
        return outs

    q = [t * (HEAD_DIM ** -0.5) for t in head_norm_rope(col(3), qg_ref[...])]
    k = head_norm_rope(col(4), kg_ref[...])
    v = col(5)
    if not dils:
        for c in range(4):
            kv_ref[:, c * LANES:(c + 1) * LANES] = k[c]
            q_out[:, c * LANES:(c + 1) * LANES] = q[c]
        kv_ref[:, D_B:] = v
        return

    @pl.when(pl.program_id(0) % tiles_per_seq >= tiles_per_seq - win_tiles)
    def _():
        for c in range(4):
            kv_ref[0, c * LANES:(c + 1) * LANES, :] = k[c].T
            kv_ref[0, D_B + c * LANES:D_B + (c + 1) * LANES, :] = v[:, c * LANES:(c + 1) * LANES].T

    tm = x_ref.shape[0]
    for c in range(4):
        qkv_sc[0, c] = q[c]
        qkv_sc[1, c] = k[c]
        qkv_sc[2, c] = v[:, c * LANES:(c + 1) * LANES]
    for di, dil in enumerate(dils):
        for r in range(dil):
            rows = pl.ds(r, tm // dil, stride=dil) if dil > 1 else slice(None)
            for a in range(3):
                for c in range(4):
                    planes[3 * di + a][0, r, :, c * LANES:(c + 1) * LANES] = (
                        qkv_sc[a, c, rows, :].astype(BF16))


def _inproj(x, g, w16, tabs, qg, kg, bd, tm, seq, dils, win=0):
    n = x.shape[0]
    cos, sina, sinb = tabs
    tt = cos.shape[0] // tm
    tps = seq // tm
    win_tiles = win // tm
    row = lambda i: (i, 0)
    fix = lambda i: (0, 0)
    tab = lambda i: (i % tt, 0)
    out_specs = [pl.BlockSpec((tm, D_A), row), pl.BlockSpec((tm, D_A), row)]
    out_shape = [jax.ShapeDtypeStruct((n, D_A), F32), jax.ShapeDtypeStruct((n, D_A), F32)]
    if dils:
        out_specs.append(pl.BlockSpec(
            (1, 2 * D_B, tm),
            lambda i: (i // tps, 0, jnp.maximum(i % tps - (tps - win_tiles), 0))))
        out_shape.append(jax.ShapeDtypeStruct((n // seq, 2 * D_B, win), F32))
        scratch = [pltpu.VMEM((3, D_B // LANES, tm, LANES), F32)]
    else:
        out_specs += [pl.BlockSpec((tm, 2 * D_B), row), pl.BlockSpec((tm, D_B), row)]
        out_shape += [jax.ShapeDtypeStruct((n, 2 * D_B), F32),
                      jax.ShapeDtypeStruct((n, D_B), F32)]
        scratch = []
    for dil in dils:
        for _ in range(3):
            out_specs.append(pl.BlockSpec((1, dil, tm // dil, D_B),
                                          lambda i: (i // tps, 0, i % tps, 0)))
            out_shape.append(jax.ShapeDtypeStruct((n // seq, dil, seq // dil, D_B), BF16))
    return pl.pallas_call(
        functools.partial(_inproj_kernel, dils=dils, win_tiles=win_tiles, tiles_per_seq=tps),
        grid=(n // tm,),
        in_specs=[pl.BlockSpec((tm, D_MODEL), row), pl.BlockSpec((1, D_MODEL), fix),
                  pl.BlockSpec(w16.shape, fix),
                  pl.BlockSpec((tm, LANES), tab), pl.BlockSpec((tm, LANES), tab),
                  pl.BlockSpec((tm, LANES), tab),
                  pl.BlockSpec((1, 256), fix), pl.BlockSpec((1, 256), fix),
                  pl.BlockSpec((256, 256), fix)],
        out_specs=out_specs, out_shape=out_shape, scratch_shapes=scratch,
        compiler_params=_params(1), name="inproj",
    )(x, g, w16, cos, sina, sinb, qg, kg, bd)


def _attn_prompt_kernel(q_ref, kc_ref, kp_ref, vc_ref, vp_ref, o_ref, lse_ref, *, qb):
    j = pl.program_id(1)
    nsub = qb // BAND
    lane = lax.broadcasted_iota(jnp.int32, (1, LANES), 1)
    head0 = lane < HEAD_DIM
    qi = lax.broadcasted_iota(jnp.int32, (2 * BAND, 2 * BAND), 0) % BAND + BAND
    kj = lax.broadcasted_iota(jnp.int32, (2 * BAND, 2 * BAND), 1)
    dist = qi - kj
    band = (dist >= 0) & (dist <= BAND)
    first = band & ((kj >= BAND) | (j > 0))
    for hp in range(D_B // LANES):
        ls = slice(hp * LANES, (hp + 1) * LANES)
        kcat = jnp.concatenate([kp_ref[0, :, ls], kc_ref[0, :, ls]], axis=0)
        vcat = jnp.concatenate([vp_ref[0, :, ls], vc_ref[0, :, ls]], axis=0)
        for sub in range(nsub):
            qs = q_ref[0, sub * BAND:(sub + 1) * BAND, ls]
            ks = kcat[sub * BAND:(sub + 2) * BAND]
            vs = vcat[sub * BAND:(sub + 2) * BAND]
            q2 = jnp.concatenate([jnp.where(head0, qs, jnp.zeros_like(qs)),
                                  jnp.where(head0, jnp.zeros_like(qs), qs)], axis=0)
            s = lax.dot_general(q2, ks, (((1,), (1,)), ((), ())), preferred_element_type=F32)
            s = jnp.where(first if sub == 0 else band, s, -jnp.inf)
            mx = jnp.max(s, axis=-1, keepdims=True)
            p = jnp.exp(s - mx)
            l = jnp.sum(p, axis=-1, keepdims=True)
            pv = jnp.dot(p.astype(BF16), vs, preferred_element_type=F32) / l
            lse = mx + jnp.log(l)
            o_ref[0, sub * BAND:(sub + 1) * BAND, ls] = jnp.where(
                head0, pv[:BAND], pv[BAND:]).astype(BF16)
            lse_ref[0, sub * BAND:(sub + 1) * BAND, ls] = jnp.where(head0, lse[:BAND], lse[BAND:])


def _attn_prompt(q, k, v, qb):
    b, m, w = q.shape
    nsub = qb // BAND
    cur = lambda bi, j: (bi, j, 0)
    prev = lambda bi, j: (bi, jnp.maximum(j * nsub - 1, 0), 0)
    return pl.pallas_call(
        functools.partial(_attn_prompt_kernel, qb=qb),
        grid=(b, m // qb),
        in_specs=[pl.BlockSpec((1, qb, w), cur), pl.BlockSpec((1, qb, w), cur),
                  pl.BlockSpec((1, BAND, w), prev), pl.BlockSpec((1, qb, w), cur),
                  pl.BlockSpec((1, BAND, w), prev)],
        out_specs=[pl.BlockSpec((1, qb, w), cur), pl.BlockSpec((1, qb, w), cur)],
        out_shape=[jax.ShapeDtypeStruct((b, m, w), BF16), jax.ShapeDtypeStruct((b, m, w), F32)],
        compiler_params=_params(2), name="attn_prompt",
    )(q, k, k, v, v)


def _outproj_prompt_kernel(z_ref, zh_ref, gb_ref, cw_ref, *rest, tiles_per_batch, dils):
    nb = len(dils)
    o_refs, l_refs = rest[:nb], rest[nb:2 * nb]
    w_ref, h_ref, out_ref = rest[2 * nb:2 * nb + 3]
    o_scs, l_scs = rest[2 * nb + 3:3 * nb + 3], rest[3 * nb + 3:]
    i = pl.program_id(0)
    z = z_ref[...]
    tm = z.shape[0]
    for dil, o_ref, l_ref, o_sc, l_sc in zip(dils, o_refs, l_refs, o_scs, l_scs):
        for r in range(dil):
            rows = pl.ds(r, tm // dil, stride=dil) if dil > 1 else slice(None)
            for c in range(D_B // LANES):
                ls = slice(c * LANES, (c + 1) * LANES)
                o_sc[c, rows, :] = o_ref[0, r, :, ls].astype(F32)
                l_sc[c, rows, :] = l_ref[0, r, :, ls]
    unstage = lambda sc: jnp.concatenate([sc[c] for c in range(D_B // LANES)], axis=1)
    zh = jnp.where(i % tiles_per_batch == 0, 0.0, zh_ref[...])
    row = lax.broadcasted_iota(jnp.int32, (tm, 1), 0)
    zm1 = jnp.where(row >= 1, pltpu.roll(z, 1, 0), zh[7:8])
    zm2 = jnp.where(row >= 2, pltpu.roll(z, 2, 0), jnp.where(row == 1, zh[7:8], zh[6:7]))
    cw = cw_ref[...]
    a_out = gb_ref[...] * (cw[0:1] * zm2 + cw[1:2] * zm1 + cw[2:3] * z)
    lses = [unstage(l_sc) for l_sc in l_scs]
    lm = functools.reduce(jnp.maximum, lses)
    es = [jnp.exp(l - lm) for l in lses]
    o = sum(e * unstage(o_sc) for e, o_sc in zip(es, o_scs)) / sum(es)
    y = jnp.dot(a_out.astype(BF16), w_ref[:D_A, :], preferred_element_type=F32)
    y = y + jnp.dot(o.astype(BF16), w_ref[D_A:, :], preferred_element_type=F32)
    out_ref[...] = h_ref[...] + y


def _outproj_prompt(z, gb, cw, os_, ls_, w16, h, tm, seq, dils):
    n = z.shape[0]
    tps = seq // tm
    row = lambda i: (i, 0)
    fix = lambda i: (0, 0)
    halo = lambda i: (jnp.maximum(i * (tm // 8) - 1, 0), 0)
    half = pl.BlockSpec((tm, D_A), row)
    plane = [pl.BlockSpec((1, dil, tm // dil, D_B), lambda i: (i // tps, 0, i % tps, 0))
             for dil in dils]
    return pl.pallas_call(
        functools.partial(_outproj_prompt_kernel, tiles_per_batch=tps, dils=dils),
        grid=(n // tm,),
        in_specs=[half, pl.BlockSpec((8, D_A), halo), half, pl.BlockSpec((3, D_A), fix),
                  *plane, *plane,
                  pl.BlockSpec((D_MODEL, D_MODEL), fix), pl.BlockSpec((tm, D_MODEL), row)],
        out_specs=pl.BlockSpec((tm, D_MODEL), row),
        out_shape=jax.ShapeDtypeStruct((n, D_MODEL), F32),
        scratch_shapes=[pltpu.VMEM((D_B // LANES, tm, LANES), F32)] * (2 * len(dils)),
        compiler_params=_params(1), name="outproj_prompt",
    )(z, z, gb, cw, *os_, *ls_, w16, h)


def _mix_sample_kernel(q_ref, kvn_ref, win_ref, z_ref, gb_ref, st_ref, cw_ref, cat_ref):
    s_len = q_ref.shape[1]
    rows = H_B * s_len
    npad = LANES
    q = q_ref[0]
    r_i = lax.broadcasted_iota(jnp.int32, (rows, D_B), 0)
    c_i = lax.broadcasted_iota(jnp.int32, (rows, D_B), 1)
    qt = jnp.concatenate([q] * H_B, axis=0)
    qbd = jnp.where(r_i // s_len == c_i // HEAD_DIM, qt, 0.0).astype(BF16)
    kvn = jnp.concatenate([kvn_ref[0], jnp.zeros((npad - s_len, 2 * D_B), F32)], axis=0)

    def multiplicity(delta):
        cnt = jnp.zeros(delta.shape, F32)
        for _, dil in BRANCHES:
            ok = (delta >= 0) & (delta <= BAND * dil) & ((delta & (dil - 1)) == 0)
            cnt = cnt + jnp.where(ok, 1.0, 0.0)
        return cnt

    lb = win_ref.shape[3]
    nt = (((1,), (1,)), ((), ()))
    tok_c = lax.broadcasted_iota(jnp.int32, (rows, lb), 0) % s_len
    cnt_c = multiplicity(tok_c + lb - lax.broadcasted_iota(jnp.int32, (rows, lb), 1))
    tok_n = lax.broadcasted_iota(jnp.int32, (rows, npad), 0) % s_len
    cnt_n = multiplicity(tok_n - lax.broadcasted_iota(jnp.int32, (rows, npad), 1))
    sc = jnp.dot(qbd, win_ref[0, 0].astype(BF16), preferred_element_type=F32)
    sn = lax.dot_general(qbd, kvn[:, :D_B].astype(BF16), nt, preferred_element_type=F32)
    sc = jnp.where(cnt_c > 0, sc, -jnp.inf)
    sn = jnp.where(cnt_n > 0, sn, -jnp.inf)
    mx = jnp.maximum(jnp.max(sc, axis=-1, keepdims=True), jnp.max(sn, axis=-1, keepdims=True))
    pc = cnt_c * jnp.exp(sc - mx)
    pn = cnt_n * jnp.exp(sn - mx)
    den = jnp.sum(pc, axis=-1, keepdims=True) + jnp.sum(pn, axis=-1, keepdims=True)
    of = lax.dot_general(pc.astype(BF16), win_ref[0, 1].astype(BF16), nt,
                         preferred_element_type=F32)
    of = (of + jnp.dot(pn.astype(BF16), kvn[:, D_B:].astype(BF16),
                       preferred_element_type=F32)) / den
    lane_head = lax.broadcasted_iota(jnp.int32, (s_len, D_B), 1) // HEAD_DIM
    o = jnp.zeros((s_len, D_B), F32)
    for h in range(H_B):
        o = o + jnp.where(lane_head == h, of[h * s_len:(h + 1) * s_len], 0.0)
    z = z_ref[0]
    st = st_ref[0]
    t = lax.broadcasted_iota(jnp.int32, (s_len, 1), 0)
    zm1 = jnp.where(t >= 1, pltpu.roll(z, 1, 0), st[1:2])
    zm2 = jnp.where(t >= 2, pltpu.roll(z, 2, 0), jnp.where(t == 1, st[1:2], st[0:1]))
    cw = cw_ref[...]
    cat_ref[0, :, :D_A] = gb_ref[0] * (cw[0:1] * zm2 + cw[1:2] * zm1 + cw[2:3] * z)
    cat_ref[0, :, D_A:] = o


def _mix_sample(q, kvn, win, z, gb, st, cw):
    b, s_len, _ = q.shape
    per = lambda i: (i,) + (0,) * 2
    blk = lambda a: pl.BlockSpec((1,) + a.shape[1:], per)
    return pl.pallas_call(
        _mix_sample_kernel,
        grid=(b,),
        in_specs=[blk(q), blk(kvn),
                  pl.BlockSpec((1,) + win.shape[1:], lambda i: (i, 0, 0, 0)),
                  blk(z), blk(gb), blk(st), pl.BlockSpec((3, D_A), lambda i: (0, 0))],
        out_specs=pl.BlockSpec((1, s_len, D_MODEL), per),
        out_shape=jax.ShapeDtypeStruct((b, s_len, D_MODEL), F32),
        compiler_params=_params(1), name="mix_sample",
    )(q, kvn, win, z, gb, st, cw)


def _linear_res_kernel(x_ref, w_ref, h_ref, out_ref):
    out_ref[...] = h_ref[...] + jnp.dot(x_ref[...].astype(BF16), w_ref[...],
                                        preferred_element_type=F32)


def _linear_res(x, w16, h):
    n, k = x.shape
    full = lambda a: pl.BlockSpec(a.shape, lambda i: (0, 0))
    return pl.pallas_call(
        _linear_res_kernel, grid=(1,),
        in_specs=[full(x), full(w16), full(h)], out_specs=full(h),
        out_shape=jax.ShapeDtypeStruct(h.shape, F32),
        compiler_params=_params(1), name="linear_res",
    )(x, w16, h)


def _swiglu_step(xn, wg_ref, wu_ref, wd_ref, acc_sc):
    a = jnp.dot(xn, wg_ref[0], preferred_element_type=F32)
    b = jnp.dot(xn, wu_ref[0], preferred_element_type=F32)
    t = (jax.nn.silu(a) * b).astype(BF16)
    acc_sc[...] += jnp.dot(t, wd_ref[0], preferred_element_type=F32)


def _ffn_dense_kernel(x_ref, g_ref, wg_ref, wu_ref, wd_ref, out_ref, xn_sc, acc_sc):
    f = pl.program_id(1)

    @pl.when(f == 0)
    def _():
        xn_sc[...] = _rms_bf16(x_ref[...], g_ref[...])
        acc_sc[...] = jnp.zeros_like(acc_sc)

    _swiglu_step(xn_sc[...], wg_ref, wu_ref, wd_ref, acc_sc)

    @pl.when(f == pl.num_programs(1) - 1)
    def _():
        out_ref[...] = x_ref[...] + acc_sc[...]


def _ffn_dense(x, g, wg, wu, wd, *, tm, fc):
    n = x.shape[0]
    ff = wg.shape[2]
    row = lambda i, f: (i, 0)
    return pl.pallas_call(
        _ffn_dense_kernel, grid=(n // tm, ff // fc),
        in_specs=[pl.BlockSpec((tm, D_MODEL), row), pl.BlockSpec((1, D_MODEL), lambda i, f: (0, 0)),
                  pl.BlockSpec((1, D_MODEL, fc), lambda i, f: (0, 0, f)),
                  pl.BlockSpec((1, D_MODEL, fc), lambda i, f: (0, 0, f)),
                  pl.BlockSpec((1, fc, D_MODEL), lambda i, f: (0, f, 0))],
        out_specs=pl.BlockSpec((tm, D_MODEL), row),
        out_shape=jax.ShapeDtypeStruct((n, D_MODEL), F32),
        scratch_shapes=[pltpu.VMEM((tm, D_MODEL), BF16), pltpu.VMEM((tm, D_MODEL), F32)],
        compiler_params=_params(2), name="ffn_dense",
    )(x, g, wg, wu, wd)


def _pack_bf16_pairs(x):
    half = x.shape[1] // 2
    lo = lax.bitcast_convert_type(x[:, :half].astype(BF16).astype(F32), jnp.int32)
    hi = lax.bitcast_convert_type(x[:, half:].astype(BF16).astype(F32), jnp.int32)
    return (hi & -65536) | lax.shift_right_logical(lo, 16)


def _unpack_bf16_pairs(p):
    lo = lax.bitcast_convert_type(p << 16, F32)
    hi = lax.bitcast_convert_type(p & -65536, F32)
    return jnp.concatenate([lo, hi], axis=1)


def _ffn_experts_kernel(te_ref, tv_ref, x_ref, wg_ref, wu_ref, wd_ref, out_ref, xn_sc, acc_sc):
    del te_ref
    i, f = pl.program_id(0), pl.program_id(1)
    n_valid = tv_ref[i]

    @pl.when(n_valid > 0)
    def _():
        @pl.when(f == 0)
        def _():
            row = lax.broadcasted_iota(jnp.int32, x_ref.shape, 0)
            xn_sc[...] = _unpack_bf16_pairs(jnp.where(row < n_valid, x_ref[...], 0)).astype(BF16)
            acc_sc[...] = jnp.zeros_like(acc_sc)

        def step(rows):
            xn = xn_sc[:rows]
            a = jnp.dot(xn, wg_ref[0].astype(BF16), preferred_element_type=F32)
            b = jnp.dot(xn, wu_ref[0].astype(BF16), preferred_element_type=F32)
            t = (jax.nn.silu(a) * b).astype(BF16)
            acc_sc[:rows] += jnp.dot(t, wd_ref[0].astype(BF16), preferred_element_type=F32)

        tm = xn_sc.shape[0]
        pl.when(n_valid > tm // 2)(lambda: step(tm))
        pl.when(n_valid <= tm // 2)(lambda: step(tm // 2))

        @pl.when(f == pl.num_programs(1) - 1)
        def _():
            out_ref[...] = _pack_bf16_pairs(acc_sc[...])


def _ffn_experts(tile_expert, tile_valid, x, wg, wu, wd, *, tm, fc):
    n, wpk = x.shape
    ff = wg.shape[2]
    row = lambda i, f, te, tv: (i, 0)
    grid_spec = pltpu.PrefetchScalarGridSpec(
        num_scalar_prefetch=2, grid=(n // tm, ff // fc),
        in_specs=[pl.BlockSpec((tm, wpk), row),
                  pl.BlockSpec((1, D_MODEL, fc), lambda i, f, te, tv: (te[i], 0, f)),
                  pl.BlockSpec((1, D_MODEL, fc), lambda i, f, te, tv: (te[i], 0, f)),
                  pl.BlockSpec((1, fc, D_MODEL), lambda i, f, te, tv: (te[i], f, 0))],
        out_specs=pl.BlockSpec((tm, wpk), row),
        scratch_shapes=[pltpu.VMEM((tm, D_MODEL), BF16), pltpu.VMEM((tm, D_MODEL), F32)])
    return pl.pallas_call(
        _ffn_experts_kernel, grid_spec=grid_spec,
        out_shape=jax.ShapeDtypeStruct((n, wpk), jnp.int32),
        compiler_params=_params(2), name="ffn_experts",
    )(tile_expert, tile_valid, x, wg, wu, wd)


def _sc_worker_id():
    return lax.axis_index("s") * SC_CORES + lax.axis_index("c")


def _sc_dispatch(x_a, x_b, slots_a, slots_b, n_slots):
    w = x_a.shape[1]
    per_a = x_a.shape[0] // SC_WORKERS
    per_b = x_b.shape[0] // SC_WORKERS
    ch = min(SC_CHUNK // 2, per_a)
    n_chunks = per_a // ch
    mesh = plsc.VectorSubcoreMesh(core_axis_name="c", subcore_axis_name="s")
    dma = pltpu.SemaphoreType.DMA

    @functools.partial(
        pl.kernel, mesh=mesh, out_type=jax.ShapeDtypeStruct((n_slots, w), x_a.dtype),
        scratch_types=[pltpu.VMEM((per_a,), jnp.int32), pltpu.VMEM((per_a,), jnp.int32),
                       pltpu.VMEM((ch, w), x_a.dtype), pltpu.VMEM((ch, w), x_a.dtype),
                       pltpu.VMEM((per_b,), jnp.int32), pltpu.VMEM((per_b, w), x_b.dtype),
                       dma, dma, dma, dma, dma, dma])
    def dispatch(xa_hbm, xb_hbm, a0_hbm, a1_hbm, b0_hbm, b1_hbm, out_hbm,
                 idx0_v, idx1_v, rows0_v, rows1_v, idxb_v, rowsb_v,
                 lsem0, lsem1, s0sem0, s0sem1, s1sem0, s1sem1):
        wid = _sc_worker_id()
        base_a = pl.multiple_of(wid * per_a, 8)
        pltpu.sync_copy(a0_hbm.at[pl.ds(base_a, per_a)], idx0_v)
        pltpu.sync_copy(a1_hbm.at[pl.ds(base_a, per_a)], idx1_v)
        rows, lsem = (rows0_v, rows1_v), (lsem0, lsem1)
        ssem = ((s0sem0, s0sem1), (s1sem0, s1sem1))

        def load(c):
            return pltpu.async_copy(xa_hbm.at[pl.ds(base_a + c * ch, ch)], rows[c % 2], lsem[c % 2])

        def scatter(c):
            return [pltpu.async_copy(rows[c % 2], out_hbm.at[idx_v.at[pl.ds(c * ch, ch)]],
                                     ssem[k][c % 2])
                    for k, idx_v in enumerate((idx0_v, idx1_v))]

        loads, scatters = {0: load(0)}, {}
        for c in range(n_chunks):
            loads[c].wait()
            if c + 1 < n_chunks:
                for cp in scatters.pop(c - 1, ()):
                    cp.wait()
                loads[c + 1] = load(c + 1)
            scatters[c] = scatter(c)
        for cps in scatters.values():
            for cp in cps:
                cp.wait()

        base_b = pl.multiple_of(wid * per_b, 8)
        pltpu.sync_copy(xb_hbm.at[pl.ds(base_b, per_b)], rowsb_v)
        for slots_hbm in (b0_hbm, b1_hbm):
            pltpu.sync_copy(slots_hbm.at[pl.ds(base_b, per_b)], idxb_v)
            pltpu.sync_copy(rowsb_v, out_hbm.at[idxb_v])

    return dispatch(x_a, x_b, *slots_a, *slots_b)


def _sc_gather(table, idx):
    _, w = table.shape
    b = idx.shape[0]
    per_w = b // SC_WORKERS
    ch = max(c for c in range(8, SC_CHUNK + 1, 8) if per_w % c == 0)
    n_chunks = per_w // ch
    assert per_w * SC_WORKERS == b
    mesh = plsc.VectorSubcoreMesh(core_axis_name="c", subcore_axis_name="s")
    dma = pltpu.SemaphoreType.DMA

    @functools.partial(
        pl.kernel, mesh=mesh, out_type=jax.ShapeDtypeStruct((b, w), table.dtype),
        scratch_types=[pltpu.VMEM((per_w,), jnp.int32), pltpu.VMEM((ch, w), table.dtype),
                       pltpu.VMEM((ch, w), table.dtype), dma, dma, dma, dma])
    def gather(table_hbm, idx_hbm, out_hbm, idx_v, rows0_v, rows1_v, gsem0, gsem1, wsem0, wsem1):
        wid = _sc_worker_id()
        base = pl.multiple_of(wid * per_w, 8)
        pltpu.sync_copy(idx_hbm.at[pl.ds(base, per_w)], idx_v)
        rows, gsem, wsem = (rows0_v, rows1_v), (gsem0, gsem1), (wsem0, wsem1)

        def fetch(c):
            return pltpu.async_copy(table_hbm.at[idx_v.at[pl.ds(c * ch, ch)]], rows[c % 2],
                                    gsem[c % 2])

        def write(c):
            return pltpu.async_copy(rows[c % 2], out_hbm.at[pl.ds(base + c * ch, ch)], wsem[c % 2])

        fetches, writes = {0: fetch(0)}, {}
        for c in range(n_chunks):
            fetches[c].wait()
            if c + 1 < n_chunks:
                if c >= 1:
                    writes.pop(c - 1).wait()
                fetches[c + 1] = fetch(c + 1)
            writes[c] = write(c)
        for cp in writes.values():
            cp.wait()

    return gather(table, idx)


def _combine_kernel(h_ref, sel_ref, y0_ref, y1_ref, out_ref):
    sel = sel_ref[...]
    out_ref[...] = (h_ref[...] + sel[:, 2:3] * _unpack_bf16_pairs(y0_ref[...])
                    + sel[:, 3:4] * _unpack_bf16_pairs(y1_ref[...]))


def _combine(h, sel, yb, blk_off, tm):
    n = h.shape[0]
    wpk = yb.shape[1]
    half_blocks = yb.shape[0] // 2 // tm
    return pl.pallas_call(
        _combine_kernel, grid=(n // tm,),
        in_specs=[pl.BlockSpec((tm, D_MODEL), lambda i: (i, 0)),
                  pl.BlockSpec((tm, LANES), lambda i: (i, 0)),
                  pl.BlockSpec((tm, wpk), lambda i: (i + blk_off, 0)),
                  pl.BlockSpec((tm, wpk), lambda i: (i + blk_off + half_blocks, 0))],
        out_specs=pl.BlockSpec((tm, D_MODEL), lambda i: (i, 0)),
        out_shape=jax.ShapeDtypeStruct((n, D_MODEL), F32),
        compiler_params=_params(1), name="combine",
    )(h, sel, yb, yb)


def _route_plan(sels, counts, te_rows, n_tiles):
    counts = counts[0, :N_EXPERTS].astype(jnp.int32)
    padded = (counts + te_rows - 1) // te_rows * te_rows
    ends = jnp.cumsum(padded)
    starts = ends - padded
    experts = jnp.arange(N_EXPERTS, dtype=jnp.int32)

    def slots_of(sel, k):
        e = sel[:, k].astype(jnp.int32)
        start = jnp.sum(jnp.where(e[:, None] == experts[None, :], starts[None, :], 0), axis=1)
        return start + sel[:, 4 + k].astype(jnp.int32)

    slots = [(slots_of(sel, 0), slots_of(sel, 1)) for sel in sels]
    tile_start = jnp.arange(n_tiles, dtype=jnp.int32) * te_rows
    tile_expert = jnp.minimum(jnp.sum(tile_start[:, None] >= ends[None, :], axis=1),
                              N_EXPERTS - 1).astype(jnp.int32)
    in_group = tile_start - starts[tile_expert]
    tile_valid = jnp.clip(counts[tile_expert] - in_group, 0, te_rows)
    tile_valid = jnp.where(tile_start < ends[-1], tile_valid, 0).astype(jnp.int32)
    return slots, tile_expert, tile_valid


def _gmlp_kernel(h_ref, g_ref, wuv_ref, vg_ref, mix_ref, bs_ref, wo_ref, out_ref, v_ref, um_sc,
                 *, period):
    h = h_ref[...]
    tm = h.shape[0]
    ln = mix_ref.shape[1]
    gw = D_MODEL // C_GROUPS
    xn = _rms_bf16(h, g_ref[...])
    vv = jax.nn.gelu(jnp.dot(xn, wuv_ref[:, D_MODEL:], preferred_element_type=F32))
    v = vv * lax.rsqrt(jnp.mean(vv * vv, axis=-1, keepdims=True) + EPS) * vg_ref[...]
    v_ref[0] = v[tm - v_ref.shape[1]:]
    vb = v.astype(BF16)
    u = jax.nn.gelu(jnp.dot(xn, wuv_ref[:, :D_MODEL], preferred_element_type=F32))
    r = lax.broadcasted_iota(jnp.int32, (ln, ln), 0)
    c = lax.broadcasted_iota(jnp.int32, (ln, ln), 1)
    causal = (r // period == c // period) & (c % period <= r % period)
    for gi in range(C_GROUPS):
        wm = jnp.where(causal, mix_ref[gi], 0.0).astype(BF16)
        bias = bs_ref[:, gi:gi + 1]
        for ch in range(tm // ln):
            rs = slice(ch * ln, (ch + 1) * ln)
            cs = slice(gi * gw, (gi + 1) * gw)
            mixed = jnp.dot(wm, vb[rs, cs], preferred_element_type=F32) + bias
            um_sc[rs, cs] = (u[rs, cs] * mixed).astype(BF16)
    out_ref[...] = h + jnp.dot(um_sc[...], wo_ref[...], preferred_element_type=F32)


def _gmlp(h, g, wuv16, vg, mix, bs, wo16, *, tm, period, seq, tail):
    n = h.shape[0]
    tps = seq // tm
    row = lambda i: (i, 0)
    full = lambda a: pl.BlockSpec(a.shape, lambda i: (0,) * a.ndim)
    return pl.pallas_call(
        functools.partial(_gmlp_kernel, period=period),
        grid=(n // tm,),
        in_specs=[pl.BlockSpec((tm, D_MODEL), row), full(g), full(wuv16), full(vg), full(mix),
                  full(bs), full(wo16)],
        out_specs=[pl.BlockSpec((tm, D_MODEL), row),
                   pl.BlockSpec((1, tail, D_MODEL), lambda i: (i // tps, 0, 0))],
        out_shape=[jax.ShapeDtypeStruct((n, D_MODEL), F32),
                   jax.ShapeDtypeStruct((n // seq, tail, D_MODEL), F32)],
        scratch_shapes=[pltpu.VMEM((tm, D_MODEL), BF16)],
        compiler_params=_params(1), name="gmlp",
    )(h, g, wuv16, vg, mix, bs, wo16)


def _router_kernel(h_ref, g_ref, wr_ref, cin_ref, xn_ref, sel_ref, cnt_ref):
    @pl.when(pl.program_id(0) == 0)
    def _():
        cnt_ref[...] = cin_ref[...]

    x = h_ref[...]
    xn = x * lax.rsqrt(jnp.mean(x * x, axis=-1, keepdims=True) + EPS) * g_ref[...]
    xn_ref[...] = _pack_bf16_pairs(xn)
    x_hi = xn.astype(BF16)
    x_lo = (xn - x_hi.astype(F32)).astype(BF16)
    w = wr_ref[...]
    w_hi = w.astype(BF16)
    w_lo = (w - w_hi.astype(F32)).astype(BF16)
    logits = (jnp.dot(x_hi, w_hi, preferred_element_type=F32)
              + (jnp.dot(x_lo, w_hi, preferred_element_type=F32)
                 + jnp.dot(x_hi, w_lo, preferred_element_type=F32)))
    lane = lax.broadcasted_iota(jnp.int32, logits.shape, 1)
    lanef = lane.astype(F32)
    logits = jnp.where(lane < N_EXPERTS, logits, -jnp.inf)
    m1 = jnp.max(logits, axis=-1, keepdims=True)
    i1 = jnp.min(jnp.where(logits == m1, lanef, float(LANES)), axis=-1, keepdims=True)
    rest = jnp.where(lanef == i1, -jnp.inf, logits)
    m2 = jnp.max(rest, axis=-1, keepdims=True)
    i2 = jnp.min(jnp.where(rest == m2, lanef, float(LANES)), axis=-1, keepdims=True)
    e2 = jnp.exp(m2 - m1)
    w1 = 1.0 / (1.0 + e2)
    w2 = e2 / (1.0 + e2)
    tm = x.shape[0]
    hit1, hit2 = lanef == i1, lanef == i2
    picked = jnp.where(hit1 | hit2, 1.0, 0.0)
    r_i = lax.broadcasted_iota(jnp.int32, (tm, tm), 0)
    c_i = lax.broadcasted_iota(jnp.int32, (tm, tm), 1)
    below = jnp.where(c_i < r_i, 1.0, 0.0).astype(BF16)
    before = jnp.dot(below, picked.astype(BF16), preferred_element_type=F32) + cnt_ref[...]
    rank1 = jnp.sum(jnp.where(hit1, before, 0.0), axis=-1, keepdims=True)
    rank2 = jnp.sum(jnp.where(hit2, before, 0.0), axis=-1, keepdims=True)
    cnt_ref[...] += jnp.sum(picked, axis=0, keepdims=True)
    vals = (i1, i2, w1, w2, rank1, rank2)
    sel = jnp.zeros_like(logits)
    for li, val in enumerate(vals):
        sel = jnp.where(lane == li, val, sel)
    sel_ref[...] = sel


def _router(h, g, wr_pad, counts_in, tm):
    n = h.shape[0]
    row = lambda i: (i, 0)
    fix = lambda i: (0, 0)
    return pl.pallas_call(
        _router_kernel, grid=(n // tm,),
        in_specs=[pl.BlockSpec((tm, D_MODEL), row), pl.BlockSpec((1, D_MODEL), fix),
                  pl.BlockSpec((D_MODEL, LANES), fix), pl.BlockSpec((1, LANES), fix)],
        out_specs=[pl.BlockSpec((tm, D_MODEL // 2), row), pl.BlockSpec((tm, LANES), row),
                   pl.BlockSpec((1, LANES), fix)],
        out_shape=[jax.ShapeDtypeStruct((n, D_MODEL // 2), jnp.int32),
                   jax.ShapeDtypeStruct((n, LANES), F32),
                   jax.ShapeDtypeStruct((1, LANES), F32)],
        compiler_params=_params(1), name="router",
    )(h, g, wr_pad, counts_in)


def _rope_tables(pos):
    half = HEAD_DIM // 2
    inv = ROPE_THETA ** (-jnp.arange(half, dtype=F32) / half)
    ang = pos.astype(F32)[:, None] * inv[None, :]
    cos, sin = jnp.cos(ang), jnp.sin(ang)
    zero = jnp.zeros_like(sin)
    rep = lambda a, b: jnp.tile(jnp.concatenate([a, b], axis=-1), (1, LANES // HEAD_DIM))
    return rep(cos, cos), rep(-sin, zero), rep(zero, sin)


def _tile_for(n, pref):
    return pref if n % pref == 0 else n


def kernel(x_prompt, x_sample, state_conv, cache_kv_win, norm_mix_e, w_in_e, conv_w, q_gain,
           k_gain, w_out_e, norm_ffn_e, ffn_w_gate, ffn_w_up, ffn_w_down, norm_mix_o, w_uv,
           v_gain, w_s, b_s, w_out_o, norm_moe, w_router, moe_w_gate, moe_w_up, moe_w_down):
    bp, seq, _ = x_prompt.shape
    bs_, s_len, _ = x_sample.shape
    n_p, n_s = bp * seq, bs_ * s_len
    hp = x_prompt.reshape(n_p, D_MODEL)
    hs = x_sample.reshape(n_s, D_MODEL)
    tm_p = 512
    row = lambda a: a.reshape(1, -1)

    tabs_p = _rope_tables(jnp.arange(seq, dtype=jnp.int32))
    tabs_s = tuple(jnp.tile(t, (bs_, 1)) for t in
                   _rope_tables(PAST_LEN + jnp.arange(s_len, dtype=jnp.int32)))
    hd = jnp.arange(256) // HEAD_DIM
    bd = (hd[:, None] == hd[None, :]).astype(BF16)

    w_in16 = w_in_e[0].astype(BF16)
    w_out16 = w_out_e[0].astype(BF16)
    qg = jnp.tile(q_gain[0], 4).reshape(1, 256)
    kg = jnp.tile(k_gain[0], 4).reshape(1, 256)
    g0 = row(norm_mix_e[0])

    dils = tuple(dil for _, dil in BRANCHES)
    lw = min(2048, seq)
    z_p, gb_p, kvt_p, *planes = _inproj(hp, g0, w_in16, tabs_p, qg, kg, bd, tm_p, seq, dils, lw)
    z_s, gb_s, kv_s, q_s = _inproj(hs, g0, w_in16, tabs_s, qg, kg, bd, n_s, n_s, ())

    os_, ls_ = [], []
    for di, dil in enumerate(dils):
        m = seq // dil
        q_d, k_d, v_d = (a.reshape(bp * dil, m, D_B) for a in planes[3 * di:3 * di + 3])
        o, lse = _attn_prompt(q_d, k_d, v_d, qb=min(256, m))
        os_.append(o.reshape(bp, dil, m, D_B))
        ls_.append(lse.reshape(bp, dil, m, D_B))
    hp = _outproj_prompt(z_p, gb_p, conv_w[0], os_, ls_, w_out16, hp, tm_p, seq, dils)

    cat_s = _mix_sample(q_s.reshape(bs_, s_len, D_B),
                        kv_s.reshape(bs_, s_len, 2 * D_B),
                        jnp.transpose(cache_kv_win[0], (0, 2, 3, 4, 1)).reshape(
                            bs_, 2, D_B, -1),
                        z_s.reshape(bs_, s_len, D_A), gb_s.reshape(bs_, s_len, D_A),
                        state_conv[0], conv_w[0])
    hs = _linear_res(cat_s.reshape(n_s, D_MODEL), w_out16, hs)

    conv_prompt = z_p.reshape(bp, seq, D_A)[:, seq - 2:][None]
    conv_sample = z_s.reshape(bs_, s_len, D_A)[:, s_len - 2:][None]
    kv_prompt = jnp.transpose(kvt_p.reshape(bp, 2, H_B, HEAD_DIM, lw), (0, 4, 1, 2, 3))[None]
    kv_sample = kv_s.reshape(bs_, s_len, 2, H_B, HEAD_DIM)[None]

    fg, fu, fd = (ffn_w_gate.astype(BF16), ffn_w_up.astype(BF16), ffn_w_down.astype(BF16))
    gf = row(norm_ffn_e[0])
    hp = _ffn_dense(hp, gf, fg, fu, fd, tm=tm_p, fc=1408)
    hs = _ffn_dense(hs, gf, fg, fu, fd, tm=n_s, fc=1408)

    wuv16 = w_uv[0].astype(BF16)
    wo16 = w_out_o[0].astype(BF16)
    g1, vg = row(norm_mix_o[0]), row(v_gain[0])
    tail = seq - ((seq - 1) // CHUNK) * CHUNK
    hp, v_tail_p = _gmlp(hp, g1, wuv16, vg, w_s[0], b_s[0].T, wo16, tm=tm_p, period=CHUNK,
                         seq=seq, tail=tail)
    mix_s = jnp.tile(w_s[0][:, :s_len, :s_len], (1, bs_, bs_))
    bias_s = jnp.tile(b_s[0][:, :s_len].T, (bs_, 1))
    hs, v_all_s = _gmlp(hs, g1, wuv16, vg, mix_s, bias_s, wo16, tm=n_s, period=s_len,
                        seq=n_s, tail=n_s)
    v_prompt = v_tail_p[None]
    v_sample = v_all_s.reshape(bs_, s_len, D_MODEL)[None]

    wr_pad = jnp.pad(w_router[0], ((0, 0), (0, LANES - N_EXPERTS)))
    gm = row(norm_moe[0])
    xn_p, sel_p, counts = _router(hp, gm, wr_pad, jnp.zeros((1, LANES), F32), tm_p)
    xn_s, sel_s, counts = _router(hs, gm, wr_pad, counts, n_s)
    n_tok = n_p + n_s
    n_tiles = -(-(2 * n_tok + N_EXPERTS * (MOE_ROWS - 1)) // MOE_ROWS)
    (slots_p, slots_s), tile_expert, tile_valid = _route_plan(
        (sel_p, sel_s), counts, MOE_ROWS, n_tiles)
    slot0 = jnp.concatenate([slots_p[0], slots_s[0]])
    slot1 = jnp.concatenate([slots_p[1], slots_s[1]])
    xs = _sc_dispatch(xn_p, xn_s, slots_p, slots_s, n_tiles * MOE_ROWS)
    ys = _ffn_experts(tile_expert, tile_valid, xs, moe_w_gate[0], moe_w_up[0], moe_w_down[0],
                      tm=MOE_ROWS, fc=896)
    yb = _sc_gather(ys, jnp.concatenate([slot0, slot1]))
    hp = _combine(hp, sel_p, yb, 0, n_s)
    hs = _combine(hs, sel_s, yb, n_p // n_s, n_s)

    return (hp.reshape(bp, seq, D_MODEL), hs.reshape(bs_, s_len, D_MODEL),
            conv_prompt, conv_sample, kv_prompt, kv_sample, v_prompt, v_sample)
```

```python
import functools

import jax
import jax.numpy as jnp
from jax import lax
from jax.experimental import pallas as pl
from jax.experimental.pallas import tpu as pltpu
from jax.experimental.pallas import tpu_sc as plsc

F32 = jnp.float32
BF16 = jnp.bfloat16

D_MODEL = 1024
D_A = 512
H_B = 8
HEAD_DIM = 64
D_B = H_B * HEAD_DIM
BRANCHES = ((128, 1), (512, 4), (2048, 16))
BAND = 128
ROPE_THETA = 10000.0
PAST_LEN = 16384
N_EXPERTS = 8
C_GROUPS = 4
CHUNK = 128
EPS = 1e-6
LANES = 128
VMEM_LIMIT = 56 * 1024 * 1024
SC_CORES = 2
SC_WORKERS = SC_CORES * 16
SC_CHUNK = 128
MOE_ROWS = 1024


def _params(n_axes):
    return pltpu.CompilerParams(dimension_semantics=("arbitrary",) * n_axes,
                                vmem_limit_bytes=VMEM_LIMIT)


def _rms_bf16(x, g):
    ms = jnp.mean(x * x, axis=-1, keepdims=True)
    return (x * lax.rsqrt(ms + EPS) * g).astype(BF16)


def _inproj_kernel(x_ref, g_ref, w_ref, cos_ref, sina_ref, sinb_ref, qg_ref, kg_ref, bd_ref,
                   z_ref, gb_ref, kv_ref, *rest, dils, win_tiles, tiles_per_seq):
    if dils:
        planes, qkv_sc = rest[:-1], rest[-1]
    else:
        (q_out,) = rest
    xn = _rms_bf16(x_ref[...], g_ref[...])

    def col(j):
        return jnp.dot(xn, w_ref[:, j * 512:(j + 1) * 512], preferred_element_type=F32)

    z_ref[...] = col(1) * col(0)
    gb_ref[...] = col(2).astype(gb_ref.dtype)
    cos, sina, sinb = cos_ref[...], sina_ref[...], sinb_ref[...]

    def head_norm_rope(t, gain):
        outs = []
        for c2 in range(2):
            tc = t[:, c2 * 256:(c2 + 1) * 256]
            ss = jnp.dot((tc * tc).astype(BF16), bd_ref[...], preferred_element_type=F32)
            tn = tc * lax.rsqrt(ss * (1.0 / HEAD_DIM) + EPS) * gain
            for c in range(2):
                u = tn[:, c * LANES:(c + 1) * LANES]
                outs.append(u * cos + pltpu.roll(u, LANES - 32, 1) * sina
                            + pltpu.roll(u, 32, 1) * sinb)
        return outs

    q = [t * (HEAD_DIM ** -0.5) for t in head_norm_rope(col(3), qg_ref[...])]
    k = head_norm_rope(col(4), kg_ref[...])
    v = col(5)
    if not dils:
        for c in range(4):
            kv_ref[:, c * LANES:(c + 1) * LANES] = k[c]
            q_out[:, c * LANES:(c + 1) * LANES] = q[c]
        kv_ref[:, D_B:] = v
        return

    @pl.when(pl.program_id(0) % tiles_per_seq >= tiles_per_seq - win_tiles)
    def _():
        for c in range(4):
            kv_ref[0, c * LANES:(c + 1) * LANES, :] = k[c].T
            kv_ref[0, D_B + c * LANES:D_B + (c + 1) * LANES, :] = v[:, c * LANES:(c + 1) * LANES].T

    tm = x_ref.shape[0]
    for c in range(4):
        qkv_sc[0, c] = q[c]
        qkv_sc[1, c] = k[c]
        qkv_sc[2, c] = v[:, c * LANES:(c + 1) * LANES]
    for di, dil in enumerate(dils):
        for r in range(dil):
            rows = pl.ds(r, tm // dil, stride=dil) if dil > 1 else slice(None)
            for a in range(3):
                for c in range(4):
                    planes[3 * di + a][0, r, :, c * LANES:(c + 1) * LANES] = (
                        qkv_sc[a, c, rows, :].astype(BF16))


def _inproj(x, g, w16, tabs, qg, kg, bd, tm, seq, dils, win=0):
    n = x.shape[0]
    cos, sina, sinb = tabs
    tt = cos.shape[0] // tm
    tps = seq // tm
    win_tiles = win // tm
    row = lambda i: (i, 0)
    fix = lambda i: (0, 0)
    tab = lambda i: (i % tt, 0)
    out_specs = [pl.BlockSpec((tm, D_A), row), pl.BlockSpec((tm, D_A), row)]
    out_shape = [jax.ShapeDtypeStruct((n, D_A), F32),
                 jax.ShapeDtypeStruct((n, D_A), BF16 if dils else F32)]
    if dils:
        out_specs.append(pl.BlockSpec(
            (1, 2 * D_B, tm),
            lambda i: (i // tps, 0, jnp.maximum(i % tps - (tps - win_tiles), 0))))
        out_shape.append(jax.ShapeDtypeStruct((n // seq, 2 * D_B, win), F32))
        scratch = [pltpu.VMEM((3, D_B // LANES, tm, LANES), F32)]
    else:
        out_specs += [pl.BlockSpec((tm, 2 * D_B), row), pl.BlockSpec((tm, D_B), row)]
        out_shape += [jax.ShapeDtypeStruct((n, 2 * D_B), F32),
                      jax.ShapeDtypeStruct((n, D_B), F32)]
        scratch = []
    for dil in dils:
        for _ in range(3):
            out_specs.append(pl.BlockSpec((1, dil, tm // dil, D_B),
                                          lambda i: (i // tps, 0, i % tps, 0)))
            out_shape.append(jax.ShapeDtypeStruct((n // seq, dil, seq // dil, D_B), BF16))
    return pl.pallas_call(
        functools.partial(_inproj_kernel, dils=dils, win_tiles=win_tiles, tiles_per_seq=tps),
        grid=(n // tm,),
        in_specs=[pl.BlockSpec((tm, D_MODEL), row), pl.BlockSpec((1, D_MODEL), fix),
                  pl.BlockSpec(w16.shape, fix),
                  pl.BlockSpec((tm, LANES), tab), pl.BlockSpec((tm, LANES), tab),
                  pl.BlockSpec((tm, LANES), tab),
                  pl.BlockSpec((1, 256), fix), pl.BlockSpec((1, 256), fix),
                  pl.BlockSpec((256, 256), fix)],
        out_specs=out_specs, out_shape=out_shape, scratch_shapes=scratch,
        compiler_params=_params(1), name="inproj",
    )(x, g, w16, cos, sina, sinb, qg, kg, bd)


def _attn_prompt_kernel(q_ref, kc_ref, kp_ref, vc_ref, vp_ref, o_ref, lse_ref, *, qb):
    j = pl.program_id(1)
    nsub = qb // BAND
    lane = lax.broadcasted_iota(jnp.int32, (1, LANES), 1)
    head0 = lane < HEAD_DIM
    qi = lax.broadcasted_iota(jnp.int32, (2 * BAND, 2 * BAND), 0) % BAND + BAND
    kj = lax.broadcasted_iota(jnp.int32, (2 * BAND, 2 * BAND), 1)
    dist = qi - kj
    band = (dist >= 0) & (dist <= BAND)
    first = band & ((kj >= BAND) | (j > 0))
    lanes = [slice(hp * LANES, (hp + 1) * LANES) for hp in range(D_B // LANES)]
    kcats = [jnp.concatenate([kp_ref[0, :, ls], kc_ref[0, :, ls]], axis=0) for ls in lanes]
    vcats = [jnp.concatenate([vp_ref[0, :, ls], vc_ref[0, :, ls]], axis=0) for ls in lanes]
    for sub in range(nsub):
        rows = slice(sub * BAND, (sub + 1) * BAND)
        lse_all = jnp.zeros((BAND, LANES), F32)
        for hp, ls in enumerate(lanes):
            qs = q_ref[0, rows, ls]
            ks = kcats[hp][sub * BAND:(sub + 2) * BAND]
            vs = vcats[hp][sub * BAND:(sub + 2) * BAND]
            q2 = jnp.concatenate([jnp.where(head0, qs, jnp.zeros_like(qs)),
                                  jnp.where(head0, jnp.zeros_like(qs), qs)], axis=0)
            s = lax.dot_general(q2, ks, (((1,), (1,)), ((), ())), preferred_element_type=F32)
            s = jnp.where(first if sub == 0 else band, s, -jnp.inf)
            mx = jnp.max(s, axis=-1, keepdims=True)
            p = jnp.exp(s - mx)
            l = jnp.sum(p, axis=-1, keepdims=True)
            pv = jnp.dot(p.astype(BF16), vs, preferred_element_type=F32) / l
            lse = mx + jnp.log(l)
            o_ref[0, rows, ls] = jnp.where(head0, pv[:BAND], pv[BAND:]).astype(BF16)
            lse_all = jnp.where(lane == 2 * hp, lse[:BAND],
                                jnp.where(lane == 2 * hp + 1, lse[BAND:], lse_all))
        lse_ref[0, rows, :] = lse_all


def _attn_prompt(q, k, v, qb):
    b, m, w = q.shape
    nsub = qb // BAND
    cur = lambda bi, j: (bi, j, 0)
    prev = lambda bi, j: (bi, jnp.maximum(j * nsub - 1, 0), 0)
    return pl.pallas_call(
        functools.partial(_attn_prompt_kernel, qb=qb),
        grid=(b, m // qb),
        in_specs=[pl.BlockSpec((1, qb, w), cur), pl.BlockSpec((1, qb, w), cur),
                  pl.BlockSpec((1, BAND, w), prev), pl.BlockSpec((1, qb, w), cur),
                  pl.BlockSpec((1, BAND, w), prev)],
        out_specs=[pl.BlockSpec((1, qb, w), cur), pl.BlockSpec((1, qb, LANES), cur)],
        out_shape=[jax.ShapeDtypeStruct((b, m, w), BF16),
                   jax.ShapeDtypeStruct((b, m, LANES), F32)],
        compiler_params=_params(2), name="attn_prompt",
    )(q, k, k, v, v)


def _outproj_prompt_kernel(z_ref, zh_ref, gb_ref, cw_ref, *rest, tiles_per_batch, dils):
    nb = len(dils)
    o_refs, l_refs = rest[:nb], rest[nb:2 * nb]
    w_ref, h_ref, out_ref = rest[2 * nb:2 * nb + 3]
    o_scs, l_scs = rest[2 * nb + 3:3 * nb + 3], rest[3 * nb + 3:]
    i = pl.program_id(0)
    z = z_ref[...]
    tm = z.shape[0]
    for dil, o_ref, l_ref, o_sc, l_sc in zip(dils, o_refs, l_refs, o_scs, l_scs):
        for r in range(dil):
            rows = pl.ds(r, tm // dil, stride=dil) if dil > 1 else slice(None)
            for c in range(D_B // LANES):
                ls = slice(c * LANES, (c + 1) * LANES)
                o_sc[c, rows, :] = o_ref[0, r, :, ls].astype(F32)
            l_sc[rows, :] = l_ref[0, r]
    unstage = lambda sc: jnp.concatenate([sc[c] for c in range(D_B // LANES)], axis=1)
    zh = jnp.where(i % tiles_per_batch == 0, 0.0, zh_ref[...])
    row = lax.broadcasted_iota(jnp.int32, (tm, 1), 0)
    zm1 = jnp.where(row >= 1, pltpu.roll(z, 1, 0), zh[7:8])
    zm2 = jnp.where(row >= 2, pltpu.roll(z, 2, 0), jnp.where(row == 1, zh[7:8], zh[6:7]))
    cw = cw_ref[...]
    a_out = gb_ref[...].astype(F32) * (cw[0:1] * zm2 + cw[1:2] * zm1 + cw[2:3] * z)
    lses = [l_sc[...] for l_sc in l_scs]
    lm = functools.reduce(jnp.maximum, lses)
    es = [jnp.exp(l - lm) for l in lses]
    tot = sum(es)
    spread = (lax.broadcasted_iota(jnp.int32, (LANES, D_B), 1) // HEAD_DIM
              == lax.broadcasted_iota(jnp.int32, (LANES, D_B), 0)).astype(BF16)

    def widen(wgt):
        hi = wgt.astype(BF16)
        lo = (wgt - hi.astype(F32)).astype(BF16)
        return (jnp.dot(hi, spread, preferred_element_type=F32)
                + jnp.dot(lo, spread, preferred_element_type=F32))

    o = sum(widen(e / tot) * unstage(o_sc) for e, o_sc in zip(es, o_scs))
    y = jnp.dot(a_out.astype(BF16), w_ref[:D_A, :], preferred_element_type=F32)
    y = y + jnp.dot(o.astype(BF16), w_ref[D_A:, :], preferred_element_type=F32)
    out_ref[...] = h_ref[...] + y


def _outproj_prompt(z, gb, cw, os_, ls_, w16, h, tm, seq, dils):
    n = z.shape[0]
    tps = seq // tm
    row = lambda i: (i, 0)
    fix = lambda i: (0, 0)
    halo = lambda i: (jnp.maximum(i * (tm // 8) - 1, 0), 0)
    half = pl.BlockSpec((tm, D_A), row)
    plane = lambda width: [pl.BlockSpec((1, dil, tm // dil, width),
                                        lambda i: (i // tps, 0, i % tps, 0)) for dil in dils]
    return pl.pallas_call(
        functools.partial(_outproj_prompt_kernel, tiles_per_batch=tps, dils=dils),
        grid=(n // tm,),
        in_specs=[half, pl.BlockSpec((8, D_A), halo), half, pl.BlockSpec((3, D_A), fix),
                  *plane(D_B), *plane(LANES),
                  pl.BlockSpec((D_MODEL, D_MODEL), fix), pl.BlockSpec((tm, D_MODEL), row)],
        out_specs=pl.BlockSpec((tm, D_MODEL), row),
        out_shape=jax.ShapeDtypeStruct((n, D_MODEL), F32),
        scratch_shapes=([pltpu.VMEM((D_B // LANES, tm, LANES), F32)] * len(dils)
                        + [pltpu.VMEM((tm, LANES), F32)] * len(dils)),
        compiler_params=_params(1), name="outproj_prompt",
    )(z, z, gb, cw, *os_, *ls_, w16, h)


def _mix_sample_kernel(q_ref, kvn_ref, win_ref, z_ref, gb_ref, st_ref, cw_ref, cat_ref):
    s_len = q_ref.shape[1]
    rows = H_B * s_len
    npad = LANES
    q = q_ref[0]
    r_i = lax.broadcasted_iota(jnp.int32, (rows, D_B), 0)
    c_i = lax.broadcasted_iota(jnp.int32, (rows, D_B), 1)
    qt = jnp.concatenate([q] * H_B, axis=0)
    qbd = jnp.where(r_i // s_len == c_i // HEAD_DIM, qt, 0.0).astype(BF16)
    kvn = jnp.concatenate([kvn_ref[0], jnp.zeros((npad - s_len, 2 * D_B), F32)], axis=0)

    def multiplicity(delta):
        cnt = jnp.zeros(delta.shape, F32)
        for _, dil in BRANCHES:
            ok = (delta >= 0) & (delta <= BAND * dil) & ((delta & (dil - 1)) == 0)
            cnt = cnt + jnp.where(ok, 1.0, 0.0)
        return cnt

    lb = win_ref.shape[3]
    nt = (((1,), (1,)), ((), ()))
    tok_c = lax.broadcasted_iota(jnp.int32, (rows, lb), 0) % s_len
    cnt_c = multiplicity(tok_c + lb - lax.broadcasted_iota(jnp.int32, (rows, lb), 1))
    tok_n = lax.broadcasted_iota(jnp.int32, (rows, npad), 0) % s_len
    cnt_n = multiplicity(tok_n - lax.broadcasted_iota(jnp.int32, (rows, npad), 1))
    sc = jnp.dot(qbd, win_ref[0, 0].astype(BF16), preferred_element_type=F32)
    sn = lax.dot_general(qbd, kvn[:, :D_B].astype(BF16), nt, preferred_element_type=F32)
    sc = jnp.where(cnt_c > 0, sc, -jnp.inf)
    sn = jnp.where(cnt_n > 0, sn, -jnp.inf)
    mx = jnp.maximum(jnp.max(sc, axis=-1, keepdims=True), jnp.max(sn, axis=-1, keepdims=True))
    pc = cnt_c * jnp.exp(sc - mx)
    pn = cnt_n * jnp.exp(sn - mx)
    den = jnp.sum(pc, axis=-1, keepdims=True) + jnp.sum(pn, axis=-1, keepdims=True)
    of = lax.dot_general(pc.astype(BF16), win_ref[0, 1].astype(BF16), nt,
                         preferred_element_type=F32)
    of = (of + jnp.dot(pn.astype(BF16), kvn[:, D_B:].astype(BF16),
                       preferred_element_type=F32)) / den
    lane_head = lax.broadcasted_iota(jnp.int32, (s_len, D_B), 1) // HEAD_DIM
    o = jnp.zeros((s_len, D_B), F32)
    for h in range(H_B):
        o = o + jnp.where(lane_head == h, of[h * s_len:(h + 1) * s_len], 0.0)
    z = z_ref[0]
    st = st_ref[0]
    t = lax.broadcasted_iota(jnp.int32, (s_len, 1), 0)
    zm1 = jnp.where(t >= 1, pltpu.roll(z, 1, 0), st[1:2])
    zm2 = jnp.where(t >= 2, pltpu.roll(z, 2, 0), jnp.where(t == 1, st[1:2], st[0:1]))
    cw = cw_ref[...]
    cat_ref[0, :, :D_A] = gb_ref[0] * (cw[0:1] * zm2 + cw[1:2] * zm1 + cw[2:3] * z)
    cat_ref[0, :, D_A:] = o


def _mix_sample(q, kvn, win, z, gb, st, cw):
    b, s_len, _ = q.shape
    per = lambda i: (i,) + (0,) * 2
    blk = lambda a: pl.BlockSpec((1,) + a.shape[1:], per)
    return pl.pallas_call(
        _mix_sample_kernel,
        grid=(b,),
        in_specs=[blk(q), blk(kvn),
                  pl.BlockSpec((1,) + win.shape[1:], lambda i: (i, 0, 0, 0)),
                  blk(z), blk(gb), blk(st), pl.BlockSpec((3, D_A), lambda i: (0, 0))],
        out_specs=pl.BlockSpec((1, s_len, D_MODEL), per),
        out_shape=jax.ShapeDtypeStruct((b, s_len, D_MODEL), F32),
        compiler_params=_params(1), name="mix_sample",
    )(q, kvn, win, z, gb, st, cw)


def _linear_res_kernel(x_ref, w_ref, h_ref, out_ref):
    out_ref[...] = h_ref[...] + jnp.dot(x_ref[...].astype(BF16), w_ref[...],
                                        preferred_element_type=F32)


def _linear_res(x, w16, h):
    n, k = x.shape
    full = lambda a: pl.BlockSpec(a.shape, lambda i: (0, 0))
    return pl.pallas_call(
        _linear_res_kernel, grid=(1,),
        in_specs=[full(x), full(w16), full(h)], out_specs=full(h),
        out_shape=jax.ShapeDtypeStruct(h.shape, F32),
        compiler_params=_params(1), name="linear_res",
    )(x, w16, h)


def _swiglu_step(xn, wg_ref, wu_ref, wd_ref, acc_sc):
    a = jnp.dot(xn, wg_ref[0], preferred_element_type=F32)
    b = jnp.dot(xn, wu_ref[0], preferred_element_type=F32)
    t = (jax.nn.silu(a) * b).astype(BF16)
    acc_sc[...] += jnp.dot(t, wd_ref[0], preferred_element_type=F32)


def _ffn_dense_kernel(x_ref, g_ref, wg_ref, wu_ref, wd_ref, out_ref, xn_sc, acc_sc):
    f = pl.program_id(1)

    @pl.when(f == 0)
    def _():
        xn_sc[...] = _rms_bf16(x_ref[...], g_ref[...])
        acc_sc[...] = jnp.zeros_like(acc_sc)

    _swiglu_step(xn_sc[...], wg_ref, wu_ref, wd_ref, acc_sc)

    @pl.when(f == pl.num_programs(1) - 1)
    def _():
        out_ref[...] = x_ref[...] + acc_sc[...]


def _ffn_dense(x, g, wg, wu, wd, *, tm, fc):
    n = x.shape[0]
    ff = wg.shape[2]
    row = lambda i, f: (i, 0)
    return pl.pallas_call(
        _ffn_dense_kernel, grid=(n // tm, ff // fc),
        in_specs=[pl.BlockSpec((tm, D_MODEL), row), pl.BlockSpec((1, D_MODEL), lambda i, f: (0, 0)),
                  pl.BlockSpec((1, D_MODEL, fc), lambda i, f: (0, 0, f)),
                  pl.BlockSpec((1, D_MODEL, fc), lambda i, f: (0, 0, f)),
                  pl.BlockSpec((1, fc, D_MODEL), lambda i, f: (0, f, 0))],
        out_specs=pl.BlockSpec((tm, D_MODEL), row),
        out_shape=jax.ShapeDtypeStruct((n, D_MODEL), F32),
        scratch_shapes=[pltpu.VMEM((tm, D_MODEL), BF16), pltpu.VMEM((tm, D_MODEL), F32)],
        compiler_params=_params(2), name="ffn_dense",
    )(x, g, wg, wu, wd)


def _pack_bf16_pairs(x):
    half = x.shape[1] // 2
    lo = lax.bitcast_convert_type(x[:, :half].astype(BF16).astype(F32), jnp.int32)
    hi = lax.bitcast_convert_type(x[:, half:].astype(BF16).astype(F32), jnp.int32)
    return (hi & -65536) | lax.shift_right_logical(lo, 16)


def _unpack_bf16_pairs(p):
    lo = lax.bitcast_convert_type(p << 16, F32)
    hi = lax.bitcast_convert_type(p & -65536, F32)
    return jnp.concatenate([lo, hi], axis=1)


def _ffn_experts_kernel(te_ref, tv_ref, x_ref, wg_ref, wu_ref, wd_ref, out_ref, xn_sc, acc_sc):
    del te_ref
    i, f = pl.program_id(0), pl.program_id(1)
    n_valid = tv_ref[i]

    @pl.when(n_valid > 0)
    def _():
        @pl.when(f == 0)
        def _():
            row = lax.broadcasted_iota(jnp.int32, x_ref.shape, 0)
            xn_sc[...] = _unpack_bf16_pairs(jnp.where(row < n_valid, x_ref[...], 0)).astype(BF16)
            acc_sc[...] = jnp.zeros_like(acc_sc)

        def step(rows):
            xn = xn_sc[:rows]
            a = jnp.dot(xn, wg_ref[0].astype(BF16), preferred_element_type=F32)
            b = jnp.dot(xn, wu_ref[0].astype(BF16), preferred_element_type=F32)
            t = (jax.nn.silu(a) * b).astype(BF16)
            acc_sc[:rows] += jnp.dot(t, wd_ref[0].astype(BF16), preferred_element_type=F32)

        tm = xn_sc.shape[0]
        quarter = tm // 4
        for nq in range(1, 5):
            pl.when((n_valid > (nq - 1) * quarter) & (n_valid <= nq * quarter))(
                functools.partial(step, nq * quarter))

        @pl.when(f == pl.num_programs(1) - 1)
        def _():
            out_ref[...] = _pack_bf16_pairs(acc_sc[...])


def _ffn_experts(tile_expert, tile_valid, x, wg, wu, wd, *, tm, fc):
    n, wpk = x.shape
    ff = wg.shape[2]
    row = lambda i, f, te, tv: (i, 0)
    grid_spec = pltpu.PrefetchScalarGridSpec(
        num_scalar_prefetch=2, grid=(n // tm, ff // fc),
        in_specs=[pl.BlockSpec((tm, wpk), row),
                  pl.BlockSpec((1, D_MODEL, fc), lambda i, f, te, tv: (te[i], 0, f)),
                  pl.BlockSpec((1, D_MODEL, fc), lambda i, f, te, tv: (te[i], 0, f)),
                  pl.BlockSpec((1, fc, D_MODEL), lambda i, f, te, tv: (te[i], f, 0))],
        out_specs=pl.BlockSpec((tm, wpk), row),
        scratch_shapes=[pltpu.VMEM((tm, D_MODEL), BF16), pltpu.VMEM((tm, D_MODEL), F32)])
    return pl.pallas_call(
        _ffn_experts_kernel, grid_spec=grid_spec,
        out_shape=jax.ShapeDtypeStruct((n, wpk), jnp.int32),
        compiler_params=_params(2), name="ffn_experts",
    )(tile_expert, tile_valid, x, wg, wu, wd)


def _sc_worker_id():
    return lax.axis_index("s") * SC_CORES + lax.axis_index("c")


def _sc_dispatch(x_a, x_b, slots_a, slots_b, n_slots):
    w = x_a.shape[1]
    per_a = x_a.shape[0] // SC_WORKERS
    per_b = x_b.shape[0] // SC_WORKERS
    ch = min(SC_CHUNK // 2, per_a)
    n_chunks = per_a // ch
    mesh = plsc.VectorSubcoreMesh(core_axis_name="c", subcore_axis_name="s")
    dma = pltpu.SemaphoreType.DMA

    @functools.partial(
        pl.kernel, mesh=mesh, out_type=jax.ShapeDtypeStruct((n_slots, w), x_a.dtype),
        scratch_types=[pltpu.VMEM((per_a,), jnp.int32), pltpu.VMEM((per_a,), jnp.int32),
                       pltpu.VMEM((ch, w), x_a.dtype), pltpu.VMEM((ch, w), x_a.dtype),
                       pltpu.VMEM((per_b,), jnp.int32), pltpu.VMEM((per_b, w), x_b.dtype),
                       dma, dma, dma, dma, dma, dma])
    def dispatch(xa_hbm, xb_hbm, a0_hbm, a1_hbm, b0_hbm, b1_hbm, out_hbm,
                 idx0_v, idx1_v, rows0_v, rows1_v, idxb_v, rowsb_v,
                 lsem0, lsem1, s0sem0, s0sem1, s1sem0, s1sem1):
        wid = _sc_worker_id()
        base_a = pl.multiple_of(wid * per_a, 8)
        pltpu.sync_copy(a0_hbm.at[pl.ds(base_a, per_a)], idx0_v)
        pltpu.sync_copy(a1_hbm.at[pl.ds(base_a, per_a)], idx1_v)
        rows, lsem = (rows0_v, rows1_v), (lsem0, lsem1)
        ssem = ((s0sem0, s0sem1), (s1sem0, s1sem1))

        def load(c):
            return pltpu.async_copy(xa_hbm.at[pl.ds(base_a + c * ch, ch)], rows[c % 2], lsem[c % 2])

        def scatter(c):
            return [pltpu.async_copy(rows[c % 2], out_hbm.at[idx_v.at[pl.ds(c * ch, ch)]],
                                     ssem[k][c % 2])
                    for k, idx_v in enumerate((idx0_v, idx1_v))]

        loads, scatters = {0: load(0)}, {}
        for c in range(n_chunks):
            loads[c].wait()
            if c + 1 < n_chunks:
                for cp in scatters.pop(c - 1, ()):
                    cp.wait()
                loads[c + 1] = load(c + 1)
            scatters[c] = scatter(c)
        for cps in scatters.values():
            for cp in cps:
                cp.wait()

        base_b = pl.multiple_of(wid * per_b, 8)
        pltpu.sync_copy(xb_hbm.at[pl.ds(base_b, per_b)], rowsb_v)
        for slots_hbm in (b0_hbm, b1_hbm):
            pltpu.sync_copy(slots_hbm.at[pl.ds(base_b, per_b)], idxb_v)
            pltpu.sync_copy(rowsb_v, out_hbm.at[idxb_v])

    return dispatch(x_a, x_b, *slots_a, *slots_b)


def _sc_gather(table, idx):
    _, w = table.shape
    b = idx.shape[0]
    per_w = b // SC_WORKERS
    ch = max(c for c in range(8, SC_CHUNK + 1, 8) if per_w % c == 0)
    n_chunks = per_w // ch
    assert per_w * SC_WORKERS == b
    mesh = plsc.VectorSubcoreMesh(core_axis_name="c", subcore_axis_name="s")
    dma = pltpu.SemaphoreType.DMA

    @functools.partial(
        pl.kernel, mesh=mesh, out_type=jax.ShapeDtypeStruct((b, w), table.dtype),
        scratch_types=[pltpu.VMEM((per_w,), jnp.int32), pltpu.VMEM((ch, w), table.dtype),
                       pltpu.VMEM((ch, w), table.dtype), dma, dma, dma, dma])
    def gather(table_hbm, idx_hbm, out_hbm, idx_v, rows0_v, rows1_v, gsem0, gsem1, wsem0, wsem1):
        wid = _sc_worker_id()
        base = pl.multiple_of(wid * per_w, 8)
        pltpu.sync_copy(idx_hbm.at[pl.ds(base, per_w)], idx_v)
        rows, gsem, wsem = (rows0_v, rows1_v), (gsem0, gsem1), (wsem0, wsem1)

        def fetch(c):
            return pltpu.async_copy(table_hbm.at[idx_v.at[pl.ds(c * ch, ch)]], rows[c % 2],
                                    gsem[c % 2])

        def write(c):
            return pltpu.async_copy(rows[c % 2], out_hbm.at[pl.ds(base + c * ch, ch)], wsem[c % 2])

        fetches, writes = {0: fetch(0)}, {}
        for c in range(n_chunks):
            fetches[c].wait()
            if c + 1 < n_chunks:
                if c >= 1:
                    writes.pop(c - 1).wait()
                fetches[c + 1] = fetch(c + 1)
            writes[c] = write(c)
        for cp in writes.values():
            cp.wait()

    return gather(table, idx)


def _combine_kernel(h_ref, sel_ref, y0_ref, y1_ref, out_ref):
    sel = sel_ref[...]
    out_ref[...] = (h_ref[...] + sel[:, 2:3] * _unpack_bf16_pairs(y0_ref[...])
                    + sel[:, 3:4] * _unpack_bf16_pairs(y1_ref[...]))


def _combine(h, sel, yb, blk_off, tm):
    n = h.shape[0]
    wpk = yb.shape[1]
    half_blocks = yb.shape[0] // 2 // tm
    return pl.pallas_call(
        _combine_kernel, grid=(n // tm,),
        in_specs=[pl.BlockSpec((tm, D_MODEL), lambda i: (i, 0)),
                  pl.BlockSpec((tm, LANES), lambda i: (i, 0)),
                  pl.BlockSpec((tm, wpk), lambda i: (i + blk_off, 0)),
                  pl.BlockSpec((tm, wpk), lambda i: (i + blk_off + half_blocks, 0))],
        out_specs=pl.BlockSpec((tm, D_MODEL), lambda i: (i, 0)),
        out_shape=jax.ShapeDtypeStruct((n, D_MODEL), F32),
        compiler_params=_params(1), name="combine",
    )(h, sel, yb, yb)


def _route_plan(sels, counts, te_rows, n_tiles):
    counts = counts[0, :N_EXPERTS].astype(jnp.int32)
    padded = (counts + te_rows - 1) // te_rows * te_rows
    ends = jnp.cumsum(padded)
    starts = ends - padded
    experts = jnp.arange(N_EXPERTS, dtype=jnp.int32)

    def slots_of(sel, k):
        e = sel[:, k].astype(jnp.int32)
        start = jnp.sum(jnp.where(e[:, None] == experts[None, :], starts[None, :], 0), axis=1)
        return start + sel[:, 4 + k].astype(jnp.int32)

    slots = [(slots_of(sel, 0), slots_of(sel, 1)) for sel in sels]
    tile_start = jnp.arange(n_tiles, dtype=jnp.int32) * te_rows
    tile_expert = jnp.minimum(jnp.sum(tile_start[:, None] >= ends[None, :], axis=1),
                              N_EXPERTS - 1).astype(jnp.int32)
    in_group = tile_start - starts[tile_expert]
    tile_valid = jnp.clip(counts[tile_expert] - in_group, 0, te_rows)
    tile_valid = jnp.where(tile_start < ends[-1], tile_valid, 0).astype(jnp.int32)
    return slots, tile_expert, tile_valid


def _gmlp_kernel(h_ref, g_ref, wuv_ref, vg_ref, mix_ref, bs_ref, wo_ref, out_ref, v_ref, um_sc,
                 *, period):
    h = h_ref[...]
    tm = h.shape[0]
    ln = mix_ref.shape[1]
    gw = D_MODEL // C_GROUPS
    xn = _rms_bf16(h, g_ref[...])
    vv = jax.nn.gelu(jnp.dot(xn, wuv_ref[:, D_MODEL:], preferred_element_type=F32))
    v = vv * lax.rsqrt(jnp.mean(vv * vv, axis=-1, keepdims=True) + EPS) * vg_ref[...]
    v_ref[0] = v[tm - v_ref.shape[1]:]
    vb = v.astype(BF16)
    u = jax.nn.gelu(jnp.dot(xn, wuv_ref[:, :D_MODEL], preferred_element_type=F32))
    r = lax.broadcasted_iota(jnp.int32, (ln, ln), 0)
    c = lax.broadcasted_iota(jnp.int32, (ln, ln), 1)
    causal = (r // period == c // period) & (c % period <= r % period)
    for gi in range(C_GROUPS):
        wm = jnp.where(causal, mix_ref[gi], 0.0).astype(BF16)
        bias = bs_ref[:, gi:gi + 1]
        for ch in range(tm // ln):
            rs = slice(ch * ln, (ch + 1) * ln)
            cs = slice(gi * gw, (gi + 1) * gw)
            mixed = jnp.dot(wm, vb[rs, cs], preferred_element_type=F32) + bias
            um_sc[rs, cs] = (u[rs, cs] * mixed).astype(BF16)
    out_ref[...] = h + jnp.dot(um_sc[...], wo_ref[...], preferred_element_type=F32)


def _gmlp(h, g, wuv16, vg, mix, bs, wo16, *, tm, period, seq, tail):
    n = h.shape[0]
    tps = seq // tm
    row = lambda i: (i, 0)
    full = lambda a: pl.BlockSpec(a.shape, lambda i: (0,) * a.ndim)
    return pl.pallas_call(
        functools.partial(_gmlp_kernel, period=period),
        grid=(n // tm,),
        in_specs=[pl.BlockSpec((tm, D_MODEL), row), full(g), full(wuv16), full(vg), full(mix),
                  full(bs), full(wo16)],
        out_specs=[pl.BlockSpec((tm, D_MODEL), row),
                   pl.BlockSpec((1, tail, D_MODEL), lambda i: (i // tps, 0, 0))],
        out_shape=[jax.ShapeDtypeStruct((n, D_MODEL), F32),
                   jax.ShapeDtypeStruct((n // seq, tail, D_MODEL), F32)],
        scratch_shapes=[pltpu.VMEM((tm, D_MODEL), BF16)],
        compiler_params=_params(1), name="gmlp",
    )(h, g, wuv16, vg, mix, bs, wo16)


def _router_kernel(h_ref, g_ref, wr_ref, cin_ref, xn_ref, sel_ref, cnt_ref):
    @pl.when(pl.program_id(0) == 0)
    def _():
        cnt_ref[...] = cin_ref[...]

    x = h_ref[...]
    xn = x * lax.rsqrt(jnp.mean(x * x, axis=-1, keepdims=True) + EPS) * g_ref[...]
    xn_ref[...] = _pack_bf16_pairs(xn)
    x_hi = xn.astype(BF16)
    x_lo = (xn - x_hi.astype(F32)).astype(BF16)
    w = wr_ref[...]
    w_hi = w.astype(BF16)
    w_lo = (w - w_hi.astype(F32)).astype(BF16)
    logits = (jnp.dot(x_hi, w_hi, preferred_element_type=F32)
              + (jnp.dot(x_lo, w_hi, preferred_element_type=F32)
                 + jnp.dot(x_hi, w_lo, preferred_element_type=F32)))
    lane = lax.broadcasted_iota(jnp.int32, logits.shape, 1)
    lanef = lane.astype(F32)
    logits = jnp.where(lane < N_EXPERTS, logits, -jnp.inf)
    m1 = jnp.max(logits, axis=-1, keepdims=True)
    i1 = jnp.min(jnp.where(logits == m1, lanef, float(LANES)), axis=-1, keepdims=True)
    rest = jnp.where(lanef == i1, -jnp.inf, logits)
    m2 = jnp.max(rest, axis=-1, keepdims=True)
    i2 = jnp.min(jnp.where(rest == m2, lanef, float(LANES)), axis=-1, keepdims=True)
    e2 = jnp.exp(m2 - m1)
    w1 = 1.0 / (1.0 + e2)
    w2 = e2 / (1.0 + e2)
    tm = x.shape[0]
    hit1, hit2 = lanef == i1, lanef == i2
    picked = jnp.where(hit1 | hit2, 1.0, 0.0)
    r_i = lax.broadcasted_iota(jnp.int32, (tm, tm), 0)
    c_i = lax.broadcasted_iota(jnp.int32, (tm, tm), 1)
    below = jnp.where(c_i < r_i, 1.0, 0.0).astype(BF16)
    before = jnp.dot(below, picked.astype(BF16), preferred_element_type=F32) + cnt_ref[...]
    rank1 = jnp.sum(jnp.where(hit1, before, 0.0), axis=-1, keepdims=True)
    rank2 = jnp.sum(jnp.where(hit2, before, 0.0), axis=-1, keepdims=True)
    cnt_ref[...] += jnp.sum(picked, axis=0, keepdims=True)
    vals = (i1, i2, w1, w2, rank1, rank2)
    sel = jnp.zeros_like(logits)
    for li, val in enumerate(vals):
        sel = jnp.where(lane == li, val, sel)
    sel_ref[...] = sel


def _router(h, g, wr_pad, counts_in, tm):
    n = h.shape[0]
    row = lambda i: (i, 0)
    fix = lambda i: (0, 0)
    return pl.pallas_call(
        _router_kernel, grid=(n // tm,),
        in_specs=[pl.BlockSpec((tm, D_MODEL), row), pl.BlockSpec((1, D_MODEL), fix),
                  pl.BlockSpec((D_MODEL, LANES), fix), pl.BlockSpec((1, LANES), fix)],
        out_specs=[pl.BlockSpec((tm, D_MODEL // 2), row), pl.BlockSpec((tm, LANES), row),
                   pl.BlockSpec((1, LANES), fix)],
        out_shape=[jax.ShapeDtypeStruct((n, D_MODEL // 2), jnp.int32),
                   jax.ShapeDtypeStruct((n, LANES), F32),
                   jax.ShapeDtypeStruct((1, LANES), F32)],
        compiler_params=_params(1), name="router",
    )(h, g, wr_pad, counts_in)


def _rope_tables(pos):
    half = HEAD_DIM // 2
    inv = ROPE_THETA ** (-jnp.arange(half, dtype=F32) / half)
    ang = pos.astype(F32)[:, None] * inv[None, :]
    cos, sin = jnp.cos(ang), jnp.sin(ang)
    zero = jnp.zeros_like(sin)
    rep = lambda a, b: jnp.tile(jnp.concatenate([a, b], axis=-1), (1, LANES // HEAD_DIM))
    return rep(cos, cos), rep(-sin, zero), rep(zero, sin)


def _tile_for(n, pref):
    return pref if n % pref == 0 else n


def kernel(x_prompt, x_sample, state_conv, cache_kv_win, norm_mix_e, w_in_e, conv_w, q_gain,
           k_gain, w_out_e, norm_ffn_e, ffn_w_gate, ffn_w_up, ffn_w_down, norm_mix_o, w_uv,
           v_gain, w_s, b_s, w_out_o, norm_moe, w_router, moe_w_gate, moe_w_up, moe_w_down):
    bp, seq, _ = x_prompt.shape
    bs_, s_len, _ = x_sample.shape
    n_p, n_s = bp * seq, bs_ * s_len
    hp = x_prompt.reshape(n_p, D_MODEL)
    hs = x_sample.reshape(n_s, D_MODEL)
    tm_p = 512
    row = lambda a: a.reshape(1, -1)

    tabs_p = _rope_tables(jnp.arange(seq, dtype=jnp.int32))
    tabs_s = tuple(jnp.tile(t, (bs_, 1)) for t in
                   _rope_tables(PAST_LEN + jnp.arange(s_len, dtype=jnp.int32)))
    hd = jnp.arange(256) // HEAD_DIM
    bd = (hd[:, None] == hd[None, :]).astype(BF16)

    w_in16 = w_in_e[0].astype(BF16)
    w_out16 = w_out_e[0].astype(BF16)
    qg = jnp.tile(q_gain[0], 4).reshape(1, 256)
    kg = jnp.tile(k_gain[0], 4).reshape(1, 256)
    g0 = row(norm_mix_e[0])

    dils = tuple(dil for _, dil in BRANCHES)
    lw = min(2048, seq)
    z_p, gb_p, kvt_p, *planes = _inproj(hp, g0, w_in16, tabs_p, qg, kg, bd, tm_p, seq, dils, lw)
    z_s, gb_s, kv_s, q_s = _inproj(hs, g0, w_in16, tabs_s, qg, kg, bd, n_s, n_s, ())

    os_, ls_ = [], []
    for di, dil in enumerate(dils):
        m = seq // dil
        q_d, k_d, v_d = (a.reshape(bp * dil, m, D_B) for a in planes[3 * di:3 * di + 3])
        o, lse = _attn_prompt(q_d, k_d, v_d, qb=min(256, m))
        os_.append(o.reshape(bp, dil, m, D_B))
        ls_.append(lse.reshape(bp, dil, m, LANES))
    hp = _outproj_prompt(z_p, gb_p, conv_w[0], os_, ls_, w_out16, hp, tm_p, seq, dils)

    cat_s = _mix_sample(q_s.reshape(bs_, s_len, D_B),
                        kv_s.reshape(bs_, s_len, 2 * D_B),
                        jnp.transpose(cache_kv_win[0], (0, 2, 3, 4, 1)).reshape(
                            bs_, 2, D_B, -1),
                        z_s.reshape(bs_, s_len, D_A), gb_s.reshape(bs_, s_len, D_A),
                        state_conv[0], conv_w[0])
    hs = _linear_res(cat_s.reshape(n_s, D_MODEL), w_out16, hs)

    conv_prompt = z_p.reshape(bp, seq, D_A)[:, seq - 2:][None]
    conv_sample = z_s.reshape(bs_, s_len, D_A)[:, s_len - 2:][None]
    kv_prompt = jnp.transpose(kvt_p.reshape(bp, 2, H_B, HEAD_DIM, lw), (0, 4, 1, 2, 3))[None]
    kv_sample = kv_s.reshape(bs_, s_len, 2, H_B, HEAD_DIM)[None]

    fg, fu, fd = (ffn_w_gate.astype(BF16), ffn_w_up.astype(BF16), ffn_w_down.astype(BF16))
    gf = row(norm_ffn_e[0])
    hp = _ffn_dense(hp, gf, fg, fu, fd, tm=2 * tm_p, fc=1408)
    hs = _ffn_dense(hs, gf, fg, fu, fd, tm=n_s, fc=1408)

    wuv16 = w_uv[0].astype(BF16)
    wo16 = w_out_o[0].astype(BF16)
    g1, vg = row(norm_mix_o[0]), row(v_gain[0])
    tail = seq - ((seq - 1) // CHUNK) * CHUNK
    hp, v_tail_p = _gmlp(hp, g1, wuv16, vg, w_s[0], b_s[0].T, wo16, tm=tm_p, period=CHUNK,
                         seq=seq, tail=tail)
    mix_s = jnp.tile(w_s[0][:, :s_len, :s_len], (1, bs_, bs_))
    bias_s = jnp.tile(b_s[0][:, :s_len].T, (bs_, 1))
    hs, v_all_s = _gmlp(hs, g1, wuv16, vg, mix_s, bias_s, wo16, tm=n_s, period=s_len,
                        seq=n_s, tail=n_s)
    v_prompt = v_tail_p[None]
    v_sample = v_all_s.reshape(bs_, s_len, D_MODEL)[None]

    wr_pad = jnp.pad(w_router[0], ((0, 0), (0, LANES - N_EXPERTS)))
    gm = row(norm_moe[0])
    xn_p, sel_p, counts = _router(hp, gm, wr_pad, jnp.zeros((1, LANES), F32), tm_p)
    xn_s, sel_s, counts = _router(hs, gm, wr_pad, counts, n_s)
    n_tok = n_p + n_s
    n_tiles = -(-(2 * n_tok + N_EXPERTS * (MOE_ROWS - 1)) // MOE_ROWS)
    (slots_p, slots_s), tile_expert, tile_valid = _route_plan(
        (sel_p, sel_s), counts, MOE_ROWS, n_tiles)
    slot0 = jnp.concatenate([slots_p[0], slots_s[0]])
    slot1 = jnp.concatenate([slots_p[1], slots_s[1]])
    xs = _sc_dispatch(xn_p, xn_s, slots_p, slots_s, n_tiles * MOE_ROWS)
    ys = _ffn_experts(tile_expert, tile_valid, xs, moe_w_gate[0], moe_w_up[0], moe_w_down[0],
                      tm=MOE_ROWS, fc=896)
    yb = _sc_gather(ys, jnp.concatenate([slot0, slot1]))
    hp = _combine(hp, sel_p, yb, 0, n_s)
    hs = _combine(hs, sel_s, yb, n_p // n_s, n_s)

    return (hp.reshape(bp, seq, D_MODEL), hs.reshape(bs_, s_len, D_MODEL),
            conv_prompt, conv_sample, kv_prompt, kv_sample, v_prompt, v_sample)
```

```python
import functools

import jax
import jax.numpy as jnp
from jax import lax
from jax.experimental import pallas as pl
from jax.experimental.pallas import tpu as pltpu
from jax.experimental.pallas import tpu_sc as plsc

F32 = jnp.float32
BF16 = jnp.bfloat16

D_MODEL = 1024
D_A = 512
H_B = 8
HEAD_DIM = 64
D_B = H_B * HEAD_DIM
BRANCHES = ((128, 1), (512, 4), (2048, 16))
BAND = 128
ROPE_THETA = 10000.0
PAST_LEN = 16384
N_EXPERTS = 8
C_GROUPS = 4
CHUNK = 128
EPS = 1e-6
LANES = 128
VMEM_LIMIT = 56 * 1024 * 1024
SC_CORES = 2
SC_WORKERS = SC_CORES * 16
SC_CHUNK = 128
MOE_ROWS = 1024


def _params(n_axes):
    return pltpu.CompilerParams(dimension_semantics=("arbitrary",) * n_axes,
                                vmem_limit_bytes=VMEM_LIMIT)


def _rms_bf16(x, g):
    ms = jnp.mean(x * x, axis=-1, keepdims=True)
    return (x * lax.rsqrt(ms + EPS) * g).astype(BF16)


def _inproj_kernel(x_ref, g_ref, w_ref, cos_ref, sina_ref, sinb_ref, qg_ref, kg_ref, bd_ref,
                   z_ref, gb_ref, kv_ref, *rest, dils, win_tiles, tiles_per_seq):
    if dils:
        planes, qkv_sc = rest[:-1], rest[-1]
    else:
        (q_out,) = rest
    xn = _rms_bf16(x_ref[...], g_ref[...])

    def col(j):
        return jnp.dot(xn, w_ref[:, j * 512:(j + 1) * 512], preferred_element_type=F32)

    z_ref[...] = col(1) * col(0)
    gb_ref[...] = col(2).astype(gb_ref.dtype)
    cos, sina, sinb = cos_ref[...], sina_ref[...], sinb_ref[...]

    def head_norm_rope(t, gain):
        outs = []
        for c2 in range(2):
            tc = t[:, c2 * 256:(c2 + 1) * 256]
            ss = jnp.dot((tc * tc).astype(BF16), bd_ref[...], preferred_element_type=F32)
            tn = tc * lax.rsqrt(ss * (1.0 / HEAD_DIM) + EPS) * gain
            for c in range(2):
                u = tn[:, c * LANES:(c + 1) * LANES]
                outs.append(u * cos + pltpu.roll(u, LANES - 32, 1) * sina
                            + pltpu.roll(u, 32, 1) * sinb)
        return outs

    q = [t * (HEAD_DIM ** -0.5) for t in head_norm_rope(col(3), qg_ref[...])]
    k = head_norm_rope(col(4), kg_ref[...])
    v = col(5)
    if not dils:
        for c in range(4):
            kv_ref[:, c * LANES:(c + 1) * LANES] = k[c]
            q_out[:, c * LANES:(c + 1) * LANES] = q[c]
        kv_ref[:, D_B:] = v
        return

    @pl.when(pl.program_id(0) % tiles_per_seq >= tiles_per_seq - win_tiles)
    def _():
        for c in range(4):
            kv_ref[0, c * LANES:(c + 1) * LANES, :] = k[c].T
            kv_ref[0, D_B + c * LANES:D_B + (c + 1) * LANES, :] = v[:, c * LANES:(c + 1) * LANES].T

    tm = x_ref.shape[0]
    for c in range(4):
        qkv_sc[0, c] = q[c]
        qkv_sc[1, c] = k[c]
        qkv_sc[2, c] = v[:, c * LANES:(c + 1) * LANES]
    for di, dil in enumerate(dils):
        for r in range(dil):
            for a in range(3):
                for c in range(4):
                    if dil == 1:
                        rows_ac = (q[c], k[c], v[:, c * LANES:(c + 1) * LANES])[a]
                    else:
                        rows_ac = qkv_sc[a, c, pl.ds(r, tm // dil, stride=dil), :]
                    planes[3 * di + a][0, r, :, c * LANES:(c + 1) * LANES] = rows_ac.astype(BF16)


def _inproj(x, g, w16, tabs, qg, kg, bd, tm, seq, dils, win=0):
    n = x.shape[0]
    cos, sina, sinb = tabs
    tt = cos.shape[0] // tm
    tps = seq // tm
    win_tiles = win // tm
    row = lambda i: (i, 0)
    fix = lambda i: (0, 0)
    tab = lambda i: (i % tt, 0)
    out_specs = [pl.BlockSpec((tm, D_A), row), pl.BlockSpec((tm, D_A), row)]
    out_shape = [jax.ShapeDtypeStruct((n, D_A), F32),
                 jax.ShapeDtypeStruct((n, D_A), BF16 if dils else F32)]
    if dils:
        out_specs.append(pl.BlockSpec(
            (1, 2 * D_B, tm),
            lambda i: (i // tps, 0, jnp.maximum(i % tps - (tps - win_tiles), 0))))
        out_shape.append(jax.ShapeDtypeStruct((n // seq, 2 * D_B, win), F32))
        scratch = [pltpu.VMEM((3, D_B // LANES, tm, LANES), F32)]
    else:
        out_specs += [pl.BlockSpec((tm, 2 * D_B), row), pl.BlockSpec((tm, D_B), row)]
        out_shape += [jax.ShapeDtypeStruct((n, 2 * D_B), F32),
                      jax.ShapeDtypeStruct((n, D_B), F32)]
        scratch = []
    for dil in dils:
        for _ in range(3):
            out_specs.append(pl.BlockSpec((1, dil, tm // dil, D_B),
                                          lambda i: (i // tps, 0, i % tps, 0)))
            out_shape.append(jax.ShapeDtypeStruct((n // seq, dil, seq // dil, D_B), BF16))
    return pl.pallas_call(
        functools.partial(_inproj_kernel, dils=dils, win_tiles=win_tiles, tiles_per_seq=tps),
        grid=(n // tm,),
        in_specs=[pl.BlockSpec((tm, D_MODEL), row), pl.BlockSpec((1, D_MODEL), fix),
                  pl.BlockSpec(w16.shape, fix),
                  pl.BlockSpec((tm, LANES), tab), pl.BlockSpec((tm, LANES), tab),
                  pl.BlockSpec((tm, LANES), tab),
                  pl.BlockSpec((1, 256), fix), pl.BlockSpec((1, 256), fix),
                  pl.BlockSpec((256, 256), fix)],
        out_specs=out_specs, out_shape=out_shape, scratch_shapes=scratch,
        compiler_params=_params(1), name="inproj",
    )(x, g, w16, cos, sina, sinb, qg, kg, bd)


def _attn_prompt_kernel(q_ref, kc_ref, kp_ref, vc_ref, vp_ref, o_ref, lse_ref, *, qb):
    j = pl.program_id(1)
    nsub = qb // BAND
    lane = lax.broadcasted_iota(jnp.int32, (1, LANES), 1)
    head0 = lane < HEAD_DIM
    qi = lax.broadcasted_iota(jnp.int32, (2 * BAND, 2 * BAND), 0) % BAND + BAND
    kj = lax.broadcasted_iota(jnp.int32, (2 * BAND, 2 * BAND), 1)
    dist = qi - kj
    band = (dist >= 0) & (dist <= BAND)
    first = band & ((kj >= BAND) | (j > 0))
    lanes = [slice(hp * LANES, (hp + 1) * LANES) for hp in range(D_B // LANES)]
    kcats = [jnp.concatenate([kp_ref[0, :, ls], kc_ref[0, :, ls]], axis=0) for ls in lanes]
    vcats = [jnp.concatenate([vp_ref[0, :, ls], vc_ref[0, :, ls]], axis=0) for ls in lanes]
    for sub in range(nsub):
        rows = slice(sub * BAND, (sub + 1) * BAND)
        lse_all = jnp.zeros((BAND, LANES), F32)
        for hp, ls in enumerate(lanes):
            qs = q_ref[0, rows, ls]
            ks = kcats[hp][sub * BAND:(sub + 2) * BAND]
            vs = vcats[hp][sub * BAND:(sub + 2) * BAND]
            q2 = jnp.concatenate([jnp.where(head0, qs, jnp.zeros_like(qs)),
                                  jnp.where(head0, jnp.zeros_like(qs), qs)], axis=0)
            s = lax.dot_general(q2, ks, (((1,), (1,)), ((), ())), preferred_element_type=F32)
            s = jnp.where(first if sub == 0 else band, s, -jnp.inf)
            mx = jnp.max(s, axis=-1, keepdims=True)
            p = jnp.exp(s - mx)
            l = jnp.sum(p, axis=-1, keepdims=True)
            pv = jnp.dot(p.astype(BF16), vs, preferred_element_type=F32) / l
            lse = mx + jnp.log(l)
            o_ref[0, rows, ls] = jnp.where(head0, pv[:BAND], pv[BAND:]).astype(BF16)
            lse_all = jnp.where(lane == 2 * hp, lse[:BAND],
                                jnp.where(lane == 2 * hp + 1, lse[BAND:], lse_all))
        lse_ref[0, rows, :] = lse_all


def _attn_prompt(q, k, v, qb):
    b, m, w = q.shape
    nsub = qb // BAND
    cur = lambda bi, j: (bi, j, 0)
    prev = lambda bi, j: (bi, jnp.maximum(j * nsub - 1, 0), 0)
    return pl.pallas_call(
        functools.partial(_attn_prompt_kernel, qb=qb),
        grid=(b, m // qb),
        in_specs=[pl.BlockSpec((1, qb, w), cur), pl.BlockSpec((1, qb, w), cur),
                  pl.BlockSpec((1, BAND, w), prev), pl.BlockSpec((1, qb, w), cur),
                  pl.BlockSpec((1, BAND, w), prev)],
        out_specs=[pl.BlockSpec((1, qb, w), cur), pl.BlockSpec((1, qb, LANES), cur)],
        out_shape=[jax.ShapeDtypeStruct((b, m, w), BF16),
                   jax.ShapeDtypeStruct((b, m, LANES), F32)],
        compiler_params=_params(2), name="attn_prompt",
    )(q, k, k, v, v)


def _outproj_prompt_kernel(z_ref, zh_ref, gb_ref, cw_ref, *rest, tiles_per_batch, dils):
    nb = len(dils)
    o_refs, l_refs = rest[:nb], rest[nb:2 * nb]
    w_ref, h_ref, out_ref = rest[2 * nb:2 * nb + 3]
    o_scs, l_scs = rest[2 * nb + 3:3 * nb + 3], rest[3 * nb + 3:]
    i = pl.program_id(0)
    z = z_ref[...]
    tm = z.shape[0]
    for dil, o_ref, l_ref, o_sc, l_sc in zip(dils, o_refs, l_refs, o_scs, l_scs):
        for r in range(dil):
            rows = pl.ds(r, tm // dil, stride=dil) if dil > 1 else slice(None)
            for c in range(D_B // LANES):
                ls = slice(c * LANES, (c + 1) * LANES)
                o_sc[c, rows, :] = o_ref[0, r, :, ls].astype(F32)
            l_sc[rows, :] = l_ref[0, r]
    unstage = lambda sc: jnp.concatenate([sc[c] for c in range(D_B // LANES)], axis=1)
    zh = jnp.where(i % tiles_per_batch == 0, 0.0, zh_ref[...])
    row = lax.broadcasted_iota(jnp.int32, (tm, 1), 0)
    zm1 = jnp.where(row >= 1, pltpu.roll(z, 1, 0), zh[7:8])
    zm2 = jnp.where(row >= 2, pltpu.roll(z, 2, 0), jnp.where(row == 1, zh[7:8], zh[6:7]))
    cw = cw_ref[...]
    a_out = gb_ref[...].astype(F32) * (cw[0:1] * zm2 + cw[1:2] * zm1 + cw[2:3] * z)
    lses = [l_sc[...] for l_sc in l_scs]
    lm = functools.reduce(jnp.maximum, lses)
    es = [jnp.exp(l - lm) for l in lses]
    tot = sum(es)
    spread = (lax.broadcasted_iota(jnp.int32, (LANES, D_B), 1) // HEAD_DIM
              == lax.broadcasted_iota(jnp.int32, (LANES, D_B), 0)).astype(BF16)

    def widen(wgt):
        hi = wgt.astype(BF16)
        lo = (wgt - hi.astype(F32)).astype(BF16)
        return (jnp.dot(hi, spread, preferred_element_type=F32)
                + jnp.dot(lo, spread, preferred_element_type=F32))

    o = sum(widen(e / tot) * unstage(o_sc) for e, o_sc in zip(es, o_scs))
    y = jnp.dot(a_out.astype(BF16), w_ref[:D_A, :], preferred_element_type=F32)
    y = y + jnp.dot(o.astype(BF16), w_ref[D_A:, :], preferred_element_type=F32)
    out_ref[...] = h_ref[...] + y


def _outproj_prompt(z, gb, cw, os_, ls_, w16, h, tm, seq, dils):
    n = z.shape[0]
    tps = seq // tm
    row = lambda i: (i, 0)
    fix = lambda i: (0, 0)
    halo = lambda i: (jnp.maximum(i * (tm // 8) - 1, 0), 0)
    half = pl.BlockSpec((tm, D_A), row)
    plane = lambda width: [pl.BlockSpec((1, dil, tm // dil, width),
                                        lambda i: (i // tps, 0, i % tps, 0)) for dil in dils]
    return pl.pallas_call(
        functools.partial(_outproj_prompt_kernel, tiles_per_batch=tps, dils=dils),
        grid=(n // tm,),
        in_specs=[half, pl.BlockSpec((8, D_A), halo), half, pl.BlockSpec((3, D_A), fix),
                  *plane(D_B), *plane(LANES),
                  pl.BlockSpec((D_MODEL, D_MODEL), fix), pl.BlockSpec((tm, D_MODEL), row)],
        out_specs=pl.BlockSpec((tm, D_MODEL), row),
        out_shape=jax.ShapeDtypeStruct((n, D_MODEL), F32),
        scratch_shapes=([pltpu.VMEM((D_B // LANES, tm, LANES), F32)] * len(dils)
                        + [pltpu.VMEM((tm, LANES), F32)] * len(dils)),
        compiler_params=_params(1), name="outproj_prompt",
    )(z, z, gb, cw, *os_, *ls_, w16, h)


def _mix_sample_kernel(q_ref, kvn_ref, win_ref, z_ref, gb_ref, st_ref, cw_ref, cat_ref):
    s_len = q_ref.shape[1]
    rows = H_B * s_len
    npad = LANES
    q = q_ref[0]
    r_i = lax.broadcasted_iota(jnp.int32, (rows, D_B), 0)
    c_i = lax.broadcasted_iota(jnp.int32, (rows, D_B), 1)
    qt = jnp.concatenate([q] * H_B, axis=0)
    qbd = jnp.where(r_i // s_len == c_i // HEAD_DIM, qt, 0.0).astype(BF16)
    kvn = jnp.concatenate([kvn_ref[0], jnp.zeros((npad - s_len, 2 * D_B), F32)], axis=0)

    def multiplicity(delta):
        cnt = jnp.zeros(delta.shape, F32)
        for _, dil in BRANCHES:
            ok = (delta >= 0) & (delta <= BAND * dil) & ((delta & (dil - 1)) == 0)
            cnt = cnt + jnp.where(ok, 1.0, 0.0)
        return cnt

    lb = win_ref.shape[3]
    nt = (((1,), (1,)), ((), ()))
    tok_c = lax.broadcasted_iota(jnp.int32, (rows, lb), 0) % s_len
    cnt_c = multiplicity(tok_c + lb - lax.broadcasted_iota(jnp.int32, (rows, lb), 1))
    tok_n = lax.broadcasted_iota(jnp.int32, (rows, npad), 0) % s_len
    cnt_n = multiplicity(tok_n - lax.broadcasted_iota(jnp.int32, (rows, npad), 1))
    sc = jnp.dot(qbd, win_ref[0, 0].astype(BF16), preferred_element_type=F32)
    sn = lax.dot_general(qbd, kvn[:, :D_B].astype(BF16), nt, preferred_element_type=F32)
    sc = jnp.where(cnt_c > 0, sc, -jnp.inf)
    sn = jnp.where(cnt_n > 0, sn, -jnp.inf)
    mx = jnp.maximum(jnp.max(sc, axis=-1, keepdims=True), jnp.max(sn, axis=-1, keepdims=True))
    pc = cnt_c * jnp.exp(sc - mx)
    pn = cnt_n * jnp.exp(sn - mx)
    den = jnp.sum(pc, axis=-1, keepdims=True) + jnp.sum(pn, axis=-1, keepdims=True)
    of = lax.dot_general(pc.astype(BF16), win_ref[0, 1].astype(BF16), nt,
                         preferred_element_type=F32)
    of = (of + jnp.dot(pn.astype(BF16), kvn[:, D_B:].astype(BF16),
                       preferred_element_type=F32)) / den
    lane_head = lax.broadcasted_iota(jnp.int32, (s_len, D_B), 1) // HEAD_DIM
    o = jnp.zeros((s_len, D_B), F32)
    for h in range(H_B):
        o = o + jnp.where(lane_head == h, of[h * s_len:(h + 1) * s_len], 0.0)
    z = z_ref[0]
    st = st_ref[0]
    t = lax.broadcasted_iota(jnp.int32, (s_len, 1), 0)
    zm1 = jnp.where(t >= 1, pltpu.roll(z, 1, 0), st[1:2])
    zm2 = jnp.where(t >= 2, pltpu.roll(z, 2, 0), jnp.where(t == 1, st[1:2], st[0:1]))
    cw = cw_ref[...]
    cat_ref[0, :, :D_A] = gb_ref[0] * (cw[0:1] * zm2 + cw[1:2] * zm1 + cw[2:3] * z)
    cat_ref[0, :, D_A:] = o


def _mix_sample(q, kvn, win, z, gb, st, cw):
    b, s_len, _ = q.shape
    per = lambda i: (i,) + (0,) * 2
    blk = lambda a: pl.BlockSpec((1,) + a.shape[1:], per)
    return pl.pallas_call(
        _mix_sample_kernel,
        grid=(b,),
        in_specs=[blk(q), blk(kvn),
                  pl.BlockSpec((1,) + win.shape[1:], lambda i: (i, 0, 0, 0)),
                  blk(z), blk(gb), blk(st), pl.BlockSpec((3, D_A), lambda i: (0, 0))],
        out_specs=pl.BlockSpec((1, s_len, D_MODEL), per),
        out_shape=jax.ShapeDtypeStruct((b, s_len, D_MODEL), F32),
        compiler_params=_params(1), name="mix_sample",
    )(q, kvn, win, z, gb, st, cw)


def _linear_res_kernel(x_ref, w_ref, h_ref, out_ref):
    out_ref[...] = h_ref[...] + jnp.dot(x_ref[...].astype(BF16), w_ref[...],
                                        preferred_element_type=F32)


def _linear_res(x, w16, h):
    n, k = x.shape
    full = lambda a: pl.BlockSpec(a.shape, lambda i: (0, 0))
    return pl.pallas_call(
        _linear_res_kernel, grid=(1,),
        in_specs=[full(x), full(w16), full(h)], out_specs=full(h),
        out_shape=jax.ShapeDtypeStruct(h.shape, F32),
        compiler_params=_params(1), name="linear_res",
    )(x, w16, h)


def _swiglu_step(xn, wg_ref, wu_ref, wd_ref, acc_sc):
    a = jnp.dot(xn, wg_ref[0], preferred_element_type=F32)
    b = jnp.dot(xn, wu_ref[0], preferred_element_type=F32)
    t = (jax.nn.silu(a) * b).astype(BF16)
    acc_sc[...] += jnp.dot(t, wd_ref[0], preferred_element_type=F32)


def _ffn_dense_kernel(x_ref, g_ref, wg_ref, wu_ref, wd_ref, out_ref, xn_sc, acc_sc):
    f = pl.program_id(1)

    @pl.when(f == 0)
    def _():
        xn_sc[...] = _rms_bf16(x_ref[...], g_ref[...])
        acc_sc[...] = jnp.zeros_like(acc_sc)

    _swiglu_step(xn_sc[...], wg_ref, wu_ref, wd_ref, acc_sc)

    @pl.when(f == pl.num_programs(1) - 1)
    def _():
        out_ref[...] = x_ref[...] + acc_sc[...]


def _ffn_dense(x, g, wg, wu, wd, *, tm, fc):
    n = x.shape[0]
    ff = wg.shape[2]
    row = lambda i, f: (i, 0)
    return pl.pallas_call(
        _ffn_dense_kernel, grid=(n // tm, ff // fc),
        in_specs=[pl.BlockSpec((tm, D_MODEL), row), pl.BlockSpec((1, D_MODEL), lambda i, f: (0, 0)),
                  pl.BlockSpec((1, D_MODEL, fc), lambda i, f: (0, 0, f)),
                  pl.BlockSpec((1, D_MODEL, fc), lambda i, f: (0, 0, f)),
                  pl.BlockSpec((1, fc, D_MODEL), lambda i, f: (0, f, 0))],
        out_specs=pl.BlockSpec((tm, D_MODEL), row),
        out_shape=jax.ShapeDtypeStruct((n, D_MODEL), F32),
        scratch_shapes=[pltpu.VMEM((tm, D_MODEL), BF16), pltpu.VMEM((tm, D_MODEL), F32)],
        compiler_params=_params(2), name="ffn_dense",
    )(x, g, wg, wu, wd)


def _pack_bf16_pairs(x):
    half = x.shape[1] // 2
    lo = lax.bitcast_convert_type(x[:, :half].astype(BF16).astype(F32), jnp.int32)
    hi = lax.bitcast_convert_type(x[:, half:].astype(BF16).astype(F32), jnp.int32)
    return (hi & -65536) | lax.shift_right_logical(lo, 16)


def _unpack_bf16_pairs(p):
    lo = lax.bitcast_convert_type(p << 16, F32)
    hi = lax.bitcast_convert_type(p & -65536, F32)
    return jnp.concatenate([lo, hi], axis=1)


def _ffn_experts_kernel(te_ref, tv_ref, x_ref, wg_ref, wu_ref, wd_ref, out_ref, xn_sc, acc_sc):
    del te_ref
    i, f = pl.program_id(0), pl.program_id(1)
    n_valid = tv_ref[i]

    @pl.when(n_valid > 0)
    def _():
        @pl.when(f == 0)
        def _():
            row = lax.broadcasted_iota(jnp.int32, x_ref.shape, 0)
            xn_sc[...] = _unpack_bf16_pairs(jnp.where(row < n_valid, x_ref[...], 0)).astype(BF16)
            acc_sc[...] = jnp.zeros_like(acc_sc)

        def step(rows):
            xn = xn_sc[:rows]
            a = jnp.dot(xn, wg_ref[0].astype(BF16), preferred_element_type=F32)
            b = jnp.dot(xn, wu_ref[0].astype(BF16), preferred_element_type=F32)
            t = (jax.nn.silu(a) * b).astype(BF16)
            acc_sc[:rows] += jnp.dot(t, wd_ref[0].astype(BF16), preferred_element_type=F32)

        tm = xn_sc.shape[0]
        quarter = tm // 4
        for nq in range(1, 5):
            pl.when((n_valid > (nq - 1) * quarter) & (n_valid <= nq * quarter))(
                functools.partial(step, nq * quarter))

        @pl.when(f == pl.num_programs(1) - 1)
        def _():
            out_ref[...] = _pack_bf16_pairs(acc_sc[...])


def _ffn_experts(tile_expert, tile_valid, x, wg, wu, wd, *, tm, fc):
    n, wpk = x.shape
    ff = wg.shape[2]
    row = lambda i, f, te, tv: (i, 0)
    grid_spec = pltpu.PrefetchScalarGridSpec(
        num_scalar_prefetch=2, grid=(n // tm, ff // fc),
        in_specs=[pl.BlockSpec((tm, wpk), row),
                  pl.BlockSpec((1, D_MODEL, fc), lambda i, f, te, tv: (te[i], 0, f)),
                  pl.BlockSpec((1, D_MODEL, fc), lambda i, f, te, tv: (te[i], 0, f)),
                  pl.BlockSpec((1, fc, D_MODEL), lambda i, f, te, tv: (te[i], f, 0))],
        out_specs=pl.BlockSpec((tm, wpk), row),
        scratch_shapes=[pltpu.VMEM((tm, D_MODEL), BF16), pltpu.VMEM((tm, D_MODEL), F32)])
    return pl.pallas_call(
        _ffn_experts_kernel, grid_spec=grid_spec,
        out_shape=jax.ShapeDtypeStruct((n, wpk), jnp.int32),
        compiler_params=_params(2), name="ffn_experts",
    )(tile_expert, tile_valid, x, wg, wu, wd)


def _sc_worker_id():
    return lax.axis_index("s") * SC_CORES + lax.axis_index("c")


def _sc_dispatch(x_a, x_b, slots_a, slots_b, n_slots):
    w = x_a.shape[1]
    per_a = x_a.shape[0] // SC_WORKERS
    per_b = x_b.shape[0] // SC_WORKERS
    ch = min(SC_CHUNK // 2, per_a)
    n_chunks = per_a // ch
    mesh = plsc.VectorSubcoreMesh(core_axis_name="c", subcore_axis_name="s")
    dma = pltpu.SemaphoreType.DMA

    @functools.partial(
        pl.kernel, mesh=mesh, out_type=jax.ShapeDtypeStruct((n_slots, w), x_a.dtype),
        scratch_types=[pltpu.VMEM((per_a,), jnp.int32), pltpu.VMEM((per_a,), jnp.int32),
                       pltpu.VMEM((ch, w), x_a.dtype), pltpu.VMEM((ch, w), x_a.dtype),
                       pltpu.VMEM((per_b,), jnp.int32), pltpu.VMEM((per_b, w), x_b.dtype),
                       dma, dma, dma, dma, dma, dma])
    def dispatch(xa_hbm, xb_hbm, a0_hbm, a1_hbm, b0_hbm, b1_hbm, out_hbm,
                 idx0_v, idx1_v, rows0_v, rows1_v, idxb_v, rowsb_v,
                 lsem0, lsem1, s0sem0, s0sem1, s1sem0, s1sem1):
        wid = _sc_worker_id()
        base_a = pl.multiple_of(wid * per_a, 8)
        pltpu.sync_copy(a0_hbm.at[pl.ds(base_a, per_a)], idx0_v)
        pltpu.sync_copy(a1_hbm.at[pl.ds(base_a, per_a)], idx1_v)
        rows, lsem = (rows0_v, rows1_v), (lsem0, lsem1)
        ssem = ((s0sem0, s0sem1), (s1sem0, s1sem1))

        def load(c):
            return pltpu.async_copy(xa_hbm.at[pl.ds(base_a + c * ch, ch)], rows[c % 2], lsem[c % 2])

        def scatter(c):
            return [pltpu.async_copy(rows[c % 2], out_hbm.at[idx_v.at[pl.ds(c * ch, ch)]],
                                     ssem[k][c % 2])
                    for k, idx_v in enumerate((idx0_v, idx1_v))]

        loads, scatters = {0: load(0)}, {}
        for c in range(n_chunks):
            loads[c].wait()
            if c + 1 < n_chunks:
                for cp in scatters.pop(c - 1, ()):
                    cp.wait()
                loads[c + 1] = load(c + 1)
            scatters[c] = scatter(c)
        for cps in scatters.values():
            for cp in cps:
                cp.wait()

        base_b = pl.multiple_of(wid * per_b, 8)
        pltpu.sync_copy(xb_hbm.at[pl.ds(base_b, per_b)], rowsb_v)
        for slots_hbm in (b0_hbm, b1_hbm):
            pltpu.sync_copy(slots_hbm.at[pl.ds(base_b, per_b)], idxb_v)
            pltpu.sync_copy(rowsb_v, out_hbm.at[idxb_v])

    return dispatch(x_a, x_b, *slots_a, *slots_b)


def _sc_gather(table, idx):
    _, w = table.shape
    b = idx.shape[0]
    per_w = b // SC_WORKERS
    ch = max(c for c in range(8, SC_CHUNK + 1, 8) if per_w % c == 0)
    n_chunks = per_w // ch
    assert per_w * SC_WORKERS == b
    mesh = plsc.VectorSubcoreMesh(core_axis_name="c", subcore_axis_name="s")
    dma = pltpu.SemaphoreType.DMA

    @functools.partial(
        pl.kernel, mesh=mesh, out_type=jax.ShapeDtypeStruct((b, w), table.dtype),
        scratch_types=[pltpu.VMEM((per_w,), jnp.int32), pltpu.VMEM((ch, w), table.dtype),
                       pltpu.VMEM((ch, w), table.dtype), dma, dma, dma, dma])
    def gather(table_hbm, idx_hbm, out_hbm, idx_v, rows0_v, rows1_v, gsem0, gsem1, wsem0, wsem1):
        wid = _sc_worker_id()
        base = pl.multiple_of(wid * per_w, 8)
        pltpu.sync_copy(idx_hbm.at[pl.ds(base, per_w)], idx_v)
        rows, gsem, wsem = (rows0_v, rows1_v), (gsem0, gsem1), (wsem0, wsem1)

        def fetch(c):
            return pltpu.async_copy(table_hbm.at[idx_v.at[pl.ds(c * ch, ch)]], rows[c % 2],
                                    gsem[c % 2])

        def write(c):
            return pltpu.async_copy(rows[c % 2], out_hbm.at[pl.ds(base + c * ch, ch)], wsem[c % 2])

        fetches, writes = {0: fetch(0)}, {}
        for c in range(n_chunks):
            fetches[c].wait()
            if c + 1 < n_chunks:
                if c >= 1:
                    writes.pop(c - 1).wait()
                fetches[c + 1] = fetch(c + 1)
            writes[c] = write(c)
        for cp in writes.values():
            cp.wait()

    return gather(table, idx)


def _combine_kernel(h_ref, sel_ref, y0_ref, y1_ref, out_ref):
    sel = sel_ref[...]
    out_ref[...] = (h_ref[...] + sel[:, 2:3] * _unpack_bf16_pairs(y0_ref[...])
                    + sel[:, 3:4] * _unpack_bf16_pairs(y1_ref[...]))


def _combine(h, sel, yb, blk_off, tm):
    n = h.shape[0]
    wpk = yb.shape[1]
    half_blocks = yb.shape[0] // 2 // tm
    return pl.pallas_call(
        _combine_kernel, grid=(n // tm,),
        in_specs=[pl.BlockSpec((tm, D_MODEL), lambda i: (i, 0)),
                  pl.BlockSpec((tm, LANES), lambda i: (i, 0)),
                  pl.BlockSpec((tm, wpk), lambda i: (i + blk_off, 0)),
                  pl.BlockSpec((tm, wpk), lambda i: (i + blk_off + half_blocks, 0))],
        out_specs=pl.BlockSpec((tm, D_MODEL), lambda i: (i, 0)),
        out_shape=jax.ShapeDtypeStruct((n, D_MODEL), F32),
        compiler_params=_params(1), name="combine",
    )(h, sel, yb, yb)


def _route_plan(sels, counts, te_rows, n_tiles):
    counts = counts[0, :N_EXPERTS].astype(jnp.int32)
    padded = (counts + te_rows - 1) // te_rows * te_rows
    ends = jnp.cumsum(padded)
    starts = ends - padded
    experts = jnp.arange(N_EXPERTS, dtype=jnp.int32)

    def slots_of(sel, k):
        e = sel[:, k].astype(jnp.int32)
        start = jnp.sum(jnp.where(e[:, None] == experts[None, :], starts[None, :], 0), axis=1)
        return start + sel[:, 4 + k].astype(jnp.int32)

    slots = [(slots_of(sel, 0), slots_of(sel, 1)) for sel in sels]
    tile_start = jnp.arange(n_tiles, dtype=jnp.int32) * te_rows
    tile_expert = jnp.minimum(jnp.sum(tile_start[:, None] >= ends[None, :], axis=1),
                              N_EXPERTS - 1).astype(jnp.int32)
    in_group = tile_start - starts[tile_expert]
    tile_valid = jnp.clip(counts[tile_expert] - in_group, 0, te_rows)
    tile_valid = jnp.where(tile_start < ends[-1], tile_valid, 0).astype(jnp.int32)
    return slots, tile_expert, tile_valid


def _gmlp_kernel(h_ref, g_ref, wuv_ref, vg_ref, mix_ref, bs_ref, wo_ref, out_ref, v_ref, um_sc,
                 *, period):
    h = h_ref[...]
    tm = h.shape[0]
    ln = bs_ref.shape[0]
    gw = D_MODEL // C_GROUPS
    xn = _rms_bf16(h, g_ref[...])
    vv = jax.nn.gelu(jnp.dot(xn, wuv_ref[:, D_MODEL:], preferred_element_type=F32))
    v = vv * lax.rsqrt(jnp.mean(vv * vv, axis=-1, keepdims=True) + EPS) * vg_ref[...]
    v_ref[0] = v[tm - v_ref.shape[1]:]
    vb = v.astype(BF16)
    u = jax.nn.gelu(jnp.dot(xn, wuv_ref[:, :D_MODEL], preferred_element_type=F32))
    r = lax.broadcasted_iota(jnp.int32, (ln, ln), 0)
    c = lax.broadcasted_iota(jnp.int32, (ln, ln), 1)
    causal = (r // period == c // period) & (c % period <= r % period)
    nw = mix_ref.shape[1]
    if ln != nw:
        pick = (lax.broadcasted_iota(jnp.int32, (ln, nw), 0) % period
                == lax.broadcasted_iota(jnp.int32, (ln, nw), 1)).astype(BF16)
        wr = lax.broadcasted_iota(jnp.int32, (nw, nw), 0)
        wc = lax.broadcasted_iota(jnp.int32, (nw, nw), 1)
        corner = (wr < period) & (wc <= wr)
    for gi in range(C_GROUPS):
        if ln == nw:
            wm = jnp.where(causal, mix_ref[gi], 0.0).astype(BF16)
        else:
            rows_w = jnp.dot(pick, jnp.where(corner, mix_ref[gi], 0.0).astype(BF16),
                             preferred_element_type=F32).astype(BF16)
            tiled = lax.dot_general(rows_w, pick, (((1,), (1,)), ((), ())),
                                    preferred_element_type=F32)
            wm = jnp.where(causal, tiled, 0.0).astype(BF16)
        bias = bs_ref[:, gi:gi + 1]
        for ch in range(tm // ln):
            rs = slice(ch * ln, (ch + 1) * ln)
            cs = slice(gi * gw, (gi + 1) * gw)
            mixed = jnp.dot(wm, vb[rs, cs], preferred_element_type=F32) + bias
            um_sc[rs, cs] = (u[rs, cs] * mixed).astype(BF16)
    out_ref[...] = h + jnp.dot(um_sc[...], wo_ref[...], preferred_element_type=F32)


def _gmlp(h, g, wuv16, vg, mix, bs, wo16, *, tm, period, seq, tail):
    n = h.shape[0]
    tps = seq // tm
    row = lambda i: (i, 0)
    full = lambda a: pl.BlockSpec(a.shape, lambda i: (0,) * a.ndim)
    return pl.pallas_call(
        functools.partial(_gmlp_kernel, period=period),
        grid=(n // tm,),
        in_specs=[pl.BlockSpec((tm, D_MODEL), row), full(g), full(wuv16), full(vg), full(mix),
                  full(bs), full(wo16)],
        out_specs=[pl.BlockSpec((tm, D_MODEL), row),
                   pl.BlockSpec((1, tail, D_MODEL), lambda i: (i // tps, 0, 0))],
        out_shape=[jax.ShapeDtypeStruct((n, D_MODEL), F32),
                   jax.ShapeDtypeStruct((n // seq, tail, D_MODEL), F32)],
        scratch_shapes=[pltpu.VMEM((tm, D_MODEL), BF16)],
        compiler_params=_params(1), name="gmlp",
    )(h, g, wuv16, vg, mix, bs, wo16)


def _router_kernel(h_ref, g_ref, wr_ref, cin_ref, xn_ref, sel_ref, cnt_ref):
    @pl.when(pl.program_id(0) == 0)
    def _():
        cnt_ref[...] = cin_ref[...]

    x = h_ref[...]
    xn = x * lax.rsqrt(jnp.mean(x * x, axis=-1, keepdims=True) + EPS) * g_ref[...]
    xn_ref[...] = _pack_bf16_pairs(xn)
    x_hi = xn.astype(BF16)
    x_lo = (xn - x_hi.astype(F32)).astype(BF16)
    w = wr_ref[...]
    w_hi = w.astype(BF16)
    w_lo = (w - w_hi.astype(F32)).astype(BF16)
    logits = (jnp.dot(x_hi, w_hi, preferred_element_type=F32)
              + (jnp.dot(x_lo, w_hi, preferred_element_type=F32)
                 + jnp.dot(x_hi, w_lo, preferred_element_type=F32)))
    lane = lax.broadcasted_iota(jnp.int32, logits.shape, 1)
    lanef = lane.astype(F32)
    logits = jnp.where(lane < N_EXPERTS, logits, -jnp.inf)
    m1 = jnp.max(logits, axis=-1, keepdims=True)
    i1 = jnp.min(jnp.where(logits == m1, lanef, float(LANES)), axis=-1, keepdims=True)
    rest = jnp.where(lanef == i1, -jnp.inf, logits)
    m2 = jnp.max(rest, axis=-1, keepdims=True)
    i2 = jnp.min(jnp.where(rest == m2, lanef, float(LANES)), axis=-1, keepdims=True)
    e2 = jnp.exp(m2 - m1)
    w1 = 1.0 / (1.0 + e2)
    w2 = e2 / (1.0 + e2)
    tm = x.shape[0]
    hit1, hit2 = lanef == i1, lanef == i2
    picked = jnp.where(hit1 | hit2, 1.0, 0.0)
    r_i = lax.broadcasted_iota(jnp.int32, (tm, tm), 0)
    c_i = lax.broadcasted_iota(jnp.int32, (tm, tm), 1)
    below = jnp.where(c_i < r_i, 1.0, 0.0).astype(BF16)
    before = jnp.dot(below, picked.astype(BF16), preferred_element_type=F32) + cnt_ref[...]
    rank1 = jnp.sum(jnp.where(hit1, before, 0.0), axis=-1, keepdims=True)
    rank2 = jnp.sum(jnp.where(hit2, before, 0.0), axis=-1, keepdims=True)
    cnt_ref[...] += jnp.sum(picked, axis=0, keepdims=True)
    vals = (i1, i2, w1, w2, rank1, rank2)
    sel = jnp.zeros_like(logits)
    for li, val in enumerate(vals):
        sel = jnp.where(lane == li, val, sel)
    sel_ref[...] = sel


def _router(h, g, wr_pad, counts_in, tm):
    n = h.shape[0]
    row = lambda i: (i, 0)
    fix = lambda i: (0, 0)
    return pl.pallas_call(
        _router_kernel, grid=(n // tm,),
        in_specs=[pl.BlockSpec((tm, D_MODEL), row), pl.BlockSpec((1, D_MODEL), fix),
                  pl.BlockSpec((D_MODEL, LANES), fix), pl.BlockSpec((1, LANES), fix)],
        out_specs=[pl.BlockSpec((tm, D_MODEL // 2), row), pl.BlockSpec((tm, LANES), row),
                   pl.BlockSpec((1, LANES), fix)],
        out_shape=[jax.ShapeDtypeStruct((n, D_MODEL // 2), jnp.int32),
                   jax.ShapeDtypeStruct((n, LANES), F32),
                   jax.ShapeDtypeStruct((1, LANES), F32)],
        compiler_params=_params(1), name="router",
    )(h, g, wr_pad, counts_in)


def _rope_tables(pos):
    half = HEAD_DIM // 2
    inv = ROPE_THETA ** (-jnp.arange(half, dtype=F32) / half)
    ang = pos.astype(F32)[:, None] * inv[None, :]
    cos, sin = jnp.cos(ang), jnp.sin(ang)
    zero = jnp.zeros_like(sin)
    rep = lambda a, b: jnp.tile(jnp.concatenate([a, b], axis=-1), (1, LANES // HEAD_DIM))
    return rep(cos, cos), rep(-sin, zero), rep(zero, sin)


def _tile_for(n, pref):
    return pref if n % pref == 0 else n


def kernel(x_prompt, x_sample, state_conv, cache_kv_win, norm_mix_e, w_in_e, conv_w, q_gain,
           k_gain, w_out_e, norm_ffn_e, ffn_w_gate, ffn_w_up, ffn_w_down, norm_mix_o, w_uv,
           v_gain, w_s, b_s, w_out_o, norm_moe, w_router, moe_w_gate, moe_w_up, moe_w_down):
    bp, seq, _ = x_prompt.shape
    bs_, s_len, _ = x_sample.shape
    n_p, n_s = bp * seq, bs_ * s_len
    hp = x_prompt.reshape(n_p, D_MODEL)
    hs = x_sample.reshape(n_s, D_MODEL)
    tm_p = 512
    row = lambda a: a.reshape(1, -1)

    tabs_p = _rope_tables(jnp.arange(seq, dtype=jnp.int32))
    tabs_s = tuple(jnp.tile(t, (bs_, 1)) for t in
                   _rope_tables(PAST_LEN + jnp.arange(s_len, dtype=jnp.int32)))
    hd = jnp.arange(256) // HEAD_DIM
    bd = (hd[:, None] == hd[None, :]).astype(BF16)

    w_in16 = w_in_e[0].astype(BF16)
    w_out16 = w_out_e[0].astype(BF16)
    qg = jnp.tile(q_gain[0], 4).reshape(1, 256)
    kg = jnp.tile(k_gain[0], 4).reshape(1, 256)
    g0 = row(norm_mix_e[0])

    dils = tuple(dil for _, dil in BRANCHES)
    lw = min(2048, seq)
    z_p, gb_p, kvt_p, *planes = _inproj(hp, g0, w_in16, tabs_p, qg, kg, bd, tm_p, seq, dils, lw)
    z_s, gb_s, kv_s, q_s = _inproj(hs, g0, w_in16, tabs_s, qg, kg, bd, n_s, n_s, ())

    os_, ls_ = [], []
    for di, dil in enumerate(dils):
        m = seq // dil
        q_d, k_d, v_d = (a.reshape(bp * dil, m, D_B) for a in planes[3 * di:3 * di + 3])
        o, lse = _attn_prompt(q_d, k_d, v_d, qb=min(512, m))
        os_.append(o.reshape(bp, dil, m, D_B))
        ls_.append(lse.reshape(bp, dil, m, LANES))
    hp = _outproj_prompt(z_p, gb_p, conv_w[0], os_, ls_, w_out16, hp, tm_p, seq, dils)

    cat_s = _mix_sample(q_s.reshape(bs_, s_len, D_B),
                        kv_s.reshape(bs_, s_len, 2 * D_B),
                        jnp.transpose(cache_kv_win[0], (0, 2, 3, 4, 1)).reshape(
                            bs_, 2, D_B, -1),
                        z_s.reshape(bs_, s_len, D_A), gb_s.reshape(bs_, s_len, D_A),
                        state_conv[0], conv_w[0])
    hs = _linear_res(cat_s.reshape(n_s, D_MODEL), w_out16, hs)

    conv_prompt = z_p.reshape(bp, seq, D_A)[:, seq - 2:][None]
    conv_sample = z_s.reshape(bs_, s_len, D_A)[:, s_len - 2:][None]
    kv_prompt = jnp.transpose(kvt_p.reshape(bp, 2, H_B, HEAD_DIM, lw), (0, 4, 1, 2, 3))[None]
    kv_sample = kv_s.reshape(bs_, s_len, 2, H_B, HEAD_DIM)[None]

    fg, fu, fd = (ffn_w_gate.astype(BF16), ffn_w_up.astype(BF16), ffn_w_down.astype(BF16))
    gf = row(norm_ffn_e[0])
    hp = _ffn_dense(hp, gf, fg, fu, fd, tm=2 * tm_p, fc=1408)
    hs = _ffn_dense(hs, gf, fg, fu, fd, tm=n_s, fc=1408)

    wuv16 = w_uv[0].astype(BF16)
    wo16 = w_out_o[0].astype(BF16)
    g1, vg = row(norm_mix_o[0]), row(v_gain[0])
    tail = seq - ((seq - 1) // CHUNK) * CHUNK
    hp, v_tail_p = _gmlp(hp, g1, wuv16, vg, w_s[0], b_s[0].T, wo16, tm=tm_p, period=CHUNK,
                         seq=seq, tail=tail)
    bias_s = jnp.tile(b_s[0][:, :s_len].T, (bs_, 1))
    hs, v_all_s = _gmlp(hs, g1, wuv16, vg, w_s[0], bias_s, wo16, tm=n_s, period=s_len,
                        seq=n_s, tail=n_s)
    v_prompt = v_tail_p[None]
    v_sample = v_all_s.reshape(bs_, s_len, D_MODEL)[None]

    wr_pad = jnp.pad(w_router[0], ((0, 0), (0, LANES - N_EXPERTS)))
    gm = row(norm_moe[0])
    xn_p, sel_p, counts = _router(hp, gm, wr_pad, jnp.zeros((1, LANES), F32), tm_p)
    xn_s, sel_s, counts = _router(hs, gm, wr_pad, counts, n_s)
    n_tok = n_p + n_s
    n_tiles = -(-(2 * n_tok + N_EXPERTS * (MOE_ROWS - 1)) // MOE_ROWS)
    (slots_p, slots_s), tile_expert, tile_valid = _route_plan(
        (sel_p, sel_s), counts, MOE_ROWS, n_tiles)
    slot0 = jnp.concatenate([slots_p[0], slots_s[0]])
    slot1 = jnp.concatenate([slots_p[1], slots_s[1]])
    xs = _sc_dispatch(xn_p, xn_s, slots_p, slots_s, n_tiles * MOE_ROWS)
    ys = _ffn_experts(tile_expert, tile_valid, xs, moe_w_gate[0], moe_w_up[0], moe_w_down[0],
                      tm=MOE_ROWS, fc=896)
    yb = _sc_gather(ys, jnp.concatenate([slot0, slot1]))
    hp = _combine(hp, sel_p, yb, 0, n_s)
    hs = _combine(hs, sel_s, yb, n_p // n_s, n_s)

    return (hp.reshape(bp, seq, D_MODEL), hs.reshape(bs_, s_len, D_MODEL),
            conv_prompt, conv_sample, kv_prompt, kv_sample, v_prompt, v_sample)
```

```python
import functools

import jax
import jax.numpy as jnp
from jax import lax
from jax.experimental import pallas as pl
from jax.experimental.pallas import tpu as pltpu
from jax.experimental.pallas import tpu_sc as plsc

F32 = jnp.float32
BF16 = jnp.bfloat16

D_MODEL = 1024
D_A = 512
H_B = 8
HEAD_DIM = 64
D_B = H_B * HEAD_DIM
BRANCHES = ((128, 1), (512, 4), (2048, 16))
BAND = 128
ROPE_THETA = 10000.0
PAST_LEN = 16384
N_EXPERTS = 8
C_GROUPS = 4
CHUNK = 128
EPS = 1e-6
LANES = 128
VMEM_LIMIT = 56 * 1024 * 1024
SC_CORES = 2
SC_WORKERS = SC_CORES * 16
SC_CHUNK = 128
MOE_ROWS = 1024


def _params(n_axes):
    return pltpu.CompilerParams(dimension_semantics=("arbitrary",) * n_axes,
                                vmem_limit_bytes=VMEM_LIMIT)


def _rms_bf16(x, g):
    ms = jnp.mean(x * x, axis=-1, keepdims=True)
    return (x * lax.rsqrt(ms + EPS) * g).astype(BF16)


def _inproj_kernel(x_ref, g_ref, w_ref, cos_ref, sina_ref, sinb_ref, qg_ref, kg_ref, bd_ref,
                   z_ref, gb_ref, kv_ref, *rest, dils, win_tiles, tiles_per_seq):
    if dils:
        planes, qkv_sc = rest[:-1], rest[-1]
    else:
        (q_out,) = rest
    xn = _rms_bf16(x_ref[...], g_ref[...])

    def col(j):
        return jnp.dot(xn, w_ref[:, j * 512:(j + 1) * 512], preferred_element_type=F32)

    z_ref[...] = col(1) * col(0)
    gb_ref[...] = col(2).astype(gb_ref.dtype)
    cos, sina, sinb = cos_ref[...], sina_ref[...], sinb_ref[...]

    def head_norm_rope(t, gain):
        outs = []
        for c2 in range(2):
            tc = t[:, c2 * 256:(c2 + 1) * 256]
            ss = jnp.dot((tc * tc).astype(BF16), bd_ref[...], preferred_element_type=F32)
            tn = tc * lax.rsqrt(ss * (1.0 / HEAD_DIM) + EPS) * gain
            for c in range(2):
                u = tn[:, c * LANES:(c + 1) * LANES]
                outs.append(u * cos + pltpu.roll(u, LANES - 32, 1) * sina
                            + pltpu.roll(u, 32, 1) * sinb)
        return outs

    q = [t * (HEAD_DIM ** -0.5) for t in head_norm_rope(col(3), qg_ref[...])]
    k = head_norm_rope(col(4), kg_ref[...])
    v = col(5)
    if not dils:
        for c in range(4):
            kv_ref[:, c * LANES:(c + 1) * LANES] = k[c]
            q_out[:, c * LANES:(c + 1) * LANES] = q[c]
        kv_ref[:, D_B:] = v
        return

    @pl.when(pl.program_id(0) % tiles_per_seq >= tiles_per_seq - win_tiles)
    def _():
        for c in range(4):
            kv_ref[0, c * LANES:(c + 1) * LANES, :] = k[c].T
            kv_ref[0, D_B + c * LANES:D_B + (c + 1) * LANES, :] = v[:, c * LANES:(c + 1) * LANES].T

    tm = x_ref.shape[0]
    for a, val in enumerate((q, k, [v[:, c * LANES:(c + 1) * LANES] for c in range(4)])):
        for c in range(4):
            qkv_sc[0, a, c] = val[c]
            planes[a][0, 0, :, c * LANES:(c + 1) * LANES] = val[c].astype(BF16)
    for di in range(1, len(dils)):
        dil, prev = dils[di], dils[di - 1]
        ratio = dil // prev
        src, dst = qkv_sc.at[(di - 1) % 2], qkv_sc.at[di % 2]
        for r in range(dil):
            start = (r % prev) * (tm // prev) + r // prev
            for a in range(3):
                for c in range(4):
                    rows_ac = src[a, c, pl.ds(start, tm // dil, stride=ratio), :]
                    planes[3 * di + a][0, r, :, c * LANES:(c + 1) * LANES] = rows_ac.astype(BF16)
                    if di + 1 < len(dils):
                        dst[a, c, r * (tm // dil):(r + 1) * (tm // dil), :] = rows_ac


def _inproj(x, g, w16, tabs, qg, kg, bd, tm, seq, dils, win=0):
    n = x.shape[0]
    cos, sina, sinb = tabs
    tt = cos.shape[0] // tm
    tps = seq // tm
    win_tiles = win // tm
    row = lambda i: (i, 0)
    fix = lambda i: (0, 0)
    tab = lambda i: (i % tt, 0)
    out_specs = [pl.BlockSpec((tm, D_A), row), pl.BlockSpec((tm, D_A), row)]
    out_shape = [jax.ShapeDtypeStruct((n, D_A), F32),
                 jax.ShapeDtypeStruct((n, D_A), BF16 if dils else F32)]
    if dils:
        out_specs.append(pl.BlockSpec(
            (1, 2 * D_B, tm),
            lambda i: (i // tps, 0, jnp.maximum(i % tps - (tps - win_tiles), 0))))
        out_shape.append(jax.ShapeDtypeStruct((n // seq, 2 * D_B, win), F32))
        assert dils[0] == 1 and all(b % a == 0 for a, b in zip(dils, dils[1:]))
        scratch = [pltpu.VMEM((2, 3, D_B // LANES, tm, LANES), F32)]
    else:
        out_specs += [pl.BlockSpec((tm, 2 * D_B), row), pl.BlockSpec((tm, D_B), row)]
        out_shape += [jax.ShapeDtypeStruct((n, 2 * D_B), F32),
                      jax.ShapeDtypeStruct((n, D_B), F32)]
        scratch = []
    for dil in dils:
        for _ in range(3):
            out_specs.append(pl.BlockSpec((1, dil, tm // dil, D_B),
                                          lambda i: (i // tps, 0, i % tps, 0)))
            out_shape.append(jax.ShapeDtypeStruct((n // seq, dil, seq // dil, D_B), BF16))
    return pl.pallas_call(
        functools.partial(_inproj_kernel, dils=dils, win_tiles=win_tiles, tiles_per_seq=tps),
        grid=(n // tm,),
        in_specs=[pl.BlockSpec((tm, D_MODEL), row), pl.BlockSpec((1, D_MODEL), fix),
                  pl.BlockSpec(w16.shape, fix),
                  pl.BlockSpec((tm, LANES), tab), pl.BlockSpec((tm, LANES), tab),
                  pl.BlockSpec((tm, LANES), tab),
                  pl.BlockSpec((1, 256), fix), pl.BlockSpec((1, 256), fix),
                  pl.BlockSpec((256, 256), fix)],
        out_specs=out_specs, out_shape=out_shape, scratch_shapes=scratch,
        compiler_params=_params(1), name="inproj",
    )(x, g, w16, cos, sina, sinb, qg, kg, bd)


def _attn_prompt_kernel(q_ref, kc_ref, kp_ref, vc_ref, vp_ref, o_ref, lse_ref, *, qb):
    j = pl.program_id(1)
    nsub = qb // BAND
    lane = lax.broadcasted_iota(jnp.int32, (1, LANES), 1)
    head0 = lane < HEAD_DIM
    qi = lax.broadcasted_iota(jnp.int32, (2 * BAND, 2 * BAND), 0) % BAND + BAND
    kj = lax.broadcasted_iota(jnp.int32, (2 * BAND, 2 * BAND), 1)
    dist = qi - kj
    band = (dist >= 0) & (dist <= BAND)
    first = band & ((kj >= BAND) | (j > 0))
    lanes = [slice(hp * LANES, (hp + 1) * LANES) for hp in range(D_B // LANES)]
    kcats = [jnp.concatenate([kp_ref[0, :, ls], kc_ref[0, :, ls]], axis=0) for ls in lanes]
    vcats = [jnp.concatenate([vp_ref[0, :, ls], vc_ref[0, :, ls]], axis=0) for ls in lanes]
    for sub in range(nsub):
        rows = slice(sub * BAND, (sub + 1) * BAND)
        lse_all = jnp.zeros((BAND, LANES), F32)
        for hp, ls in enumerate(lanes):
            qs = q_ref[0, rows, ls]
            ks = kcats[hp][sub * BAND:(sub + 2) * BAND]
            vs = vcats[hp][sub * BAND:(sub + 2) * BAND]
            q2 = jnp.concatenate([jnp.where(head0, qs, jnp.zeros_like(qs)),
                                  jnp.where(head0, jnp.zeros_like(qs), qs)], axis=0)
            s = lax.dot_general(q2, ks, (((1,), (1,)), ((), ())), preferred_element_type=F32)
            s = jnp.where(first if sub == 0 else band, s, -jnp.inf)
            mx = jnp.max(s, axis=-1, keepdims=True)
            p = jnp.exp(s - mx)
            l = jnp.sum(p, axis=-1, keepdims=True)
            pv = jnp.dot(p.astype(BF16), vs, preferred_element_type=F32) / l
            lse = mx + jnp.log(l)
            o_ref[0, rows, ls] = jnp.where(head0, pv[:BAND], pv[BAND:]).astype(BF16)
            lse_all = jnp.where(lane == 2 * hp, lse[:BAND],
                                jnp.where(lane == 2 * hp + 1, lse[BAND:], lse_all))
        lse_ref[0, rows, :] = lse_all


def _attn_prompt(q, k, v, qb):
    b, m, w = q.shape
    nsub = qb // BAND
    cur = lambda bi, j: (bi, j, 0)
    prev = lambda bi, j: (bi, jnp.maximum(j * nsub - 1, 0), 0)
    return pl.pallas_call(
        functools.partial(_attn_prompt_kernel, qb=qb),
        grid=(b, m // qb),
        in_specs=[pl.BlockSpec((1, qb, w), cur), pl.BlockSpec((1, qb, w), cur),
                  pl.BlockSpec((1, BAND, w), prev), pl.BlockSpec((1, qb, w), cur),
                  pl.BlockSpec((1, BAND, w), prev)],
        out_specs=[pl.BlockSpec((1, qb, w), cur), pl.BlockSpec((1, qb, LANES), cur)],
        out_shape=[jax.ShapeDtypeStruct((b, m, w), BF16),
                   jax.ShapeDtypeStruct((b, m, LANES), F32)],
        compiler_params=_params(2), name="attn_prompt",
    )(q, k, k, v, v)


def _outproj_prompt_kernel(z_ref, zh_ref, gb_ref, cw_ref, *rest, tiles_per_batch, dils):
    nb = len(dils)
    o_refs, l_refs = rest[:nb], rest[nb:2 * nb]
    w_ref, h_ref, out_ref = rest[2 * nb:2 * nb + 3]
    o_scs, l_scs, tmp_sc = rest[2 * nb + 3:3 * nb + 3], rest[3 * nb + 3:4 * nb + 3], rest[-1]
    i = pl.program_id(0)
    z = z_ref[...]
    tm = z.shape[0]
    def put_rows(dst, spare, val, r, dil):
        if dil == 1:
            dst[...] = val
        elif dil <= 4:
            dst[pl.ds(r, tm // dil, stride=dil), :] = val
        else:
            spare[pl.ds((r % 4) * (tm // 4) + r // 4, tm // dil, stride=dil // 4), :] = val

    def finish_rows(dst, spare, dil):
        if dil > 4:
            for b in range(4):
                dst[pl.ds(b, tm // 4, stride=4), :] = spare[b * (tm // 4):(b + 1) * (tm // 4), :]

    for dil, o_ref, l_ref, o_sc, l_sc in zip(dils, o_refs, l_refs, o_scs, l_scs):
        assert dil <= 16
        for r in range(dil):
            for c in range(D_B // LANES):
                ls = slice(c * LANES, (c + 1) * LANES)
                put_rows(o_sc.at[c], tmp_sc.at[c], o_ref[0, r, :, ls].astype(F32), r, dil)
            put_rows(l_sc, tmp_sc.at[D_B // LANES], l_ref[0, r], r, dil)
        for c in range(D_B // LANES):
            finish_rows(o_sc.at[c], tmp_sc.at[c], dil)
        finish_rows(l_sc, tmp_sc.at[D_B // LANES], dil)
    unstage = lambda sc: jnp.concatenate([sc[c] for c in range(D_B // LANES)], axis=1)
    zh = jnp.where(i % tiles_per_batch == 0, 0.0, zh_ref[...])
    row = lax.broadcasted_iota(jnp.int32, (tm, 1), 0)
    zm1 = jnp.where(row >= 1, pltpu.roll(z, 1, 0), zh[7:8])
    zm2 = jnp.where(row >= 2, pltpu.roll(z, 2, 0), jnp.where(row == 1, zh[7:8], zh[6:7]))
    cw = cw_ref[...]
    a_out = gb_ref[...].astype(F32) * (cw[0:1] * zm2 + cw[1:2] * zm1 + cw[2:3] * z)
    lses = [l_sc[...] for l_sc in l_scs]
    lm = functools.reduce(jnp.maximum, lses)
    es = [jnp.exp(l - lm) for l in lses]
    tot = sum(es)
    spread = (lax.broadcasted_iota(jnp.int32, (LANES, D_B), 1) // HEAD_DIM
              == lax.broadcasted_iota(jnp.int32, (LANES, D_B), 0)).astype(BF16)

    def widen(wgt):
        hi = wgt.astype(BF16)
        lo = (wgt - hi.astype(F32)).astype(BF16)
        return (jnp.dot(hi, spread, preferred_element_type=F32)
                + jnp.dot(lo, spread, preferred_element_type=F32))

    o = sum(widen(e / tot) * unstage(o_sc) for e, o_sc in zip(es, o_scs))
    y = jnp.dot(a_out.astype(BF16), w_ref[:D_A, :], preferred_element_type=F32)
    y = y + jnp.dot(o.astype(BF16), w_ref[D_A:, :], preferred_element_type=F32)
    out_ref[...] = h_ref[...] + y


def _outproj_prompt(z, gb, cw, os_, ls_, w16, h, tm, seq, dils):
    n = z.shape[0]
    tps = seq // tm
    row = lambda i: (i, 0)
    fix = lambda i: (0, 0)
    halo = lambda i: (jnp.maximum(i * (tm // 8) - 1, 0), 0)
    half = pl.BlockSpec((tm, D_A), row)
    plane = lambda width: [pl.BlockSpec((1, dil, tm // dil, width),
                                        lambda i: (i // tps, 0, i % tps, 0)) for dil in dils]
    return pl.pallas_call(
        functools.partial(_outproj_prompt_kernel, tiles_per_batch=tps, dils=dils),
        grid=(n // tm,),
        in_specs=[half, pl.BlockSpec((8, D_A), halo), half, pl.BlockSpec((3, D_A), fix),
                  *plane(D_B), *plane(LANES),
                  pl.BlockSpec((D_MODEL, D_MODEL), fix), pl.BlockSpec((tm, D_MODEL), row)],
        out_specs=pl.BlockSpec((tm, D_MODEL), row),
        out_shape=jax.ShapeDtypeStruct((n, D_MODEL), F32),
        scratch_shapes=([pltpu.VMEM((D_B // LANES, tm, LANES), F32)] * len(dils)
                        + [pltpu.VMEM((tm, LANES), F32)] * len(dils)
                        + [pltpu.VMEM((D_B // LANES + 1, tm, LANES), F32)]),
        compiler_params=_params(1), name="outproj_prompt",
    )(z, z, gb, cw, *os_, *ls_, w16, h)


def _mix_sample_kernel(q_ref, kvn_ref, win_ref, z_ref, gb_ref, st_ref, cw_ref, cat_ref):
    s_len = q_ref.shape[1]
    rows = H_B * s_len
    npad = LANES
    q = q_ref[0]
    r_i = lax.broadcasted_iota(jnp.int32, (rows, D_B), 0)
    c_i = lax.broadcasted_iota(jnp.int32, (rows, D_B), 1)
    qt = jnp.concatenate([q] * H_B, axis=0)
    qbd = jnp.where(r_i // s_len == c_i // HEAD_DIM, qt, 0.0).astype(BF16)
    kvn = jnp.concatenate([kvn_ref[0], jnp.zeros((npad - s_len, 2 * D_B), F32)], axis=0)

    def multiplicity(delta):
        cnt = jnp.zeros(delta.shape, F32)
        for _, dil in BRANCHES:
            ok = (delta >= 0) & (delta <= BAND * dil) & ((delta & (dil - 1)) == 0)
            cnt = cnt + jnp.where(ok, 1.0, 0.0)
        return cnt

    lb = win_ref.shape[3]
    nt = (((1,), (1,)), ((), ()))
    tok_c = lax.broadcasted_iota(jnp.int32, (rows, lb), 0) % s_len
    cnt_c = multiplicity(tok_c + lb - lax.broadcasted_iota(jnp.int32, (rows, lb), 1))
    tok_n = lax.broadcasted_iota(jnp.int32, (rows, npad), 0) % s_len
    cnt_n = multiplicity(tok_n - lax.broadcasted_iota(jnp.int32, (rows, npad), 1))
    sc = jnp.dot(qbd, win_ref[0, 0].astype(BF16), preferred_element_type=F32)
    sn = lax.dot_general(qbd, kvn[:, :D_B].astype(BF16), nt, preferred_element_type=F32)
    sc = jnp.where(cnt_c > 0, sc, -jnp.inf)
    sn = jnp.where(cnt_n > 0, sn, -jnp.inf)
    mx = jnp.maximum(jnp.max(sc, axis=-1, keepdims=True), jnp.max(sn, axis=-1, keepdims=True))
    pc = cnt_c * jnp.exp(sc - mx)
    pn = cnt_n * jnp.exp(sn - mx)
    den = jnp.sum(pc, axis=-1, keepdims=True) + jnp.sum(pn, axis=-1, keepdims=True)
    of = lax.dot_general(pc.astype(BF16), win_ref[0, 1].astype(BF16), nt,
                         preferred_element_type=F32)
    of = (of + jnp.dot(pn.astype(BF16), kvn[:, D_B:].astype(BF16),
                       preferred_element_type=F32)) / den
    lane_head = lax.broadcasted_iota(jnp.int32, (s_len, D_B), 1) // HEAD_DIM
    o = jnp.zeros((s_len, D_B), F32)
    for h in range(H_B):
        o = o + jnp.where(lane_head == h, of[h * s_len:(h + 1) * s_len], 0.0)
    z = z_ref[0]
    st = st_ref[0]
    t = lax.broadcasted_iota(jnp.int32, (s_len, 1), 0)
    zm1 = jnp.where(t >= 1, pltpu.roll(z, 1, 0), st[1:2])
    zm2 = jnp.where(t >= 2, pltpu.roll(z, 2, 0), jnp.where(t == 1, st[1:2], st[0:1]))
    cw = cw_ref[...]
    cat_ref[0, :, :D_A] = gb_ref[0] * (cw[0:1] * zm2 + cw[1:2] * zm1 + cw[2:3] * z)
    cat_ref[0, :, D_A:] = o


def _mix_sample(q, kvn, win, z, gb, st, cw):
    b, s_len, _ = q.shape
    per = lambda i: (i,) + (0,) * 2
    blk = lambda a: pl.BlockSpec((1,) + a.shape[1:], per)
    return pl.pallas_call(
        _mix_sample_kernel,
        grid=(b,),
        in_specs=[blk(q), blk(kvn),
                  pl.BlockSpec((1,) + win.shape[1:], lambda i: (i, 0, 0, 0)),
                  blk(z), blk(gb), blk(st), pl.BlockSpec((3, D_A), lambda i: (0, 0))],
        out_specs=pl.BlockSpec((1, s_len, D_MODEL), per),
        out_shape=jax.ShapeDtypeStruct((b, s_len, D_MODEL), F32),
        compiler_params=_params(1), name="mix_sample",
    )(q, kvn, win, z, gb, st, cw)


def _linear_res_kernel(x_ref, w_ref, h_ref, out_ref):
    out_ref[...] = h_ref[...] + jnp.dot(x_ref[...].astype(BF16), w_ref[...],
                                        preferred_element_type=F32)


def _linear_res(x, w16, h):
    n, k = x.shape
    full = lambda a: pl.BlockSpec(a.shape, lambda i: (0, 0))
    return pl.pallas_call(
        _linear_res_kernel, grid=(1,),
        in_specs=[full(x), full(w16), full(h)], out_specs=full(h),
        out_shape=jax.ShapeDtypeStruct(h.shape, F32),
        compiler_params=_params(1), name="linear_res",
    )(x, w16, h)


def _swiglu_step(xn, wg_ref, wu_ref, wd_ref, acc_sc):
    a = jnp.dot(xn, wg_ref[0], preferred_element_type=F32)
    b = jnp.dot(xn, wu_ref[0], preferred_element_type=F32)
    t = (jax.nn.silu(a) * b).astype(BF16)
    acc_sc[...] += jnp.dot(t, wd_ref[0], preferred_element_type=F32)


def _ffn_dense_kernel(x_ref, g_ref, wg_ref, wu_ref, wd_ref, out_ref, xn_sc, acc_sc):
    f = pl.program_id(1)

    @pl.when(f == 0)
    def _():
        xn_sc[...] = _rms_bf16(x_ref[...], g_ref[...])
        acc_sc[...] = jnp.zeros_like(acc_sc)

    _swiglu_step(xn_sc[...], wg_ref, wu_ref, wd_ref, acc_sc)

    @pl.when(f == pl.num_programs(1) - 1)
    def _():
        out_ref[...] = x_ref[...] + acc_sc[...]


def _ffn_dense(x, g, wg, wu, wd, *, tm, fc):
    n = x.shape[0]
    ff = wg.shape[2]
    row = lambda i, f: (i, 0)
    return pl.pallas_call(
        _ffn_dense_kernel, grid=(n // tm, ff // fc),
        in_specs=[pl.BlockSpec((tm, D_MODEL), row), pl.BlockSpec((1, D_MODEL), lambda i, f: (0, 0)),
                  pl.BlockSpec((1, D_MODEL, fc), lambda i, f: (0, 0, f)),
                  pl.BlockSpec((1, D_MODEL, fc), lambda i, f: (0, 0, f)),
                  pl.BlockSpec((1, fc, D_MODEL), lambda i, f: (0, f, 0))],
        out_specs=pl.BlockSpec((tm, D_MODEL), row),
        out_shape=jax.ShapeDtypeStruct((n, D_MODEL), F32),
        scratch_shapes=[pltpu.VMEM((tm, D_MODEL), BF16), pltpu.VMEM((tm, D_MODEL), F32)],
        compiler_params=_params(2), name="ffn_dense",
    )(x, g, wg, wu, wd)


def _pack_bf16_pairs(x):
    half = x.shape[1] // 2
    lo = lax.bitcast_convert_type(x[:, :half].astype(BF16).astype(F32), jnp.int32)
    hi = lax.bitcast_convert_type(x[:, half:].astype(BF16).astype(F32), jnp.int32)
    return (hi & -65536) | lax.shift_right_logical(lo, 16)


def _unpack_bf16_pairs(p):
    lo = lax.bitcast_convert_type(p << 16, F32)
    hi = lax.bitcast_convert_type(p & -65536, F32)
    return jnp.concatenate([lo, hi], axis=1)


def _ffn_experts_kernel(te_ref, tv_ref, x_ref, wg_ref, wu_ref, wd_ref, out_ref, xn_sc, acc_sc):
    del te_ref
    i, f = pl.program_id(0), pl.program_id(1)
    n_valid = tv_ref[i]

    @pl.when(n_valid > 0)
    def _():
        @pl.when(f == 0)
        def _():
            row = lax.broadcasted_iota(jnp.int32, x_ref.shape, 0)
            xn_sc[...] = _unpack_bf16_pairs(jnp.where(row < n_valid, x_ref[...], 0)).astype(BF16)
            acc_sc[...] = jnp.zeros_like(acc_sc)

        def step(rows):
            xn = xn_sc[:rows]
            a = jnp.dot(xn, wg_ref[0].astype(BF16), preferred_element_type=F32)
            b = jnp.dot(xn, wu_ref[0].astype(BF16), preferred_element_type=F32)
            t = (jax.nn.silu(a) * b).astype(BF16)
            acc_sc[:rows] += jnp.dot(t, wd_ref[0].astype(BF16), preferred_element_type=F32)

        tm = xn_sc.shape[0]
        quarter = tm // 4
        for nq in range(1, 5):
            pl.when((n_valid > (nq - 1) * quarter) & (n_valid <= nq * quarter))(
                functools.partial(step, nq * quarter))

        @pl.when(f == pl.num_programs(1) - 1)
        def _():
            out_ref[...] = _pack_bf16_pairs(acc_sc[...])


def _ffn_experts(tile_expert, tile_valid, x, wg, wu, wd, *, tm, fc):
    n, wpk = x.shape
    ff = wg.shape[2]
    row = lambda i, f, te, tv: (i, 0)
    grid_spec = pltpu.PrefetchScalarGridSpec(
        num_scalar_prefetch=2, grid=(n // tm, ff // fc),
        in_specs=[pl.BlockSpec((tm, wpk), row),
                  pl.BlockSpec((1, D_MODEL, fc), lambda i, f, te, tv: (te[i], 0, f)),
                  pl.BlockSpec((1, D_MODEL, fc), lambda i, f, te, tv: (te[i], 0, f)),
                  pl.BlockSpec((1, fc, D_MODEL), lambda i, f, te, tv: (te[i], f, 0))],
        out_specs=pl.BlockSpec((tm, wpk), row),
        scratch_shapes=[pltpu.VMEM((tm, D_MODEL), BF16), pltpu.VMEM((tm, D_MODEL), F32)])
    return pl.pallas_call(
        _ffn_experts_kernel, grid_spec=grid_spec,
        out_shape=jax.ShapeDtypeStruct((n, wpk), jnp.int32),
        compiler_params=_params(2), name="ffn_experts",
    )(tile_expert, tile_valid, x, wg, wu, wd)


def _sc_worker_id():
    return lax.axis_index("s") * SC_CORES + lax.axis_index("c")


def _sc_dispatch(x_a, x_b, slots_a, slots_b, n_slots):
    w = x_a.shape[1]
    per_a = x_a.shape[0] // SC_WORKERS
    per_b = x_b.shape[0] // SC_WORKERS
    ch = min(SC_CHUNK // 2, per_a)
    n_chunks = per_a // ch
    mesh = plsc.VectorSubcoreMesh(core_axis_name="c", subcore_axis_name="s")
    dma = pltpu.SemaphoreType.DMA

    @functools.partial(
        pl.kernel, mesh=mesh, out_type=jax.ShapeDtypeStruct((n_slots, w), x_a.dtype),
        scratch_types=[pltpu.VMEM((per_a,), jnp.int32), pltpu.VMEM((per_a,), jnp.int32),
                       pltpu.VMEM((ch, w), x_a.dtype), pltpu.VMEM((ch, w), x_a.dtype),
                       pltpu.VMEM((per_b,), jnp.int32), pltpu.VMEM((per_b, w), x_b.dtype),
                       dma, dma, dma, dma, dma, dma])
    def dispatch(xa_hbm, xb_hbm, a0_hbm, a1_hbm, b0_hbm, b1_hbm, out_hbm,
                 idx0_v, idx1_v, rows0_v, rows1_v, idxb_v, rowsb_v,
                 lsem0, lsem1, s0sem0, s0sem1, s1sem0, s1sem1):
        wid = _sc_worker_id()
        base_a = pl.multiple_of(wid * per_a, 8)
        pltpu.sync_copy(a0_hbm.at[pl.ds(base_a, per_a)], idx0_v)
        pltpu.sync_copy(a1_hbm.at[pl.ds(base_a, per_a)], idx1_v)
        rows, lsem = (rows0_v, rows1_v), (lsem0, lsem1)
        ssem = ((s0sem0, s0sem1), (s1sem0, s1sem1))

        def load(c):
            return pltpu.async_copy(xa_hbm.at[pl.ds(base_a + c * ch, ch)], rows[c % 2], lsem[c % 2])

        def scatter(c):
            return [pltpu.async_copy(rows[c % 2], out_hbm.at[idx_v.at[pl.ds(c * ch, ch)]],
                                     ssem[k][c % 2])
                    for k, idx_v in enumerate((idx0_v, idx1_v))]

        loads, scatters = {0: load(0)}, {}
        for c in range(n_chunks):
            loads[c].wait()
            if c + 1 < n_chunks:
                for cp in scatters.pop(c - 1, ()):
                    cp.wait()
                loads[c + 1] = load(c + 1)
            scatters[c] = scatter(c)
        for cps in scatters.values():
            for cp in cps:
                cp.wait()

        base_b = pl.multiple_of(wid * per_b, 8)
        pltpu.sync_copy(xb_hbm.at[pl.ds(base_b, per_b)], rowsb_v)
        for slots_hbm in (b0_hbm, b1_hbm):
            pltpu.sync_copy(slots_hbm.at[pl.ds(base_b, per_b)], idxb_v)
            pltpu.sync_copy(rowsb_v, out_hbm.at[idxb_v])

    return dispatch(x_a, x_b, *slots_a, *slots_b)


def _sc_gather(table, idx):
    _, w = table.shape
    b = idx.shape[0]
    per_w = b // SC_WORKERS
    ch = max(c for c in range(8, SC_CHUNK + 1, 8) if per_w % c == 0)
    n_chunks = per_w // ch
    assert per_w * SC_WORKERS == b
    mesh = plsc.VectorSubcoreMesh(core_axis_name="c", subcore_axis_name="s")
    dma = pltpu.SemaphoreType.DMA

    @functools.partial(
        pl.kernel, mesh=mesh, out_type=jax.ShapeDtypeStruct((b, w), table.dtype),
        scratch_types=[pltpu.VMEM((per_w,), jnp.int32), pltpu.VMEM((ch, w), table.dtype),
                       pltpu.VMEM((ch, w), table.dtype), dma, dma, dma, dma])
    def gather(table_hbm, idx_hbm, out_hbm, idx_v, rows0_v, rows1_v, gsem0, gsem1, wsem0, wsem1):
        wid = _sc_worker_id()
        base = pl.multiple_of(wid * per_w, 8)
        pltpu.sync_copy(idx_hbm.at[pl.ds(base, per_w)], idx_v)
        rows, gsem, wsem = (rows0_v, rows1_v), (gsem0, gsem1), (wsem0, wsem1)

        def fetch(c):
            return pltpu.async_copy(table_hbm.at[idx_v.at[pl.ds(c * ch, ch)]], rows[c % 2],
                                    gsem[c % 2])

        def write(c):
            return pltpu.async_copy(rows[c % 2], out_hbm.at[pl.ds(base + c * ch, ch)], wsem[c % 2])

        fetches, writes = {0: fetch(0)}, {}
        for c in range(n_chunks):
            fetches[c].wait()
            if c + 1 < n_chunks:
                if c >= 1:
                    writes.pop(c - 1).wait()
                fetches[c + 1] = fetch(c + 1)
            writes[c] = write(c)
        for cp in writes.values():
            cp.wait()

    return gather(table, idx)


def _combine_kernel(h_ref, sel_ref, y0_ref, y1_ref, out_ref):
    sel = sel_ref[...]
    out_ref[...] = (h_ref[...] + sel[:, 2:3] * _unpack_bf16_pairs(y0_ref[...])
                    + sel[:, 3:4] * _unpack_bf16_pairs(y1_ref[...]))


def _combine(h, sel, yb, blk_off, tm):
    n = h.shape[0]
    wpk = yb.shape[1]
    half_blocks = yb.shape[0] // 2 // tm
    return pl.pallas_call(
        _combine_kernel, grid=(n // tm,),
        in_specs=[pl.BlockSpec((tm, D_MODEL), lambda i: (i, 0)),
                  pl.BlockSpec((tm, LANES), lambda i: (i, 0)),
                  pl.BlockSpec((tm, wpk), lambda i: (i + blk_off, 0)),
                  pl.BlockSpec((tm, wpk), lambda i: (i + blk_off + half_blocks, 0))],
        out_specs=pl.BlockSpec((tm, D_MODEL), lambda i: (i, 0)),
        out_shape=jax.ShapeDtypeStruct((n, D_MODEL), F32),
        compiler_params=_params(1), name="combine",
    )(h, sel, yb, yb)


def _route_plan(sels, counts, te_rows, n_tiles):
    counts = counts[0, :N_EXPERTS].astype(jnp.int32)
    padded = (counts + te_rows - 1) // te_rows * te_rows
    ends = jnp.cumsum(padded)
    starts = ends - padded
    experts = jnp.arange(N_EXPERTS, dtype=jnp.int32)

    def slots_of(sel, k):
        e = sel[:, k].astype(jnp.int32)
        start = jnp.sum(jnp.where(e[:, None] == experts[None, :], starts[None, :], 0), axis=1)
        return start + sel[:, 4 + k].astype(jnp.int32)

    slots = [(slots_of(sel, 0), slots_of(sel, 1)) for sel in sels]
    tile_start = jnp.arange(n_tiles, dtype=jnp.int32) * te_rows
    tile_expert = jnp.minimum(jnp.sum(tile_start[:, None] >= ends[None, :], axis=1),
                              N_EXPERTS - 1).astype(jnp.int32)
    in_group = tile_start - starts[tile_expert]
    tile_valid = jnp.clip(counts[tile_expert] - in_group, 0, te_rows)
    tile_valid = jnp.where(tile_start < ends[-1], tile_valid, 0).astype(jnp.int32)
    return slots, tile_expert, tile_valid


def _gmlp_kernel(h_ref, g_ref, wuv_ref, vg_ref, mix_ref, bs_ref, wo_ref, out_ref, v_ref, um_sc,
                 *, period):
    h = h_ref[...]
    tm = h.shape[0]
    ln = bs_ref.shape[0]
    gw = D_MODEL // C_GROUPS
    xn = _rms_bf16(h, g_ref[...])
    vv = jax.nn.gelu(jnp.dot(xn, wuv_ref[:, D_MODEL:], preferred_element_type=F32))
    v = vv * lax.rsqrt(jnp.mean(vv * vv, axis=-1, keepdims=True) + EPS) * vg_ref[...]
    v_ref[0] = v[tm - v_ref.shape[1]:]
    vb = v.astype(BF16)
    u = jax.nn.gelu(jnp.dot(xn, wuv_ref[:, :D_MODEL], preferred_element_type=F32))
    r = lax.broadcasted_iota(jnp.int32, (ln, ln), 0)
    c = lax.broadcasted_iota(jnp.int32, (ln, ln), 1)
    causal = (r // period == c // period) & (c % period <= r % period)
    nw = mix_ref.shape[1]
    if ln != nw:
        pick = (lax.broadcasted_iota(jnp.int32, (ln, nw), 0) % period
                == lax.broadcasted_iota(jnp.int32, (ln, nw), 1)).astype(BF16)
        wr = lax.broadcasted_iota(jnp.int32, (nw, nw), 0)
        wc = lax.broadcasted_iota(jnp.int32, (nw, nw), 1)
        corner = (wr < period) & (wc <= wr)
    for gi in range(C_GROUPS):
        if ln == nw:
            wm = jnp.where(causal, mix_ref[gi], 0.0).astype(BF16)
        else:
            rows_w = jnp.dot(pick, jnp.where(corner, mix_ref[gi], 0.0).astype(BF16),
                             preferred_element_type=F32).astype(BF16)
            tiled = lax.dot_general(rows_w, pick, (((1,), (1,)), ((), ())),
                                    preferred_element_type=F32)
            wm = jnp.where(causal, tiled, 0.0).astype(BF16)
        bias = bs_ref[:, gi:gi + 1]
        for ch in range(tm // ln):
            rs = slice(ch * ln, (ch + 1) * ln)
            cs = slice(gi * gw, (gi + 1) * gw)
            mixed = jnp.dot(wm, vb[rs, cs], preferred_element_type=F32) + bias
            um_sc[rs, cs] = (u[rs, cs] * mixed).astype(BF16)
    out_ref[...] = h + jnp.dot(um_sc[...], wo_ref[...], preferred_element_type=F32)


def _gmlp(h, g, wuv16, vg, mix, bs, wo16, *, tm, period, seq, tail):
    n = h.shape[0]
    tps = seq // tm
    row = lambda i: (i, 0)
    full = lambda a: pl.BlockSpec(a.shape, lambda i: (0,) * a.ndim)
    return pl.pallas_call(
        functools.partial(_gmlp_kernel, period=period),
        grid=(n // tm,),
        in_specs=[pl.BlockSpec((tm, D_MODEL), row), full(g), full(wuv16), full(vg), full(mix),
                  full(bs), full(wo16)],
        out_specs=[pl.BlockSpec((tm, D_MODEL), row),
                   pl.BlockSpec((1, tail, D_MODEL), lambda i: (i // tps, 0, 0))],
        out_shape=[jax.ShapeDtypeStruct((n, D_MODEL), F32),
                   jax.ShapeDtypeStruct((n // seq, tail, D_MODEL), F32)],
        scratch_shapes=[pltpu.VMEM((tm, D_MODEL), BF16)],
        compiler_params=_params(1), name="gmlp",
    )(h, g, wuv16, vg, mix, bs, wo16)


def _router_kernel(h_ref, g_ref, wr_ref, cin_ref, xn_ref, sel_ref, cnt_ref):
    @pl.when(pl.program_id(0) == 0)
    def _():
        cnt_ref[...] = cin_ref[...]

    x = h_ref[...]
    xn = x * lax.rsqrt(jnp.mean(x * x, axis=-1, keepdims=True) + EPS) * g_ref[...]
    xn_ref[...] = _pack_bf16_pairs(xn)
    x_hi = xn.astype(BF16)
    x_lo = (xn - x_hi.astype(F32)).astype(BF16)
    w = wr_ref[...]
    w_hi = w.astype(BF16)
    w_lo = (w - w_hi.astype(F32)).astype(BF16)
    logits = (jnp.dot(x_hi, w_hi, preferred_element_type=F32)
              + (jnp.dot(x_lo, w_hi, preferred_element_type=F32)
                 + jnp.dot(x_hi, w_lo, preferred_element_type=F32)))
    lane = lax.broadcasted_iota(jnp.int32, logits.shape, 1)
    lanef = lane.astype(F32)
    logits = jnp.where(lane < N_EXPERTS, logits, -jnp.inf)
    m1 = jnp.max(logits, axis=-1, keepdims=True)
    i1 = jnp.min(jnp.where(logits == m1, lanef, float(LANES)), axis=-1, keepdims=True)
    rest = jnp.where(lanef == i1, -jnp.inf, logits)
    m2 = jnp.max(rest, axis=-1, keepdims=True)
    i2 = jnp.min(jnp.where(rest == m2, lanef, float(LANES)), axis=-1, keepdims=True)
    e2 = jnp.exp(m2 - m1)
    w1 = 1.0 / (1.0 + e2)
    w2 = e2 / (1.0 + e2)
    tm = x.shape[0]
    hit1, hit2 = lanef == i1, lanef == i2
    picked = jnp.where(hit1 | hit2, 1.0, 0.0)
    r_i = lax.broadcasted_iota(jnp.int32, (tm, tm), 0)
    c_i = lax.broadcasted_iota(jnp.int32, (tm, tm), 1)
    below = jnp.where(c_i < r_i, 1.0, 0.0).astype(BF16)
    before = jnp.dot(below, picked.astype(BF16), preferred_element_type=F32) + cnt_ref[...]
    rank1 = jnp.sum(jnp.where(hit1, before, 0.0), axis=-1, keepdims=True)
    rank2 = jnp.sum(jnp.where(hit2, before, 0.0), axis=-1, keepdims=True)
    cnt_ref[...] += jnp.sum(picked, axis=0, keepdims=True)
    vals = (i1, i2, w1, w2, rank1, rank2)
    sel = jnp.zeros_like(logits)
    for li, val in enumerate(vals):
        sel = jnp.where(lane == li, val, sel)
    sel_ref[...] = sel


def _router(h, g, wr_pad, counts_in, tm):
    n = h.shape[0]
    row = lambda i: (i, 0)
    fix = lambda i: (0, 0)
    return pl.pallas_call(
        _router_kernel, grid=(n // tm,),
        in_specs=[pl.BlockSpec((tm, D_MODEL), row), pl.BlockSpec((1, D_MODEL), fix),
                  pl.BlockSpec((D_MODEL, LANES), fix), pl.BlockSpec((1, LANES), fix)],
        out_specs=[pl.BlockSpec((tm, D_MODEL // 2), row), pl.BlockSpec((tm, LANES), row),
                   pl.BlockSpec((1, LANES), fix)],
        out_shape=[jax.ShapeDtypeStruct((n, D_MODEL // 2), jnp.int32),
                   jax.ShapeDtypeStruct((n, LANES), F32),
                   jax.ShapeDtypeStruct((1, LANES), F32)],
        compiler_params=_params(1), name="router",
    )(h, g, wr_pad, counts_in)


def _rope_tables(pos):
    half = HEAD_DIM // 2
    inv = ROPE_THETA ** (-jnp.arange(half, dtype=F32) / half)
    ang = pos.astype(F32)[:, None] * inv[None, :]
    cos, sin = jnp.cos(ang), jnp.sin(ang)
    zero = jnp.zeros_like(sin)
    rep = lambda a, b: jnp.tile(jnp.concatenate([a, b], axis=-1), (1, LANES // HEAD_DIM))
    return rep(cos, cos), rep(-sin, zero), rep(zero, sin)


def _tile_for(n, pref):
    return pref if n % pref == 0 else n


def kernel(x_prompt, x_sample, state_conv, cache_kv_win, norm_mix_e, w_in_e, conv_w, q_gain,
           k_gain, w_out_e, norm_ffn_e, ffn_w_gate, ffn_w_up, ffn_w_down, norm_mix_o, w_uv,
           v_gain, w_s, b_s, w_out_o, norm_moe, w_router, moe_w_gate, moe_w_up, moe_w_down):
    bp, seq, _ = x_prompt.shape
    bs_, s_len, _ = x_sample.shape
    n_p, n_s = bp * seq, bs_ * s_len
    hp = x_prompt.reshape(n_p, D_MODEL)
    hs = x_sample.reshape(n_s, D_MODEL)
    tm_p = 512
    row = lambda a: a.reshape(1, -1)

    tabs_p = _rope_tables(jnp.arange(seq, dtype=jnp.int32))
    tabs_s = tuple(jnp.tile(t, (bs_, 1)) for t in
                   _rope_tables(PAST_LEN + jnp.arange(s_len, dtype=jnp.int32)))
    hd = jnp.arange(256) // HEAD_DIM
    bd = (hd[:, None] == hd[None, :]).astype(BF16)

    w_in16 = w_in_e[0].astype(BF16)
    w_out16 = w_out_e[0].astype(BF16)
    qg = jnp.tile(q_gain[0], 4).reshape(1, 256)
    kg = jnp.tile(k_gain[0], 4).reshape(1, 256)
    g0 = row(norm_mix_e[0])

    dils = tuple(dil for _, dil in BRANCHES)
    lw = min(2048, seq)
    z_p, gb_p, kvt_p, *planes = _inproj(hp, g0, w_in16, tabs_p, qg, kg, bd, tm_p, seq, dils, lw)
    z_s, gb_s, kv_s, q_s = _inproj(hs, g0, w_in16, tabs_s, qg, kg, bd, n_s, n_s, ())

    os_, ls_ = [], []
    for di, dil in enumerate(dils):
        m = seq // dil
        q_d, k_d, v_d = (a.reshape(bp * dil, m, D_B) for a in planes[3 * di:3 * di + 3])
        o, lse = _attn_prompt(q_d, k_d, v_d, qb=min(512, m))
        os_.append(o.reshape(bp, dil, m, D_B))
        ls_.append(lse.reshape(bp, dil, m, LANES))
    hp = _outproj_prompt(z_p, gb_p, conv_w[0], os_, ls_, w_out16, hp, tm_p, seq, dils)

    cat_s = _mix_sample(q_s.reshape(bs_, s_len, D_B),
                        kv_s.reshape(bs_, s_len, 2 * D_B),
                        jnp.transpose(cache_kv_win[0], (0, 2, 3, 4, 1)).reshape(
                            bs_, 2, D_B, -1),
                        z_s.reshape(bs_, s_len, D_A), gb_s.reshape(bs_, s_len, D_A),
                        state_conv[0], conv_w[0])
    hs = _linear_res(cat_s.reshape(n_s, D_MODEL), w_out16, hs)

    conv_prompt = z_p.reshape(bp, seq, D_A)[:, seq - 2:][None]
    conv_sample = z_s.reshape(bs_, s_len, D_A)[:, s_len - 2:][None]
    kv_prompt = jnp.transpose(kvt_p.reshape(bp, 2, H_B, HEAD_DIM, lw), (0, 4, 1, 2, 3))[None]
    kv_sample = kv_s.reshape(bs_, s_len, 2, H_B, HEAD_DIM)[None]

    fg, fu, fd = (ffn_w_gate.astype(BF16), ffn_w_up.astype(BF16), ffn_w_down.astype(BF16))
    gf = row(norm_ffn_e[0])
    hp = _ffn_dense(hp, gf, fg, fu, fd, tm=2 * tm_p, fc=1408)
    hs = _ffn_dense(hs, gf, fg, fu, fd, tm=n_s, fc=1408)

    wuv16 = w_uv[0].astype(BF16)
    wo16 = w_out_o[0].astype(BF16)
    g1, vg = row(norm_mix_o[0]), row(v_gain[0])
    tail = seq - ((seq - 1) // CHUNK) * CHUNK
    hp, v_tail_p = _gmlp(hp, g1, wuv16, vg, w_s[0], b_s[0].T, wo16, tm=tm_p, period=CHUNK,
                         seq=seq, tail=tail)
    bias_s = jnp.tile(b_s[0][:, :s_len].T, (bs_, 1))
    hs, v_all_s = _gmlp(hs, g1, wuv16, vg, w_s[0], bias_s, wo16, tm=n_s, period=s_len,
                        seq=n_s, tail=n_s)
    v_prompt = v_tail_p[None]
    v_sample = v_all_s.reshape(bs_, s_len, D_MODEL)[None]

    wr_pad = jnp.pad(w_router[0], ((0, 0), (0, LANES - N_EXPERTS)))
    gm = row(norm_moe[0])
    xn_p, sel_p, counts = _router(hp, gm, wr_pad, jnp.zeros((1, LANES), F32), tm_p)
    xn_s, sel_s, counts = _router(hs, gm, wr_pad, counts, n_s)
    n_tok = n_p + n_s
    n_tiles = -(-(2 * n_tok + N_EXPERTS * (MOE_ROWS - 1)) // MOE_ROWS)
    (slots_p, slots_s), tile_expert, tile_valid = _route_plan(
        (sel_p, sel_s), counts, MOE_ROWS, n_tiles)
    slot0 = jnp.concatenate([slots_p[0], slots_s[0]])
    slot1 = jnp.concatenate([slots_p[1], slots_s[1]])
    xs = _sc_dispatch(xn_p, xn_s, slots_p, slots_s, n_tiles * MOE_ROWS)
    ys = _ffn_experts(tile_expert, tile_valid, xs, moe_w_gate[0], moe_w_up[0], moe_w_down[0],
                      tm=MOE_ROWS, fc=896)
    yb = _sc_gather(ys, jnp.concatenate([slot0, slot1]))
    hp = _combine(hp, sel_p, yb, 0, n_s)
    hs = _combine(hs, sel_s, yb, n_p // n_s, n_s)

    return (hp.reshape(bp, seq, D_MODEL), hs.reshape(bs_, s_len, D_MODEL),
            conv_prompt, conv_sample, kv_prompt, kv_sample, v_prompt, v_sample)
```

```python
import functools

import jax
import jax.numpy as jnp
from jax import lax
from jax.experimental import pallas as pl
from jax.experimental.pallas import tpu as pltpu
from jax.experimental.pallas import tpu_sc as plsc

F32 = jnp.float32
BF16 = jnp.bfloat16

D_MODEL = 1024
D_A = 512
H_B = 8
HEAD_DIM = 64
D_B = H_B * HEAD_DIM
BRANCHES = ((128, 1), (512, 4), (2048, 16))
BAND = 128
ROPE_THETA = 10000.0
PAST_LEN = 16384
N_EXPERTS = 8
C_GROUPS = 4
CHUNK = 128
EPS = 1e-6
LANES = 128
VMEM_LIMIT = 56 * 1024 * 1024
SC_CORES = 2
SC_WORKERS = SC_CORES * 16
SC_CHUNK = 128
MOE_ROWS = 1024


def _params(n_axes):
    return pltpu.CompilerParams(dimension_semantics=("arbitrary",) * n_axes,
                                vmem_limit_bytes=VMEM_LIMIT)


def _rms_bf16(x, g):
    ms = jnp.mean(x * x, axis=-1, keepdims=True)
    return (x * lax.rsqrt(ms + EPS) * g).astype(BF16)


def _inproj_kernel(x_ref, g_ref, w_ref, cos_ref, sina_ref, sinb_ref, qg_ref, kg_ref, bd_ref,
                   z_ref, gb_ref, kv_ref, *rest, dils, win_tiles, tiles_per_seq):
    if dils:
        planes, qkv_sc = rest[:-1], rest[-1]
    else:
        (q_out,) = rest
    xn = _rms_bf16(x_ref[...], g_ref[...])

    def col(j):
        return jnp.dot(xn, w_ref[:, j * 512:(j + 1) * 512], preferred_element_type=F32)

    z_ref[...] = col(1) * col(0)
    gb_ref[...] = col(2).astype(gb_ref.dtype)
    cos, sina, sinb = cos_ref[...], sina_ref[...], sinb_ref[...]

    def head_norm_rope(t, gain):
        outs = []
        for c2 in range(2):
            tc = t[:, c2 * 256:(c2 + 1) * 256]
            ss = jnp.dot((tc * tc).astype(BF16), bd_ref[...], preferred_element_type=F32)
            tn = tc * lax.rsqrt(ss * (1.0 / HEAD_DIM) + EPS) * gain
            for c in range(2):
                u = tn[:, c * LANES:(c + 1) * LANES]
                outs.append(u * cos + pltpu.roll(u, LANES - 32, 1) * sina
                            + pltpu.roll(u, 32, 1) * sinb)
        return outs

    q = [t * (HEAD_DIM ** -0.5) for t in head_norm_rope(col(3), qg_ref[...])]
    k = head_norm_rope(col(4), kg_ref[...])
    v = col(5)
    if not dils:
        for c in range(4):
            kv_ref[:, c * LANES:(c + 1) * LANES] = k[c]
            q_out[:, c * LANES:(c + 1) * LANES] = q[c]
        kv_ref[:, D_B:] = v
        return

    @pl.when(pl.program_id(0) % tiles_per_seq >= tiles_per_seq - win_tiles)
    def _():
        for c in range(4):
            kv_ref[0, c * LANES:(c + 1) * LANES, :] = k[c].T
            kv_ref[0, D_B + c * LANES:D_B + (c + 1) * LANES, :] = v[:, c * LANES:(c + 1) * LANES].T

    tm = x_ref.shape[0]
    for a, val in enumerate((q, k, [v[:, c * LANES:(c + 1) * LANES] for c in range(4)])):
        for c in range(4):
            qkv_sc[0, a, c] = val[c]
            planes[a][0, 0, :, c * LANES:(c + 1) * LANES] = val[c].astype(BF16)
    for di in range(1, len(dils)):
        dil, prev = dils[di], dils[di - 1]
        ratio = dil // prev
        src, dst = qkv_sc.at[(di - 1) % 2], qkv_sc.at[di % 2]
        for r in range(dil):
            start = (r % prev) * (tm // prev) + r // prev
            for a in range(3):
                for c in range(4):
                    rows_ac = src[a, c, pl.ds(start, tm // dil, stride=ratio), :]
                    planes[3 * di + a][0, r, :, c * LANES:(c + 1) * LANES] = rows_ac.astype(BF16)
                    if di + 1 < len(dils):
                        dst[a, c, r * (tm // dil):(r + 1) * (tm // dil), :] = rows_ac


def _inproj(x, g, w16, tabs, qg, kg, bd, tm, seq, dils, win=0):
    n = x.shape[0]
    cos, sina, sinb = tabs
    tt = cos.shape[0] // tm
    tps = seq // tm
    win_tiles = win // tm
    row = lambda i: (i, 0)
    fix = lambda i: (0, 0)
    tab = lambda i: (i % tt, 0)
    out_specs = [pl.BlockSpec((tm, D_A), row), pl.BlockSpec((tm, D_A), row)]
    out_shape = [jax.ShapeDtypeStruct((n, D_A), F32),
                 jax.ShapeDtypeStruct((n, D_A), BF16 if dils else F32)]
    if dils:
        out_specs.append(pl.BlockSpec(
            (1, 2 * D_B, tm),
            lambda i: (i // tps, 0, jnp.maximum(i % tps - (tps - win_tiles), 0))))
        out_shape.append(jax.ShapeDtypeStruct((n // seq, 2 * D_B, win), F32))
        assert dils[0] == 1 and all(b % a == 0 for a, b in zip(dils, dils[1:]))
        scratch = [pltpu.VMEM((2, 3, D_B // LANES, tm, LANES), F32)]
    else:
        out_specs += [pl.BlockSpec((tm, 2 * D_B), row), pl.BlockSpec((tm, D_B), row)]
        out_shape += [jax.ShapeDtypeStruct((n, 2 * D_B), F32),
                      jax.ShapeDtypeStruct((n, D_B), F32)]
        scratch = []
    for dil in dils:
        for _ in range(3):
            out_specs.append(pl.BlockSpec((1, dil, tm // dil, D_B),
                                          lambda i: (i // tps, 0, i % tps, 0)))
            out_shape.append(jax.ShapeDtypeStruct((n // seq, dil, seq // dil, D_B), BF16))
    return pl.pallas_call(
        functools.partial(_inproj_kernel, dils=dils, win_tiles=win_tiles, tiles_per_seq=tps),
        grid=(n // tm,),
        in_specs=[pl.BlockSpec((tm, D_MODEL), row), pl.BlockSpec((1, D_MODEL), fix),
                  pl.BlockSpec(w16.shape, fix),
                  pl.BlockSpec((tm, LANES), tab), pl.BlockSpec((tm, LANES), tab),
                  pl.BlockSpec((tm, LANES), tab),
                  pl.BlockSpec((1, 256), fix), pl.BlockSpec((1, 256), fix),
                  pl.BlockSpec((256, 256), fix)],
        out_specs=out_specs, out_shape=out_shape, scratch_shapes=scratch,
        compiler_params=_params(1), name="inproj",
    )(x, g, w16, cos, sina, sinb, qg, kg, bd)


def _attn_prompt_kernel(q_ref, kc_ref, kp_ref, vc_ref, vp_ref, o_ref, lse_ref, *, qb):
    j = pl.program_id(1)
    nsub = qb // BAND
    lane = lax.broadcasted_iota(jnp.int32, (1, LANES), 1)
    head0 = lane < HEAD_DIM
    qi = lax.broadcasted_iota(jnp.int32, (2 * BAND, 2 * BAND), 0) % BAND + BAND
    kj = lax.broadcasted_iota(jnp.int32, (2 * BAND, 2 * BAND), 1)
    dist = qi - kj
    band = (dist >= 0) & (dist <= BAND)
    first = band & ((kj >= BAND) | (j > 0))
    lanes = [slice(hp * LANES, (hp + 1) * LANES) for hp in range(D_B // LANES)]
    for pi, sub in ((pi, sub) for pi in range(q_ref.shape[0]) for sub in range(nsub)):
        rows = slice(sub * BAND, (sub + 1) * BAND)
        keys = slice((sub - 1) * BAND, (sub + 1) * BAND)
        lse_all = jnp.zeros((BAND, LANES), F32)
        for hp, ls in enumerate(lanes):
            qs = q_ref[pi, rows, ls]
            if sub == 0:
                ks = jnp.concatenate([kp_ref[pi, :, ls], kc_ref[pi, :BAND, ls]], axis=0)
                vs = jnp.concatenate([vp_ref[pi, :, ls], vc_ref[pi, :BAND, ls]], axis=0)
            else:
                ks, vs = kc_ref[pi, keys, ls], vc_ref[pi, keys, ls]
            q2 = jnp.concatenate([jnp.where(head0, qs, jnp.zeros_like(qs)),
                                  jnp.where(head0, jnp.zeros_like(qs), qs)], axis=0)
            s = lax.dot_general(q2, ks, (((1,), (1,)), ((), ())), preferred_element_type=F32)
            s = jnp.where(first if sub == 0 else band, s, -jnp.inf)
            mx = jnp.max(s, axis=-1, keepdims=True)
            p = jnp.exp(s - mx)
            l = jnp.sum(p, axis=-1, keepdims=True)
            pv = jnp.dot(p.astype(BF16), vs, preferred_element_type=F32) / l
            lse = mx + jnp.log(l)
            o_ref[pi, rows, ls] = jnp.where(head0, pv[:BAND], pv[BAND:]).astype(BF16)
            lse_all = jnp.where(lane == 2 * hp, lse[:BAND],
                                jnp.where(lane == 2 * hp + 1, lse[BAND:], lse_all))
        lse_ref[pi, rows, :] = lse_all


def _attn_prompt(q, k, v, qb, pb):
    b, m, w = q.shape
    nsub = qb // BAND
    cur = lambda bi, j: (bi, j, 0)
    prev = lambda bi, j: (bi, jnp.maximum(j * nsub - 1, 0), 0)
    return pl.pallas_call(
        functools.partial(_attn_prompt_kernel, qb=qb),
        grid=(b // pb, m // qb),
        in_specs=[pl.BlockSpec((pb, qb, w), cur), pl.BlockSpec((pb, qb, w), cur),
                  pl.BlockSpec((pb, BAND, w), prev), pl.BlockSpec((pb, qb, w), cur),
                  pl.BlockSpec((pb, BAND, w), prev)],
        out_specs=[pl.BlockSpec((pb, qb, w), cur), pl.BlockSpec((pb, qb, LANES), cur)],
        out_shape=[jax.ShapeDtypeStruct((b, m, w), BF16),
                   jax.ShapeDtypeStruct((b, m, LANES), F32)],
        compiler_params=_params(2), name="attn_prompt",
    )(q, k, k, v, v)


def _outproj_prompt_kernel(z_ref, zh_ref, gb_ref, cw_ref, *rest, tiles_per_batch, dils):
    nb = len(dils)
    o_refs, l_refs = rest[:nb], rest[nb:2 * nb]
    w_ref, h_ref, out_ref = rest[2 * nb:2 * nb + 3]
    o_scs, l_scs, tmp_sc = rest[2 * nb + 3:3 * nb + 3], rest[3 * nb + 3:4 * nb + 3], rest[-1]
    i = pl.program_id(0)
    z = z_ref[...]
    tm = z.shape[0]
    def put_rows(dst, spare, val, r, dil):
        if dil == 1:
            dst[...] = val
        elif dil <= 4:
            dst[pl.ds(r, tm // dil, stride=dil), :] = val
        else:
            spare[pl.ds((r % 4) * (tm // 4) + r // 4, tm // dil, stride=dil // 4), :] = val

    def finish_rows(dst, spare, dil):
        if dil > 4:
            for b in range(4):
                dst[pl.ds(b, tm // 4, stride=4), :] = spare[b * (tm // 4):(b + 1) * (tm // 4), :]

    for dil, o_ref, l_ref, o_sc, l_sc in zip(dils, o_refs, l_refs, o_scs, l_scs):
        assert dil <= 16
        for r in range(dil):
            for c in range(D_B // LANES):
                ls = slice(c * LANES, (c + 1) * LANES)
                put_rows(o_sc.at[c], tmp_sc.at[c], o_ref[0, r, :, ls].astype(F32), r, dil)
            put_rows(l_sc, tmp_sc.at[D_B // LANES], l_ref[0, r], r, dil)
        for c in range(D_B // LANES):
            finish_rows(o_sc.at[c], tmp_sc.at[c], dil)
        finish_rows(l_sc, tmp_sc.at[D_B // LANES], dil)
    unstage = lambda sc: jnp.concatenate([sc[c] for c in range(D_B // LANES)], axis=1)
    zh = jnp.where(i % tiles_per_batch == 0, 0.0, zh_ref[...])
    row = lax.broadcasted_iota(jnp.int32, (tm, 1), 0)
    zm1 = jnp.where(row >= 1, pltpu.roll(z, 1, 0), zh[7:8])
    zm2 = jnp.where(row >= 2, pltpu.roll(z, 2, 0), jnp.where(row == 1, zh[7:8], zh[6:7]))
    cw = cw_ref[...]
    a_out = gb_ref[...].astype(F32) * (cw[0:1] * zm2 + cw[1:2] * zm1 + cw[2:3] * z)
    lses = [l_sc[...] for l_sc in l_scs]
    lm = functools.reduce(jnp.maximum, lses)
    es = [jnp.exp(l - lm) for l in lses]
    tot = sum(es)
    spread = (lax.broadcasted_iota(jnp.int32, (LANES, D_B), 1) // HEAD_DIM
              == lax.broadcasted_iota(jnp.int32, (LANES, D_B), 0)).astype(BF16)

    def widen(wgt):
        hi = wgt.astype(BF16)
        lo = (wgt - hi.astype(F32)).astype(BF16)
        return (jnp.dot(hi, spread, preferred_element_type=F32)
                + jnp.dot(lo, spread, preferred_element_type=F32))

    o = sum(widen(e / tot) * unstage(o_sc) for e, o_sc in zip(es, o_scs))
    y = jnp.dot(a_out.astype(BF16), w_ref[:D_A, :], preferred_element_type=F32)
    y = y + jnp.dot(o.astype(BF16), w_ref[D_A:, :], preferred_element_type=F32)
    out_ref[...] = h_ref[...] + y


def _outproj_prompt(z, gb, cw, os_, ls_, w16, h, tm, seq, dils):
    n = z.shape[0]
    tps = seq // tm
    row = lambda i: (i, 0)
    fix = lambda i: (0, 0)
    halo = lambda i: (jnp.maximum(i * (tm // 8) - 1, 0), 0)
    half = pl.BlockSpec((tm, D_A), row)
    plane = lambda width: [pl.BlockSpec((1, dil, tm // dil, width),
                                        lambda i: (i // tps, 0, i % tps, 0)) for dil in dils]
    return pl.pallas_call(
        functools.partial(_outproj_prompt_kernel, tiles_per_batch=tps, dils=dils),
        grid=(n // tm,),
        in_specs=[half, pl.BlockSpec((8, D_A), halo), half, pl.BlockSpec((3, D_A), fix),
                  *plane(D_B), *plane(LANES),
                  pl.BlockSpec((D_MODEL, D_MODEL), fix), pl.BlockSpec((tm, D_MODEL), row)],
        out_specs=pl.BlockSpec((tm, D_MODEL), row),
        out_shape=jax.ShapeDtypeStruct((n, D_MODEL), F32),
        scratch_shapes=([pltpu.VMEM((D_B // LANES, tm, LANES), F32)] * len(dils)
                        + [pltpu.VMEM((tm, LANES), F32)] * len(dils)
                        + [pltpu.VMEM((D_B // LANES + 1, tm, LANES), F32)]),
        compiler_params=_params(1), name="outproj_prompt",
    )(z, z, gb, cw, *os_, *ls_, w16, h)


def _mix_sample_kernel(q_ref, kvn_ref, win_ref, z_ref, gb_ref, st_ref, cw_ref, cat_ref):
    s_len = q_ref.shape[1]
    rows = H_B * s_len
    npad = LANES
    q = q_ref[0]
    r_i = lax.broadcasted_iota(jnp.int32, (rows, D_B), 0)
    c_i = lax.broadcasted_iota(jnp.int32, (rows, D_B), 1)
    qt = jnp.concatenate([q] * H_B, axis=0)
    qbd = jnp.where(r_i // s_len == c_i // HEAD_DIM, qt, 0.0).astype(BF16)
    kvn = jnp.concatenate([kvn_ref[0], jnp.zeros((npad - s_len, 2 * D_B), F32)], axis=0)

    def multiplicity(delta):
        cnt = jnp.zeros(delta.shape, F32)
        for _, dil in BRANCHES:
            ok = (delta >= 0) & (delta <= BAND * dil) & ((delta & (dil - 1)) == 0)
            cnt = cnt + jnp.where(ok, 1.0, 0.0)
        return cnt

    lb = win_ref.shape[3]
    nt = (((1,), (1,)), ((), ()))
    tok_c = lax.broadcasted_iota(jnp.int32, (rows, lb), 0) % s_len
    cnt_c = multiplicity(tok_c + lb - lax.broadcasted_iota(jnp.int32, (rows, lb), 1))
    tok_n = lax.broadcasted_iota(jnp.int32, (rows, npad), 0) % s_len
    cnt_n = multiplicity(tok_n - lax.broadcasted_iota(jnp.int32, (rows, npad), 1))
    sc = jnp.dot(qbd, win_ref[0, 0].astype(BF16), preferred_element_type=F32)
    sn = lax.dot_general(qbd, kvn[:, :D_B].astype(BF16), nt, preferred_element_type=F32)
    sc = jnp.where(cnt_c > 0, sc, -jnp.inf)
    sn = jnp.where(cnt_n > 0, sn, -jnp.inf)
    mx = jnp.maximum(jnp.max(sc, axis=-1, keepdims=True), jnp.max(sn, axis=-1, keepdims=True))
    pc = cnt_c * jnp.exp(sc - mx)
    pn = cnt_n * jnp.exp(sn - mx)
    den = jnp.sum(pc, axis=-1, keepdims=True) + jnp.sum(pn, axis=-1, keepdims=True)
    of = lax.dot_general(pc.astype(BF16), win_ref[0, 1].astype(BF16), nt,
                         preferred_element_type=F32)
    of = (of + jnp.dot(pn.astype(BF16), kvn[:, D_B:].astype(BF16),
                       preferred_element_type=F32)) / den
    lane_head = lax.broadcasted_iota(jnp.int32, (s_len, D_B), 1) // HEAD_DIM
    o = jnp.zeros((s_len, D_B), F32)
    for h in range(H_B):
        o = o + jnp.where(lane_head == h, of[h * s_len:(h + 1) * s_len], 0.0)
    z = z_ref[0]
    st = st_ref[0]
    t = lax.broadcasted_iota(jnp.int32, (s_len, 1), 0)
    zm1 = jnp.where(t >= 1, pltpu.roll(z, 1, 0), st[1:2])
    zm2 = jnp.where(t >= 2, pltpu.roll(z, 2, 0), jnp.where(t == 1, st[1:2], st[0:1]))
    cw = cw_ref[...]
    cat_ref[0, :, :D_A] = gb_ref[0] * (cw[0:1] * zm2 + cw[1:2] * zm1 + cw[2:3] * z)
    cat_ref[0, :, D_A:] = o


def _mix_sample(q, kvn, win, z, gb, st, cw):
    b, s_len, _ = q.shape
    per = lambda i: (i,) + (0,) * 2
    blk = lambda a: pl.BlockSpec((1,) + a.shape[1:], per)
    return pl.pallas_call(
        _mix_sample_kernel,
        grid=(b,),
        in_specs=[blk(q), blk(kvn),
                  pl.BlockSpec((1,) + win.shape[1:], lambda i: (i, 0, 0, 0)),
                  blk(z), blk(gb), blk(st), pl.BlockSpec((3, D_A), lambda i: (0, 0))],
        out_specs=pl.BlockSpec((1, s_len, D_MODEL), per),
        out_shape=jax.ShapeDtypeStruct((b, s_len, D_MODEL), F32),
        compiler_params=_params(1), name="mix_sample",
    )(q, kvn, win, z, gb, st, cw)


def _linear_res_kernel(x_ref, w_ref, h_ref, out_ref):
    out_ref[...] = h_ref[...] + jnp.dot(x_ref[...].astype(BF16), w_ref[...],
                                        preferred_element_type=F32)


def _linear_res(x, w16, h):
    n, k = x.shape
    full = lambda a: pl.BlockSpec(a.shape, lambda i: (0, 0))
    return pl.pallas_call(
        _linear_res_kernel, grid=(1,),
        in_specs=[full(x), full(w16), full(h)], out_specs=full(h),
        out_shape=jax.ShapeDtypeStruct(h.shape, F32),
        compiler_params=_params(1), name="linear_res",
    )(x, w16, h)


def _swiglu_step(xn, wg_ref, wu_ref, wd_ref, acc_sc):
    a = jnp.dot(xn, wg_ref[0], preferred_element_type=F32)
    b = jnp.dot(xn, wu_ref[0], preferred_element_type=F32)
    t = (jax.nn.silu(a) * b).astype(BF16)
    acc_sc[...] += jnp.dot(t, wd_ref[0], preferred_element_type=F32)


def _ffn_dense_kernel(x_ref, g_ref, wg_ref, wu_ref, wd_ref, out_ref, xn_sc, acc_sc):
    f = pl.program_id(1)

    @pl.when(f == 0)
    def _():
        xn_sc[...] = _rms_bf16(x_ref[...], g_ref[...])
        acc_sc[...] = jnp.zeros_like(acc_sc)

    _swiglu_step(xn_sc[...], wg_ref, wu_ref, wd_ref, acc_sc)

    @pl.when(f == pl.num_programs(1) - 1)
    def _():
        out_ref[...] = x_ref[...] + acc_sc[...]


def _ffn_dense(x, g, wg, wu, wd, *, tm, fc):
    n = x.shape[0]
    ff = wg.shape[2]
    row = lambda i, f: (i, 0)
    return pl.pallas_call(
        _ffn_dense_kernel, grid=(n // tm, ff // fc),
        in_specs=[pl.BlockSpec((tm, D_MODEL), row), pl.BlockSpec((1, D_MODEL), lambda i, f: (0, 0)),
                  pl.BlockSpec((1, D_MODEL, fc), lambda i, f: (0, 0, f)),
                  pl.BlockSpec((1, D_MODEL, fc), lambda i, f: (0, 0, f)),
                  pl.BlockSpec((1, fc, D_MODEL), lambda i, f: (0, f, 0))],
        out_specs=pl.BlockSpec((tm, D_MODEL), row),
        out_shape=jax.ShapeDtypeStruct((n, D_MODEL), F32),
        scratch_shapes=[pltpu.VMEM((tm, D_MODEL), BF16), pltpu.VMEM((tm, D_MODEL), F32)],
        compiler_params=_params(2), name="ffn_dense",
    )(x, g, wg, wu, wd)


def _pack_bf16_pairs(x):
    half = x.shape[1] // 2
    lo = lax.bitcast_convert_type(x[:, :half].astype(BF16).astype(F32), jnp.int32)
    hi = lax.bitcast_convert_type(x[:, half:].astype(BF16).astype(F32), jnp.int32)
    return (hi & -65536) | lax.shift_right_logical(lo, 16)


def _unpack_bf16_pairs(p):
    lo = lax.bitcast_convert_type(p << 16, F32)
    hi = lax.bitcast_convert_type(p & -65536, F32)
    return jnp.concatenate([lo, hi], axis=1)


def _ffn_experts_kernel(te_ref, tv_ref, x_ref, wg_ref, wu_ref, wd_ref, out_ref, xn_sc, acc_sc):
    del te_ref
    i, f = pl.program_id(0), pl.program_id(1)
    n_valid = tv_ref[i]

    @pl.when(n_valid > 0)
    def _():
        @pl.when(f == 0)
        def _():
            row = lax.broadcasted_iota(jnp.int32, x_ref.shape, 0)
            xn_sc[...] = _unpack_bf16_pairs(jnp.where(row < n_valid, x_ref[...], 0)).astype(BF16)
            acc_sc[...] = jnp.zeros_like(acc_sc)

        def step(rows):
            xn = xn_sc[:rows]
            a = jnp.dot(xn, wg_ref[0].astype(BF16), preferred_element_type=F32)
            b = jnp.dot(xn, wu_ref[0].astype(BF16), preferred_element_type=F32)
            t = (jax.nn.silu(a) * b).astype(BF16)
            acc_sc[:rows] += jnp.dot(t, wd_ref[0].astype(BF16), preferred_element_type=F32)

        tm = xn_sc.shape[0]
        quarter = tm // 4
        for nq in range(1, 5):
            pl.when((n_valid > (nq - 1) * quarter) & (n_valid <= nq * quarter))(
                functools.partial(step, nq * quarter))

        @pl.when(f == pl.num_programs(1) - 1)
        def _():
            out_ref[...] = _pack_bf16_pairs(acc_sc[...])


def _ffn_experts(tile_expert, tile_valid, x, wg, wu, wd, *, tm, fc):
    n, wpk = x.shape
    ff = wg.shape[2]
    row = lambda i, f, te, tv: (i, 0)
    grid_spec = pltpu.PrefetchScalarGridSpec(
        num_scalar_prefetch=2, grid=(n // tm, ff // fc),
        in_specs=[pl.BlockSpec((tm, wpk), row),
                  pl.BlockSpec((1, D_MODEL, fc), lambda i, f, te, tv: (te[i], 0, f)),
                  pl.BlockSpec((1, D_MODEL, fc), lambda i, f, te, tv: (te[i], 0, f)),
                  pl.BlockSpec((1, fc, D_MODEL), lambda i, f, te, tv: (te[i], f, 0))],
        out_specs=pl.BlockSpec((tm, wpk), row),
        scratch_shapes=[pltpu.VMEM((tm, D_MODEL), BF16), pltpu.VMEM((tm, D_MODEL), F32)])
    return pl.pallas_call(
        _ffn_experts_kernel, grid_spec=grid_spec,
        out_shape=jax.ShapeDtypeStruct((n, wpk), jnp.int32),
        compiler_params=_params(2), name="ffn_experts",
    )(tile_expert, tile_valid, x, wg, wu, wd)


def _sc_worker_id():
    return lax.axis_index("s") * SC_CORES + lax.axis_index("c")


def _sc_dispatch(x_a, x_b, slots_a, slots_b, n_slots):
    w = x_a.shape[1]
    per_a = x_a.shape[0] // SC_WORKERS
    per_b = x_b.shape[0] // SC_WORKERS
    ch = min(SC_CHUNK // 2, per_a)
    n_chunks = per_a // ch
    mesh = plsc.VectorSubcoreMesh(core_axis_name="c", subcore_axis_name="s")
    dma = pltpu.SemaphoreType.DMA

    @functools.partial(
        pl.kernel, mesh=mesh, out_type=jax.ShapeDtypeStruct((n_slots, w), x_a.dtype),
        scratch_types=[pltpu.VMEM((per_a,), jnp.int32), pltpu.VMEM((per_a,), jnp.int32),
                       pltpu.VMEM((ch, w), x_a.dtype), pltpu.VMEM((ch, w), x_a.dtype),
                       pltpu.VMEM((per_b,), jnp.int32), pltpu.VMEM((per_b, w), x_b.dtype),
                       dma, dma, dma, dma, dma, dma])
    def dispatch(xa_hbm, xb_hbm, a0_hbm, a1_hbm, b0_hbm, b1_hbm, out_hbm,
                 idx0_v, idx1_v, rows0_v, rows1_v, idxb_v, rowsb_v,
                 lsem0, lsem1, s0sem0, s0sem1, s1sem0, s1sem1):
        wid = _sc_worker_id()
        base_a = pl.multiple_of(wid * per_a, 8)
        pltpu.sync_copy(a0_hbm.at[pl.ds(base_a, per_a)], idx0_v)
        pltpu.sync_copy(a1_hbm.at[pl.ds(base_a, per_a)], idx1_v)
        rows, lsem = (rows0_v, rows1_v), (lsem0, lsem1)
        ssem = ((s0sem0, s0sem1), (s1sem0, s1sem1))

        def load(c):
            return pltpu.async_copy(xa_hbm.at[pl.ds(base_a + c * ch, ch)], rows[c % 2], lsem[c % 2])

        def scatter(c):
            return [pltpu.async_copy(rows[c % 2], out_hbm.at[idx_v.at[pl.ds(c * ch, ch)]],
                                     ssem[k][c % 2])
                    for k, idx_v in enumerate((idx0_v, idx1_v))]

        loads, scatters = {0: load(0)}, {}
        for c in range(n_chunks):
            loads[c].wait()
            if c + 1 < n_chunks:
                for cp in scatters.pop(c - 1, ()):
                    cp.wait()
                loads[c + 1] = load(c + 1)
            scatters[c] = scatter(c)
        for cps in scatters.values():
            for cp in cps:
                cp.wait()

        base_b = pl.multiple_of(wid * per_b, 8)
        pltpu.sync_copy(xb_hbm.at[pl.ds(base_b, per_b)], rowsb_v)
        for slots_hbm in (b0_hbm, b1_hbm):
            pltpu.sync_copy(slots_hbm.at[pl.ds(base_b, per_b)], idxb_v)
            pltpu.sync_copy(rowsb_v, out_hbm.at[idxb_v])

    return dispatch(x_a, x_b, *slots_a, *slots_b)


def _sc_gather(table, idx):
    _, w = table.shape
    b = idx.shape[0]
    per_w = b // SC_WORKERS
    ch = max(c for c in range(8, SC_CHUNK + 1, 8) if per_w % c == 0)
    n_chunks = per_w // ch
    assert per_w * SC_WORKERS == b
    mesh = plsc.VectorSubcoreMesh(core_axis_name="c", subcore_axis_name="s")
    dma = pltpu.SemaphoreType.DMA

    @functools.partial(
        pl.kernel, mesh=mesh, out_type=jax.ShapeDtypeStruct((b, w), table.dtype),
        scratch_types=[pltpu.VMEM((per_w,), jnp.int32), pltpu.VMEM((ch, w), table.dtype),
                       pltpu.VMEM((ch, w), table.dtype), dma, dma, dma, dma])
    def gather(table_hbm, idx_hbm, out_hbm, idx_v, rows0_v, rows1_v, gsem0, gsem1, wsem0, wsem1):
        wid = _sc_worker_id()
        base = pl.multiple_of(wid * per_w, 8)
        pltpu.sync_copy(idx_hbm.at[pl.ds(base, per_w)], idx_v)
        rows, gsem, wsem = (rows0_v, rows1_v), (gsem0, gsem1), (wsem0, wsem1)

        def fetch(c):
            return pltpu.async_copy(table_hbm.at[idx_v.at[pl.ds(c * ch, ch)]], rows[c % 2],
                                    gsem[c % 2])

        def write(c):
            return pltpu.async_copy(rows[c % 2], out_hbm.at[pl.ds(base + c * ch, ch)], wsem[c % 2])

        fetches, writes = {0: fetch(0)}, {}
        for c in range(n_chunks):
            fetches[c].wait()
            if c + 1 < n_chunks:
                if c >= 1:
                    writes.pop(c - 1).wait()
                fetches[c + 1] = fetch(c + 1)
            writes[c] = write(c)
        for cp in writes.values():
            cp.wait()

    return gather(table, idx)


def _combine_kernel(h_ref, sel_ref, y0_ref, y1_ref, out_ref):
    sel = sel_ref[...]
    out_ref[...] = (h_ref[...] + sel[:, 2:3] * _unpack_bf16_pairs(y0_ref[...])
                    + sel[:, 3:4] * _unpack_bf16_pairs(y1_ref[...]))


def _combine(h, sel, yb, blk_off, tm):
    n = h.shape[0]
    wpk = yb.shape[1]
    half_blocks = yb.shape[0] // 2 // tm
    return pl.pallas_call(
        _combine_kernel, grid=(n // tm,),
        in_specs=[pl.BlockSpec((tm, D_MODEL), lambda i: (i, 0)),
                  pl.BlockSpec((tm, LANES), lambda i: (i, 0)),
                  pl.BlockSpec((tm, wpk), lambda i: (i + blk_off, 0)),
                  pl.BlockSpec((tm, wpk), lambda i: (i + blk_off + half_blocks, 0))],
        out_specs=pl.BlockSpec((tm, D_MODEL), lambda i: (i, 0)),
        out_shape=jax.ShapeDtypeStruct((n, D_MODEL), F32),
        compiler_params=_params(1), name="combine",
    )(h, sel, yb, yb)


def _route_plan(sels, counts, te_rows, n_tiles):
    counts = counts[0, :N_EXPERTS].astype(jnp.int32)
    padded = (counts + te_rows - 1) // te_rows * te_rows
    ends = jnp.cumsum(padded)
    starts = ends - padded
    experts = jnp.arange(N_EXPERTS, dtype=jnp.int32)

    def slots_of(sel, k):
        e = sel[:, k].astype(jnp.int32)
        start = jnp.sum(jnp.where(e[:, None] == experts[None, :], starts[None, :], 0), axis=1)
        return start + sel[:, 4 + k].astype(jnp.int32)

    slots = [(slots_of(sel, 0), slots_of(sel, 1)) for sel in sels]
    tile_start = jnp.arange(n_tiles, dtype=jnp.int32) * te_rows
    tile_expert = jnp.minimum(jnp.sum(tile_start[:, None] >= ends[None, :], axis=1),
                              N_EXPERTS - 1).astype(jnp.int32)
    in_group = tile_start - starts[tile_expert]
    tile_valid = jnp.clip(counts[tile_expert] - in_group, 0, te_rows)
    tile_valid = jnp.where(tile_start < ends[-1], tile_valid, 0).astype(jnp.int32)
    return slots, tile_expert, tile_valid


def _gmlp_kernel(h_ref, g_ref, wuv_ref, vg_ref, mix_ref, bs_ref, wo_ref, out_ref, v_ref, um_sc,
                 *, period):
    h = h_ref[...]
    tm = h.shape[0]
    ln = bs_ref.shape[0]
    gw = D_MODEL // C_GROUPS
    xn = _rms_bf16(h, g_ref[...])
    vv = jax.nn.gelu(jnp.dot(xn, wuv_ref[:, D_MODEL:], preferred_element_type=F32))
    v = vv * lax.rsqrt(jnp.mean(vv * vv, axis=-1, keepdims=True) + EPS) * vg_ref[...]
    v_ref[0] = v[tm - v_ref.shape[1]:]
    vb = v.astype(BF16)
    u = jax.nn.gelu(jnp.dot(xn, wuv_ref[:, :D_MODEL], preferred_element_type=F32))
    r = lax.broadcasted_iota(jnp.int32, (ln, ln), 0)
    c = lax.broadcasted_iota(jnp.int32, (ln, ln), 1)
    causal = (r // period == c // period) & (c % period <= r % period)
    nw = mix_ref.shape[1]
    if ln != nw:
        pick = (lax.broadcasted_iota(jnp.int32, (ln, nw), 0) % period
                == lax.broadcasted_iota(jnp.int32, (ln, nw), 1)).astype(BF16)
        wr = lax.broadcasted_iota(jnp.int32, (nw, nw), 0)
        wc = lax.broadcasted_iota(jnp.int32, (nw, nw), 1)
        corner = (wr < period) & (wc <= wr)
    for gi in range(C_GROUPS):
        if ln == nw:
            wm = jnp.where(causal, mix_ref[gi], 0.0).astype(BF16)
        else:
            rows_w = jnp.dot(pick, jnp.where(corner, mix_ref[gi], 0.0).astype(BF16),
                             preferred_element_type=F32).astype(BF16)
            tiled = lax.dot_general(rows_w, pick, (((1,), (1,)), ((), ())),
                                    preferred_element_type=F32)
            wm = jnp.where(causal, tiled, 0.0).astype(BF16)
        bias = bs_ref[:, gi:gi + 1]
        for ch in range(tm // ln):
            rs = slice(ch * ln, (ch + 1) * ln)
            cs = slice(gi * gw, (gi + 1) * gw)
            mixed = jnp.dot(wm, vb[rs, cs], preferred_element_type=F32) + bias
            um_sc[rs, cs] = (u[rs, cs] * mixed).astype(BF16)
    out_ref[...] = h + jnp.dot(um_sc[...], wo_ref[...], preferred_element_type=F32)


def _gmlp(h, g, wuv16, vg, mix, bs, wo16, *, tm, period, seq, tail):
    n = h.shape[0]
    tps = seq // tm
    row = lambda i: (i, 0)
    full = lambda a: pl.BlockSpec(a.shape, lambda i: (0,) * a.ndim)
    return pl.pallas_call(
        functools.partial(_gmlp_kernel, period=period),
        grid=(n // tm,),
        in_specs=[pl.BlockSpec((tm, D_MODEL), row), full(g), full(wuv16), full(vg), full(mix),
                  full(bs), full(wo16)],
        out_specs=[pl.BlockSpec((tm, D_MODEL), row),
                   pl.BlockSpec((1, tail, D_MODEL), lambda i: (i // tps, 0, 0))],
        out_shape=[jax.ShapeDtypeStruct((n, D_MODEL), F32),
                   jax.ShapeDtypeStruct((n // seq, tail, D_MODEL), F32)],
        scratch_shapes=[pltpu.VMEM((tm, D_MODEL), BF16)],
        compiler_params=_params(1), name="gmlp",
    )(h, g, wuv16, vg, mix, bs, wo16)


def _router_kernel(h_ref, g_ref, wr_ref, cin_ref, xn_ref, sel_ref, cnt_ref):
    @pl.when(pl.program_id(0) == 0)
    def _():
        cnt_ref[...] = cin_ref[...]

    x = h_ref[...]
    xn = x * lax.rsqrt(jnp.mean(x * x, axis=-1, keepdims=True) + EPS) * g_ref[...]
    xn_ref[...] = _pack_bf16_pairs(xn)
    x_hi = xn.astype(BF16)
    x_lo = (xn - x_hi.astype(F32)).astype(BF16)
    w = wr_ref[...]
    w_hi = w.astype(BF16)
    w_lo = (w - w_hi.astype(F32)).astype(BF16)
    logits = (jnp.dot(x_hi, w_hi, preferred_element_type=F32)
              + (jnp.dot(x_lo, w_hi, preferred_element_type=F32)
                 + jnp.dot(x_hi, w_lo, preferred_element_type=F32)))
    lane = lax.broadcasted_iota(jnp.int32, logits.shape, 1)
    lanef = lane.astype(F32)
    logits = jnp.where(lane < N_EXPERTS, logits, -jnp.inf)
    m1 = jnp.max(logits, axis=-1, keepdims=True)
    i1 = jnp.min(jnp.where(logits == m1, lanef, float(LANES)), axis=-1, keepdims=True)
    rest = jnp.where(lanef == i1, -jnp.inf, logits)
    m2 = jnp.max(rest, axis=-1, keepdims=True)
    i2 = jnp.min(jnp.where(rest == m2, lanef, float(LANES)), axis=-1, keepdims=True)
    e2 = jnp.exp(m2 - m1)
    w1 = 1.0 / (1.0 + e2)
    w2 = e2 / (1.0 + e2)
    tm = x.shape[0]
    hit1, hit2 = lanef == i1, lanef == i2
    picked = jnp.where(hit1 | hit2, 1.0, 0.0)
    r_i = lax.broadcasted_iota(jnp.int32, (tm, tm), 0)
    c_i = lax.broadcasted_iota(jnp.int32, (tm, tm), 1)
    below = jnp.where(c_i < r_i, 1.0, 0.0).astype(BF16)
    before = jnp.dot(below, picked.astype(BF16), preferred_element_type=F32) + cnt_ref[...]
    rank1 = jnp.sum(jnp.where(hit1, before, 0.0), axis=-1, keepdims=True)
    rank2 = jnp.sum(jnp.where(hit2, before, 0.0), axis=-1, keepdims=True)
    cnt_ref[...] += jnp.sum(picked, axis=0, keepdims=True)
    vals = (i1, i2, w1, w2, rank1, rank2)
    sel = jnp.zeros_like(logits)
    for li, val in enumerate(vals):
        sel = jnp.where(lane == li, val, sel)
    sel_ref[...] = sel


def _router(h, g, wr_pad, counts_in, tm):
    n = h.shape[0]
    row = lambda i: (i, 0)
    fix = lambda i: (0, 0)
    return pl.pallas_call(
        _router_kernel, grid=(n // tm,),
        in_specs=[pl.BlockSpec((tm, D_MODEL), row), pl.BlockSpec((1, D_MODEL), fix),
                  pl.BlockSpec((D_MODEL, LANES), fix), pl.BlockSpec((1, LANES), fix)],
        out_specs=[pl.BlockSpec((tm, D_MODEL // 2), row), pl.BlockSpec((tm, LANES), row),
                   pl.BlockSpec((1, LANES), fix)],
        out_shape=[jax.ShapeDtypeStruct((n, D_MODEL // 2), jnp.int32),
                   jax.ShapeDtypeStruct((n, LANES), F32),
                   jax.ShapeDtypeStruct((1, LANES), F32)],
        compiler_params=_params(1), name="router",
    )(h, g, wr_pad, counts_in)


def _rope_tables(pos):
    half = HEAD_DIM // 2
    inv = ROPE_THETA ** (-jnp.arange(half, dtype=F32) / half)
    ang = pos.astype(F32)[:, None] * inv[None, :]
    cos, sin = jnp.cos(ang), jnp.sin(ang)
    zero = jnp.zeros_like(sin)
    rep = lambda a, b: jnp.tile(jnp.concatenate([a, b], axis=-1), (1, LANES // HEAD_DIM))
    return rep(cos, cos), rep(-sin, zero), rep(zero, sin)


def _tile_for(n, pref):
    return pref if n % pref == 0 else n


def kernel(x_prompt, x_sample, state_conv, cache_kv_win, norm_mix_e, w_in_e, conv_w, q_gain,
           k_gain, w_out_e, norm_ffn_e, ffn_w_gate, ffn_w_up, ffn_w_down, norm_mix_o, w_uv,
           v_gain, w_s, b_s, w_out_o, norm_moe, w_router, moe_w_gate, moe_w_up, moe_w_down):
    bp, seq, _ = x_prompt.shape
    bs_, s_len, _ = x_sample.shape
    n_p, n_s = bp * seq, bs_ * s_len
    hp = x_prompt.reshape(n_p, D_MODEL)
    hs = x_sample.reshape(n_s, D_MODEL)
    tm_p = 512
    row = lambda a: a.reshape(1, -1)

    tabs_p = _rope_tables(jnp.arange(seq, dtype=jnp.int32))
    tabs_s = tuple(jnp.tile(t, (bs_, 1)) for t in
                   _rope_tables(PAST_LEN + jnp.arange(s_len, dtype=jnp.int32)))
    hd = jnp.arange(256) // HEAD_DIM
    bd = (hd[:, None] == hd[None, :]).astype(BF16)

    w_in16 = w_in_e[0].astype(BF16)
    w_out16 = w_out_e[0].astype(BF16)
    qg = jnp.tile(q_gain[0], 4).reshape(1, 256)
    kg = jnp.tile(k_gain[0], 4).reshape(1, 256)
    g0 = row(norm_mix_e[0])

    dils = tuple(dil for _, dil in BRANCHES)
    lw = min(2048, seq)
    z_p, gb_p, kvt_p, *planes = _inproj(hp, g0, w_in16, tabs_p, qg, kg, bd, tm_p, seq, dils, lw)
    z_s, gb_s, kv_s, q_s = _inproj(hs, g0, w_in16, tabs_s, qg, kg, bd, n_s, n_s, ())

    os_, ls_ = [], []
    for di, dil in enumerate(dils):
        m = seq // dil
        q_d, k_d, v_d = (a.reshape(bp * dil, m, D_B) for a in planes[3 * di:3 * di + 3])
        qb = min(512, m)
        o, lse = _attn_prompt(q_d, k_d, v_d, qb=qb, pb=512 // qb)
        os_.append(o.reshape(bp, dil, m, D_B))
        ls_.append(lse.reshape(bp, dil, m, LANES))
    hp = _outproj_prompt(z_p, gb_p, conv_w[0], os_, ls_, w_out16, hp, tm_p, seq, dils)

    cat_s = _mix_sample(q_s.reshape(bs_, s_len, D_B),
                        kv_s.reshape(bs_, s_len, 2 * D_B),
                        jnp.transpose(cache_kv_win[0], (0, 2, 3, 4, 1)).reshape(
                            bs_, 2, D_B, -1),
                        z_s.reshape(bs_, s_len, D_A), gb_s.reshape(bs_, s_len, D_A),
                        state_conv[0], conv_w[0])
    hs = _linear_res(cat_s.reshape(n_s, D_MODEL), w_out16, hs)

    conv_prompt = z_p.reshape(bp, seq, D_A)[:, seq - 2:][None]
    conv_sample = z_s.reshape(bs_, s_len, D_A)[:, s_len - 2:][None]
    kv_prompt = jnp.transpose(kvt_p.reshape(bp, 2, H_B, HEAD_DIM, lw), (0, 4, 1, 2, 3))[None]
    kv_sample = kv_s.reshape(bs_, s_len, 2, H_B, HEAD_DIM)[None]

    fg, fu, fd = (ffn_w_gate.astype(BF16), ffn_w_up.astype(BF16), ffn_w_down.astype(BF16))
    gf = row(norm_ffn_e[0])
    hp = _ffn_dense(hp, gf, fg, fu, fd, tm=2 * tm_p, fc=1408)
    hs = _ffn_dense(hs, gf, fg, fu, fd, tm=n_s, fc=1408)

    wuv16 = w_uv[0].astype(BF16)
    wo16 = w_out_o[0].astype(BF16)
    g1, vg = row(norm_mix_o[0]), row(v_gain[0])
    tail = seq - ((seq - 1) // CHUNK) * CHUNK
    hp, v_tail_p = _gmlp(hp, g1, wuv16, vg, w_s[0], b_s[0].T, wo16, tm=tm_p, period=CHUNK,
                         seq=seq, tail=tail)
    bias_s = jnp.tile(b_s[0][:, :s_len].T, (bs_, 1))
    hs, v_all_s = _gmlp(hs, g1, wuv16, vg, w_s[0], bias_s, wo16, tm=n_s, period=s_len,
                        seq=n_s, tail=n_s)
    v_prompt = v_tail_p[None]
    v_sample = v_all_s.reshape(bs_, s_len, D_MODEL)[None]

    wr_pad = jnp.pad(w_router[0], ((0, 0), (0, LANES - N_EXPERTS)))
    gm = row(norm_moe[0])
    xn_p, sel_p, counts = _router(hp, gm, wr_pad, jnp.zeros((1, LANES), F32), tm_p)
    xn_s, sel_s, counts = _router(hs, gm, wr_pad, counts, n_s)
    n_tok = n_p + n_s
    n_tiles = -(-(2 * n_tok + N_EXPERTS * (MOE_ROWS - 1)) // MOE_ROWS)
    (slots_p, slots_s), tile_expert, tile_valid = _route_plan(
        (sel_p, sel_s), counts, MOE_ROWS, n_tiles)
    slot0 = jnp.concatenate([slots_p[0], slots_s[0]])
    slot1 = jnp.concatenate([slots_p[1], slots_s[1]])
    xs = _sc_dispatch(xn_p, xn_s, slots_p, slots_s, n_tiles * MOE_ROWS)
    ys = _ffn_experts(tile_expert, tile_valid, xs, moe_w_gate[0], moe_w_up[0], moe_w_down[0],
                      tm=MOE_ROWS, fc=512)
    yb = _sc_gather(ys, jnp.concatenate([slot0, slot1]))
    hp = _combine(hp, sel_p, yb, 0, n_s)
    hs = _combine(hs, sel_s, yb, n_p // n_s, n_s)

    return (hp.reshape(bp, seq, D_MODEL), hs.reshape(bs_, s_len, D_MODEL),
            conv_prompt, conv_sample, kv_prompt, kv_sample, v_prompt, v_sample)
```

```python
import functools

import jax
import jax.numpy as jnp
from jax import lax
from jax.experimental import pallas as pl
from jax.experimental.pallas import tpu as pltpu
from jax.experimental.pallas import tpu_sc as plsc

F32 = jnp.float32
BF16 = jnp.bfloat16

D_MODEL = 1024
D_A = 512
H_B = 8
HEAD_DIM = 64
D_B = H_B * HEAD_DIM
BRANCHES = ((128, 1), (512, 4), (2048, 16))
BAND = 128
ROPE_THETA = 10000.0
PAST_LEN = 16384
N_EXPERTS = 8
C_GROUPS = 4
CHUNK = 128
EPS = 1e-6
LANES = 128
VMEM_LIMIT = 56 * 1024 * 1024
SC_CORES = 2
SC_WORKERS = SC_CORES * 16
SC_CHUNK = 128
MOE_ROWS = 1024


def _params(n_axes):
    return pltpu.CompilerParams(dimension_semantics=("arbitrary",) * n_axes,
                                vmem_limit_bytes=VMEM_LIMIT)


def _rms_bf16(x, g):
    ms = jnp.mean(x * x, axis=-1, keepdims=True)
    return (x * lax.rsqrt(ms + EPS) * g).astype(BF16)


def _inproj_kernel(x_ref, g_ref, w_ref, cos_ref, sina_ref, sinb_ref, qg_ref, kg_ref, bd_ref,
                   z_ref, gb_ref, kv_ref, *rest, dils, win_tiles, tiles_per_seq):
    if dils:
        planes, qkv_sc = rest[:-1], rest[-1]
    else:
        (q_out,) = rest
    xn = _rms_bf16(x_ref[...], g_ref[...])

    def col(j):
        return jnp.dot(xn, w_ref[:, j * 512:(j + 1) * 512], preferred_element_type=F32)

    z_ref[...] = col(1) * col(0)
    gb_ref[...] = col(2).astype(gb_ref.dtype)
    cos, sina, sinb = cos_ref[...], sina_ref[...], sinb_ref[...]

    def head_norm_rope(t, gain):
        outs = []
        for c2 in range(2):
            tc = t[:, c2 * 256:(c2 + 1) * 256]
            ss = jnp.dot((tc * tc).astype(BF16), bd_ref[...], preferred_element_type=F32)
            tn = tc * lax.rsqrt(ss * (1.0 / HEAD_DIM) + EPS) * gain
            for c in range(2):
                u = tn[:, c * LANES:(c + 1) * LANES]
                outs.append(u * cos + pltpu.roll(u, LANES - 32, 1) * sina
                            + pltpu.roll(u, 32, 1) * sinb)
        return outs

    q = [t * (HEAD_DIM ** -0.5) for t in head_norm_rope(col(3), qg_ref[...])]
    k = head_norm_rope(col(4), kg_ref[...])
    v = col(5)
    if not dils:
        for c in range(4):
            kv_ref[:, c * LANES:(c + 1) * LANES] = k[c]
            q_out[:, c * LANES:(c + 1) * LANES] = q[c]
        kv_ref[:, D_B:] = v
        return

    @pl.when(pl.program_id(0) % tiles_per_seq >= tiles_per_seq - win_tiles)
    def _():
        for c in range(4):
            kv_ref[0, c * LANES:(c + 1) * LANES, :] = k[c].T
            kv_ref[0, D_B + c * LANES:D_B + (c + 1) * LANES, :] = v[:, c * LANES:(c + 1) * LANES].T

    tm = x_ref.shape[0]
    for a, val in enumerate((q, k, [v[:, c * LANES:(c + 1) * LANES] for c in range(4)])):
        for c in range(4):
            qkv_sc[0, a, c] = val[c]
            planes[a][0, 0, :, c * LANES:(c + 1) * LANES] = val[c].astype(BF16)
    for di in range(1, len(dils)):
        dil, prev = dils[di], dils[di - 1]
        ratio = dil // prev
        src, dst = qkv_sc.at[(di - 1) % 2], qkv_sc.at[di % 2]
        for r in range(dil):
            start = (r % prev) * (tm // prev) + r // prev
            for a in range(3):
                for c in range(4):
                    rows_ac = src[a, c, pl.ds(start, tm // dil, stride=ratio), :]
                    planes[3 * di + a][0, r, :, c * LANES:(c + 1) * LANES] = rows_ac.astype(BF16)
                    if di + 1 < len(dils):
                        dst[a, c, r * (tm // dil):(r + 1) * (tm // dil), :] = rows_ac


def _inproj(x, g, w16, tabs, qg, kg, bd, tm, seq, dils, win=0):
    n = x.shape[0]
    cos, sina, sinb = tabs
    tt = cos.shape[0] // tm
    tps = seq // tm
    win_tiles = win // tm
    row = lambda i: (i, 0)
    fix = lambda i: (0, 0)
    tab = lambda i: (i % tt, 0)
    out_specs = [pl.BlockSpec((tm, D_A), row), pl.BlockSpec((tm, D_A), row)]
    out_shape = [jax.ShapeDtypeStruct((n, D_A), F32),
                 jax.ShapeDtypeStruct((n, D_A), BF16 if dils else F32)]
    if dils:
        out_specs.append(pl.BlockSpec(
            (1, 2 * D_B, tm),
            lambda i: (i // tps, 0, jnp.maximum(i % tps - (tps - win_tiles), 0))))
        out_shape.append(jax.ShapeDtypeStruct((n // seq, 2 * D_B, win), F32))
        assert dils[0] == 1 and all(b % a == 0 for a, b in zip(dils, dils[1:]))
        scratch = [pltpu.VMEM((2, 3, D_B // LANES, tm, LANES), F32)]
    else:
        out_specs += [pl.BlockSpec((tm, 2 * D_B), row), pl.BlockSpec((tm, D_B), row)]
        out_shape += [jax.ShapeDtypeStruct((n, 2 * D_B), F32),
                      jax.ShapeDtypeStruct((n, D_B), F32)]
        scratch = []
    for dil in dils:
        for _ in range(3):
            out_specs.append(pl.BlockSpec((1, dil, tm // dil, D_B),
                                          lambda i: (i // tps, 0, i % tps, 0)))
            out_shape.append(jax.ShapeDtypeStruct((n // seq, dil, seq // dil, D_B), BF16))
    return pl.pallas_call(
        functools.partial(_inproj_kernel, dils=dils, win_tiles=win_tiles, tiles_per_seq=tps),
        grid=(n // tm,),
        in_specs=[pl.BlockSpec((tm, D_MODEL), row), pl.BlockSpec((1, D_MODEL), fix),
                  pl.BlockSpec(w16.shape, fix),
                  pl.BlockSpec((tm, LANES), tab), pl.BlockSpec((tm, LANES), tab),
                  pl.BlockSpec((tm, LANES), tab),
                  pl.BlockSpec((1, 256), fix), pl.BlockSpec((1, 256), fix),
                  pl.BlockSpec((256, 256), fix)],
        out_specs=out_specs, out_shape=out_shape, scratch_shapes=scratch,
        compiler_params=_params(1), name="inproj",
    )(x, g, w16, cos, sina, sinb, qg, kg, bd)


def _attn_prompt_kernel(q_ref, kc_ref, kp_ref, vc_ref, vp_ref, o_ref, lse_ref, *, qb):
    j = pl.program_id(1)
    nsub = qb // BAND
    lane = lax.broadcasted_iota(jnp.int32, (1, LANES), 1)
    head0 = lane < HEAD_DIM
    qi = lax.broadcasted_iota(jnp.int32, (2 * BAND, 2 * BAND), 0) % BAND + BAND
    kj = lax.broadcasted_iota(jnp.int32, (2 * BAND, 2 * BAND), 1)
    dist = qi - kj
    band = (dist >= 0) & (dist <= BAND)
    first = band & ((kj >= BAND) | (j > 0))
    lanes = [slice(hp * LANES, (hp + 1) * LANES) for hp in range(D_B // LANES)]
    for pi, sub in ((pi, sub) for pi in range(q_ref.shape[0]) for sub in range(nsub)):
        rows = slice(sub * BAND, (sub + 1) * BAND)
        keys = slice((sub - 1) * BAND, (sub + 1) * BAND)
        lse_all = jnp.zeros((BAND, LANES), F32)
        for hp, ls in enumerate(lanes):
            qs = q_ref[pi, rows, ls]
            if sub == 0:
                ks = jnp.concatenate([kp_ref[pi, :, ls], kc_ref[pi, :BAND, ls]], axis=0)
                vs = jnp.concatenate([vp_ref[pi, :, ls], vc_ref[pi, :BAND, ls]], axis=0)
            else:
                ks, vs = kc_ref[pi, keys, ls], vc_ref[pi, keys, ls]
            q2 = jnp.concatenate([jnp.where(head0, qs, jnp.zeros_like(qs)),
                                  jnp.where(head0, jnp.zeros_like(qs), qs)], axis=0)
            s = lax.dot_general(q2, ks, (((1,), (1,)), ((), ())), preferred_element_type=F32)
            s = jnp.where(first if sub == 0 else band, s, -jnp.inf)
            mx = jnp.max(s, axis=-1, keepdims=True)
            p = jnp.exp(s - mx)
            l = jnp.sum(p, axis=-1, keepdims=True)
            pv = jnp.dot(p.astype(BF16), vs, preferred_element_type=F32) / l
            lse = mx + jnp.log(l)
            o_ref[pi, rows, ls] = jnp.where(head0, pv[:BAND], pv[BAND:]).astype(BF16)
            lse_all = jnp.where(lane == 2 * hp, lse[:BAND],
                                jnp.where(lane == 2 * hp + 1, lse[BAND:], lse_all))
        lse_ref[pi, rows, :] = lse_all


def _attn_prompt(q, k, v, qb, pb):
    b, m, w = q.shape
    nsub = qb // BAND
    cur = lambda bi, j: (bi, j, 0)
    prev = lambda bi, j: (bi, jnp.maximum(j * nsub - 1, 0), 0)
    return pl.pallas_call(
        functools.partial(_attn_prompt_kernel, qb=qb),
        grid=(b // pb, m // qb),
        in_specs=[pl.BlockSpec((pb, qb, w), cur), pl.BlockSpec((pb, qb, w), cur),
                  pl.BlockSpec((pb, BAND, w), prev), pl.BlockSpec((pb, qb, w), cur),
                  pl.BlockSpec((pb, BAND, w), prev)],
        out_specs=[pl.BlockSpec((pb, qb, w), cur), pl.BlockSpec((pb, qb, LANES), cur)],
        out_shape=[jax.ShapeDtypeStruct((b, m, w), BF16),
                   jax.ShapeDtypeStruct((b, m, LANES), F32)],
        compiler_params=_params(2), name="attn_prompt",
    )(q, k, k, v, v)


def _outproj_prompt_kernel(z_ref, zh_ref, gb_ref, cw_ref, *rest, tiles_per_batch, dils):
    nb = len(dils)
    o_refs, l_refs = rest[:nb], rest[nb:2 * nb]
    w_ref, h_ref, out_ref = rest[2 * nb:2 * nb + 3]
    o_scs, l_scs, tmp_sc = rest[2 * nb + 3:3 * nb + 3], rest[3 * nb + 3:4 * nb + 3], rest[-1]
    i = pl.program_id(0)
    z = z_ref[...]
    tm = z.shape[0]
    def put_rows(dst, spare, val, r, dil):
        if dil == 1:
            dst[...] = val
        elif dil <= 4:
            dst[pl.ds(r, tm // dil, stride=dil), :] = val
        else:
            spare[pl.ds((r % 4) * (tm // 4) + r // 4, tm // dil, stride=dil // 4), :] = val

    def finish_rows(dst, spare, dil):
        if dil > 4:
            for b in range(4):
                dst[pl.ds(b, tm // 4, stride=4), :] = spare[b * (tm // 4):(b + 1) * (tm // 4), :]

    for dil, o_ref, l_ref, o_sc, l_sc in zip(dils, o_refs, l_refs, o_scs, l_scs):
        assert dil <= 16
        for r in range(dil):
            for c in range(D_B // LANES):
                ls = slice(c * LANES, (c + 1) * LANES)
                put_rows(o_sc.at[c], tmp_sc.at[c], o_ref[0, r, :, ls].astype(F32), r, dil)
            put_rows(l_sc, tmp_sc.at[D_B // LANES], l_ref[0, r], r, dil)
        for c in range(D_B // LANES):
            finish_rows(o_sc.at[c], tmp_sc.at[c], dil)
        finish_rows(l_sc, tmp_sc.at[D_B // LANES], dil)
    unstage = lambda sc: jnp.concatenate([sc[c] for c in range(D_B // LANES)], axis=1)
    zh = jnp.where(i % tiles_per_batch == 0, 0.0, zh_ref[...])
    row = lax.broadcasted_iota(jnp.int32, (tm, 1), 0)
    zm1 = jnp.where(row >= 1, pltpu.roll(z, 1, 0), zh[7:8])
    zm2 = jnp.where(row >= 2, pltpu.roll(z, 2, 0), jnp.where(row == 1, zh[7:8], zh[6:7]))
    cw = cw_ref[...]
    a_out = gb_ref[...].astype(F32) * (cw[0:1] * zm2 + cw[1:2] * zm1 + cw[2:3] * z)
    lses = [l_sc[...] for l_sc in l_scs]
    lm = functools.reduce(jnp.maximum, lses)
    es = [jnp.exp(l - lm) for l in lses]
    tot = sum(es)
    spread = (lax.broadcasted_iota(jnp.int32, (LANES, D_B), 1) // HEAD_DIM
              == lax.broadcasted_iota(jnp.int32, (LANES, D_B), 0)).astype(BF16)

    def widen(wgt):
        hi = wgt.astype(BF16)
        lo = (wgt - hi.astype(F32)).astype(BF16)
        return (jnp.dot(hi, spread, preferred_element_type=F32)
                + jnp.dot(lo, spread, preferred_element_type=F32))

    o = sum(widen(e / tot) * unstage(o_sc) for e, o_sc in zip(es, o_scs))
    y = jnp.dot(a_out.astype(BF16), w_ref[:D_A, :], preferred_element_type=F32)
    y = y + jnp.dot(o.astype(BF16), w_ref[D_A:, :], preferred_element_type=F32)
    out_ref[...] = h_ref[...] + y


def _outproj_prompt(z, gb, cw, os_, ls_, w16, h, tm, seq, dils):
    n = z.shape[0]
    tps = seq // tm
    row = lambda i: (i, 0)
    fix = lambda i: (0, 0)
    halo = lambda i: (jnp.maximum(i * (tm // 8) - 1, 0), 0)
    half = pl.BlockSpec((tm, D_A), row)
    plane = lambda width: [pl.BlockSpec((1, dil, tm // dil, width),
                                        lambda i: (i // tps, 0, i % tps, 0)) for dil in dils]
    return pl.pallas_call(
        functools.partial(_outproj_prompt_kernel, tiles_per_batch=tps, dils=dils),
        grid=(n // tm,),
        in_specs=[half, pl.BlockSpec((8, D_A), halo), half, pl.BlockSpec((3, D_A), fix),
                  *plane(D_B), *plane(LANES),
                  pl.BlockSpec((D_MODEL, D_MODEL), fix), pl.BlockSpec((tm, D_MODEL), row)],
        out_specs=pl.BlockSpec((tm, D_MODEL), row),
        out_shape=jax.ShapeDtypeStruct((n, D_MODEL), F32),
        scratch_shapes=([pltpu.VMEM((D_B // LANES, tm, LANES), F32)] * len(dils)
                        + [pltpu.VMEM((tm, LANES), F32)] * len(dils)
                        + [pltpu.VMEM((D_B // LANES + 1, tm, LANES), F32)]),
        compiler_params=_params(1), name="outproj_prompt",
    )(z, z, gb, cw, *os_, *ls_, w16, h)


def _mix_sample_kernel(q_ref, kvn_ref, win_ref, z_ref, gb_ref, st_ref, cw_ref, cat_ref):
    s_len = q_ref.shape[1]
    rows = H_B * s_len
    npad = LANES
    q = q_ref[0]
    r_i = lax.broadcasted_iota(jnp.int32, (rows, D_B), 0)
    c_i = lax.broadcasted_iota(jnp.int32, (rows, D_B), 1)
    qt = jnp.concatenate([q] * H_B, axis=0)
    qbd = jnp.where(r_i // s_len == c_i // HEAD_DIM, qt, 0.0).astype(BF16)
    kvn = jnp.concatenate([kvn_ref[0], jnp.zeros((npad - s_len, 2 * D_B), F32)], axis=0)

    def multiplicity(delta):
        cnt = jnp.zeros(delta.shape, F32)
        for _, dil in BRANCHES:
            ok = (delta >= 0) & (delta <= BAND * dil) & ((delta & (dil - 1)) == 0)
            cnt = cnt + jnp.where(ok, 1.0, 0.0)
        return cnt

    lb = win_ref.shape[3]
    nt = (((1,), (1,)), ((), ()))
    tok_c = lax.broadcasted_iota(jnp.int32, (rows, lb), 0) % s_len
    cnt_c = multiplicity(tok_c + lb - lax.broadcasted_iota(jnp.int32, (rows, lb), 1))
    tok_n = lax.broadcasted_iota(jnp.int32, (rows, npad), 0) % s_len
    cnt_n = multiplicity(tok_n - lax.broadcasted_iota(jnp.int32, (rows, npad), 1))
    sc = jnp.dot(qbd, win_ref[0, 0].astype(BF16), preferred_element_type=F32)
    sn = lax.dot_general(qbd, kvn[:, :D_B].astype(BF16), nt, preferred_element_type=F32)
    sc = jnp.where(cnt_c > 0, sc, -jnp.inf)
    sn = jnp.where(cnt_n > 0, sn, -jnp.inf)
    mx = jnp.maximum(jnp.max(sc, axis=-1, keepdims=True), jnp.max(sn, axis=-1, keepdims=True))
    pc = cnt_c * jnp.exp(sc - mx)
    pn = cnt_n * jnp.exp(sn - mx)
    den = jnp.sum(pc, axis=-1, keepdims=True) + jnp.sum(pn, axis=-1, keepdims=True)
    of = lax.dot_general(pc.astype(BF16), win_ref[0, 1].astype(BF16), nt,
                         preferred_element_type=F32)
    of = (of + jnp.dot(pn.astype(BF16), kvn[:, D_B:].astype(BF16),
                       preferred_element_type=F32)) / den
    lane_head = lax.broadcasted_iota(jnp.int32, (s_len, D_B), 1) // HEAD_DIM
    o = jnp.zeros((s_len, D_B), F32)
    for h in range(H_B):
        o = o + jnp.where(lane_head == h, of[h * s_len:(h + 1) * s_len], 0.0)
    z = z_ref[0]
    st = st_ref[0]
    t = lax.broadcasted_iota(jnp.int32, (s_len, 1), 0)
    zm1 = jnp.where(t >= 1, pltpu.roll(z, 1, 0), st[1:2])
    zm2 = jnp.where(t >= 2, pltpu.roll(z, 2, 0), jnp.where(t == 1, st[1:2], st[0:1]))
    cw = cw_ref[...]
    cat_ref[0, :, :D_A] = gb_ref[0] * (cw[0:1] * zm2 + cw[1:2] * zm1 + cw[2:3] * z)
    cat_ref[0, :, D_A:] = o


def _mix_sample(q, kvn, win, z, gb, st, cw):
    b, s_len, _ = q.shape
    per = lambda i: (i,) + (0,) * 2
    blk = lambda a: pl.BlockSpec((1,) + a.shape[1:], per)
    return pl.pallas_call(
        _mix_sample_kernel,
        grid=(b,),
        in_specs=[blk(q), blk(kvn),
                  pl.BlockSpec((1,) + win.shape[1:], lambda i: (i, 0, 0, 0)),
                  blk(z), blk(gb), blk(st), pl.BlockSpec((3, D_A), lambda i: (0, 0))],
        out_specs=pl.BlockSpec((1, s_len, D_MODEL), per),
        out_shape=jax.ShapeDtypeStruct((b, s_len, D_MODEL), F32),
        compiler_params=_params(1), name="mix_sample",
    )(q, kvn, win, z, gb, st, cw)


def _linear_res_kernel(x_ref, w_ref, h_ref, out_ref):
    out_ref[...] = h_ref[...] + jnp.dot(x_ref[...].astype(BF16), w_ref[...],
                                        preferred_element_type=F32)


def _linear_res(x, w16, h):
    n, k = x.shape
    full = lambda a: pl.BlockSpec(a.shape, lambda i: (0, 0))
    return pl.pallas_call(
        _linear_res_kernel, grid=(1,),
        in_specs=[full(x), full(w16), full(h)], out_specs=full(h),
        out_shape=jax.ShapeDtypeStruct(h.shape, F32),
        compiler_params=_params(1), name="linear_res",
    )(x, w16, h)


def _swiglu_step(xn, wg_ref, wu_ref, wd_ref, acc_sc):
    a = jnp.dot(xn, wg_ref[0], preferred_element_type=F32)
    b = jnp.dot(xn, wu_ref[0], preferred_element_type=F32)
    t = (jax.nn.silu(a) * b).astype(BF16)
    acc_sc[...] += jnp.dot(t, wd_ref[0], preferred_element_type=F32)


def _ffn_dense_kernel(x_ref, g_ref, wg_ref, wu_ref, wd_ref, out_ref, xn_sc, acc_sc):
    f = pl.program_id(1)

    @pl.when(f == 0)
    def _():
        xn_sc[...] = _rms_bf16(x_ref[...], g_ref[...])
        acc_sc[...] = jnp.zeros_like(acc_sc)

    _swiglu_step(xn_sc[...], wg_ref, wu_ref, wd_ref, acc_sc)

    @pl.when(f == pl.num_programs(1) - 1)
    def _():
        out_ref[...] = x_ref[...] + acc_sc[...]


def _ffn_dense(x, g, wg, wu, wd, *, tm, fc):
    n = x.shape[0]
    ff = wg.shape[2]
    row = lambda i, f: (i, 0)
    return pl.pallas_call(
        _ffn_dense_kernel, grid=(n // tm, ff // fc),
        in_specs=[pl.BlockSpec((tm, D_MODEL), row), pl.BlockSpec((1, D_MODEL), lambda i, f: (0, 0)),
                  pl.BlockSpec((1, D_MODEL, fc), lambda i, f: (0, 0, f)),
                  pl.BlockSpec((1, D_MODEL, fc), lambda i, f: (0, 0, f)),
                  pl.BlockSpec((1, fc, D_MODEL), lambda i, f: (0, f, 0))],
        out_specs=pl.BlockSpec((tm, D_MODEL), row),
        out_shape=jax.ShapeDtypeStruct((n, D_MODEL), F32),
        scratch_shapes=[pltpu.VMEM((tm, D_MODEL), BF16), pltpu.VMEM((tm, D_MODEL), F32)],
        compiler_params=_params(2), name="ffn_dense",
    )(x, g, wg, wu, wd)


def _pack_bf16_pairs(x):
    half = x.shape[1] // 2
    lo = lax.bitcast_convert_type(x[:, :half].astype(BF16).astype(F32), jnp.int32)
    hi = lax.bitcast_convert_type(x[:, half:].astype(BF16).astype(F32), jnp.int32)
    return (hi & -65536) | lax.shift_right_logical(lo, 16)


def _unpack_bf16_pairs(p):
    lo = lax.bitcast_convert_type(p << 16, F32)
    hi = lax.bitcast_convert_type(p & -65536, F32)
    return jnp.concatenate([lo, hi], axis=1)


def _ffn_experts_kernel(te_ref, tv_ref, x_ref, wg_ref, wu_ref, wd_ref, out_ref, xn_sc, acc_sc):
    del te_ref
    i, f = pl.program_id(0), pl.program_id(1)
    n_valid = tv_ref[i]

    @pl.when(n_valid > 0)
    def _():
        @pl.when(f == 0)
        def _():
            row = lax.broadcasted_iota(jnp.int32, x_ref.shape, 0)
            xn_sc[...] = _unpack_bf16_pairs(jnp.where(row < n_valid, x_ref[...], 0)).astype(BF16)
            acc_sc[...] = jnp.zeros_like(acc_sc)

        def step(rows):
            xn = xn_sc[:rows]
            a = jnp.dot(xn, wg_ref[0].astype(BF16), preferred_element_type=F32)
            b = jnp.dot(xn, wu_ref[0].astype(BF16), preferred_element_type=F32)
            t = (jax.nn.silu(a) * b).astype(BF16)
            acc_sc[:rows] += jnp.dot(t, wd_ref[0].astype(BF16), preferred_element_type=F32)

        tm = xn_sc.shape[0]
        quarter = tm // 4
        for nq in range(1, 5):
            pl.when((n_valid > (nq - 1) * quarter) & (n_valid <= nq * quarter))(
                functools.partial(step, nq * quarter))

        @pl.when(f == pl.num_programs(1) - 1)
        def _():
            out_ref[...] = _pack_bf16_pairs(acc_sc[...])


def _ffn_experts(tile_expert, tile_valid, x, wg, wu, wd, *, tm, fc):
    n, wpk = x.shape
    ff = wg.shape[2]
    row = lambda i, f, te, tv: (i, 0)
    grid_spec = pltpu.PrefetchScalarGridSpec(
        num_scalar_prefetch=2, grid=(n // tm, ff // fc),
        in_specs=[pl.BlockSpec((tm, wpk), row),
                  pl.BlockSpec((1, D_MODEL, fc), lambda i, f, te, tv: (te[i], 0, f)),
                  pl.BlockSpec((1, D_MODEL, fc), lambda i, f, te, tv: (te[i], 0, f)),
                  pl.BlockSpec((1, fc, D_MODEL), lambda i, f, te, tv: (te[i], f, 0))],
        out_specs=pl.BlockSpec((tm, wpk), row),
        scratch_shapes=[pltpu.VMEM((tm, D_MODEL), BF16), pltpu.VMEM((tm, D_MODEL), F32)])
    return pl.pallas_call(
        _ffn_experts_kernel, grid_spec=grid_spec,
        out_shape=jax.ShapeDtypeStruct((n, wpk), jnp.int32),
        compiler_params=_params(2), name="ffn_experts",
    )(tile_expert, tile_valid, x, wg, wu, wd)


def _sc_worker_id():
    return lax.axis_index("s") * SC_CORES + lax.axis_index("c")


def _sc_dispatch(x_a, x_b, slots_a, slots_b, n_slots):
    w = x_a.shape[1]
    per_a = x_a.shape[0] // SC_WORKERS
    per_b = x_b.shape[0] // SC_WORKERS
    ch = min(SC_CHUNK // 2, per_a)
    n_chunks = per_a // ch
    mesh = plsc.VectorSubcoreMesh(core_axis_name="c", subcore_axis_name="s")
    dma = pltpu.SemaphoreType.DMA

    @functools.partial(
        pl.kernel, mesh=mesh, out_type=jax.ShapeDtypeStruct((n_slots, w), x_a.dtype),
        scratch_types=[pltpu.VMEM((per_a,), jnp.int32), pltpu.VMEM((per_a,), jnp.int32),
                       pltpu.VMEM((ch, w), x_a.dtype), pltpu.VMEM((ch, w), x_a.dtype),
                       pltpu.VMEM((per_b,), jnp.int32), pltpu.VMEM((per_b, w), x_b.dtype),
                       dma, dma, dma, dma, dma, dma])
    def dispatch(xa_hbm, xb_hbm, a0_hbm, a1_hbm, b0_hbm, b1_hbm, out_hbm,
                 idx0_v, idx1_v, rows0_v, rows1_v, idxb_v, rowsb_v,
                 lsem0, lsem1, s0sem0, s0sem1, s1sem0, s1sem1):
        wid = _sc_worker_id()
        base_a = pl.multiple_of(wid * per_a, 8)
        pltpu.sync_copy(a0_hbm.at[pl.ds(base_a, per_a)], idx0_v)
        pltpu.sync_copy(a1_hbm.at[pl.ds(base_a, per_a)], idx1_v)
        rows, lsem = (rows0_v, rows1_v), (lsem0, lsem1)
        ssem = ((s0sem0, s0sem1), (s1sem0, s1sem1))

        def load(c):
            return pltpu.async_copy(xa_hbm.at[pl.ds(base_a + c * ch, ch)], rows[c % 2], lsem[c % 2])

        def scatter(c):
            return [pltpu.async_copy(rows[c % 2], out_hbm.at[idx_v.at[pl.ds(c * ch, ch)]],
                                     ssem[k][c % 2])
                    for k, idx_v in enumerate((idx0_v, idx1_v))]

        loads, scatters = {0: load(0)}, {}
        for c in range(n_chunks):
            loads[c].wait()
            if c + 1 < n_chunks:
                for cp in scatters.pop(c - 1, ()):
                    cp.wait()
                loads[c + 1] = load(c + 1)
            scatters[c] = scatter(c)
        for cps in scatters.values():
            for cp in cps:
                cp.wait()

        base_b = pl.multiple_of(wid * per_b, 8)
        pltpu.sync_copy(xb_hbm.at[pl.ds(base_b, per_b)], rowsb_v)
        for slots_hbm in (b0_hbm, b1_hbm):
            pltpu.sync_copy(slots_hbm.at[pl.ds(base_b, per_b)], idxb_v)
            pltpu.sync_copy(rowsb_v, out_hbm.at[idxb_v])

    return dispatch(x_a, x_b, *slots_a, *slots_b)


def _sc_gather(table, idx):
    _, w = table.shape
    b = idx.shape[0]
    per_w = b // SC_WORKERS
    ch = max(c for c in range(8, SC_CHUNK + 1, 8) if per_w % c == 0)
    n_chunks = per_w // ch
    assert per_w * SC_WORKERS == b
    mesh = plsc.VectorSubcoreMesh(core_axis_name="c", subcore_axis_name="s")
    dma = pltpu.SemaphoreType.DMA

    @functools.partial(
        pl.kernel, mesh=mesh, out_type=jax.ShapeDtypeStruct((b, w), table.dtype),
        scratch_types=[pltpu.VMEM((per_w,), jnp.int32), pltpu.VMEM((ch, w), table.dtype),
                       pltpu.VMEM((ch, w), table.dtype), dma, dma, dma, dma])
    def gather(table_hbm, idx_hbm, out_hbm, idx_v, rows0_v, rows1_v, gsem0, gsem1, wsem0, wsem1):
        wid = _sc_worker_id()
        base = pl.multiple_of(wid * per_w, 8)
        pltpu.sync_copy(idx_hbm.at[pl.ds(base, per_w)], idx_v)
        rows, gsem, wsem = (rows0_v, rows1_v), (gsem0, gsem1), (wsem0, wsem1)

        def fetch(c):
            return pltpu.async_copy(table_hbm.at[idx_v.at[pl.ds(c * ch, ch)]], rows[c % 2],
                                    gsem[c % 2])

        def write(c):
            return pltpu.async_copy(rows[c % 2], out_hbm.at[pl.ds(base + c * ch, ch)], wsem[c % 2])

        fetches, writes = {0: fetch(0)}, {}
        for c in range(n_chunks):
            fetches[c].wait()
            if c + 1 < n_chunks:
                if c >= 1:
                    writes.pop(c - 1).wait()
                fetches[c + 1] = fetch(c + 1)
            writes[c] = write(c)
        for cp in writes.values():
            cp.wait()

    return gather(table, idx)


def _combine_kernel(h_ref, sel_ref, y0_ref, y1_ref, out_ref):
    sel = sel_ref[...]
    out_ref[...] = (h_ref[...] + sel[:, 2:3] * _unpack_bf16_pairs(y0_ref[...])
                    + sel[:, 3:4] * _unpack_bf16_pairs(y1_ref[...]))


def _combine(h, sel, yb, blk_off, tm):
    n = h.shape[0]
    wpk = yb.shape[1]
    half_blocks = yb.shape[0] // 2 // tm
    return pl.pallas_call(
        _combine_kernel, grid=(n // tm,),
        in_specs=[pl.BlockSpec((tm, D_MODEL), lambda i: (i, 0)),
                  pl.BlockSpec((tm, LANES), lambda i: (i, 0)),
                  pl.BlockSpec((tm, wpk), lambda i: (i + blk_off, 0)),
                  pl.BlockSpec((tm, wpk), lambda i: (i + blk_off + half_blocks, 0))],
        out_specs=pl.BlockSpec((tm, D_MODEL), lambda i: (i, 0)),
        out_shape=jax.ShapeDtypeStruct((n, D_MODEL), F32),
        compiler_params=_params(1), name="combine",
    )(h, sel, yb, yb)


def _route_plan(sels, counts, te_rows, n_tiles):
    counts = counts[0, :N_EXPERTS].astype(jnp.int32)
    padded = (counts + te_rows - 1) // te_rows * te_rows
    ends = jnp.cumsum(padded)
    starts = ends - padded
    experts = jnp.arange(N_EXPERTS, dtype=jnp.int32)

    def slots_of(sel, k):
        e = sel[:, k].astype(jnp.int32)
        start = jnp.sum(jnp.where(e[:, None] == experts[None, :], starts[None, :], 0), axis=1)
        return start + sel[:, 4 + k].astype(jnp.int32)

    slots = [(slots_of(sel, 0), slots_of(sel, 1)) for sel in sels]
    tile_start = jnp.arange(n_tiles, dtype=jnp.int32) * te_rows
    tile_expert = jnp.minimum(jnp.sum(tile_start[:, None] >= ends[None, :], axis=1),
                              N_EXPERTS - 1).astype(jnp.int32)
    in_group = tile_start - starts[tile_expert]
    tile_valid = jnp.clip(counts[tile_expert] - in_group, 0, te_rows)
    tile_valid = jnp.where(tile_start < ends[-1], tile_valid, 0).astype(jnp.int32)
    return slots, tile_expert, tile_valid


def _gmlp_kernel(h_ref, g_ref, wuv_ref, vg_ref, mix_ref, bs_ref, wo_ref, out_ref, v_ref, um_sc,
                 *, period):
    h = h_ref[...]
    tm = h.shape[0]
    ln = bs_ref.shape[0]
    gw = D_MODEL // C_GROUPS
    xn = _rms_bf16(h, g_ref[...])
    vv = jax.nn.gelu(jnp.dot(xn, wuv_ref[:, D_MODEL:], preferred_element_type=F32))
    v = vv * lax.rsqrt(jnp.mean(vv * vv, axis=-1, keepdims=True) + EPS) * vg_ref[...]
    v_ref[0] = v[tm - v_ref.shape[1]:]
    vb = v.astype(BF16)
    u = jax.nn.gelu(jnp.dot(xn, wuv_ref[:, :D_MODEL], preferred_element_type=F32))
    r = lax.broadcasted_iota(jnp.int32, (ln, ln), 0)
    c = lax.broadcasted_iota(jnp.int32, (ln, ln), 1)
    causal = (r // period == c // period) & (c % period <= r % period)
    nw = mix_ref.shape[1]
    if ln != nw:
        pick = (lax.broadcasted_iota(jnp.int32, (ln, nw), 0) % period
                == lax.broadcasted_iota(jnp.int32, (ln, nw), 1)).astype(BF16)
        wr = lax.broadcasted_iota(jnp.int32, (nw, nw), 0)
        wc = lax.broadcasted_iota(jnp.int32, (nw, nw), 1)
        corner = (wr < period) & (wc <= wr)
    for gi in range(C_GROUPS):
        if ln == nw:
            wm = jnp.where(causal, mix_ref[gi], 0.0).astype(BF16)
        else:
            rows_w = jnp.dot(pick, jnp.where(corner, mix_ref[gi], 0.0).astype(BF16),
                             preferred_element_type=F32).astype(BF16)
            tiled = lax.dot_general(rows_w, pick, (((1,), (1,)), ((), ())),
                                    preferred_element_type=F32)
            wm = jnp.where(causal, tiled, 0.0).astype(BF16)
        bias = bs_ref[:, gi:gi + 1]
        for ch in range(tm // ln):
            rs = slice(ch * ln, (ch + 1) * ln)
            cs = slice(gi * gw, (gi + 1) * gw)
            mixed = jnp.dot(wm, vb[rs, cs], preferred_element_type=F32) + bias
            um_sc[rs, cs] = (u[rs, cs] * mixed).astype(BF16)
    out_ref[...] = h + jnp.dot(um_sc[...], wo_ref[...], preferred_element_type=F32)


def _gmlp(h, g, wuv16, vg, mix, bs, wo16, *, tm, period, seq, tail):
    n = h.shape[0]
    tps = seq // tm
    row = lambda i: (i, 0)
    full = lambda a: pl.BlockSpec(a.shape, lambda i: (0,) * a.ndim)
    return pl.pallas_call(
        functools.partial(_gmlp_kernel, period=period),
        grid=(n // tm,),
        in_specs=[pl.BlockSpec((tm, D_MODEL), row), full(g), full(wuv16), full(vg), full(mix),
                  full(bs), full(wo16)],
        out_specs=[pl.BlockSpec((tm, D_MODEL), row),
                   pl.BlockSpec((1, tail, D_MODEL), lambda i: (i // tps, 0, 0))],
        out_shape=[jax.ShapeDtypeStruct((n, D_MODEL), F32),
                   jax.ShapeDtypeStruct((n // seq, tail, D_MODEL), F32)],
        scratch_shapes=[pltpu.VMEM((tm, D_MODEL), BF16)],
        compiler_params=_params(1), name="gmlp",
    )(h, g, wuv16, vg, mix, bs, wo16)


def _router_kernel(h_ref, g_ref, wr_ref, cin_ref, xn_ref, sel_ref, cnt_ref):
    @pl.when(pl.program_id(0) == 0)
    def _():
        cnt_ref[...] = cin_ref[...]

    x = h_ref[...]
    xn = x * lax.rsqrt(jnp.mean(x * x, axis=-1, keepdims=True) + EPS) * g_ref[...]
    xn_ref[...] = _pack_bf16_pairs(xn)
    x_hi = xn.astype(BF16)
    x_lo = (xn - x_hi.astype(F32)).astype(BF16)
    w = wr_ref[...]
    w_hi = w.astype(BF16)
    w_lo = (w - w_hi.astype(F32)).astype(BF16)
    logits = (jnp.dot(x_hi, w_hi, preferred_element_type=F32)
              + (jnp.dot(x_lo, w_hi, preferred_element_type=F32)
                 + jnp.dot(x_hi, w_lo, preferred_element_type=F32)))
    lane = lax.broadcasted_iota(jnp.int32, logits.shape, 1)
    lanef = lane.astype(F32)
    logits = jnp.where(lane < N_EXPERTS, logits, -jnp.inf)
    m1 = jnp.max(logits, axis=-1, keepdims=True)
    i1 = jnp.min(jnp.where(logits == m1, lanef, float(LANES)), axis=-1, keepdims=True)
    rest = jnp.where(lanef == i1, -jnp.inf, logits)
    m2 = jnp.max(rest, axis=-1, keepdims=True)
    i2 = jnp.min(jnp.where(rest == m2, lanef, float(LANES)), axis=-1, keepdims=True)
    e2 = jnp.exp(m2 - m1)
    w1 = 1.0 / (1.0 + e2)
    w2 = e2 / (1.0 + e2)
    tm = x.shape[0]
    hit1, hit2 = lanef == i1, lanef == i2
    picked = jnp.where(hit1 | hit2, 1.0, 0.0)
    r_i = lax.broadcasted_iota(jnp.int32, (tm, tm), 0)
    c_i = lax.broadcasted_iota(jnp.int32, (tm, tm), 1)
    below = jnp.where(c_i < r_i, 1.0, 0.0).astype(BF16)
    before = jnp.dot(below, picked.astype(BF16), preferred_element_type=F32) + cnt_ref[...]
    rank1 = jnp.sum(jnp.where(hit1, before, 0.0), axis=-1, keepdims=True)
    rank2 = jnp.sum(jnp.where(hit2, before, 0.0), axis=-1, keepdims=True)
    cnt_ref[...] += jnp.sum(picked, axis=0, keepdims=True)
    vals = (i1, i2, w1, w2, rank1, rank2)
    sel = jnp.zeros_like(logits)
    for li, val in enumerate(vals):
        sel = jnp.where(lane == li, val, sel)
    sel_ref[...] = sel


def _router(h, g, wr_pad, counts_in, tm):
    n = h.shape[0]
    row = lambda i: (i, 0)
    fix = lambda i: (0, 0)
    return pl.pallas_call(
        _router_kernel, grid=(n // tm,),
        in_specs=[pl.BlockSpec((tm, D_MODEL), row), pl.BlockSpec((1, D_MODEL), fix),
                  pl.BlockSpec((D_MODEL, LANES), fix), pl.BlockSpec((1, LANES), fix)],
        out_specs=[pl.BlockSpec((tm, D_MODEL // 2), row), pl.BlockSpec((tm, LANES), row),
                   pl.BlockSpec((1, LANES), fix)],
        out_shape=[jax.ShapeDtypeStruct((n, D_MODEL // 2), jnp.int32),
                   jax.ShapeDtypeStruct((n, LANES), F32),
                   jax.ShapeDtypeStruct((1, LANES), F32)],
        compiler_params=_params(1), name="router",
    )(h, g, wr_pad, counts_in)


def _rope_tables(pos):
    half = HEAD_DIM // 2
    inv = ROPE_THETA ** (-jnp.arange(half, dtype=F32) / half)
    ang = pos.astype(F32)[:, None] * inv[None, :]
    cos, sin = jnp.cos(ang), jnp.sin(ang)
    zero = jnp.zeros_like(sin)
    rep = lambda a, b: jnp.tile(jnp.concatenate([a, b], axis=-1), (1, LANES // HEAD_DIM))
    return rep(cos, cos), rep(-sin, zero), rep(zero, sin)


def _tile_for(n, pref):
    return pref if n % pref == 0 else n


def kernel(x_prompt, x_sample, state_conv, cache_kv_win, norm_mix_e, w_in_e, conv_w, q_gain,
           k_gain, w_out_e, norm_ffn_e, ffn_w_gate, ffn_w_up, ffn_w_down, norm_mix_o, w_uv,
           v_gain, w_s, b_s, w_out_o, norm_moe, w_router, moe_w_gate, moe_w_up, moe_w_down):
    bp, seq, _ = x_prompt.shape
    bs_, s_len, _ = x_sample.shape
    n_p, n_s = bp * seq, bs_ * s_len
    hp = x_prompt.reshape(n_p, D_MODEL)
    hs = x_sample.reshape(n_s, D_MODEL)
    tm_p = 512
    row = lambda a: a.reshape(1, -1)

    tabs_p = _rope_tables(jnp.arange(seq, dtype=jnp.int32))
    tabs_s = tuple(jnp.tile(t, (bs_, 1)) for t in
                   _rope_tables(PAST_LEN + jnp.arange(s_len, dtype=jnp.int32)))
    hd = jnp.arange(256) // HEAD_DIM
    bd = (hd[:, None] == hd[None, :]).astype(BF16)

    w_in16 = w_in_e[0].astype(BF16)
    w_out16 = w_out_e[0].astype(BF16)
    qg = jnp.tile(q_gain[0], 4).reshape(1, 256)
    kg = jnp.tile(k_gain[0], 4).reshape(1, 256)
    g0 = row(norm_mix_e[0])

    dils = tuple(dil for _, dil in BRANCHES)
    lw = min(2048, seq)
    z_p, gb_p, kvt_p, *planes = _inproj(hp, g0, w_in16, tabs_p, qg, kg, bd, tm_p, seq, dils, lw)
    z_s, gb_s, kv_s, q_s = _inproj(hs, g0, w_in16, tabs_s, qg, kg, bd, n_s, n_s, ())

    os_, ls_ = [], []
    for di, dil in enumerate(dils):
        m = seq // dil
        q_d, k_d, v_d = (a.reshape(bp * dil, m, D_B) for a in planes[3 * di:3 * di + 3])
        qb = min(512, m)
        o, lse = _attn_prompt(q_d, k_d, v_d, qb=qb, pb=512 // qb)
        os_.append(o.reshape(bp, dil, m, D_B))
        ls_.append(lse.reshape(bp, dil, m, LANES))
    hp = _outproj_prompt(z_p, gb_p, conv_w[0], os_, ls_, w_out16, hp, tm_p, seq, dils)

    cat_s = _mix_sample(q_s.reshape(bs_, s_len, D_B),
                        kv_s.reshape(bs_, s_len, 2 * D_B),
                        jnp.transpose(cache_kv_win[0], (0, 2, 3, 4, 1)).reshape(
                            bs_, 2, D_B, -1),
                        z_s.reshape(bs_, s_len, D_A), gb_s.reshape(bs_, s_len, D_A),
                        state_conv[0], conv_w[0])
    hs = _linear_res(cat_s.reshape(n_s, D_MODEL), w_out16, hs)

    conv_prompt = z_p.reshape(bp, seq, D_A)[:, seq - 2:][None]
    conv_sample = z_s.reshape(bs_, s_len, D_A)[:, s_len - 2:][None]
    kv_prompt = jnp.transpose(kvt_p.reshape(bp, 2, H_B, HEAD_DIM, lw), (0, 4, 1, 2, 3))[None]
    kv_sample = kv_s.reshape(bs_, s_len, 2, H_B, HEAD_DIM)[None]

    fg, fu, fd = (ffn_w_gate.astype(BF16), ffn_w_up.astype(BF16), ffn_w_down.astype(BF16))
    gf = row(norm_ffn_e[0])
    hp = _ffn_dense(hp, gf, fg, fu, fd, tm=4 * tm_p, fc=256)
    hs = _ffn_dense(hs, gf, fg, fu, fd, tm=n_s, fc=1408)

    wuv16 = w_uv[0].astype(BF16)
    wo16 = w_out_o[0].astype(BF16)
    g1, vg = row(norm_mix_o[0]), row(v_gain[0])
    tail = seq - ((seq - 1) // CHUNK) * CHUNK
    hp, v_tail_p = _gmlp(hp, g1, wuv16, vg, w_s[0], b_s[0].T, wo16, tm=tm_p, period=CHUNK,
                         seq=seq, tail=tail)
    bias_s = jnp.tile(b_s[0][:, :s_len].T, (bs_, 1))
    hs, v_all_s = _gmlp(hs, g1, wuv16, vg, w_s[0], bias_s, wo16, tm=n_s, period=s_len,
                        seq=n_s, tail=n_s)
    v_prompt = v_tail_p[None]
    v_sample = v_all_s.reshape(bs_, s_len, D_MODEL)[None]

    wr_pad = jnp.pad(w_router[0], ((0, 0), (0, LANES - N_EXPERTS)))
    gm = row(norm_moe[0])
    xn_p, sel_p, counts = _router(hp, gm, wr_pad, jnp.zeros((1, LANES), F32), tm_p)
    xn_s, sel_s, counts = _router(hs, gm, wr_pad, counts, n_s)
    n_tok = n_p + n_s
    n_tiles = -(-(2 * n_tok + N_EXPERTS * (MOE_ROWS - 1)) // MOE_ROWS)
    (slots_p, slots_s), tile_expert, tile_valid = _route_plan(
        (sel_p, sel_s), counts, MOE_ROWS, n_tiles)
    slot0 = jnp.concatenate([slots_p[0], slots_s[0]])
    slot1 = jnp.concatenate([slots_p[1], slots_s[1]])
    xs = _sc_dispatch(xn_p, xn_s, slots_p, slots_s, n_tiles * MOE_ROWS)
    ys = _ffn_experts(tile_expert, tile_valid, xs, moe_w_gate[0], moe_w_up[0], moe_w_down[0],
                      tm=MOE_ROWS, fc=512)
    yb = _sc_gather(ys, jnp.concatenate([slot0, slot1]))
    hp = _combine(hp, sel_p, yb, 0, n_s)
    hs = _combine(hs, sel_s, yb, n_p // n_s, n_s)

    return (hp.reshape(bp, seq, D_MODEL), hs.reshape(bs_, s_len, D_MODEL),
            conv_prompt, conv_sample, kv_prompt, kv_sample, v_prompt, v_sample)
```

```python
import functools

import jax
import jax.numpy as jnp
from jax import lax
from jax.experimental import pallas as pl
from jax.experimental.pallas import tpu as pltpu
from jax.experimental.pallas import tpu_sc as plsc

F32 = jnp.float32
BF16 = jnp.bfloat16

D_MODEL = 1024
D_A = 512
H_B = 8
HEAD_DIM = 64
D_B = H_B * HEAD_DIM
BRANCHES = ((128, 1), (512, 4), (2048, 16))
BAND = 128
ROPE_THETA = 10000.0
PAST_LEN = 16384
N_EXPERTS = 8
C_GROUPS = 4
CHUNK = 128
EPS = 1e-6
LANES = 128
VMEM_LIMIT = 56 * 1024 * 1024
SC_CORES = 2
SC_SUBCORES = 16
SC_WORKERS = SC_CORES * SC_SUBCORES
SC_CHUNK = 128

ROWS = 512
ATTN_ROWS = 512
DENSE_ROWS, DENSE_FF = 1024, 1408
MOE_ROWS, MOE_FF = 1024, 512
COMBINE_ROWS = 512


def _params(n_axes):
    return pltpu.CompilerParams(dimension_semantics=("arbitrary",) * n_axes,
                                vmem_limit_bytes=VMEM_LIMIT)


def _rms_bf16(x, g):
    ms = jnp.mean(x * x, axis=-1, keepdims=True)
    return (x * lax.rsqrt(ms + EPS) * g).astype(BF16)


def _inproj_kernel(x_ref, g_ref, w_ref, cos_ref, sina_ref, sinb_ref, qg_ref, kg_ref, bd_ref,
                   z_ref, gb_ref, kv_ref, *rest, dils, win_tiles, tiles_per_seq):
    if dils:
        planes, qkv_sc = rest[:-1], rest[-1]
    else:
        (q_out,) = rest
    xn = _rms_bf16(x_ref[...], g_ref[...])

    def col(j):
        return jnp.dot(xn, w_ref[:, j * 512:(j + 1) * 512], preferred_element_type=F32)

    z_ref[...] = col(1) * col(0)
    gb_ref[...] = col(2).astype(gb_ref.dtype)
    cos, sina, sinb = cos_ref[...], sina_ref[...], sinb_ref[...]

    def head_norm_rope(t, gain):
        outs = []
        for c2 in range(2):
            tc = t[:, c2 * 256:(c2 + 1) * 256]
            ss = jnp.dot((tc * tc).astype(BF16), bd_ref[...], preferred_element_type=F32)
            tn = tc * lax.rsqrt(ss * (1.0 / HEAD_DIM) + EPS) * gain
            for c in range(2):
                u = tn[:, c * LANES:(c + 1) * LANES]
                outs.append(u * cos + pltpu.roll(u, LANES - 32, 1) * sina
                            + pltpu.roll(u, 32, 1) * sinb)
        return outs

    q = [t * (HEAD_DIM ** -0.5) for t in head_norm_rope(col(3), qg_ref[...])]
    k = head_norm_rope(col(4), kg_ref[...])
    v = col(5)
    if not dils:
        for c in range(4):
            kv_ref[:, c * LANES:(c + 1) * LANES] = k[c]
            q_out[:, c * LANES:(c + 1) * LANES] = q[c]
        kv_ref[:, D_B:] = v
        return

    @pl.when(pl.program_id(0) % tiles_per_seq >= tiles_per_seq - win_tiles)
    def _():
        for c in range(4):
            kv_ref[0, c * LANES:(c + 1) * LANES, :] = k[c].T
            kv_ref[0, D_B + c * LANES:D_B + (c + 1) * LANES, :] = v[:, c * LANES:(c + 1) * LANES].T

    tm = x_ref.shape[0]
    for a, val in enumerate((q, k, [v[:, c * LANES:(c + 1) * LANES] for c in range(4)])):
        for c in range(4):
            qkv_sc[0, a, c] = val[c]
            planes[a][0, 0, :, c * LANES:(c + 1) * LANES] = val[c].astype(BF16)
    for di in range(1, len(dils)):
        dil, prev = dils[di], dils[di - 1]
        ratio = dil // prev
        src, dst = qkv_sc.at[(di - 1) % 2], qkv_sc.at[di % 2]
        for r in range(dil):
            start = (r % prev) * (tm // prev) + r // prev
            for a in range(3):
                for c in range(4):
                    rows_ac = src[a, c, pl.ds(start, tm // dil, stride=ratio), :]
                    planes[3 * di + a][0, r, :, c * LANES:(c + 1) * LANES] = rows_ac.astype(BF16)
                    if di + 1 < len(dils):
                        dst[a, c, r * (tm // dil):(r + 1) * (tm // dil), :] = rows_ac


def _inproj(x, g, w16, tabs, qg, kg, bd, tm, seq, dils, win=0):
    n = x.shape[0]
    cos, sina, sinb = tabs
    tt = cos.shape[0] // tm
    tps = seq // tm
    win_tiles = win // tm
    row = lambda i: (i, 0)
    fix = lambda i: (0, 0)
    tab = lambda i: (i % tt, 0)
    out_specs = [pl.BlockSpec((tm, D_A), row), pl.BlockSpec((tm, D_A), row)]
    out_shape = [jax.ShapeDtypeStruct((n, D_A), F32),
                 jax.ShapeDtypeStruct((n, D_A), BF16 if dils else F32)]
    if dils:
        out_specs.append(pl.BlockSpec(
            (1, 2 * D_B, tm),
            lambda i: (i // tps, 0, jnp.maximum(i % tps - (tps - win_tiles), 0))))
        out_shape.append(jax.ShapeDtypeStruct((n // seq, 2 * D_B, win), F32))
        assert dils[0] == 1 and all(b % a == 0 for a, b in zip(dils, dils[1:]))
        scratch = [pltpu.VMEM((2, 3, D_B // LANES, tm, LANES), F32)]
    else:
        out_specs += [pl.BlockSpec((tm, 2 * D_B), row), pl.BlockSpec((tm, D_B), row)]
        out_shape += [jax.ShapeDtypeStruct((n, 2 * D_B), F32),
                      jax.ShapeDtypeStruct((n, D_B), F32)]
        scratch = []
    for dil in dils:
        for _ in range(3):
            out_specs.append(pl.BlockSpec((1, dil, tm // dil, D_B),
                                          lambda i: (i // tps, 0, i % tps, 0)))
            out_shape.append(jax.ShapeDtypeStruct((n // seq, dil, seq // dil, D_B), BF16))
    return pl.pallas_call(
        functools.partial(_inproj_kernel, dils=dils, win_tiles=win_tiles, tiles_per_seq=tps),
        grid=(n // tm,),
        in_specs=[pl.BlockSpec((tm, D_MODEL), row), pl.BlockSpec((1, D_MODEL), fix),
                  pl.BlockSpec(w16.shape, fix),
                  pl.BlockSpec((tm, LANES), tab), pl.BlockSpec((tm, LANES), tab),
                  pl.BlockSpec((tm, LANES), tab),
                  pl.BlockSpec((1, 256), fix), pl.BlockSpec((1, 256), fix),
                  pl.BlockSpec((256, 256), fix)],
        out_specs=out_specs, out_shape=out_shape, scratch_shapes=scratch,
        compiler_params=_params(1), name="inproj",
    )(x, g, w16, cos, sina, sinb, qg, kg, bd)


def _attn_prompt_kernel(q_ref, kc_ref, kp_ref, vc_ref, vp_ref, o_ref, lse_ref, *, qb):
    j = pl.program_id(1)
    nsub = qb // BAND
    lane = lax.broadcasted_iota(jnp.int32, (1, LANES), 1)
    head0 = lane < HEAD_DIM
    qi = lax.broadcasted_iota(jnp.int32, (2 * BAND, 2 * BAND), 0) % BAND + BAND
    kj = lax.broadcasted_iota(jnp.int32, (2 * BAND, 2 * BAND), 1)
    dist = qi - kj
    band = (dist >= 0) & (dist <= BAND)
    first = band & ((kj >= BAND) | (j > 0))
    lanes = [slice(hp * LANES, (hp + 1) * LANES) for hp in range(D_B // LANES)]
    for pi, sub in ((pi, sub) for pi in range(q_ref.shape[0]) for sub in range(nsub)):
        rows = slice(sub * BAND, (sub + 1) * BAND)
        keys = slice((sub - 1) * BAND, (sub + 1) * BAND)
        lse_all = jnp.zeros((BAND, LANES), F32)
        for hp, ls in enumerate(lanes):
            qs = q_ref[pi, rows, ls]
            if sub == 0:
                ks = jnp.concatenate([kp_ref[pi, :, ls], kc_ref[pi, :BAND, ls]], axis=0)
                vs = jnp.concatenate([vp_ref[pi, :, ls], vc_ref[pi, :BAND, ls]], axis=0)
            else:
                ks, vs = kc_ref[pi, keys, ls], vc_ref[pi, keys, ls]
            q2 = jnp.concatenate([jnp.where(head0, qs, jnp.zeros_like(qs)),
                                  jnp.where(head0, jnp.zeros_like(qs), qs)], axis=0)
            s = lax.dot_general(q2, ks, (((1,), (1,)), ((), ())), preferred_element_type=F32)
            s = jnp.where(first if sub == 0 else band, s, -jnp.inf)
            mx = jnp.max(s, axis=-1, keepdims=True)
            p = jnp.exp(s - mx)
            l = jnp.sum(p, axis=-1, keepdims=True)
            pv = jnp.dot(p.astype(BF16), vs, preferred_element_type=F32) / l
            lse = mx + jnp.log(l)
            o_ref[pi, rows, ls] = jnp.where(head0, pv[:BAND], pv[BAND:]).astype(BF16)
            lse_all = jnp.where(lane == 2 * hp, lse[:BAND],
                                jnp.where(lane == 2 * hp + 1, lse[BAND:], lse_all))
        lse_ref[pi, rows, :] = lse_all


def _attn_prompt(q, k, v, qb, pb):
    b, m, w = q.shape
    nsub = qb // BAND
    cur = lambda bi, j: (bi, j, 0)
    prev = lambda bi, j: (bi, jnp.maximum(j * nsub - 1, 0), 0)
    return pl.pallas_call(
        functools.partial(_attn_prompt_kernel, qb=qb),
        grid=(b // pb, m // qb),
        in_specs=[pl.BlockSpec((pb, qb, w), cur), pl.BlockSpec((pb, qb, w), cur),
                  pl.BlockSpec((pb, BAND, w), prev), pl.BlockSpec((pb, qb, w), cur),
                  pl.BlockSpec((pb, BAND, w), prev)],
        out_specs=[pl.BlockSpec((pb, qb, w), cur), pl.BlockSpec((pb, qb, LANES), cur)],
        out_shape=[jax.ShapeDtypeStruct((b, m, w), BF16),
                   jax.ShapeDtypeStruct((b, m, LANES), F32)],
        compiler_params=_params(2), name="attn_prompt",
    )(q, k, k, v, v)


def _outproj_prompt_kernel(z_ref, zh_ref, gb_ref, cw_ref, *rest, tiles_per_batch, dils):
    nb = len(dils)
    o_refs, l_refs = rest[:nb], rest[nb:2 * nb]
    w_ref, h_ref, out_ref = rest[2 * nb:2 * nb + 3]
    o_scs, l_scs, tmp_sc = rest[2 * nb + 3:3 * nb + 3], rest[3 * nb + 3:4 * nb + 3], rest[-1]
    i = pl.program_id(0)
    z = z_ref[...]
    tm = z.shape[0]
    def put_rows(dst, spare, val, r, dil):
        if dil == 1:
            dst[...] = val
        elif dil <= 4:
            dst[pl.ds(r, tm // dil, stride=dil), :] = val
        else:
            spare[pl.ds((r % 4) * (tm // 4) + r // 4, tm // dil, stride=dil // 4), :] = val

    def finish_rows(dst, spare, dil):
        if dil > 4:
            for b in range(4):
                dst[pl.ds(b, tm // 4, stride=4), :] = spare[b * (tm // 4):(b + 1) * (tm // 4), :]

    for dil, o_ref, l_ref, o_sc, l_sc in zip(dils, o_refs, l_refs, o_scs, l_scs):
        assert dil <= 16
        for r in range(dil):
            for c in range(D_B // LANES):
                ls = slice(c * LANES, (c + 1) * LANES)
                put_rows(o_sc.at[c], tmp_sc.at[c], o_ref[0, r, :, ls].astype(F32), r, dil)
            put_rows(l_sc, tmp_sc.at[D_B // LANES], l_ref[0, r], r, dil)
        for c in range(D_B // LANES):
            finish_rows(o_sc.at[c], tmp_sc.at[c], dil)
        finish_rows(l_sc, tmp_sc.at[D_B // LANES], dil)
    unstage = lambda sc: jnp.concatenate([sc[c] for c in range(D_B // LANES)], axis=1)
    zh = jnp.where(i % tiles_per_batch == 0, 0.0, zh_ref[...])
    row = lax.broadcasted_iota(jnp.int32, (tm, 1), 0)
    zm1 = jnp.where(row >= 1, pltpu.roll(z, 1, 0), zh[7:8])
    zm2 = jnp.where(row >= 2, pltpu.roll(z, 2, 0), jnp.where(row == 1, zh[7:8], zh[6:7]))
    cw = cw_ref[...]
    a_out = gb_ref[...].astype(F32) * (cw[0:1] * zm2 + cw[1:2] * zm1 + cw[2:3] * z)
    lses = [l_sc[...] for l_sc in l_scs]
    lm = functools.reduce(jnp.maximum, lses)
    es = [jnp.exp(l - lm) for l in lses]
    tot = sum(es)
    spread = (lax.broadcasted_iota(jnp.int32, (LANES, D_B), 1) // HEAD_DIM
              == lax.broadcasted_iota(jnp.int32, (LANES, D_B), 0)).astype(BF16)

    def widen(wgt):
        hi = wgt.astype(BF16)
        lo = (wgt - hi.astype(F32)).astype(BF16)
        return (jnp.dot(hi, spread, preferred_element_type=F32)
                + jnp.dot(lo, spread, preferred_element_type=F32))

    o = sum(widen(e / tot) * unstage(o_sc) for e, o_sc in zip(es, o_scs))
    y = jnp.dot(a_out.astype(BF16), w_ref[:D_A, :], preferred_element_type=F32)
    y = y + jnp.dot(o.astype(BF16), w_ref[D_A:, :], preferred_element_type=F32)
    out_ref[...] = h_ref[...] + y


def _outproj_prompt(z, gb, cw, os_, ls_, w16, h, tm, seq, dils):
    n = z.shape[0]
    tps = seq // tm
    row = lambda i: (i, 0)
    fix = lambda i: (0, 0)
    halo = lambda i: (jnp.maximum(i * (tm // 8) - 1, 0), 0)
    half = pl.BlockSpec((tm, D_A), row)
    plane = lambda width: [pl.BlockSpec((1, dil, tm // dil, width),
                                        lambda i: (i // tps, 0, i % tps, 0)) for dil in dils]
    return pl.pallas_call(
        functools.partial(_outproj_prompt_kernel, tiles_per_batch=tps, dils=dils),
        grid=(n // tm,),
        in_specs=[half, pl.BlockSpec((8, D_A), halo), half, pl.BlockSpec((3, D_A), fix),
                  *plane(D_B), *plane(LANES),
                  pl.BlockSpec((D_MODEL, D_MODEL), fix), pl.BlockSpec((tm, D_MODEL), row)],
        out_specs=pl.BlockSpec((tm, D_MODEL), row),
        out_shape=jax.ShapeDtypeStruct((n, D_MODEL), F32),
        scratch_shapes=([pltpu.VMEM((D_B // LANES, tm, LANES), F32)] * len(dils)
                        + [pltpu.VMEM((tm, LANES), F32)] * len(dils)
                        + [pltpu.VMEM((D_B // LANES + 1, tm, LANES), F32)]),
        compiler_params=_params(1), name="outproj_prompt",
    )(z, z, gb, cw, *os_, *ls_, w16, h)


def _mix_sample_kernel(q_ref, kvn_ref, win_ref, z_ref, gb_ref, st_ref, cw_ref, cat_ref):
    s_len = q_ref.shape[1]
    rows = H_B * s_len
    npad = LANES
    q = q_ref[0]
    r_i = lax.broadcasted_iota(jnp.int32, (rows, D_B), 0)
    c_i = lax.broadcasted_iota(jnp.int32, (rows, D_B), 1)
    qt = jnp.concatenate([q] * H_B, axis=0)
    qbd = jnp.where(r_i // s_len == c_i // HEAD_DIM, qt, 0.0).astype(BF16)
    kvn = jnp.concatenate([kvn_ref[0], jnp.zeros((npad - s_len, 2 * D_B), F32)], axis=0)

    def multiplicity(delta):
        cnt = jnp.zeros(delta.shape, F32)
        for _, dil in BRANCHES:
            ok = (delta >= 0) & (delta <= BAND * dil) & ((delta & (dil - 1)) == 0)
            cnt = cnt + jnp.where(ok, 1.0, 0.0)
        return cnt

    lb = win_ref.shape[3]
    nt = (((1,), (1,)), ((), ()))
    tok_c = lax.broadcasted_iota(jnp.int32, (rows, lb), 0) % s_len
    cnt_c = multiplicity(tok_c + lb - lax.broadcasted_iota(jnp.int32, (rows, lb), 1))
    tok_n = lax.broadcasted_iota(jnp.int32, (rows, npad), 0) % s_len
    cnt_n = multiplicity(tok_n - lax.broadcasted_iota(jnp.int32, (rows, npad), 1))
    sc = jnp.dot(qbd, win_ref[0, 0].astype(BF16), preferred_element_type=F32)
    sn = lax.dot_general(qbd, kvn[:, :D_B].astype(BF16), nt, preferred_element_type=F32)
    sc = jnp.where(cnt_c > 0, sc, -jnp.inf)
    sn = jnp.where(cnt_n > 0, sn, -jnp.inf)
    mx = jnp.maximum(jnp.max(sc, axis=-1, keepdims=True), jnp.max(sn, axis=-1, keepdims=True))
    pc = cnt_c * jnp.exp(sc - mx)
    pn = cnt_n * jnp.exp(sn - mx)
    den = jnp.sum(pc, axis=-1, keepdims=True) + jnp.sum(pn, axis=-1, keepdims=True)
    of = lax.dot_general(pc.astype(BF16), win_ref[0, 1].astype(BF16), nt,
                         preferred_element_type=F32)
    of = (of + jnp.dot(pn.astype(BF16), kvn[:, D_B:].astype(BF16),
                       preferred_element_type=F32)) / den
    lane_head = lax.broadcasted_iota(jnp.int32, (s_len, D_B), 1) // HEAD_DIM
    o = jnp.zeros((s_len, D_B), F32)
    for h in range(H_B):
        o = o + jnp.where(lane_head == h, of[h * s_len:(h + 1) * s_len], 0.0)
    z = z_ref[0]
    st = st_ref[0]
    t = lax.broadcasted_iota(jnp.int32, (s_len, 1), 0)
    zm1 = jnp.where(t >= 1, pltpu.roll(z, 1, 0), st[1:2])
    zm2 = jnp.where(t >= 2, pltpu.roll(z, 2, 0), jnp.where(t == 1, st[1:2], st[0:1]))
    cw = cw_ref[...]
    cat_ref[0, :, :D_A] = gb_ref[0] * (cw[0:1] * zm2 + cw[1:2] * zm1 + cw[2:3] * z)
    cat_ref[0, :, D_A:] = o


def _mix_sample(q, kvn, win, z, gb, st, cw):
    b, s_len, _ = q.shape
    per = lambda i: (i,) + (0,) * 2
    blk = lambda a: pl.BlockSpec((1,) + a.shape[1:], per)
    return pl.pallas_call(
        _mix_sample_kernel,
        grid=(b,),
        in_specs=[blk(q), blk(kvn),
                  pl.BlockSpec((1,) + win.shape[1:], lambda i: (i, 0, 0, 0)),
                  blk(z), blk(gb), blk(st), pl.BlockSpec((3, D_A), lambda i: (0, 0))],
        out_specs=pl.BlockSpec((1, s_len, D_MODEL), per),
        out_shape=jax.ShapeDtypeStruct((b, s_len, D_MODEL), F32),
        compiler_params=_params(1), name="mix_sample",
    )(q, kvn, win, z, gb, st, cw)


def _linear_res_kernel(x_ref, w_ref, h_ref, out_ref):
    out_ref[...] = h_ref[...] + jnp.dot(x_ref[...].astype(BF16), w_ref[...],
                                        preferred_element_type=F32)


def _linear_res(x, w16, h):
    n, k = x.shape
    full = lambda a: pl.BlockSpec(a.shape, lambda i: (0, 0))
    return pl.pallas_call(
        _linear_res_kernel, grid=(1,),
        in_specs=[full(x), full(w16), full(h)], out_specs=full(h),
        out_shape=jax.ShapeDtypeStruct(h.shape, F32),
        compiler_params=_params(1), name="linear_res",
    )(x, w16, h)


def _swiglu_step(xn, wg_ref, wu_ref, wd_ref, acc_sc):
    a = jnp.dot(xn, wg_ref[0], preferred_element_type=F32)
    b = jnp.dot(xn, wu_ref[0], preferred_element_type=F32)
    t = (jax.nn.silu(a) * b).astype(BF16)
    acc_sc[...] += jnp.dot(t, wd_ref[0], preferred_element_type=F32)


def _ffn_dense_kernel(x_ref, g_ref, wg_ref, wu_ref, wd_ref, out_ref, xn_sc, acc_sc):
    f = pl.program_id(1)

    @pl.when(f == 0)
    def _():
        xn_sc[...] = _rms_bf16(x_ref[...], g_ref[...])
        acc_sc[...] = jnp.zeros_like(acc_sc)

    _swiglu_step(xn_sc[...], wg_ref, wu_ref, wd_ref, acc_sc)

    @pl.when(f == pl.num_programs(1) - 1)
    def _():
        out_ref[...] = x_ref[...] + acc_sc[...]


def _ffn_dense(x, g, wg, wu, wd, *, tm, fc):
    n = x.shape[0]
    ff = wg.shape[2]
    row = lambda i, f: (i, 0)
    return pl.pallas_call(
        _ffn_dense_kernel, grid=(n // tm, ff // fc),
        in_specs=[pl.BlockSpec((tm, D_MODEL), row), pl.BlockSpec((1, D_MODEL), lambda i, f: (0, 0)),
                  pl.BlockSpec((1, D_MODEL, fc), lambda i, f: (0, 0, f)),
                  pl.BlockSpec((1, D_MODEL, fc), lambda i, f: (0, 0, f)),
                  pl.BlockSpec((1, fc, D_MODEL), lambda i, f: (0, f, 0))],
        out_specs=pl.BlockSpec((tm, D_MODEL), row),
        out_shape=jax.ShapeDtypeStruct((n, D_MODEL), F32),
        scratch_shapes=[pltpu.VMEM((tm, D_MODEL), BF16), pltpu.VMEM((tm, D_MODEL), F32)],
        compiler_params=_params(2), name="ffn_dense",
    )(x, g, wg, wu, wd)


def _pack_bf16_pairs(x):
    half = x.shape[1] // 2
    lo = lax.bitcast_convert_type(x[:, :half].astype(BF16).astype(F32), jnp.int32)
    hi = lax.bitcast_convert_type(x[:, half:].astype(BF16).astype(F32), jnp.int32)
    return (hi & -65536) | lax.shift_right_logical(lo, 16)


def _unpack_bf16_pairs(p):
    lo = lax.bitcast_convert_type(p << 16, F32)
    hi = lax.bitcast_convert_type(p & -65536, F32)
    return jnp.concatenate([lo, hi], axis=1)


def _ffn_experts_kernel(te_ref, tv_ref, x_ref, wg_ref, wu_ref, wd_ref, out_ref, xn_sc, acc_sc):
    del te_ref
    i, f = pl.program_id(0), pl.program_id(1)
    n_valid = tv_ref[i]

    @pl.when(n_valid > 0)
    def _():
        @pl.when(f == 0)
        def _():
            row = lax.broadcasted_iota(jnp.int32, x_ref.shape, 0)
            xn_sc[...] = _unpack_bf16_pairs(jnp.where(row < n_valid, x_ref[...], 0)).astype(BF16)
            acc_sc[...] = jnp.zeros_like(acc_sc)

        def step(rows):
            xn = xn_sc[:rows]
            a = jnp.dot(xn, wg_ref[0].astype(BF16), preferred_element_type=F32)
            b = jnp.dot(xn, wu_ref[0].astype(BF16), preferred_element_type=F32)
            t = (jax.nn.silu(a) * b).astype(BF16)
            acc_sc[:rows] += jnp.dot(t, wd_ref[0].astype(BF16), preferred_element_type=F32)

        tm = xn_sc.shape[0]
        quarter = tm // 4
        for nq in range(1, 5):
            pl.when((n_valid > (nq - 1) * quarter) & (n_valid <= nq * quarter))(
                functools.partial(step, nq * quarter))

        @pl.when(f == pl.num_programs(1) - 1)
        def _():
            out_ref[...] = _pack_bf16_pairs(acc_sc[...])


def _ffn_experts(tile_expert, tile_valid, x, wg, wu, wd, *, tm, fc):
    n, wpk = x.shape
    ff = wg.shape[2]
    row = lambda i, f, te, tv: (i, 0)
    grid_spec = pltpu.PrefetchScalarGridSpec(
        num_scalar_prefetch=2, grid=(n // tm, ff // fc),
        in_specs=[pl.BlockSpec((tm, wpk), row),
                  pl.BlockSpec((1, D_MODEL, fc), lambda i, f, te, tv: (te[i], 0, f)),
                  pl.BlockSpec((1, D_MODEL, fc), lambda i, f, te, tv: (te[i], 0, f)),
                  pl.BlockSpec((1, fc, D_MODEL), lambda i, f, te, tv: (te[i], f, 0))],
        out_specs=pl.BlockSpec((tm, wpk), row),
        scratch_shapes=[pltpu.VMEM((tm, D_MODEL), BF16), pltpu.VMEM((tm, D_MODEL), F32)])
    return pl.pallas_call(
        _ffn_experts_kernel, grid_spec=grid_spec,
        out_shape=jax.ShapeDtypeStruct((n, wpk), jnp.int32),
        compiler_params=_params(2), name="ffn_experts",
    )(tile_expert, tile_valid, x, wg, wu, wd)


def _sc_worker_id():
    return lax.axis_index("s") * SC_CORES + lax.axis_index("c")


def _sc_dispatch(x_a, x_b, slots_a, slots_b, n_slots):
    w = x_a.shape[1]
    per_a = x_a.shape[0] // SC_WORKERS
    per_b = x_b.shape[0] // SC_WORKERS
    ch = min(SC_CHUNK // 2, per_a)
    n_chunks = per_a // ch
    mesh = plsc.VectorSubcoreMesh(core_axis_name="c", subcore_axis_name="s")
    dma = pltpu.SemaphoreType.DMA

    @functools.partial(
        pl.kernel, mesh=mesh, out_type=jax.ShapeDtypeStruct((n_slots, w), x_a.dtype),
        scratch_types=[pltpu.VMEM((per_a,), jnp.int32), pltpu.VMEM((per_a,), jnp.int32),
                       pltpu.VMEM((ch, w), x_a.dtype), pltpu.VMEM((ch, w), x_a.dtype),
                       pltpu.VMEM((per_b,), jnp.int32), pltpu.VMEM((per_b, w), x_b.dtype),
                       dma, dma, dma, dma, dma, dma])
    def dispatch(xa_hbm, xb_hbm, a0_hbm, a1_hbm, b0_hbm, b1_hbm, out_hbm,
                 idx0_v, idx1_v, rows0_v, rows1_v, idxb_v, rowsb_v,
                 lsem0, lsem1, s0sem0, s0sem1, s1sem0, s1sem1):
        wid = _sc_worker_id()
        base_a = pl.multiple_of(wid * per_a, 8)
        pltpu.sync_copy(a0_hbm.at[pl.ds(base_a, per_a)], idx0_v)
        pltpu.sync_copy(a1_hbm.at[pl.ds(base_a, per_a)], idx1_v)
        rows, lsem = (rows0_v, rows1_v), (lsem0, lsem1)
        ssem = ((s0sem0, s0sem1), (s1sem0, s1sem1))

        def load(c):
            return pltpu.async_copy(xa_hbm.at[pl.ds(base_a + c * ch, ch)], rows[c % 2], lsem[c % 2])

        def scatter(c):
            return [pltpu.async_copy(rows[c % 2], out_hbm.at[idx_v.at[pl.ds(c * ch, ch)]],
                                     ssem[k][c % 2])
                    for k, idx_v in enumerate((idx0_v, idx1_v))]

        loads, scatters = {0: load(0)}, {}
        for c in range(n_chunks):
            loads[c].wait()
            if c + 1 < n_chunks:
                for cp in scatters.pop(c - 1, ()):
                    cp.wait()
                loads[c + 1] = load(c + 1)
            scatters[c] = scatter(c)
        for cps in scatters.values():
            for cp in cps:
                cp.wait()

        base_b = pl.multiple_of(wid * per_b, 8)
        pltpu.sync_copy(xb_hbm.at[pl.ds(base_b, per_b)], rowsb_v)
        for slots_hbm in (b0_hbm, b1_hbm):
            pltpu.sync_copy(slots_hbm.at[pl.ds(base_b, per_b)], idxb_v)
            pltpu.sync_copy(rowsb_v, out_hbm.at[idxb_v])

    return dispatch(x_a, x_b, *slots_a, *slots_b)


def _sc_gather(table, idx):
    _, w = table.shape
    b = idx.shape[0]
    per_w = b // SC_WORKERS
    ch = max(c for c in range(8, SC_CHUNK + 1, 8) if per_w % c == 0)
    n_chunks = per_w // ch
    assert per_w * SC_WORKERS == b
    mesh = plsc.VectorSubcoreMesh(core_axis_name="c", subcore_axis_name="s")
    dma = pltpu.SemaphoreType.DMA

    @functools.partial(
        pl.kernel, mesh=mesh, out_type=jax.ShapeDtypeStruct((b, w), table.dtype),
        scratch_types=[pltpu.VMEM((per_w,), jnp.int32), pltpu.VMEM((ch, w), table.dtype),
                       pltpu.VMEM((ch, w), table.dtype), dma, dma, dma, dma])
    def gather(table_hbm, idx_hbm, out_hbm, idx_v, rows0_v, rows1_v, gsem0, gsem1, wsem0, wsem1):
        wid = _sc_worker_id()
        base = pl.multiple_of(wid * per_w, 8)
        pltpu.sync_copy(idx_hbm.at[pl.ds(base, per_w)], idx_v)
        rows, gsem, wsem = (rows0_v, rows1_v), (gsem0, gsem1), (wsem0, wsem1)

        def fetch(c):
            return pltpu.async_copy(table_hbm.at[idx_v.at[pl.ds(c * ch, ch)]], rows[c % 2],
                                    gsem[c % 2])

        def write(c):
            return pltpu.async_copy(rows[c % 2], out_hbm.at[pl.ds(base + c * ch, ch)], wsem[c % 2])

        fetches, writes = {0: fetch(0)}, {}
        for c in range(n_chunks):
            fetches[c].wait()
            if c + 1 < n_chunks:
                if c >= 1:
                    writes.pop(c - 1).wait()
                fetches[c + 1] = fetch(c + 1)
            writes[c] = write(c)
        for cp in writes.values():
            cp.wait()

    return gather(table, idx)


def _combine_kernel(h_ref, sel_ref, y0_ref, y1_ref, out_ref):
    sel = sel_ref[...]
    out_ref[...] = (h_ref[...] + sel[:, 2:3] * _unpack_bf16_pairs(y0_ref[...])
                    + sel[:, 3:4] * _unpack_bf16_pairs(y1_ref[...]))


def _combine(h, sel, yb, blk_off, tm):
    n = h.shape[0]
    wpk = yb.shape[1]
    half_blocks = yb.shape[0] // 2 // tm
    return pl.pallas_call(
        _combine_kernel, grid=(n // tm,),
        in_specs=[pl.BlockSpec((tm, D_MODEL), lambda i: (i, 0)),
                  pl.BlockSpec((tm, LANES), lambda i: (i, 0)),
                  pl.BlockSpec((tm, wpk), lambda i: (i + blk_off, 0)),
                  pl.BlockSpec((tm, wpk), lambda i: (i + blk_off + half_blocks, 0))],
        out_specs=pl.BlockSpec((tm, D_MODEL), lambda i: (i, 0)),
        out_shape=jax.ShapeDtypeStruct((n, D_MODEL), F32),
        compiler_params=_params(1), name="combine",
    )(h, sel, yb, yb)


def _route_plan(sels, counts, te_rows, n_tiles):
    counts = counts[0, :N_EXPERTS].astype(jnp.int32)
    padded = (counts + te_rows - 1) // te_rows * te_rows
    ends = jnp.cumsum(padded)
    starts = ends - padded
    experts = jnp.arange(N_EXPERTS, dtype=jnp.int32)

    def slots_of(sel, k):
        e = sel[:, k].astype(jnp.int32)
        start = jnp.sum(jnp.where(e[:, None] == experts[None, :], starts[None, :], 0), axis=1)
        return start + sel[:, 4 + k].astype(jnp.int32)

    slots = [(slots_of(sel, 0), slots_of(sel, 1)) for sel in sels]
    tile_start = jnp.arange(n_tiles, dtype=jnp.int32) * te_rows
    tile_expert = jnp.minimum(jnp.sum(tile_start[:, None] >= ends[None, :], axis=1),
                              N_EXPERTS - 1).astype(jnp.int32)
    in_group = tile_start - starts[tile_expert]
    tile_valid = jnp.clip(counts[tile_expert] - in_group, 0, te_rows)
    tile_valid = jnp.where(tile_start < ends[-1], tile_valid, 0).astype(jnp.int32)
    return slots, tile_expert, tile_valid


def _gmlp_kernel(h_ref, g_ref, wuv_ref, vg_ref, mix_ref, bs_ref, wo_ref, out_ref, v_ref, um_sc,
                 *, period):
    h = h_ref[...]
    tm = h.shape[0]
    ln = bs_ref.shape[0]
    gw = D_MODEL // C_GROUPS
    xn = _rms_bf16(h, g_ref[...])
    vv = jax.nn.gelu(jnp.dot(xn, wuv_ref[:, D_MODEL:], preferred_element_type=F32))
    v = vv * lax.rsqrt(jnp.mean(vv * vv, axis=-1, keepdims=True) + EPS) * vg_ref[...]
    v_ref[0] = v[tm - v_ref.shape[1]:]
    vb = v.astype(BF16)
    u = jax.nn.gelu(jnp.dot(xn, wuv_ref[:, :D_MODEL], preferred_element_type=F32))
    r = lax.broadcasted_iota(jnp.int32, (ln, ln), 0)
    c = lax.broadcasted_iota(jnp.int32, (ln, ln), 1)
    causal = (r // period == c // period) & (c % period <= r % period)
    nw = mix_ref.shape[1]
    if ln != nw:
        pick = (lax.broadcasted_iota(jnp.int32, (ln, nw), 0) % period
                == lax.broadcasted_iota(jnp.int32, (ln, nw), 1)).astype(BF16)
        wr = lax.broadcasted_iota(jnp.int32, (nw, nw), 0)
        wc = lax.broadcasted_iota(jnp.int32, (nw, nw), 1)
        corner = (wr < period) & (wc <= wr)
    for gi in range(C_GROUPS):
        if ln == nw:
            wm = jnp.where(causal, mix_ref[gi], 0.0).astype(BF16)
        else:
            rows_w = jnp.dot(pick, jnp.where(corner, mix_ref[gi], 0.0).astype(BF16),
                             preferred_element_type=F32).astype(BF16)
            tiled = lax.dot_general(rows_w, pick, (((1,), (1,)), ((), ())),
                                    preferred_element_type=F32)
            wm = jnp.where(causal, tiled, 0.0).astype(BF16)
        bias = bs_ref[:, gi:gi + 1]
        for ch in range(tm // ln):
            rs = slice(ch * ln, (ch + 1) * ln)
            cs = slice(gi * gw, (gi + 1) * gw)
            mixed = jnp.dot(wm, vb[rs, cs], preferred_element_type=F32) + bias
            um_sc[rs, cs] = (u[rs, cs] * mixed).astype(BF16)
    out_ref[...] = h + jnp.dot(um_sc[...], wo_ref[...], preferred_element_type=F32)


def _gmlp(h, g, wuv16, vg, mix, bs, wo16, *, tm, period, seq, tail):
    n = h.shape[0]
    tps = seq // tm
    row = lambda i: (i, 0)
    full = lambda a: pl.BlockSpec(a.shape, lambda i: (0,) * a.ndim)
    return pl.pallas_call(
        functools.partial(_gmlp_kernel, period=period),
        grid=(n // tm,),
        in_specs=[pl.BlockSpec((tm, D_MODEL), row), full(g), full(wuv16), full(vg), full(mix),
                  full(bs), full(wo16)],
        out_specs=[pl.BlockSpec((tm, D_MODEL), row),
                   pl.BlockSpec((1, tail, D_MODEL), lambda i: (i // tps, 0, 0))],
        out_shape=[jax.ShapeDtypeStruct((n, D_MODEL), F32),
                   jax.ShapeDtypeStruct((n // seq, tail, D_MODEL), F32)],
        scratch_shapes=[pltpu.VMEM((tm, D_MODEL), BF16)],
        compiler_params=_params(1), name="gmlp",
    )(h, g, wuv16, vg, mix, bs, wo16)


def _router_kernel(h_ref, g_ref, wr_ref, cin_ref, xn_ref, sel_ref, cnt_ref):
    @pl.when(pl.program_id(0) == 0)
    def _():
        cnt_ref[...] = cin_ref[...]

    x = h_ref[...]
    xn = x * lax.rsqrt(jnp.mean(x * x, axis=-1, keepdims=True) + EPS) * g_ref[...]
    xn_ref[...] = _pack_bf16_pairs(xn)
    x_hi = xn.astype(BF16)
    x_lo = (xn - x_hi.astype(F32)).astype(BF16)
    w = wr_ref[...]
    w_hi = w.astype(BF16)
    w_lo = (w - w_hi.astype(F32)).astype(BF16)
    logits = (jnp.dot(x_hi, w_hi, preferred_element_type=F32)
              + (jnp.dot(x_lo, w_hi, preferred_element_type=F32)
                 + jnp.dot(x_hi, w_lo, preferred_element_type=F32)))
    lane = lax.broadcasted_iota(jnp.int32, logits.shape, 1)
    lanef = lane.astype(F32)
    logits = jnp.where(lane < N_EXPERTS, logits, -jnp.inf)
    m1 = jnp.max(logits, axis=-1, keepdims=True)
    i1 = jnp.min(jnp.where(logits == m1, lanef, float(LANES)), axis=-1, keepdims=True)
    rest = jnp.where(lanef == i1, -jnp.inf, logits)
    m2 = jnp.max(rest, axis=-1, keepdims=True)
    i2 = jnp.min(jnp.where(rest == m2, lanef, float(LANES)), axis=-1, keepdims=True)
    e2 = jnp.exp(m2 - m1)
    w1 = 1.0 / (1.0 + e2)
    w2 = e2 / (1.0 + e2)
    tm = x.shape[0]
    hit1, hit2 = lanef == i1, lanef == i2
    picked = jnp.where(hit1 | hit2, 1.0, 0.0)
    r_i = lax.broadcasted_iota(jnp.int32, (tm, tm), 0)
    c_i = lax.broadcasted_iota(jnp.int32, (tm, tm), 1)
    below = jnp.where(c_i < r_i, 1.0, 0.0).astype(BF16)
    before = jnp.dot(below, picked.astype(BF16), preferred_element_type=F32) + cnt_ref[...]
    rank1 = jnp.sum(jnp.where(hit1, before, 0.0), axis=-1, keepdims=True)
    rank2 = jnp.sum(jnp.where(hit2, before, 0.0), axis=-1, keepdims=True)
    cnt_ref[...] += jnp.sum(picked, axis=0, keepdims=True)
    vals = (i1, i2, w1, w2, rank1, rank2)
    sel = jnp.zeros_like(logits)
    for li, val in enumerate(vals):
        sel = jnp.where(lane == li, val, sel)
    sel_ref[...] = sel


def _router(h, g, wr_pad, counts_in, tm):
    n = h.shape[0]
    row = lambda i: (i, 0)
    fix = lambda i: (0, 0)
    return pl.pallas_call(
        _router_kernel, grid=(n // tm,),
        in_specs=[pl.BlockSpec((tm, D_MODEL), row), pl.BlockSpec((1, D_MODEL), fix),
                  pl.BlockSpec((D_MODEL, LANES), fix), pl.BlockSpec((1, LANES), fix)],
        out_specs=[pl.BlockSpec((tm, D_MODEL // 2), row), pl.BlockSpec((tm, LANES), row),
                   pl.BlockSpec((1, LANES), fix)],
        out_shape=[jax.ShapeDtypeStruct((n, D_MODEL // 2), jnp.int32),
                   jax.ShapeDtypeStruct((n, LANES), F32),
                   jax.ShapeDtypeStruct((1, LANES), F32)],
        compiler_params=_params(1), name="router",
    )(h, g, wr_pad, counts_in)


def _rope_tables(pos):
    half = HEAD_DIM // 2
    inv = ROPE_THETA ** (-jnp.arange(half, dtype=F32) / half)
    ang = pos.astype(F32)[:, None] * inv[None, :]
    cos, sin = jnp.cos(ang), jnp.sin(ang)
    zero = jnp.zeros_like(sin)
    rep = lambda a, b: jnp.tile(jnp.concatenate([a, b], axis=-1), (1, LANES // HEAD_DIM))
    return rep(cos, cos), rep(-sin, zero), rep(zero, sin)


def kernel(x_prompt, x_sample, state_conv, cache_kv_win, norm_mix_e, w_in_e, conv_w, q_gain,
           k_gain, w_out_e, norm_ffn_e, ffn_w_gate, ffn_w_up, ffn_w_down, norm_mix_o, w_uv,
           v_gain, w_s, b_s, w_out_o, norm_moe, w_router, moe_w_gate, moe_w_up, moe_w_down):
    bp, seq, _ = x_prompt.shape
    bs_, s_len, _ = x_sample.shape
    n_p, n_s = bp * seq, bs_ * s_len
    hp = x_prompt.reshape(n_p, D_MODEL)
    hs = x_sample.reshape(n_s, D_MODEL)
    tm_p = ROWS
    row = lambda a: a.reshape(1, -1)

    tabs_p = _rope_tables(jnp.arange(seq, dtype=jnp.int32))
    tabs_s = tuple(jnp.tile(t, (bs_, 1)) for t in
                   _rope_tables(PAST_LEN + jnp.arange(s_len, dtype=jnp.int32)))
    hd = jnp.arange(256) // HEAD_DIM
    bd = (hd[:, None] == hd[None, :]).astype(BF16)

    w_in16 = w_in_e[0].astype(BF16)
    w_out16 = w_out_e[0].astype(BF16)
    qg = jnp.tile(q_gain[0], 4).reshape(1, 256)
    kg = jnp.tile(k_gain[0], 4).reshape(1, 256)
    g0 = row(norm_mix_e[0])

    dils = tuple(dil for _, dil in BRANCHES)
    lw = min(max(wnd for wnd, _ in BRANCHES), seq)
    z_p, gb_p, kvt_p, *planes = _inproj(hp, g0, w_in16, tabs_p, qg, kg, bd, tm_p, seq, dils, lw)
    z_s, gb_s, kv_s, q_s = _inproj(hs, g0, w_in16, tabs_s, qg, kg, bd, n_s, n_s, ())

    os_, ls_ = [], []
    for di, dil in enumerate(dils):
        m = seq // dil
        q_d, k_d, v_d = (a.reshape(bp * dil, m, D_B) for a in planes[3 * di:3 * di + 3])
        qb = min(ATTN_ROWS, m)
        o, lse = _attn_prompt(q_d, k_d, v_d, qb=qb, pb=ATTN_ROWS // qb)
        os_.append(o.reshape(bp, dil, m, D_B))
        ls_.append(lse.reshape(bp, dil, m, LANES))
    hp = _outproj_prompt(z_p, gb_p, conv_w[0], os_, ls_, w_out16, hp, tm_p, seq, dils)

    cat_s = _mix_sample(q_s.reshape(bs_, s_len, D_B),
                        kv_s.reshape(bs_, s_len, 2 * D_B),
                        jnp.transpose(cache_kv_win[0], (0, 2, 3, 4, 1)).reshape(
                            bs_, 2, D_B, -1),
                        z_s.reshape(bs_, s_len, D_A), gb_s.reshape(bs_, s_len, D_A),
                        state_conv[0], conv_w[0])
    hs = _linear_res(cat_s.reshape(n_s, D_MODEL), w_out16, hs)

    conv_prompt = z_p.reshape(bp, seq, D_A)[:, seq - 2:][None]
    conv_sample = z_s.reshape(bs_, s_len, D_A)[:, s_len - 2:][None]
    kv_prompt = jnp.transpose(kvt_p.reshape(bp, 2, H_B, HEAD_DIM, lw), (0, 4, 1, 2, 3))[None]
    kv_sample = kv_s.reshape(bs_, s_len, 2, H_B, HEAD_DIM)[None]

    fg, fu, fd = (ffn_w_gate.astype(BF16), ffn_w_up.astype(BF16), ffn_w_down.astype(BF16))
    gf = row(norm_ffn_e[0])
    hp = _ffn_dense(hp, gf, fg, fu, fd, tm=DENSE_ROWS, fc=DENSE_FF)
    hs = _ffn_dense(hs, gf, fg, fu, fd, tm=n_s, fc=DENSE_FF)

    wuv16 = w_uv[0].astype(BF16)
    wo16 = w_out_o[0].astype(BF16)
    g1, vg = row(norm_mix_o[0]), row(v_gain[0])
    tail = seq - ((seq - 1) // CHUNK) * CHUNK
    hp, v_tail_p = _gmlp(hp, g1, wuv16, vg, w_s[0], b_s[0].T, wo16, tm=tm_p, period=CHUNK,
                         seq=seq, tail=tail)
    bias_s = jnp.tile(b_s[0][:, :s_len].T, (bs_, 1))
    hs, v_all_s = _gmlp(hs, g1, wuv16, vg, w_s[0], bias_s, wo16, tm=n_s, period=s_len,
                        seq=n_s, tail=n_s)
    v_prompt = v_tail_p[None]
    v_sample = v_all_s.reshape(bs_, s_len, D_MODEL)[None]

    wr_pad = jnp.pad(w_router[0], ((0, 0), (0, LANES - N_EXPERTS)))
    gm = row(norm_moe[0])
    xn_p, sel_p, counts = _router(hp, gm, wr_pad, jnp.zeros((1, LANES), F32), tm_p)
    xn_s, sel_s, counts = _router(hs, gm, wr_pad, counts, n_s)
    n_tok = n_p + n_s
    n_tiles = -(-(2 * n_tok + N_EXPERTS * (MOE_ROWS - 1)) // MOE_ROWS)
    (slots_p, slots_s), tile_expert, tile_valid = _route_plan(
        (sel_p, sel_s), counts, MOE_ROWS, n_tiles)
    xs = _sc_dispatch(xn_p, xn_s, slots_p, slots_s, n_tiles * MOE_ROWS)
    ys = _ffn_experts(tile_expert, tile_valid, xs, moe_w_gate[0], moe_w_up[0], moe_w_down[0],
                      tm=MOE_ROWS, fc=MOE_FF)
    fill = jnp.zeros((-n_tok % COMBINE_ROWS,), jnp.int32)
    yb = _sc_gather(ys, jnp.concatenate([slots_p[0], slots_s[0], fill,
                                         slots_p[1], slots_s[1], fill]))
    hp = _combine(hp, sel_p, yb, 0, COMBINE_ROWS)
    hs = _combine(hs, sel_s, yb, n_p // n_s, n_s)

    return (hp.reshape(bp, seq, D_MODEL), hs.reshape(bs_, s_len, D_MODEL),
            conv_prompt, conv_sample, kv_prompt, kv_sample, v_prompt, v_sample)
```

```python
import functools

import jax
import jax.numpy as jnp
from jax import lax
from jax.experimental import pallas as pl
from jax.experimental.pallas import tpu as pltpu
from jax.experimental.pallas import tpu_sc as plsc

F32 = jnp.float32
BF16 = jnp.bfloat16

D_MODEL = 1024
D_A = 512
H_B = 8
HEAD_DIM = 64
D_B = H_B * HEAD_DIM
BRANCHES = ((128, 1), (512, 4), (2048, 16))
BAND = 128
ROPE_THETA = 10000.0
PAST_LEN = 16384
N_EXPERTS = 8
C_GROUPS = 4
CHUNK = 128
EPS = 1e-6
LANES = 128
VMEM_LIMIT = 56 * 1024 * 1024
SC_CORES = 2
SC_SUBCORES = 16
SC_WORKERS = SC_CORES * SC_SUBCORES
SC_CHUNK = 128

ROWS = 512
ATTN_ROWS = 512
DENSE_ROWS, DENSE_FF = 1024, 1408
MOE_ROWS, MOE_FF = 1024, 512
COMBINE_ROWS = 512


def _params(n_axes):
    return pltpu.CompilerParams(dimension_semantics=("arbitrary",) * n_axes,
                                vmem_limit_bytes=VMEM_LIMIT)


def _rms_bf16(x, g):
    ms = jnp.mean(x * x, axis=-1, keepdims=True)
    return (x * lax.rsqrt(ms + EPS) * g).astype(BF16)


def _inproj_kernel(x_ref, g_ref, w_ref, cos_ref, sina_ref, sinb_ref, qg_ref, kg_ref, bd_ref,
                   z_ref, gb_ref, kv_ref, *rest, dils, win_tiles, tiles_per_seq):
    if dils:
        planes, qkv_sc = rest[:-1], rest[-1]
    else:
        (q_out,) = rest
    xn = _rms_bf16(x_ref[...], g_ref[...])

    def col(j):
        return jnp.dot(xn, w_ref[:, j * 512:(j + 1) * 512], preferred_element_type=F32)

    z_ref[...] = col(1) * col(0)
    gb_ref[...] = col(2).astype(gb_ref.dtype)
    cos, sina, sinb = cos_ref[...], sina_ref[...], sinb_ref[...]

    def head_norm_rope(t, gain):
        outs = []
        for c2 in range(2):
            tc = t[:, c2 * 256:(c2 + 1) * 256]
            ss = jnp.dot((tc * tc).astype(BF16), bd_ref[...], preferred_element_type=F32)
            tn = tc * lax.rsqrt(ss * (1.0 / HEAD_DIM) + EPS) * gain
            for c in range(2):
                u = tn[:, c * LANES:(c + 1) * LANES]
                outs.append(u * cos + pltpu.roll(u, LANES - 32, 1) * sina
                            + pltpu.roll(u, 32, 1) * sinb)
        return outs

    q = [t * (HEAD_DIM ** -0.5) for t in head_norm_rope(col(3), qg_ref[...])]
    k = head_norm_rope(col(4), kg_ref[...])
    v = col(5)
    if not dils:
        for c in range(4):
            kv_ref[:, c * LANES:(c + 1) * LANES] = k[c]
            q_out[:, c * LANES:(c + 1) * LANES] = q[c]
        kv_ref[:, D_B:] = v
        return

    @pl.when(pl.program_id(0) % tiles_per_seq >= tiles_per_seq - win_tiles)
    def _():
        for c in range(4):
            kv_ref[0, c * LANES:(c + 1) * LANES, :] = k[c].T
            kv_ref[0, D_B + c * LANES:D_B + (c + 1) * LANES, :] = v[:, c * LANES:(c + 1) * LANES].T

    tm = x_ref.shape[0]
    for a, val in enumerate((q, k, [v[:, c * LANES:(c + 1) * LANES] for c in range(4)])):
        for c in range(4):
            qkv_sc[0, a, c] = val[c]
            planes[a][0, 0, :, c * LANES:(c + 1) * LANES] = val[c].astype(BF16)
    for di in range(1, len(dils)):
        dil, prev = dils[di], dils[di - 1]
        ratio = dil // prev
        src, dst = qkv_sc.at[(di - 1) % 2], qkv_sc.at[di % 2]
        for r in range(dil):
            start = (r % prev) * (tm // prev) + r // prev
            for a in range(3):
                for c in range(4):
                    rows_ac = src[a, c, pl.ds(start, tm // dil, stride=ratio), :]
                    planes[3 * di + a][0, r, :, c * LANES:(c + 1) * LANES] = rows_ac.astype(BF16)
                    if di + 1 < len(dils):
                        dst[a, c, r * (tm // dil):(r + 1) * (tm // dil), :] = rows_ac


def _inproj(x, g, w16, tabs, qg, kg, bd, tm, seq, dils, win=0):
    n = x.shape[0]
    cos, sina, sinb = tabs
    tt = cos.shape[0] // tm
    tps = seq // tm
    win_tiles = win // tm
    row = lambda i: (i, 0)
    fix = lambda i: (0, 0)
    tab = lambda i: (i % tt, 0)
    out_specs = [pl.BlockSpec((tm, D_A), row), pl.BlockSpec((tm, D_A), row)]
    out_shape = [jax.ShapeDtypeStruct((n, D_A), F32),
                 jax.ShapeDtypeStruct((n, D_A), BF16 if dils else F32)]
    if dils:
        out_specs.append(pl.BlockSpec(
            (1, 2 * D_B, tm),
            lambda i: (i // tps, 0, jnp.maximum(i % tps - (tps - win_tiles), 0))))
        out_shape.append(jax.ShapeDtypeStruct((n // seq, 2 * D_B, win), F32))
        assert dils[0] == 1 and all(b % a == 0 for a, b in zip(dils, dils[1:]))
        scratch = [pltpu.VMEM((2, 3, D_B // LANES, tm, LANES), F32)]
    else:
        out_specs += [pl.BlockSpec((tm, 2 * D_B), row), pl.BlockSpec((tm, D_B), row)]
        out_shape += [jax.ShapeDtypeStruct((n, 2 * D_B), F32),
                      jax.ShapeDtypeStruct((n, D_B), F32)]
        scratch = []
    for dil in dils:
        for _ in range(3):
            out_specs.append(pl.BlockSpec((1, dil, tm // dil, D_B),
                                          lambda i: (i // tps, 0, i % tps, 0)))
            out_shape.append(jax.ShapeDtypeStruct((n // seq, dil, seq // dil, D_B), BF16))
    return pl.pallas_call(
        functools.partial(_inproj_kernel, dils=dils, win_tiles=win_tiles, tiles_per_seq=tps),
        grid=(n // tm,),
        in_specs=[pl.BlockSpec((tm, D_MODEL), row), pl.BlockSpec((1, D_MODEL), fix),
                  pl.BlockSpec(w16.shape, fix),
                  pl.BlockSpec((tm, LANES), tab), pl.BlockSpec((tm, LANES), tab),
                  pl.BlockSpec((tm, LANES), tab),
                  pl.BlockSpec((1, 256), fix), pl.BlockSpec((1, 256), fix),
                  pl.BlockSpec((256, 256), fix)],
        out_specs=out_specs, out_shape=out_shape, scratch_shapes=scratch,
        compiler_params=_params(1), name="inproj",
    )(x, g, w16, cos, sina, sinb, qg, kg, bd)


def _attn_prompt_kernel(q_ref, kc_ref, kp_ref, vc_ref, vp_ref, o_ref, lse_ref, *, qb):
    j = pl.program_id(1)
    nsub = qb // BAND
    lane = lax.broadcasted_iota(jnp.int32, (1, LANES), 1)
    head0 = lane < HEAD_DIM
    qi = lax.broadcasted_iota(jnp.int32, (2 * BAND, 2 * BAND), 0) % BAND + BAND
    kj = lax.broadcasted_iota(jnp.int32, (2 * BAND, 2 * BAND), 1)
    dist = qi - kj
    band = (dist >= 0) & (dist <= BAND)
    first = band & ((kj >= BAND) | (j > 0))
    lanes = [slice(hp * LANES, (hp + 1) * LANES) for hp in range(D_B // LANES)]
    for pi, sub in ((pi, sub) for pi in range(q_ref.shape[0]) for sub in range(nsub)):
        rows = slice(sub * BAND, (sub + 1) * BAND)
        keys = slice((sub - 1) * BAND, (sub + 1) * BAND)
        lse_all = jnp.zeros((BAND, LANES), F32)
        for hp, ls in enumerate(lanes):
            qs = q_ref[pi, rows, ls]
            if sub == 0:
                ks = jnp.concatenate([kp_ref[pi, :, ls], kc_ref[pi, :BAND, ls]], axis=0)
                vs = jnp.concatenate([vp_ref[pi, :, ls], vc_ref[pi, :BAND, ls]], axis=0)
            else:
                ks, vs = kc_ref[pi, keys, ls], vc_ref[pi, keys, ls]
            q2 = jnp.concatenate([jnp.where(head0, qs, jnp.zeros_like(qs)),
                                  jnp.where(head0, jnp.zeros_like(qs), qs)], axis=0)
            s = lax.dot_general(q2, ks, (((1,), (1,)), ((), ())), preferred_element_type=F32)
            s = jnp.where(first if sub == 0 else band, s, -jnp.inf)
            mx = jnp.max(s, axis=-1, keepdims=True)
            p = jnp.exp(s - mx)
            l = jnp.sum(p, axis=-1, keepdims=True)
            pv = jnp.dot(p.astype(BF16), vs, preferred_element_type=F32) / l
            lse = mx + jnp.log(l)
            o_ref[pi, rows, ls] = jnp.where(head0, pv[:BAND], pv[BAND:]).astype(BF16)
            lse_all = jnp.where(lane == 2 * hp, lse[:BAND],
                                jnp.where(lane == 2 * hp + 1, lse[BAND:], lse_all))
        lse_ref[pi, rows, :] = lse_all


def _attn_prompt(q, k, v, qb, pb):
    b, m, w = q.shape
    nsub = qb // BAND
    cur = lambda bi, j: (bi, j, 0)
    prev = lambda bi, j: (bi, jnp.maximum(j * nsub - 1, 0), 0)
    return pl.pallas_call(
        functools.partial(_attn_prompt_kernel, qb=qb),
        grid=(b // pb, m // qb),
        in_specs=[pl.BlockSpec((pb, qb, w), cur), pl.BlockSpec((pb, qb, w), cur),
                  pl.BlockSpec((pb, BAND, w), prev), pl.BlockSpec((pb, qb, w), cur),
                  pl.BlockSpec((pb, BAND, w), prev)],
        out_specs=[pl.BlockSpec((pb, qb, w), cur), pl.BlockSpec((pb, qb, LANES), cur)],
        out_shape=[jax.ShapeDtypeStruct((b, m, w), BF16),
                   jax.ShapeDtypeStruct((b, m, LANES), F32)],
        compiler_params=_params(2), name="attn_prompt",
    )(q, k, k, v, v)


def _outproj_prompt_kernel(z_ref, zh_ref, gb_ref, cw_ref, *rest, tiles_per_batch, dils):
    nb = len(dils)
    o_refs, l_refs = rest[:nb], rest[nb:2 * nb]
    w_ref, h_ref, out_ref = rest[2 * nb:2 * nb + 3]
    o_scs, l_scs, tmp_sc = rest[2 * nb + 3:3 * nb + 3], rest[3 * nb + 3:4 * nb + 3], rest[-1]
    i = pl.program_id(0)
    z = z_ref[...]
    tm = z.shape[0]
    def put_rows(dst, spare, val, r, dil):
        if dil == 1:
            dst[...] = val
        elif dil <= 4:
            dst[pl.ds(r, tm // dil, stride=dil), :] = val
        else:
            spare[pl.ds((r % 4) * (tm // 4) + r // 4, tm // dil, stride=dil // 4), :] = val

    def finish_rows(dst, spare, dil):
        if dil > 4:
            for b in range(4):
                dst[pl.ds(b, tm // 4, stride=4), :] = spare[b * (tm // 4):(b + 1) * (tm // 4), :]

    for dil, o_ref, l_ref, o_sc, l_sc in zip(dils, o_refs, l_refs, o_scs, l_scs):
        assert dil <= 16
        for r in range(dil):
            for c in range(D_B // LANES):
                ls = slice(c * LANES, (c + 1) * LANES)
                put_rows(o_sc.at[c], tmp_sc.at[c], o_ref[0, r, :, ls].astype(F32), r, dil)
            put_rows(l_sc, tmp_sc.at[D_B // LANES], l_ref[0, r], r, dil)
        for c in range(D_B // LANES):
            finish_rows(o_sc.at[c], tmp_sc.at[c], dil)
        finish_rows(l_sc, tmp_sc.at[D_B // LANES], dil)
    unstage = lambda sc: jnp.concatenate([sc[c] for c in range(D_B // LANES)], axis=1)
    zh = jnp.where(i % tiles_per_batch == 0, 0.0, zh_ref[...])
    row = lax.broadcasted_iota(jnp.int32, (tm, 1), 0)
    zm1 = jnp.where(row >= 1, pltpu.roll(z, 1, 0), zh[7:8])
    zm2 = jnp.where(row >= 2, pltpu.roll(z, 2, 0), jnp.where(row == 1, zh[7:8], zh[6:7]))
    cw = cw_ref[...]
    a_out = gb_ref[...].astype(F32) * (cw[0:1] * zm2 + cw[1:2] * zm1 + cw[2:3] * z)
    lses = [l_sc[...] for l_sc in l_scs]
    lm = functools.reduce(jnp.maximum, lses)
    es = [jnp.exp(l - lm) for l in lses]
    tot = sum(es)
    spread = (lax.broadcasted_iota(jnp.int32, (LANES, D_B), 1) // HEAD_DIM
              == lax.broadcasted_iota(jnp.int32, (LANES, D_B), 0)).astype(BF16)

    def widen(wgt):
        hi = wgt.astype(BF16)
        lo = (wgt - hi.astype(F32)).astype(BF16)
        return (jnp.dot(hi, spread, preferred_element_type=F32)
                + jnp.dot(lo, spread, preferred_element_type=F32))

    o = sum(widen(e / tot) * unstage(o_sc) for e, o_sc in zip(es, o_scs))
    y = jnp.dot(a_out.astype(BF16), w_ref[:D_A, :], preferred_element_type=F32)
    y = y + jnp.dot(o.astype(BF16), w_ref[D_A:, :], preferred_element_type=F32)
    out_ref[...] = h_ref[...] + y


def _outproj_prompt(z, gb, cw, os_, ls_, w16, h, tm, seq, dils):
    n = z.shape[0]
    tps = seq // tm
    row = lambda i: (i, 0)
    fix = lambda i: (0, 0)
    halo = lambda i: (jnp.maximum(i * (tm // 8) - 1, 0), 0)
    half = pl.BlockSpec((tm, D_A), row)
    plane = lambda width: [pl.BlockSpec((1, dil, tm // dil, width),
                                        lambda i: (i // tps, 0, i % tps, 0)) for dil in dils]
    return pl.pallas_call(
        functools.partial(_outproj_prompt_kernel, tiles_per_batch=tps, dils=dils),
        grid=(n // tm,),
        in_specs=[half, pl.BlockSpec((8, D_A), halo), half, pl.BlockSpec((3, D_A), fix),
                  *plane(D_B), *plane(LANES),
                  pl.BlockSpec((D_MODEL, D_MODEL), fix), pl.BlockSpec((tm, D_MODEL), row)],
        out_specs=pl.BlockSpec((tm, D_MODEL), row),
        out_shape=jax.ShapeDtypeStruct((n, D_MODEL), F32),
        scratch_shapes=([pltpu.VMEM((D_B // LANES, tm, LANES), F32)] * len(dils)
                        + [pltpu.VMEM((tm, LANES), F32)] * len(dils)
                        + [pltpu.VMEM((D_B // LANES + 1, tm, LANES), F32)]),
        compiler_params=_params(1), name="outproj_prompt",
    )(z, z, gb, cw, *os_, *ls_, w16, h)


def _mix_sample_kernel(q_ref, kvn_ref, win_ref, z_ref, gb_ref, st_ref, cw_ref, cat_ref):
    s_len = q_ref.shape[1]
    rows = H_B * s_len
    npad = LANES
    q = q_ref[0]
    r_i = lax.broadcasted_iota(jnp.int32, (rows, D_B), 0)
    c_i = lax.broadcasted_iota(jnp.int32, (rows, D_B), 1)
    qt = jnp.concatenate([q] * H_B, axis=0)
    qbd = jnp.where(r_i // s_len == c_i // HEAD_DIM, qt, 0.0).astype(BF16)
    kvn = jnp.concatenate([kvn_ref[0], jnp.zeros((npad - s_len, 2 * D_B), F32)], axis=0)

    def multiplicity(delta):
        cnt = jnp.zeros(delta.shape, F32)
        for _, dil in BRANCHES:
            ok = (delta >= 0) & (delta <= BAND * dil) & ((delta & (dil - 1)) == 0)
            cnt = cnt + jnp.where(ok, 1.0, 0.0)
        return cnt

    lb = win_ref.shape[3]
    nt = (((1,), (1,)), ((), ()))
    tok_c = lax.broadcasted_iota(jnp.int32, (rows, lb), 0) % s_len
    cnt_c = multiplicity(tok_c + lb - lax.broadcasted_iota(jnp.int32, (rows, lb), 1))
    tok_n = lax.broadcasted_iota(jnp.int32, (rows, npad), 0) % s_len
    cnt_n = multiplicity(tok_n - lax.broadcasted_iota(jnp.int32, (rows, npad), 1))
    sc = jnp.dot(qbd, win_ref[0, 0].astype(BF16), preferred_element_type=F32)
    sn = lax.dot_general(qbd, kvn[:, :D_B].astype(BF16), nt, preferred_element_type=F32)
    sc = jnp.where(cnt_c > 0, sc, -jnp.inf)
    sn = jnp.where(cnt_n > 0, sn, -jnp.inf)
    mx = jnp.maximum(jnp.max(sc, axis=-1, keepdims=True), jnp.max(sn, axis=-1, keepdims=True))
    pc = cnt_c * jnp.exp(sc - mx)
    pn = cnt_n * jnp.exp(sn - mx)
    den = jnp.sum(pc, axis=-1, keepdims=True) + jnp.sum(pn, axis=-1, keepdims=True)
    of = lax.dot_general(pc.astype(BF16), win_ref[0, 1].astype(BF16), nt,
                         preferred_element_type=F32)
    of = (of + jnp.dot(pn.astype(BF16), kvn[:, D_B:].astype(BF16),
                       preferred_element_type=F32)) / den
    lane_head = lax.broadcasted_iota(jnp.int32, (s_len, D_B), 1) // HEAD_DIM
    o = jnp.zeros((s_len, D_B), F32)
    for h in range(H_B):
        o = o + jnp.where(lane_head == h, of[h * s_len:(h + 1) * s_len], 0.0)
    z = z_ref[0]
    st = st_ref[0]
    t = lax.broadcasted_iota(jnp.int32, (s_len, 1), 0)
    zm1 = jnp.where(t >= 1, pltpu.roll(z, 1, 0), st[1:2])
    zm2 = jnp.where(t >= 2, pltpu.roll(z, 2, 0), jnp.where(t == 1, st[1:2], st[0:1]))
    cw = cw_ref[...]
    cat_ref[0, :, :D_A] = gb_ref[0] * (cw[0:1] * zm2 + cw[1:2] * zm1 + cw[2:3] * z)
    cat_ref[0, :, D_A:] = o


def _mix_sample(q, kvn, win, z, gb, st, cw):
    b, s_len, _ = q.shape
    per = lambda i: (i,) + (0,) * 2
    blk = lambda a: pl.BlockSpec((1,) + a.shape[1:], per)
    return pl.pallas_call(
        _mix_sample_kernel,
        grid=(b,),
        in_specs=[blk(q), blk(kvn),
                  pl.BlockSpec((1,) + win.shape[1:], lambda i: (i, 0, 0, 0)),
                  blk(z), blk(gb), blk(st), pl.BlockSpec((3, D_A), lambda i: (0, 0))],
        out_specs=pl.BlockSpec((1, s_len, D_MODEL), per),
        out_shape=jax.ShapeDtypeStruct((b, s_len, D_MODEL), F32),
        compiler_params=_params(1), name="mix_sample",
    )(q, kvn, win, z, gb, st, cw)


def _linear_res_kernel(x_ref, w_ref, h_ref, out_ref):
    out_ref[...] = h_ref[...] + jnp.dot(x_ref[...].astype(BF16), w_ref[...],
                                        preferred_element_type=F32)


def _linear_res(x, w16, h):
    n, k = x.shape
    full = lambda a: pl.BlockSpec(a.shape, lambda i: (0, 0))
    return pl.pallas_call(
        _linear_res_kernel, grid=(1,),
        in_specs=[full(x), full(w16), full(h)], out_specs=full(h),
        out_shape=jax.ShapeDtypeStruct(h.shape, F32),
        compiler_params=_params(1), name="linear_res",
    )(x, w16, h)


def _swiglu_step(xn, wg_ref, wu_ref, wd_ref, acc_sc):
    a = jnp.dot(xn, wg_ref[0], preferred_element_type=F32)
    b = jnp.dot(xn, wu_ref[0], preferred_element_type=F32)
    t = (jax.nn.silu(a) * b).astype(BF16)
    acc_sc[...] += jnp.dot(t, wd_ref[0], preferred_element_type=F32)


def _ffn_dense_kernel(x_ref, g_ref, wg_ref, wu_ref, wd_ref, out_ref, xn_sc, acc_sc):
    f = pl.program_id(1)

    @pl.when(f == 0)
    def _():
        xn_sc[...] = _rms_bf16(x_ref[...], g_ref[...])
        acc_sc[...] = jnp.zeros_like(acc_sc)

    _swiglu_step(xn_sc[...], wg_ref, wu_ref, wd_ref, acc_sc)

    @pl.when(f == pl.num_programs(1) - 1)
    def _():
        out_ref[...] = x_ref[...] + acc_sc[...]


def _ffn_dense(x, g, wg, wu, wd, *, tm, fc):
    n = x.shape[0]
    ff = wg.shape[2]
    row = lambda i, f: (i, 0)
    return pl.pallas_call(
        _ffn_dense_kernel, grid=(n // tm, ff // fc),
        in_specs=[pl.BlockSpec((tm, D_MODEL), row), pl.BlockSpec((1, D_MODEL), lambda i, f: (0, 0)),
                  pl.BlockSpec((1, D_MODEL, fc), lambda i, f: (0, 0, f)),
                  pl.BlockSpec((1, D_MODEL, fc), lambda i, f: (0, 0, f)),
                  pl.BlockSpec((1, fc, D_MODEL), lambda i, f: (0, f, 0))],
        out_specs=pl.BlockSpec((tm, D_MODEL), row),
        out_shape=jax.ShapeDtypeStruct((n, D_MODEL), F32),
        scratch_shapes=[pltpu.VMEM((tm, D_MODEL), BF16), pltpu.VMEM((tm, D_MODEL), F32)],
        compiler_params=_params(2), name="ffn_dense",
    )(x, g, wg, wu, wd)


def _pack_bf16_pairs(x):
    half = x.shape[1] // 2
    lo = lax.bitcast_convert_type(x[:, :half].astype(BF16).astype(F32), jnp.int32)
    hi = lax.bitcast_convert_type(x[:, half:].astype(BF16).astype(F32), jnp.int32)
    return (hi & -65536) | lax.shift_right_logical(lo, 16)


def _unpack_bf16_pairs(p):
    lo = lax.bitcast_convert_type(p << 16, F32)
    hi = lax.bitcast_convert_type(p & -65536, F32)
    return jnp.concatenate([lo, hi], axis=1)


def _ffn_experts_kernel(te_ref, tv_ref, x_ref, wg_ref, wu_ref, wd_ref, out_ref, xn_sc, acc_sc):
    del te_ref
    i, f = pl.program_id(0), pl.program_id(1)
    n_valid = tv_ref[i]

    @pl.when(n_valid > 0)
    def _():
        @pl.when(f == 0)
        def _():
            row = lax.broadcasted_iota(jnp.int32, x_ref.shape, 0)
            xn_sc[...] = _unpack_bf16_pairs(jnp.where(row < n_valid, x_ref[...], 0)).astype(BF16)
            acc_sc[...] = jnp.zeros_like(acc_sc)

        def step(rows):
            xn = xn_sc[:rows]
            a = jnp.dot(xn, wg_ref[0].astype(BF16), preferred_element_type=F32)
            b = jnp.dot(xn, wu_ref[0].astype(BF16), preferred_element_type=F32)
            t = (jax.nn.silu(a) * b).astype(BF16)
            acc_sc[:rows] += jnp.dot(t, wd_ref[0].astype(BF16), preferred_element_type=F32)

        tm = xn_sc.shape[0]
        quarter = tm // 4
        for nq in range(1, 5):
            pl.when((n_valid > (nq - 1) * quarter) & (n_valid <= nq * quarter))(
                functools.partial(step, nq * quarter))

        @pl.when(f == pl.num_programs(1) - 1)
        def _():
            out_ref[...] = _pack_bf16_pairs(acc_sc[...])


def _ffn_experts(tile_expert, tile_valid, x, wg, wu, wd, *, tm, fc):
    n, wpk = x.shape
    ff = wg.shape[2]
    row = lambda i, f, te, tv: (i, 0)
    grid_spec = pltpu.PrefetchScalarGridSpec(
        num_scalar_prefetch=2, grid=(n // tm, ff // fc),
        in_specs=[pl.BlockSpec((tm, wpk), row),
                  pl.BlockSpec((1, D_MODEL, fc), lambda i, f, te, tv: (te[i], 0, f)),
                  pl.BlockSpec((1, D_MODEL, fc), lambda i, f, te, tv: (te[i], 0, f)),
                  pl.BlockSpec((1, fc, D_MODEL), lambda i, f, te, tv: (te[i], f, 0))],
        out_specs=pl.BlockSpec((tm, wpk), row),
        scratch_shapes=[pltpu.VMEM((tm, D_MODEL), BF16), pltpu.VMEM((tm, D_MODEL), F32)])
    return pl.pallas_call(
        _ffn_experts_kernel, grid_spec=grid_spec,
        out_shape=jax.ShapeDtypeStruct((n, wpk), jnp.int32),
        compiler_params=_params(2), name="ffn_experts",
    )(tile_expert, tile_valid, x, wg, wu, wd)


def _sc_worker_id():
    return lax.axis_index("s") * SC_CORES + lax.axis_index("c")


def _sc_dispatch(x_a, x_b, slots_a, slots_b, n_slots):
    w = x_a.shape[1]
    per_a = x_a.shape[0] // SC_WORKERS
    per_b = x_b.shape[0] // SC_WORKERS
    ch = min(SC_CHUNK // 2, per_a)
    n_chunks = per_a // ch
    mesh = plsc.VectorSubcoreMesh(core_axis_name="c", subcore_axis_name="s")
    dma = pltpu.SemaphoreType.DMA

    @functools.partial(
        pl.kernel, mesh=mesh, out_type=jax.ShapeDtypeStruct((n_slots, w), x_a.dtype),
        scratch_types=[pltpu.VMEM((per_a,), jnp.int32), pltpu.VMEM((per_a,), jnp.int32),
                       pltpu.VMEM((ch, w), x_a.dtype), pltpu.VMEM((ch, w), x_a.dtype),
                       pltpu.VMEM((per_b,), jnp.int32), pltpu.VMEM((per_b, w), x_b.dtype),
                       dma, dma, dma, dma, dma, dma])
    def dispatch(xa_hbm, xb_hbm, a0_hbm, a1_hbm, b0_hbm, b1_hbm, out_hbm,
                 idx0_v, idx1_v, rows0_v, rows1_v, idxb_v, rowsb_v,
                 lsem0, lsem1, s0sem0, s0sem1, s1sem0, s1sem1):
        wid = _sc_worker_id()
        base_a = pl.multiple_of(wid * per_a, 8)
        pltpu.sync_copy(a0_hbm.at[pl.ds(base_a, per_a)], idx0_v)
        pltpu.sync_copy(a1_hbm.at[pl.ds(base_a, per_a)], idx1_v)
        rows, lsem = (rows0_v, rows1_v), (lsem0, lsem1)
        ssem = ((s0sem0, s0sem1), (s1sem0, s1sem1))

        def load(c):
            return pltpu.async_copy(xa_hbm.at[pl.ds(base_a + c * ch, ch)], rows[c % 2], lsem[c % 2])

        def scatter(c):
            return [pltpu.async_copy(rows[c % 2], out_hbm.at[idx_v.at[pl.ds(c * ch, ch)]],
                                     ssem[k][c % 2])
                    for k, idx_v in enumerate((idx0_v, idx1_v))]

        loads, scatters = {0: load(0)}, {}
        for c in range(n_chunks):
            loads[c].wait()
            if c + 1 < n_chunks:
                for cp in scatters.pop(c - 1, ()):
                    cp.wait()
                loads[c + 1] = load(c + 1)
            scatters[c] = scatter(c)
        for cps in scatters.values():
            for cp in cps:
                cp.wait()

        base_b = pl.multiple_of(wid * per_b, 8)
        pltpu.sync_copy(xb_hbm.at[pl.ds(base_b, per_b)], rowsb_v)
        for slots_hbm in (b0_hbm, b1_hbm):
            pltpu.sync_copy(slots_hbm.at[pl.ds(base_b, per_b)], idxb_v)
            pltpu.sync_copy(rowsb_v, out_hbm.at[idxb_v])

    return dispatch(x_a, x_b, *slots_a, *slots_b)


def _sc_gather(table, idx):
    _, w = table.shape
    b = idx.shape[0]
    per_w = b // SC_WORKERS
    ch = max(c for c in range(8, SC_CHUNK + 1, 8) if per_w % c == 0)
    n_chunks = per_w // ch
    assert per_w * SC_WORKERS == b
    mesh = plsc.VectorSubcoreMesh(core_axis_name="c", subcore_axis_name="s")
    dma = pltpu.SemaphoreType.DMA

    @functools.partial(
        pl.kernel, mesh=mesh, out_type=jax.ShapeDtypeStruct((b, w), table.dtype),
        scratch_types=[pltpu.VMEM((per_w,), jnp.int32), pltpu.VMEM((ch, w), table.dtype),
                       pltpu.VMEM((ch, w), table.dtype), dma, dma, dma, dma])
    def gather(table_hbm, idx_hbm, out_hbm, idx_v, rows0_v, rows1_v, gsem0, gsem1, wsem0, wsem1):
        wid = _sc_worker_id()
        base = pl.multiple_of(wid * per_w, 8)
        pltpu.sync_copy(idx_hbm.at[pl.ds(base, per_w)], idx_v)
        rows, gsem, wsem = (rows0_v, rows1_v), (gsem0, gsem1), (wsem0, wsem1)

        def fetch(c):
            return pltpu.async_copy(table_hbm.at[idx_v.at[pl.ds(c * ch, ch)]], rows[c % 2],
                                    gsem[c % 2])

        def write(c):
            return pltpu.async_copy(rows[c % 2], out_hbm.at[pl.ds(base + c * ch, ch)], wsem[c % 2])

        fetches, writes = {0: fetch(0)}, {}
        for c in range(n_chunks):
            fetches[c].wait()
            if c + 1 < n_chunks:
                if c >= 1:
                    writes.pop(c - 1).wait()
                fetches[c + 1] = fetch(c + 1)
            writes[c] = write(c)
        for cp in writes.values():
            cp.wait()

    return gather(table, idx)


def _combine_kernel(h_ref, sel_ref, y0_ref, y1_ref, out_ref):
    sel = sel_ref[...]
    out_ref[...] = (h_ref[...] + sel[:, 2:3] * _unpack_bf16_pairs(y0_ref[...])
                    + sel[:, 3:4] * _unpack_bf16_pairs(y1_ref[...]))


def _combine(h, sel, yb, blk_off, tm):
    n = h.shape[0]
    wpk = yb.shape[1]
    half_blocks = yb.shape[0] // 2 // tm
    return pl.pallas_call(
        _combine_kernel, grid=(n // tm,),
        in_specs=[pl.BlockSpec((tm, D_MODEL), lambda i: (i, 0)),
                  pl.BlockSpec((tm, LANES), lambda i: (i, 0)),
                  pl.BlockSpec((tm, wpk), lambda i: (i + blk_off, 0)),
                  pl.BlockSpec((tm, wpk), lambda i: (i + blk_off + half_blocks, 0))],
        out_specs=pl.BlockSpec((tm, D_MODEL), lambda i: (i, 0)),
        out_shape=jax.ShapeDtypeStruct((n, D_MODEL), F32),
        compiler_params=_params(1), name="combine",
    )(h, sel, yb, yb)


def _route_plan(sels, counts, te_rows, n_tiles):
    counts = counts[0, :N_EXPERTS].astype(jnp.int32)
    padded = (counts + te_rows - 1) // te_rows * te_rows
    ends = jnp.cumsum(padded)
    starts = ends - padded
    experts = jnp.arange(N_EXPERTS, dtype=jnp.int32)

    def slots_of(sel, k):
        e = sel[:, k].astype(jnp.int32)
        start = jnp.sum(jnp.where(e[:, None] == experts[None, :], starts[None, :], 0), axis=1)
        return start + sel[:, 4 + k].astype(jnp.int32)

    slots = [(slots_of(sel, 0), slots_of(sel, 1)) for sel in sels]
    tile_start = jnp.arange(n_tiles, dtype=jnp.int32) * te_rows
    tile_expert = jnp.minimum(jnp.sum(tile_start[:, None] >= ends[None, :], axis=1),
                              N_EXPERTS - 1).astype(jnp.int32)
    in_group = tile_start - starts[tile_expert]
    tile_valid = jnp.clip(counts[tile_expert] - in_group, 0, te_rows)
    tile_valid = jnp.where(tile_start < ends[-1], tile_valid, 0).astype(jnp.int32)
    return slots, tile_expert, tile_valid


def _gmlp_kernel(h_ref, g_ref, wuv_ref, vg_ref, mix_ref, bs_ref, wo_ref, out_ref, v_ref, um_sc,
                 *, period):
    h = h_ref[...]
    tm = h.shape[0]
    ln = bs_ref.shape[0]
    gw = D_MODEL // C_GROUPS
    xn = _rms_bf16(h, g_ref[...])
    vv = jax.nn.gelu(jnp.dot(xn, wuv_ref[:, D_MODEL:], preferred_element_type=F32))
    v = vv * lax.rsqrt(jnp.mean(vv * vv, axis=-1, keepdims=True) + EPS) * vg_ref[...]
    v_ref[0] = v[tm - v_ref.shape[1]:]
    vb = v.astype(BF16)
    u = jax.nn.gelu(jnp.dot(xn, wuv_ref[:, :D_MODEL], preferred_element_type=F32))
    r = lax.broadcasted_iota(jnp.int32, (ln, ln), 0)
    c = lax.broadcasted_iota(jnp.int32, (ln, ln), 1)
    causal = (r // period == c // period) & (c % period <= r % period)
    nw = mix_ref.shape[1]
    if ln != nw:
        pick = (lax.broadcasted_iota(jnp.int32, (ln, nw), 0) % period
                == lax.broadcasted_iota(jnp.int32, (ln, nw), 1)).astype(BF16)
        wr = lax.broadcasted_iota(jnp.int32, (nw, nw), 0)
        wc = lax.broadcasted_iota(jnp.int32, (nw, nw), 1)
        corner = (wr < period) & (wc <= wr)
    for gi in range(C_GROUPS):
        if ln == nw:
            wm = jnp.where(causal, mix_ref[gi], 0.0).astype(BF16)
        else:
            rows_w = jnp.dot(pick, jnp.where(corner, mix_ref[gi], 0.0).astype(BF16),
                             preferred_element_type=F32).astype(BF16)
            tiled = lax.dot_general(rows_w, pick, (((1,), (1,)), ((), ())),
                                    preferred_element_type=F32)
            wm = jnp.where(causal, tiled, 0.0).astype(BF16)
        bias = bs_ref[:, gi:gi + 1]
        for ch in range(tm // ln):
            rs = slice(ch * ln, (ch + 1) * ln)
            cs = slice(gi * gw, (gi + 1) * gw)
            mixed = jnp.dot(wm, vb[rs, cs], preferred_element_type=F32) + bias
            um_sc[rs, cs] = (u[rs, cs] * mixed).astype(BF16)
    out_ref[...] = h + jnp.dot(um_sc[...], wo_ref[...], preferred_element_type=F32)


def _gmlp(h, g, wuv16, vg, mix, bs, wo16, *, tm, period, seq, tail):
    n = h.shape[0]
    tps = seq // tm
    row = lambda i: (i, 0)
    full = lambda a: pl.BlockSpec(a.shape, lambda i: (0,) * a.ndim)
    return pl.pallas_call(
        functools.partial(_gmlp_kernel, period=period),
        grid=(n // tm,),
        in_specs=[pl.BlockSpec((tm, D_MODEL), row), full(g), full(wuv16), full(vg), full(mix),
                  full(bs), full(wo16)],
        out_specs=[pl.BlockSpec((tm, D_MODEL), row),
                   pl.BlockSpec((1, tail, D_MODEL), lambda i: (i // tps, 0, 0))],
        out_shape=[jax.ShapeDtypeStruct((n, D_MODEL), F32),
                   jax.ShapeDtypeStruct((n // seq, tail, D_MODEL), F32)],
        scratch_shapes=[pltpu.VMEM((tm, D_MODEL), BF16)],
        compiler_params=_params(1), name="gmlp",
    )(h, g, wuv16, vg, mix, bs, wo16)


def _router_kernel(h_ref, g_ref, wr_ref, cin_ref, xn_ref, sel_ref, cnt_ref):
    @pl.when(pl.program_id(0) == 0)
    def _():
        cnt_ref[...] = cin_ref[...]

    x = h_ref[...]
    xn = x * lax.rsqrt(jnp.mean(x * x, axis=-1, keepdims=True) + EPS) * g_ref[...]
    xn_ref[...] = _pack_bf16_pairs(xn)
    x_hi = xn.astype(BF16)
    x_lo = (xn - x_hi.astype(F32)).astype(BF16)
    w = wr_ref[...]
    w_hi = w.astype(BF16)
    w_lo = (w - w_hi.astype(F32)).astype(BF16)
    logits = (jnp.dot(x_hi, w_hi, preferred_element_type=F32)
              + (jnp.dot(x_lo, w_hi, preferred_element_type=F32)
                 + jnp.dot(x_hi, w_lo, preferred_element_type=F32)))
    lane = lax.broadcasted_iota(jnp.int32, logits.shape, 1)
    lanef = lane.astype(F32)
    logits = jnp.where(lane < N_EXPERTS, logits, -jnp.inf)
    m1 = jnp.max(logits, axis=-1, keepdims=True)
    i1 = jnp.min(jnp.where(logits == m1, lanef, float(LANES)), axis=-1, keepdims=True)
    rest = jnp.where(lanef == i1, -jnp.inf, logits)
    m2 = jnp.max(rest, axis=-1, keepdims=True)
    i2 = jnp.min(jnp.where(rest == m2, lanef, float(LANES)), axis=-1, keepdims=True)
    e2 = jnp.exp(m2 - m1)
    w1 = 1.0 / (1.0 + e2)
    w2 = e2 / (1.0 + e2)
    tm = x.shape[0]
    hit1, hit2 = lanef == i1, lanef == i2
    picked = jnp.where(hit1 | hit2, 1.0, 0.0)
    r_i = lax.broadcasted_iota(jnp.int32, (tm, tm), 0)
    c_i = lax.broadcasted_iota(jnp.int32, (tm, tm), 1)
    below = jnp.where(c_i < r_i, 1.0, 0.0).astype(BF16)
    before = jnp.dot(below, picked.astype(BF16), preferred_element_type=F32) + cnt_ref[...]
    rank1 = jnp.sum(jnp.where(hit1, before, 0.0), axis=-1, keepdims=True)
    rank2 = jnp.sum(jnp.where(hit2, before, 0.0), axis=-1, keepdims=True)
    cnt_ref[...] += jnp.sum(picked, axis=0, keepdims=True)
    vals = (i1, i2, w1, w2, rank1, rank2)
    sel = jnp.zeros_like(logits)
    for li, val in enumerate(vals):
        sel = jnp.where(lane == li, val, sel)
    sel_ref[...] = sel


def _router(h, g, wr_pad, counts_in, tm):
    n = h.shape[0]
    row = lambda i: (i, 0)
    fix = lambda i: (0, 0)
    return pl.pallas_call(
        _router_kernel, grid=(n // tm,),
        in_specs=[pl.BlockSpec((tm, D_MODEL), row), pl.BlockSpec((1, D_MODEL), fix),
                  pl.BlockSpec((D_MODEL, LANES), fix), pl.BlockSpec((1, LANES), fix)],
        out_specs=[pl.BlockSpec((tm, D_MODEL // 2), row), pl.BlockSpec((tm, LANES), row),
                   pl.BlockSpec((1, LANES), fix)],
        out_shape=[jax.ShapeDtypeStruct((n, D_MODEL // 2), jnp.int32),
                   jax.ShapeDtypeStruct((n, LANES), F32),
                   jax.ShapeDtypeStruct((1, LANES), F32)],
        compiler_params=_params(1), name="router",
    )(h, g, wr_pad, counts_in)


def _rope_tables(pos):
    half = HEAD_DIM // 2
    inv = ROPE_THETA ** (-jnp.arange(half, dtype=F32) / half)
    ang = pos.astype(F32)[:, None] * inv[None, :]
    cos, sin = jnp.cos(ang), jnp.sin(ang)
    zero = jnp.zeros_like(sin)
    rep = lambda a, b: jnp.tile(jnp.concatenate([a, b], axis=-1), (1, LANES // HEAD_DIM))
    return rep(cos, cos), rep(-sin, zero), rep(zero, sin)


def kernel(x_prompt, x_sample, state_conv, cache_kv_win, norm_mix_e, w_in_e, conv_w, q_gain,
           k_gain, w_out_e, norm_ffn_e, ffn_w_gate, ffn_w_up, ffn_w_down, norm_mix_o, w_uv,
           v_gain, w_s, b_s, w_out_o, norm_moe, w_router, moe_w_gate, moe_w_up, moe_w_down):
    bp, seq, _ = x_prompt.shape
    bs_, s_len, _ = x_sample.shape
    n_p, n_s = bp * seq, bs_ * s_len
    hp = x_prompt.reshape(n_p, D_MODEL)
    hs = x_sample.reshape(n_s, D_MODEL)
    tm_p = ROWS
    row = lambda a: a.reshape(1, -1)

    tabs_p = _rope_tables(jnp.arange(seq, dtype=jnp.int32))
    tabs_s = tuple(jnp.tile(t, (bs_, 1)) for t in
                   _rope_tables(PAST_LEN + jnp.arange(s_len, dtype=jnp.int32)))
    hd = jnp.arange(256) // HEAD_DIM
    bd = (hd[:, None] == hd[None, :]).astype(BF16)

    w_in16 = w_in_e[0].astype(BF16)
    w_out16 = w_out_e[0].astype(BF16)
    qg = jnp.tile(q_gain[0], 4).reshape(1, 256)
    kg = jnp.tile(k_gain[0], 4).reshape(1, 256)
    g0 = row(norm_mix_e[0])

    dils = tuple(dil for _, dil in BRANCHES)
    lw = min(max(wnd for wnd, _ in BRANCHES), seq)
    z_p, gb_p, kvt_p, *planes = _inproj(hp, g0, w_in16, tabs_p, qg, kg, bd, tm_p, seq, dils, lw)
    z_s, gb_s, kv_s, q_s = _inproj(hs, g0, w_in16, tabs_s, qg, kg, bd, n_s, n_s, ())

    os_, ls_ = [], []
    for di, dil in enumerate(dils):
        m = seq // dil
        q_d, k_d, v_d = (a.reshape(bp * dil, m, D_B) for a in planes[3 * di:3 * di + 3])
        qb = min(ATTN_ROWS, m)
        o, lse = _attn_prompt(q_d, k_d, v_d, qb=qb, pb=ATTN_ROWS // qb)
        os_.append(o.reshape(bp, dil, m, D_B))
        ls_.append(lse.reshape(bp, dil, m, LANES))
    hp = _outproj_prompt(z_p, gb_p, conv_w[0], os_, ls_, w_out16, hp, tm_p, seq, dils)

    cat_s = _mix_sample(q_s.reshape(bs_, s_len, D_B),
                        kv_s.reshape(bs_, s_len, 2 * D_B),
                        jnp.transpose(cache_kv_win[0], (0, 2, 3, 4, 1)).reshape(
                            bs_, 2, D_B, -1),
                        z_s.reshape(bs_, s_len, D_A), gb_s.reshape(bs_, s_len, D_A),
                        state_conv[0], conv_w[0])
    hs = _linear_res(cat_s.reshape(n_s, D_MODEL), w_out16, hs)

    conv_prompt = z_p.reshape(bp, seq, D_A)[:, seq - 2:][None]
    conv_sample = z_s.reshape(bs_, s_len, D_A)[:, s_len - 2:][None]
    kv_prompt = jnp.transpose(kvt_p.reshape(bp, 2, H_B, HEAD_DIM, lw), (0, 4, 1, 2, 3))[None]
    kv_sample = kv_s.reshape(bs_, s_len, 2, H_B, HEAD_DIM)[None]

    fg, fu, fd = (ffn_w_gate.astype(BF16), ffn_w_up.astype(BF16), ffn_w_down.astype(BF16))
    gf = row(norm_ffn_e[0])
    hp = _ffn_dense(hp, gf, fg, fu, fd, tm=DENSE_ROWS, fc=DENSE_FF)
    hs = _ffn_dense(hs, gf, fg, fu, fd, tm=n_s, fc=DENSE_FF)

    wuv16 = w_uv[0].astype(BF16)
    wo16 = w_out_o[0].astype(BF16)
    g1, vg = row(norm_mix_o[0]), row(v_gain[0])
    tail = seq - ((seq - 1) // CHUNK) * CHUNK
    hp, v_tail_p = _gmlp(hp, g1, wuv16, vg, w_s[0], b_s[0].T, wo16, tm=tm_p, period=CHUNK,
                         seq=seq, tail=tail)
    bias_s = jnp.tile(b_s[0][:, :s_len].T, (bs_, 1))
    hs, v_all_s = _gmlp(hs, g1, wuv16, vg, w_s[0], bias_s, wo16, tm=n_s, period=s_len,
                        seq=n_s, tail=n_s)
    v_prompt = v_tail_p[None]
    v_sample = v_all_s.reshape(bs_, s_len, D_MODEL)[None]

    wr_pad = jnp.pad(w_router[0], ((0, 0), (0, LANES - N_EXPERTS)))
    gm = row(norm_moe[0])
    xn_p, sel_p, counts = _router(hp, gm, wr_pad, jnp.zeros((1, LANES), F32), tm_p)
    xn_s, sel_s, counts = _router(hs, gm, wr_pad, counts, n_s)
    n_tok = n_p + n_s
    n_tiles = -(-(2 * n_tok + N_EXPERTS * (MOE_ROWS - 1)) // MOE_ROWS)
    (slots_p, slots_s), tile_expert, tile_valid = _route_plan(
        (sel_p, sel_s), counts, MOE_ROWS, n_tiles)
    xs = _sc_dispatch(xn_p, xn_s, slots_p, slots_s, n_tiles * MOE_ROWS)
    ys = _ffn_experts(tile_expert, tile_valid, xs, moe_w_gate[0], moe_w_up[0], moe_w_down[0],
                      tm=MOE_ROWS, fc=MOE_FF)
    fill = jnp.arange(-n_tok % COMBINE_ROWS, dtype=jnp.int32)
    yb = _sc_gather(ys, jnp.concatenate([slots_p[0], slots_s[0], fill,
                                         slots_p[1], slots_s[1], fill]))
    hp = _combine(hp, sel_p, yb, 0, COMBINE_ROWS)
    hs = _combine(hs, sel_s, yb, n_p // n_s, n_s)

    return (hp.reshape(bp, seq, D_MODEL), hs.reshape(bs_, s_len, D_MODEL),
            conv_prompt, conv_sample, kv_prompt, kv_sample, v_prompt, v_sample)
```

```python
import functools

import jax
import jax.numpy as jnp
from jax import lax
from jax.experimental import pallas as pl
from jax.experimental.pallas import tpu as pltpu
from jax.experimental.pallas import tpu_sc as plsc

F32 = jnp.float32
BF16 = jnp.bfloat16

D_MODEL = 1024
D_A = 512
H_B = 8
HEAD_DIM = 64
D_B = H_B * HEAD_DIM
BRANCHES = ((128, 1), (512, 4), (2048, 16))
BAND = 128
ROPE_THETA = 10000.0
PAST_LEN = 16384
N_EXPERTS = 8
C_GROUPS = 4
CHUNK = 128
EPS = 1e-6
LANES = 128
VMEM_LIMIT = 56 * 1024 * 1024
SC_CORES = 2
SC_SUBCORES = 16
SC_WORKERS = SC_CORES * SC_SUBCORES
SC_CHUNK = 128

ROWS = 512
ATTN_ROWS = 512
DENSE_ROWS, DENSE_FF = 1024, 1408
MOE_ROWS, MOE_FF = 1024, 512
COMBINE_ROWS = 512


def _params(n_axes):
    return pltpu.CompilerParams(dimension_semantics=("arbitrary",) * n_axes,
                                vmem_limit_bytes=VMEM_LIMIT)


def _rms_bf16(x, g):
    ms = jnp.mean(x * x, axis=-1, keepdims=True)
    return (x * lax.rsqrt(ms + EPS) * g).astype(BF16)


def _inproj_kernel(x_ref, g_ref, w_ref, cos_ref, sina_ref, sinb_ref, qg_ref, kg_ref, bd_ref,
                   z_ref, gb_ref, kv_ref, *rest, dils, win_tiles, tiles_per_seq):
    if dils:
        planes, qkv_sc = rest[:-1], rest[-1]
    else:
        (q_out,) = rest
    xn = _rms_bf16(x_ref[...], g_ref[...])

    def col(j):
        return jnp.dot(xn, w_ref[:, j * 512:(j + 1) * 512], preferred_element_type=F32)

    z_ref[...] = col(1) * col(0)
    gb_ref[...] = col(2).astype(gb_ref.dtype)
    cos, sina, sinb = cos_ref[...], sina_ref[...], sinb_ref[...]

    def head_norm_rope(t, gain):
        outs = []
        for c2 in range(2):
            tc = t[:, c2 * 256:(c2 + 1) * 256]
            ss = jnp.dot((tc * tc).astype(BF16), bd_ref[...], preferred_element_type=F32)
            tn = tc * lax.rsqrt(ss * (1.0 / HEAD_DIM) + EPS) * gain
            for c in range(2):
                u = tn[:, c * LANES:(c + 1) * LANES]
                outs.append(u * cos + pltpu.roll(u, LANES - 32, 1) * sina
                            + pltpu.roll(u, 32, 1) * sinb)
        return outs

    q = [t * (HEAD_DIM ** -0.5) for t in head_norm_rope(col(3), qg_ref[...])]
    k = head_norm_rope(col(4), kg_ref[...])
    v = col(5)
    if not dils:
        for c in range(4):
            kv_ref[:, c * LANES:(c + 1) * LANES] = k[c]
            q_out[:, c * LANES:(c + 1) * LANES] = q[c]
        kv_ref[:, D_B:] = v
        return

    @pl.when(pl.program_id(0) % tiles_per_seq >= tiles_per_seq - win_tiles)
    def _():
        for c in range(4):
            kv_ref[0, c * LANES:(c + 1) * LANES, :] = k[c].T
            kv_ref[0, D_B + c * LANES:D_B + (c + 1) * LANES, :] = v[:, c * LANES:(c + 1) * LANES].T

    tm = x_ref.shape[0]
    for a, val in enumerate((q, k, [v[:, c * LANES:(c + 1) * LANES] for c in range(4)])):
        for c in range(4):
            qkv_sc[0, a, c] = val[c]
            planes[a][0, 0, :, c * LANES:(c + 1) * LANES] = val[c].astype(BF16)
    for di in range(1, len(dils)):
        dil, prev = dils[di], dils[di - 1]
        ratio = dil // prev
        src, dst = qkv_sc.at[(di - 1) % 2], qkv_sc.at[di % 2]
        for r in range(dil):
            start = (r % prev) * (tm // prev) + r // prev
            for a in range(3):
                for c in range(4):
                    rows_ac = src[a, c, pl.ds(start, tm // dil, stride=ratio), :]
                    planes[3 * di + a][0, r, :, c * LANES:(c + 1) * LANES] = rows_ac.astype(BF16)
                    if di + 1 < len(dils):
                        dst[a, c, r * (tm // dil):(r + 1) * (tm // dil), :] = rows_ac


def _inproj(x, g, w16, tabs, qg, kg, bd, tm, seq, dils, win=0):
    n = x.shape[0]
    cos, sina, sinb = tabs
    tt = cos.shape[0] // tm
    tps = seq // tm
    win_tiles = win // tm
    row = lambda i: (i, 0)
    fix = lambda i: (0, 0)
    tab = lambda i: (i % tt, 0)
    out_specs = [pl.BlockSpec((tm, D_A), row), pl.BlockSpec((tm, D_A), row)]
    out_shape = [jax.ShapeDtypeStruct((n, D_A), F32),
                 jax.ShapeDtypeStruct((n, D_A), BF16 if dils else F32)]
    if dils:
        out_specs.append(pl.BlockSpec(
            (1, 2 * D_B, tm),
            lambda i: (i // tps, 0, jnp.maximum(i % tps - (tps - win_tiles), 0))))
        out_shape.append(jax.ShapeDtypeStruct((n // seq, 2 * D_B, win), F32))
        assert dils[0] == 1 and all(b % a == 0 for a, b in zip(dils, dils[1:]))
        scratch = [pltpu.VMEM((2, 3, D_B // LANES, tm, LANES), F32)]
    else:
        out_specs += [pl.BlockSpec((tm, 2 * D_B), row), pl.BlockSpec((tm, D_B), row)]
        out_shape += [jax.ShapeDtypeStruct((n, 2 * D_B), F32),
                      jax.ShapeDtypeStruct((n, D_B), F32)]
        scratch = []
    for dil in dils:
        for _ in range(3):
            out_specs.append(pl.BlockSpec((1, dil, tm // dil, D_B),
                                          lambda i: (i // tps, 0, i % tps, 0)))
            out_shape.append(jax.ShapeDtypeStruct((n // seq, dil, seq // dil, D_B), BF16))
    return pl.pallas_call(
        functools.partial(_inproj_kernel, dils=dils, win_tiles=win_tiles, tiles_per_seq=tps),
        grid=(n // tm,),
        in_specs=[pl.BlockSpec((tm, D_MODEL), row), pl.BlockSpec((1, D_MODEL), fix),
                  pl.BlockSpec(w16.shape, fix),
                  pl.BlockSpec((tm, LANES), tab), pl.BlockSpec((tm, LANES), tab),
                  pl.BlockSpec((tm, LANES), tab),
                  pl.BlockSpec((1, 256), fix), pl.BlockSpec((1, 256), fix),
                  pl.BlockSpec((256, 256), fix)],
        out_specs=out_specs, out_shape=out_shape, scratch_shapes=scratch,
        compiler_params=_params(1), name="inproj",
    )(x, g, w16, cos, sina, sinb, qg, kg, bd)


def _attn_prompt_kernel(q_ref, kc_ref, kp_ref, vc_ref, vp_ref, o_ref, lse_ref, *, qb):
    j = pl.program_id(1)
    nsub = qb // BAND
    lane = lax.broadcasted_iota(jnp.int32, (1, LANES), 1)
    head0 = lane < HEAD_DIM
    qi = lax.broadcasted_iota(jnp.int32, (2 * BAND, 2 * BAND), 0) % BAND + BAND
    kj = lax.broadcasted_iota(jnp.int32, (2 * BAND, 2 * BAND), 1)
    dist = qi - kj
    band = (dist >= 0) & (dist <= BAND)
    first = band & ((kj >= BAND) | (j > 0))
    lanes = [slice(hp * LANES, (hp + 1) * LANES) for hp in range(D_B // LANES)]
    for pi, sub in ((pi, sub) for pi in range(q_ref.shape[0]) for sub in range(nsub)):
        rows = slice(sub * BAND, (sub + 1) * BAND)
        keys = slice((sub - 1) * BAND, (sub + 1) * BAND)
        lse_all = jnp.zeros((BAND, LANES), F32)
        for hp, ls in enumerate(lanes):
            qs = q_ref[pi, rows, ls]
            if sub == 0:
                ks = jnp.concatenate([kp_ref[pi, :, ls], kc_ref[pi, :BAND, ls]], axis=0)
                vs = jnp.concatenate([vp_ref[pi, :, ls], vc_ref[pi, :BAND, ls]], axis=0)
            else:
                ks, vs = kc_ref[pi, keys, ls], vc_ref[pi, keys, ls]
            q2 = jnp.concatenate([jnp.where(head0, qs, jnp.zeros_like(qs)),
                                  jnp.where(head0, jnp.zeros_like(qs), qs)], axis=0)
            s = lax.dot_general(q2, ks, (((1,), (1,)), ((), ())), preferred_element_type=F32)
            s = jnp.where(first if sub == 0 else band, s, -jnp.inf)
            mx = jnp.max(s, axis=-1, keepdims=True)
            p = jnp.exp(s - mx)
            l = jnp.sum(p, axis=-1, keepdims=True)
            pv = jnp.dot(p.astype(BF16), vs, preferred_element_type=F32) / l
            lse = mx + jnp.log(l)
            o_ref[pi, rows, ls] = jnp.where(head0, pv[:BAND], pv[BAND:]).astype(BF16)
            lse_all = jnp.where(lane == 2 * hp, lse[:BAND],
                                jnp.where(lane == 2 * hp + 1, lse[BAND:], lse_all))
        lse_ref[pi, rows, :] = lse_all


def _attn_prompt(q, k, v, qb, pb):
    b, m, w = q.shape
    nsub = qb // BAND
    cur = lambda bi, j: (bi, j, 0)
    prev = lambda bi, j: (bi, jnp.maximum(j * nsub - 1, 0), 0)
    return pl.pallas_call(
        functools.partial(_attn_prompt_kernel, qb=qb),
        grid=(b // pb, m // qb),
        in_specs=[pl.BlockSpec((pb, qb, w), cur), pl.BlockSpec((pb, qb, w), cur),
                  pl.BlockSpec((pb, BAND, w), prev), pl.BlockSpec((pb, qb, w), cur),
                  pl.BlockSpec((pb, BAND, w), prev)],
        out_specs=[pl.BlockSpec((pb, qb, w), cur), pl.BlockSpec((pb, qb, LANES), cur)],
        out_shape=[jax.ShapeDtypeStruct((b, m, w), BF16),
                   jax.ShapeDtypeStruct((b, m, LANES), F32)],
        compiler_params=_params(2), name="attn_prompt",
    )(q, k, k, v, v)


def _outproj_prompt_kernel(z_ref, zh_ref, gb_ref, cw_ref, *rest, tiles_per_batch, dils):
    nb = len(dils)
    o_refs, l_refs = rest[:nb], rest[nb:2 * nb]
    w_ref, h_ref, out_ref = rest[2 * nb:2 * nb + 3]
    o_scs, l_scs, tmp_sc = rest[2 * nb + 3:3 * nb + 3], rest[3 * nb + 3:4 * nb + 3], rest[-1]
    i = pl.program_id(0)
    z = z_ref[...]
    tm = z.shape[0]
    def put_rows(dst, spare, val, r, dil):
        if dil == 1:
            dst[...] = val
        elif dil <= 4:
            dst[pl.ds(r, tm // dil, stride=dil), :] = val
        else:
            spare[pl.ds((r % 4) * (tm // 4) + r // 4, tm // dil, stride=dil // 4), :] = val

    def finish_rows(dst, spare, dil):
        if dil > 4:
            for b in range(4):
                dst[pl.ds(b, tm // 4, stride=4), :] = spare[b * (tm // 4):(b + 1) * (tm // 4), :]

    for dil, o_ref, l_ref, o_sc, l_sc in zip(dils, o_refs, l_refs, o_scs, l_scs):
        assert dil <= 16
        for r in range(dil):
            for c in range(D_B // LANES):
                ls = slice(c * LANES, (c + 1) * LANES)
                put_rows(o_sc.at[c], tmp_sc.at[c], o_ref[0, r, :, ls].astype(F32), r, dil)
            put_rows(l_sc, tmp_sc.at[D_B // LANES], l_ref[0, r], r, dil)
        for c in range(D_B // LANES):
            finish_rows(o_sc.at[c], tmp_sc.at[c], dil)
        finish_rows(l_sc, tmp_sc.at[D_B // LANES], dil)
    unstage = lambda sc: jnp.concatenate([sc[c] for c in range(D_B // LANES)], axis=1)
    zh = jnp.where(i % tiles_per_batch == 0, 0.0, zh_ref[...])
    row = lax.broadcasted_iota(jnp.int32, (tm, 1), 0)
    zm1 = jnp.where(row >= 1, pltpu.roll(z, 1, 0), zh[7:8])
    zm2 = jnp.where(row >= 2, pltpu.roll(z, 2, 0), jnp.where(row == 1, zh[7:8], zh[6:7]))
    cw = cw_ref[...]
    a_out = gb_ref[...].astype(F32) * (cw[0:1] * zm2 + cw[1:2] * zm1 + cw[2:3] * z)
    lses = [l_sc[...] for l_sc in l_scs]
    lm = functools.reduce(jnp.maximum, lses)
    es = [jnp.exp(l - lm) for l in lses]
    tot = sum(es)
    spread = (lax.broadcasted_iota(jnp.int32, (LANES, D_B), 1) // HEAD_DIM
              == lax.broadcasted_iota(jnp.int32, (LANES, D_B), 0)).astype(BF16)

    def widen(wgt):
        hi = wgt.astype(BF16)
        lo = (wgt - hi.astype(F32)).astype(BF16)
        return (jnp.dot(hi, spread, preferred_element_type=F32)
                + jnp.dot(lo, spread, preferred_element_type=F32))

    o = sum(widen(e / tot) * unstage(o_sc) for e, o_sc in zip(es, o_scs))
    y = jnp.dot(a_out.astype(BF16), w_ref[:D_A, :], preferred_element_type=F32)
    y = y + jnp.dot(o.astype(BF16), w_ref[D_A:, :], preferred_element_type=F32)
    out_ref[...] = h_ref[...] + y


def _outproj_prompt(z, gb, cw, os_, ls_, w16, h, tm, seq, dils):
    n = z.shape[0]
    tps = seq // tm
    row = lambda i: (i, 0)
    fix = lambda i: (0, 0)
    halo = lambda i: (jnp.maximum(i * (tm // 8) - 1, 0), 0)
    half = pl.BlockSpec((tm, D_A), row)
    plane = lambda width: [pl.BlockSpec((1, dil, tm // dil, width),
                                        lambda i: (i // tps, 0, i % tps, 0)) for dil in dils]
    return pl.pallas_call(
        functools.partial(_outproj_prompt_kernel, tiles_per_batch=tps, dils=dils),
        grid=(n // tm,),
        in_specs=[half, pl.BlockSpec((8, D_A), halo), half, pl.BlockSpec((3, D_A), fix),
                  *plane(D_B), *plane(LANES),
                  pl.BlockSpec((D_MODEL, D_MODEL), fix), pl.BlockSpec((tm, D_MODEL), row)],
        out_specs=pl.BlockSpec((tm, D_MODEL), row),
        out_shape=jax.ShapeDtypeStruct((n, D_MODEL), F32),
        scratch_shapes=([pltpu.VMEM((D_B // LANES, tm, LANES), F32)] * len(dils)
                        + [pltpu.VMEM((tm, LANES), F32)] * len(dils)
                        + [pltpu.VMEM((D_B // LANES + 1, tm, LANES), F32)]),
        compiler_params=_params(1), name="outproj_prompt",
    )(z, z, gb, cw, *os_, *ls_, w16, h)


def _mix_sample_kernel(q_ref, kvn_ref, win_ref, z_ref, gb_ref, st_ref, cw_ref, cat_ref):
    s_len = q_ref.shape[1]
    rows = H_B * s_len
    npad = LANES
    q = q_ref[0]
    r_i = lax.broadcasted_iota(jnp.int32, (rows, D_B), 0)
    c_i = lax.broadcasted_iota(jnp.int32, (rows, D_B), 1)
    qt = jnp.concatenate([q] * H_B, axis=0)
    qbd = jnp.where(r_i // s_len == c_i // HEAD_DIM, qt, 0.0).astype(BF16)
    kvn = jnp.concatenate([kvn_ref[0], jnp.zeros((npad - s_len, 2 * D_B), F32)], axis=0)

    def multiplicity(delta):
        cnt = jnp.zeros(delta.shape, F32)
        for _, dil in BRANCHES:
            ok = (delta >= 0) & (delta <= BAND * dil) & ((delta & (dil - 1)) == 0)
            cnt = cnt + jnp.where(ok, 1.0, 0.0)
        return cnt

    lb = win_ref.shape[3]
    nt = (((1,), (1,)), ((), ()))
    tok_c = lax.broadcasted_iota(jnp.int32, (rows, lb), 0) % s_len
    cnt_c = multiplicity(tok_c + lb - lax.broadcasted_iota(jnp.int32, (rows, lb), 1))
    tok_n = lax.broadcasted_iota(jnp.int32, (rows, npad), 0) % s_len
    cnt_n = multiplicity(tok_n - lax.broadcasted_iota(jnp.int32, (rows, npad), 1))
    sc = jnp.dot(qbd, win_ref[0, 0].astype(BF16), preferred_element_type=F32)
    sn = lax.dot_general(qbd, kvn[:, :D_B].astype(BF16), nt, preferred_element_type=F32)
    sc = jnp.where(cnt_c > 0, sc, -jnp.inf)
    sn = jnp.where(cnt_n > 0, sn, -jnp.inf)
    mx = jnp.maximum(jnp.max(sc, axis=-1, keepdims=True), jnp.max(sn, axis=-1, keepdims=True))
    pc = cnt_c * jnp.exp(sc - mx)
    pn = cnt_n * jnp.exp(sn - mx)
    den = jnp.sum(pc, axis=-1, keepdims=True) + jnp.sum(pn, axis=-1, keepdims=True)
    of = lax.dot_general(pc.astype(BF16), win_ref[0, 1].astype(BF16), nt,
                         preferred_element_type=F32)
    of = (of + jnp.dot(pn.astype(BF16), kvn[:, D_B:].astype(BF16),
                       preferred_element_type=F32)) / den
    lane_head = lax.broadcasted_iota(jnp.int32, (s_len, D_B), 1) // HEAD_DIM
    o = jnp.zeros((s_len, D_B), F32)
    for h in range(H_B):
        o = o + jnp.where(lane_head == h, of[h * s_len:(h + 1) * s_len], 0.0)
    z = z_ref[0]
    st = st_ref[0]
    t = lax.broadcasted_iota(jnp.int32, (s_len, 1), 0)
    zm1 = jnp.where(t >= 1, pltpu.roll(z, 1, 0), st[1:2])
    zm2 = jnp.where(t >= 2, pltpu.roll(z, 2, 0), jnp.where(t == 1, st[1:2], st[0:1]))
    cw = cw_ref[...]
    cat_ref[0, :, :D_A] = gb_ref[0] * (cw[0:1] * zm2 + cw[1:2] * zm1 + cw[2:3] * z)
    cat_ref[0, :, D_A:] = o


def _mix_sample(q, kvn, win, z, gb, st, cw):
    b, s_len, _ = q.shape
    per = lambda i: (i,) + (0,) * 2
    blk = lambda a: pl.BlockSpec((1,) + a.shape[1:], per)
    return pl.pallas_call(
        _mix_sample_kernel,
        grid=(b,),
        in_specs=[blk(q), blk(kvn),
                  pl.BlockSpec((1,) + win.shape[1:], lambda i: (i, 0, 0, 0)),
                  blk(z), blk(gb), blk(st), pl.BlockSpec((3, D_A), lambda i: (0, 0))],
        out_specs=pl.BlockSpec((1, s_len, D_MODEL), per),
        out_shape=jax.ShapeDtypeStruct((b, s_len, D_MODEL), F32),
        compiler_params=_params(1), name="mix_sample",
    )(q, kvn, win, z, gb, st, cw)


def _linear_res_kernel(x_ref, w_ref, h_ref, out_ref):
    out_ref[...] = h_ref[...] + jnp.dot(x_ref[...].astype(BF16), w_ref[...],
                                        preferred_element_type=F32)


def _linear_res(x, w16, h):
    n, k = x.shape
    full = lambda a: pl.BlockSpec(a.shape, lambda i: (0, 0))
    return pl.pallas_call(
        _linear_res_kernel, grid=(1,),
        in_specs=[full(x), full(w16), full(h)], out_specs=full(h),
        out_shape=jax.ShapeDtypeStruct(h.shape, F32),
        compiler_params=_params(1), name="linear_res",
    )(x, w16, h)


def _swiglu_step(xn, wg_ref, wu_ref, wd_ref, acc_sc):
    a = jnp.dot(xn, wg_ref[0], preferred_element_type=F32)
    b = jnp.dot(xn, wu_ref[0], preferred_element_type=F32)
    t = (jax.nn.silu(a) * b).astype(BF16)
    acc_sc[...] += jnp.dot(t, wd_ref[0], preferred_element_type=F32)


def _ffn_dense_kernel(x_ref, g_ref, wg_ref, wu_ref, wd_ref, out_ref, xn_sc, acc_sc):
    f = pl.program_id(1)

    @pl.when(f == 0)
    def _():
        xn_sc[...] = _rms_bf16(x_ref[...], g_ref[...])
        acc_sc[...] = jnp.zeros_like(acc_sc)

    _swiglu_step(xn_sc[...], wg_ref, wu_ref, wd_ref, acc_sc)

    @pl.when(f == pl.num_programs(1) - 1)
    def _():
        out_ref[...] = x_ref[...] + acc_sc[...]


def _ffn_dense(x, g, wg, wu, wd, *, tm, fc):
    n = x.shape[0]
    ff = wg.shape[2]
    row = lambda i, f: (i, 0)
    return pl.pallas_call(
        _ffn_dense_kernel, grid=(n // tm, ff // fc),
        in_specs=[pl.BlockSpec((tm, D_MODEL), row), pl.BlockSpec((1, D_MODEL), lambda i, f: (0, 0)),
                  pl.BlockSpec((1, D_MODEL, fc), lambda i, f: (0, 0, f)),
                  pl.BlockSpec((1, D_MODEL, fc), lambda i, f: (0, 0, f)),
                  pl.BlockSpec((1, fc, D_MODEL), lambda i, f: (0, f, 0))],
        out_specs=pl.BlockSpec((tm, D_MODEL), row),
        out_shape=jax.ShapeDtypeStruct((n, D_MODEL), F32),
        scratch_shapes=[pltpu.VMEM((tm, D_MODEL), BF16), pltpu.VMEM((tm, D_MODEL), F32)],
        compiler_params=_params(2), name="ffn_dense",
    )(x, g, wg, wu, wd)


def _pack_bf16_pairs(x):
    half = x.shape[1] // 2
    lo = lax.bitcast_convert_type(x[:, :half].astype(BF16).astype(F32), jnp.int32)
    hi = lax.bitcast_convert_type(x[:, half:].astype(BF16).astype(F32), jnp.int32)
    return (hi & -65536) | lax.shift_right_logical(lo, 16)


def _unpack_bf16_pairs(p):
    lo = lax.bitcast_convert_type(p << 16, F32)
    hi = lax.bitcast_convert_type(p & -65536, F32)
    return jnp.concatenate([lo, hi], axis=1)


def _ffn_experts_kernel(te_ref, tv_ref, x_ref, wg_ref, wu_ref, wd_ref, out_ref, xn_sc, acc_sc):
    del te_ref
    i, f = pl.program_id(0), pl.program_id(1)
    n_valid = tv_ref[i]

    @pl.when(n_valid > 0)
    def _():
        @pl.when(f == 0)
        def _():
            row = lax.broadcasted_iota(jnp.int32, x_ref.shape, 0)
            xn_sc[...] = _unpack_bf16_pairs(jnp.where(row < n_valid, x_ref[...], 0)).astype(BF16)
            acc_sc[...] = jnp.zeros_like(acc_sc)

        def step(rows):
            xn = xn_sc[:rows]
            a = jnp.dot(xn, wg_ref[0].astype(BF16), preferred_element_type=F32)
            b = jnp.dot(xn, wu_ref[0].astype(BF16), preferred_element_type=F32)
            t = (jax.nn.silu(a) * b).astype(BF16)
            acc_sc[:rows] += jnp.dot(t, wd_ref[0].astype(BF16), preferred_element_type=F32)

        tm = xn_sc.shape[0]
        quarter = tm // 4
        for nq in range(1, 5):
            pl.when((n_valid > (nq - 1) * quarter) & (n_valid <= nq * quarter))(
                functools.partial(step, nq * quarter))

        @pl.when(f == pl.num_programs(1) - 1)
        def _():
            out_ref[...] = _pack_bf16_pairs(acc_sc[...])


def _ffn_experts(tile_expert, tile_valid, x, wg, wu, wd, *, tm, fc):
    n, wpk = x.shape
    ff = wg.shape[2]
    row = lambda i, f, te, tv: (i, 0)
    grid_spec = pltpu.PrefetchScalarGridSpec(
        num_scalar_prefetch=2, grid=(n // tm, ff // fc),
        in_specs=[pl.BlockSpec((tm, wpk), row),
                  pl.BlockSpec((1, D_MODEL, fc), lambda i, f, te, tv: (te[i], 0, f)),
                  pl.BlockSpec((1, D_MODEL, fc), lambda i, f, te, tv: (te[i], 0, f)),
                  pl.BlockSpec((1, fc, D_MODEL), lambda i, f, te, tv: (te[i], f, 0))],
        out_specs=pl.BlockSpec((tm, wpk), row),
        scratch_shapes=[pltpu.VMEM((tm, D_MODEL), BF16), pltpu.VMEM((tm, D_MODEL), F32)])
    return pl.pallas_call(
        _ffn_experts_kernel, grid_spec=grid_spec,
        out_shape=jax.ShapeDtypeStruct((n, wpk), jnp.int32),
        compiler_params=_params(2), name="ffn_experts",
    )(tile_expert, tile_valid, x, wg, wu, wd)


def _sc_worker_id():
    return lax.axis_index("s") * SC_CORES + lax.axis_index("c")


def _sc_dispatch(x_a, x_b, slots_a, slots_b, n_slots):
    w = x_a.shape[1]
    per_a = x_a.shape[0] // SC_WORKERS
    per_b = x_b.shape[0] // SC_WORKERS
    ch = min(SC_CHUNK // 2, per_a)
    n_chunks = per_a // ch
    mesh = plsc.VectorSubcoreMesh(core_axis_name="c", subcore_axis_name="s")
    dma = pltpu.SemaphoreType.DMA

    @functools.partial(
        pl.kernel, mesh=mesh, out_type=jax.ShapeDtypeStruct((n_slots, w), x_a.dtype),
        scratch_types=[pltpu.VMEM((per_a,), jnp.int32), pltpu.VMEM((per_a,), jnp.int32),
                       pltpu.VMEM((ch, w), x_a.dtype), pltpu.VMEM((ch, w), x_a.dtype),
                       pltpu.VMEM((per_b,), jnp.int32), pltpu.VMEM((per_b, w), x_b.dtype),
                       dma, dma, dma, dma, dma, dma])
    def dispatch(xa_hbm, xb_hbm, a0_hbm, a1_hbm, b0_hbm, b1_hbm, out_hbm,
                 idx0_v, idx1_v, rows0_v, rows1_v, idxb_v, rowsb_v,
                 lsem0, lsem1, s0sem0, s0sem1, s1sem0, s1sem1):
        wid = _sc_worker_id()
        base_a = pl.multiple_of(wid * per_a, 8)
        pltpu.sync_copy(a0_hbm.at[pl.ds(base_a, per_a)], idx0_v)
        pltpu.sync_copy(a1_hbm.at[pl.ds(base_a, per_a)], idx1_v)
        rows, lsem = (rows0_v, rows1_v), (lsem0, lsem1)
        ssem = ((s0sem0, s0sem1), (s1sem0, s1sem1))

        def load(c):
            return pltpu.async_copy(xa_hbm.at[pl.ds(base_a + c * ch, ch)], rows[c % 2], lsem[c % 2])

        def scatter(c):
            return [pltpu.async_copy(rows[c % 2], out_hbm.at[idx_v.at[pl.ds(c * ch, ch)]],
                                     ssem[k][c % 2])
                    for k, idx_v in enumerate((idx0_v, idx1_v))]

        loads, scatters = {0: load(0)}, {}
        for c in range(n_chunks):
            loads[c].wait()
            if c + 1 < n_chunks:
                for cp in scatters.pop(c - 1, ()):
                    cp.wait()
                loads[c + 1] = load(c + 1)
            scatters[c] = scatter(c)
        for cps in scatters.values():
            for cp in cps:
                cp.wait()

        base_b = pl.multiple_of(wid * per_b, 8)
        pltpu.sync_copy(xb_hbm.at[pl.ds(base_b, per_b)], rowsb_v)
        for slots_hbm in (b0_hbm, b1_hbm):
            pltpu.sync_copy(slots_hbm.at[pl.ds(base_b, per_b)], idxb_v)
            pltpu.sync_copy(rowsb_v, out_hbm.at[idxb_v])

    return dispatch(x_a, x_b, *slots_a, *slots_b)


def _sc_gather(table, idx):
    _, w = table.shape
    b = idx.shape[0]
    per_w = b // SC_WORKERS
    ch = max(c for c in range(8, SC_CHUNK + 1, 8) if per_w % c == 0)
    n_chunks = per_w // ch
    assert per_w * SC_WORKERS == b
    mesh = plsc.VectorSubcoreMesh(core_axis_name="c", subcore_axis_name="s")
    dma = pltpu.SemaphoreType.DMA

    @functools.partial(
        pl.kernel, mesh=mesh, out_type=jax.ShapeDtypeStruct((b, w), table.dtype),
        scratch_types=[pltpu.VMEM((per_w,), jnp.int32), pltpu.VMEM((ch, w), table.dtype),
                       pltpu.VMEM((ch, w), table.dtype), dma, dma, dma, dma])
    def gather(table_hbm, idx_hbm, out_hbm, idx_v, rows0_v, rows1_v, gsem0, gsem1, wsem0, wsem1):
        wid = _sc_worker_id()
        base = pl.multiple_of(wid * per_w, 8)
        pltpu.sync_copy(idx_hbm.at[pl.ds(base, per_w)], idx_v)
        rows, gsem, wsem = (rows0_v, rows1_v), (gsem0, gsem1), (wsem0, wsem1)

        def fetch(c):
            return pltpu.async_copy(table_hbm.at[idx_v.at[pl.ds(c * ch, ch)]], rows[c % 2],
                                    gsem[c % 2])

        def write(c):
            return pltpu.async_copy(rows[c % 2], out_hbm.at[pl.ds(base + c * ch, ch)], wsem[c % 2])

        fetches, writes = {0: fetch(0)}, {}
        for c in range(n_chunks):
            fetches[c].wait()
            if c + 1 < n_chunks:
                if c >= 1:
                    writes.pop(c - 1).wait()
                fetches[c + 1] = fetch(c + 1)
            writes[c] = write(c)
        for cp in writes.values():
            cp.wait()

    return gather(table, idx)


def _combine_kernel(h_ref, sel_ref, y0_ref, y1_ref, out_ref):
    sel = sel_ref[...]
    out_ref[...] = (h_ref[...] + sel[:, 2:3] * _unpack_bf16_pairs(y0_ref[...])
                    + sel[:, 3:4] * _unpack_bf16_pairs(y1_ref[...]))


def _combine(h, sel, yb, blk_off, tm):
    n = h.shape[0]
    wpk = yb.shape[1]
    half_blocks = yb.shape[0] // 2 // tm
    return pl.pallas_call(
        _combine_kernel, grid=(n // tm,),
        in_specs=[pl.BlockSpec((tm, D_MODEL), lambda i: (i, 0)),
                  pl.BlockSpec((tm, LANES), lambda i: (i, 0)),
                  pl.BlockSpec((tm, wpk), lambda i: (i + blk_off, 0)),
                  pl.BlockSpec((tm, wpk), lambda i: (i + blk_off + half_blocks, 0))],
        out_specs=pl.BlockSpec((tm, D_MODEL), lambda i: (i, 0)),
        out_shape=jax.ShapeDtypeStruct((n, D_MODEL), F32),
        compiler_params=_params(1), name="combine",
    )(h, sel, yb, yb)


def _route_plan(sels, counts, te_rows, n_tiles):
    counts = counts[0, :N_EXPERTS].astype(jnp.int32)
    padded = (counts + te_rows - 1) // te_rows * te_rows
    ends = jnp.cumsum(padded)
    starts = ends - padded
    experts = jnp.arange(N_EXPERTS, dtype=jnp.int32)

    def slots_of(selt, k):
        e = selt[k].astype(jnp.int32)
        start = jnp.sum(jnp.where(e[None, :] == experts[:, None], starts[:, None], 0), axis=0)
        return start + selt[4 + k].astype(jnp.int32)

    slots = [(slots_of(selt, 0), slots_of(selt, 1)) for selt in sels]
    tile_start = jnp.arange(n_tiles, dtype=jnp.int32) * te_rows
    tile_expert = jnp.minimum(jnp.sum(tile_start[:, None] >= ends[None, :], axis=1),
                              N_EXPERTS - 1).astype(jnp.int32)
    in_group = tile_start - starts[tile_expert]
    tile_valid = jnp.clip(counts[tile_expert] - in_group, 0, te_rows)
    tile_valid = jnp.where(tile_start < ends[-1], tile_valid, 0).astype(jnp.int32)
    return slots, tile_expert, tile_valid


def _gmlp_kernel(h_ref, g_ref, wuv_ref, vg_ref, mix_ref, bs_ref, wo_ref, out_ref, v_ref, um_sc,
                 *, period):
    h = h_ref[...]
    tm = h.shape[0]
    ln = bs_ref.shape[0]
    gw = D_MODEL // C_GROUPS
    xn = _rms_bf16(h, g_ref[...])
    vv = jax.nn.gelu(jnp.dot(xn, wuv_ref[:, D_MODEL:], preferred_element_type=F32))
    v = vv * lax.rsqrt(jnp.mean(vv * vv, axis=-1, keepdims=True) + EPS) * vg_ref[...]
    v_ref[0] = v[tm - v_ref.shape[1]:]
    vb = v.astype(BF16)
    u = jax.nn.gelu(jnp.dot(xn, wuv_ref[:, :D_MODEL], preferred_element_type=F32))
    r = lax.broadcasted_iota(jnp.int32, (ln, ln), 0)
    c = lax.broadcasted_iota(jnp.int32, (ln, ln), 1)
    causal = (r // period == c // period) & (c % period <= r % period)
    nw = mix_ref.shape[1]
    if ln != nw:
        pick = (lax.broadcasted_iota(jnp.int32, (ln, nw), 0) % period
                == lax.broadcasted_iota(jnp.int32, (ln, nw), 1)).astype(BF16)
        wr = lax.broadcasted_iota(jnp.int32, (nw, nw), 0)
        wc = lax.broadcasted_iota(jnp.int32, (nw, nw), 1)
        corner = (wr < period) & (wc <= wr)
    for gi in range(C_GROUPS):
        if ln == nw:
            wm = jnp.where(causal, mix_ref[gi], 0.0).astype(BF16)
        else:
            rows_w = jnp.dot(pick, jnp.where(corner, mix_ref[gi], 0.0).astype(BF16),
                             preferred_element_type=F32).astype(BF16)
            tiled = lax.dot_general(rows_w, pick, (((1,), (1,)), ((), ())),
                                    preferred_element_type=F32)
            wm = jnp.where(causal, tiled, 0.0).astype(BF16)
        bias = bs_ref[:, gi:gi + 1]
        for ch in range(tm // ln):
            rs = slice(ch * ln, (ch + 1) * ln)
            cs = slice(gi * gw, (gi + 1) * gw)
            mixed = jnp.dot(wm, vb[rs, cs], preferred_element_type=F32) + bias
            um_sc[rs, cs] = (u[rs, cs] * mixed).astype(BF16)
    out_ref[...] = h + jnp.dot(um_sc[...], wo_ref[...], preferred_element_type=F32)


def _gmlp(h, g, wuv16, vg, mix, bs, wo16, *, tm, period, seq, tail):
    n = h.shape[0]
    tps = seq // tm
    row = lambda i: (i, 0)
    full = lambda a: pl.BlockSpec(a.shape, lambda i: (0,) * a.ndim)
    return pl.pallas_call(
        functools.partial(_gmlp_kernel, period=period),
        grid=(n // tm,),
        in_specs=[pl.BlockSpec((tm, D_MODEL), row), full(g), full(wuv16), full(vg), full(mix),
                  full(bs), full(wo16)],
        out_specs=[pl.BlockSpec((tm, D_MODEL), row),
                   pl.BlockSpec((1, tail, D_MODEL), lambda i: (i // tps, 0, 0))],
        out_shape=[jax.ShapeDtypeStruct((n, D_MODEL), F32),
                   jax.ShapeDtypeStruct((n // seq, tail, D_MODEL), F32)],
        scratch_shapes=[pltpu.VMEM((tm, D_MODEL), BF16)],
        compiler_params=_params(1), name="gmlp",
    )(h, g, wuv16, vg, mix, bs, wo16)


def _router_kernel(h_ref, g_ref, wr_ref, cin_ref, xn_ref, sel_ref, selt_ref, cnt_ref):
    @pl.when(pl.program_id(0) == 0)
    def _():
        cnt_ref[...] = cin_ref[...]

    x = h_ref[...]
    xn = x * lax.rsqrt(jnp.mean(x * x, axis=-1, keepdims=True) + EPS) * g_ref[...]
    xn_ref[...] = _pack_bf16_pairs(xn)
    x_hi = xn.astype(BF16)
    x_lo = (xn - x_hi.astype(F32)).astype(BF16)
    w = wr_ref[...]
    w_hi = w.astype(BF16)
    w_lo = (w - w_hi.astype(F32)).astype(BF16)
    logits = (jnp.dot(x_hi, w_hi, preferred_element_type=F32)
              + (jnp.dot(x_lo, w_hi, preferred_element_type=F32)
                 + jnp.dot(x_hi, w_lo, preferred_element_type=F32)))
    lane = lax.broadcasted_iota(jnp.int32, logits.shape, 1)
    lanef = lane.astype(F32)
    logits = jnp.where(lane < N_EXPERTS, logits, -jnp.inf)
    m1 = jnp.max(logits, axis=-1, keepdims=True)
    i1 = jnp.min(jnp.where(logits == m1, lanef, float(LANES)), axis=-1, keepdims=True)
    rest = jnp.where(lanef == i1, -jnp.inf, logits)
    m2 = jnp.max(rest, axis=-1, keepdims=True)
    i2 = jnp.min(jnp.where(rest == m2, lanef, float(LANES)), axis=-1, keepdims=True)
    e2 = jnp.exp(m2 - m1)
    w1 = 1.0 / (1.0 + e2)
    w2 = e2 / (1.0 + e2)
    tm = x.shape[0]
    hit1, hit2 = lanef == i1, lanef == i2
    picked = jnp.where(hit1 | hit2, 1.0, 0.0)
    r_i = lax.broadcasted_iota(jnp.int32, (tm, tm), 0)
    c_i = lax.broadcasted_iota(jnp.int32, (tm, tm), 1)
    below = jnp.where(c_i < r_i, 1.0, 0.0).astype(BF16)
    before = jnp.dot(below, picked.astype(BF16), preferred_element_type=F32) + cnt_ref[...]
    rank1 = jnp.sum(jnp.where(hit1, before, 0.0), axis=-1, keepdims=True)
    rank2 = jnp.sum(jnp.where(hit2, before, 0.0), axis=-1, keepdims=True)
    cnt_ref[...] += jnp.sum(picked, axis=0, keepdims=True)
    vals = (i1, i2, w1, w2, rank1, rank2)
    sel = jnp.zeros_like(logits)
    for li, val in enumerate(vals):
        sel = jnp.where(lane == li, val, sel)
    sel_ref[...] = sel
    selt_ref[...] = sel.T[:selt_ref.shape[0]]


def _router(h, g, wr_pad, counts_in, tm):
    n = h.shape[0]
    row = lambda i: (i, 0)
    fix = lambda i: (0, 0)
    return pl.pallas_call(
        _router_kernel, grid=(n // tm,),
        in_specs=[pl.BlockSpec((tm, D_MODEL), row), pl.BlockSpec((1, D_MODEL), fix),
                  pl.BlockSpec((D_MODEL, LANES), fix), pl.BlockSpec((1, LANES), fix)],
        out_specs=[pl.BlockSpec((tm, D_MODEL // 2), row), pl.BlockSpec((tm, LANES), row),
                   pl.BlockSpec((8, tm), lambda i: (0, i)), pl.BlockSpec((1, LANES), fix)],
        out_shape=[jax.ShapeDtypeStruct((n, D_MODEL // 2), jnp.int32),
                   jax.ShapeDtypeStruct((n, LANES), F32),
                   jax.ShapeDtypeStruct((8, n), F32),
                   jax.ShapeDtypeStruct((1, LANES), F32)],
        compiler_params=_params(1), name="router",
    )(h, g, wr_pad, counts_in)


def _rope_tables(pos):
    half = HEAD_DIM // 2
    inv = ROPE_THETA ** (-jnp.arange(half, dtype=F32) / half)
    ang = pos.astype(F32)[:, None] * inv[None, :]
    cos, sin = jnp.cos(ang), jnp.sin(ang)
    zero = jnp.zeros_like(sin)
    rep = lambda a, b: jnp.tile(jnp.concatenate([a, b], axis=-1), (1, LANES // HEAD_DIM))
    return rep(cos, cos), rep(-sin, zero), rep(zero, sin)


def kernel(x_prompt, x_sample, state_conv, cache_kv_win, norm_mix_e, w_in_e, conv_w, q_gain,
           k_gain, w_out_e, norm_ffn_e, ffn_w_gate, ffn_w_up, ffn_w_down, norm_mix_o, w_uv,
           v_gain, w_s, b_s, w_out_o, norm_moe, w_router, moe_w_gate, moe_w_up, moe_w_down):
    bp, seq, _ = x_prompt.shape
    bs_, s_len, _ = x_sample.shape
    n_p, n_s = bp * seq, bs_ * s_len
    hp = x_prompt.reshape(n_p, D_MODEL)
    hs = x_sample.reshape(n_s, D_MODEL)
    tm_p = ROWS
    row = lambda a: a.reshape(1, -1)

    tabs_p = _rope_tables(jnp.arange(seq, dtype=jnp.int32))
    tabs_s = tuple(jnp.tile(t, (bs_, 1)) for t in
                   _rope_tables(PAST_LEN + jnp.arange(s_len, dtype=jnp.int32)))
    hd = jnp.arange(256) // HEAD_DIM
    bd = (hd[:, None] == hd[None, :]).astype(BF16)

    w_in16 = w_in_e[0].astype(BF16)
    w_out16 = w_out_e[0].astype(BF16)
    qg = jnp.tile(q_gain[0], 4).reshape(1, 256)
    kg = jnp.tile(k_gain[0], 4).reshape(1, 256)
    g0 = row(norm_mix_e[0])

    dils = tuple(dil for _, dil in BRANCHES)
    lw = min(max(wnd for wnd, _ in BRANCHES), seq)
    z_p, gb_p, kvt_p, *planes = _inproj(hp, g0, w_in16, tabs_p, qg, kg, bd, tm_p, seq, dils, lw)
    z_s, gb_s, kv_s, q_s = _inproj(hs, g0, w_in16, tabs_s, qg, kg, bd, n_s, n_s, ())

    os_, ls_ = [], []
    for di, dil in enumerate(dils):
        m = seq // dil
        q_d, k_d, v_d = (a.reshape(bp * dil, m, D_B) for a in planes[3 * di:3 * di + 3])
        qb = min(ATTN_ROWS, m)
        o, lse = _attn_prompt(q_d, k_d, v_d, qb=qb, pb=ATTN_ROWS // qb)
        os_.append(o.reshape(bp, dil, m, D_B))
        ls_.append(lse.reshape(bp, dil, m, LANES))
    hp = _outproj_prompt(z_p, gb_p, conv_w[0], os_, ls_, w_out16, hp, tm_p, seq, dils)

    cat_s = _mix_sample(q_s.reshape(bs_, s_len, D_B),
                        kv_s.reshape(bs_, s_len, 2 * D_B),
                        jnp.transpose(cache_kv_win[0], (0, 2, 3, 4, 1)).reshape(
                            bs_, 2, D_B, -1),
                        z_s.reshape(bs_, s_len, D_A), gb_s.reshape(bs_, s_len, D_A),
                        state_conv[0], conv_w[0])
    hs = _linear_res(cat_s.reshape(n_s, D_MODEL), w_out16, hs)

    conv_prompt = z_p.reshape(bp, seq, D_A)[:, seq - 2:][None]
    conv_sample = z_s.reshape(bs_, s_len, D_A)[:, s_len - 2:][None]
    kv_prompt = jnp.transpose(kvt_p.reshape(bp, 2, H_B, HEAD_DIM, lw), (0, 4, 1, 2, 3))[None]
    kv_sample = kv_s.reshape(bs_, s_len, 2, H_B, HEAD_DIM)[None]

    fg, fu, fd = (ffn_w_gate.astype(BF16), ffn_w_up.astype(BF16), ffn_w_down.astype(BF16))
    gf = row(norm_ffn_e[0])
    hp = _ffn_dense(hp, gf, fg, fu, fd, tm=DENSE_ROWS, fc=DENSE_FF)
    hs = _ffn_dense(hs, gf, fg, fu, fd, tm=n_s, fc=DENSE_FF)

    wuv16 = w_uv[0].astype(BF16)
    wo16 = w_out_o[0].astype(BF16)
    g1, vg = row(norm_mix_o[0]), row(v_gain[0])
    tail = seq - ((seq - 1) // CHUNK) * CHUNK
    hp, v_tail_p = _gmlp(hp, g1, wuv16, vg, w_s[0], b_s[0].T, wo16, tm=tm_p, period=CHUNK,
                         seq=seq, tail=tail)
    bias_s = jnp.tile(b_s[0][:, :s_len].T, (bs_, 1))
    hs, v_all_s = _gmlp(hs, g1, wuv16, vg, w_s[0], bias_s, wo16, tm=n_s, period=s_len,
                        seq=n_s, tail=n_s)
    v_prompt = v_tail_p[None]
    v_sample = v_all_s.reshape(bs_, s_len, D_MODEL)[None]

    wr_pad = jnp.pad(w_router[0], ((0, 0), (0, LANES - N_EXPERTS)))
    gm = row(norm_moe[0])
    xn_p, sel_p, selt_p, counts = _router(hp, gm, wr_pad, jnp.zeros((1, LANES), F32), tm_p)
    xn_s, sel_s, selt_s, counts = _router(hs, gm, wr_pad, counts, n_s)
    n_tok = n_p + n_s
    n_tiles = -(-(2 * n_tok + N_EXPERTS * (MOE_ROWS - 1)) // MOE_ROWS)
    (slots_p, slots_s), tile_expert, tile_valid = _route_plan(
        (selt_p, selt_s), counts, MOE_ROWS, n_tiles)
    xs = _sc_dispatch(xn_p, xn_s, slots_p, slots_s, n_tiles * MOE_ROWS)
    ys = _ffn_experts(tile_expert, tile_valid, xs, moe_w_gate[0], moe_w_up[0], moe_w_down[0],
                      tm=MOE_ROWS, fc=MOE_FF)
    fill = jnp.arange(-n_tok % COMBINE_ROWS, dtype=jnp.int32)
    yb = _sc_gather(ys, jnp.concatenate([slots_p[0], slots_s[0], fill,
                                         slots_p[1], slots_s[1], fill]))
    hp = _combine(hp, sel_p, yb, 0, COMBINE_ROWS)
    hs = _combine(hs, sel_s, yb, n_p // n_s, n_s)

    return (hp.reshape(bp, seq, D_MODEL), hs.reshape(bs_, s_len, D_MODEL),
            conv_prompt, conv_sample, kv_prompt, kv_sample, v_prompt, v_sample)
```
